```python
import jax, jax.numpy as jnp
from jax import lax
import numpy as np

D_MODEL = 1024
BATCH = 16
SEQ = 256
DEPTH = 2
DEC_BATCH = 4
DEC_SEQ = 2048
PAST_LEN = 512

GRID_W = 64
HEAD_DIM = 64
H_A = 8
KVH_A = 2
H_B = 4
H_C = 4
WINDOW = 128
BLOCK = 128
NA_ROWS = 8
NA_COLS = 16
ROPE_BASE = 10000.0
N_EXPERTS = 16
CAPACITY_FACTOR = 2
D_FF = 1024
LN_EPS = 1e-6

kernel_name = "hybrid_diffusion_parallel_heads_step"

F32 = jnp.float32


def _split_sizes():
    qa, kva = H_A * HEAD_DIM, KVH_A * HEAD_DIM
    b, c = H_B * HEAD_DIM, H_C * HEAD_DIM
    return [qa, kva, kva, b, b, b, c, c, c, c]


def _layernorm(x, g, b):
    xf = x.astype(F32)
    mu = xf.mean(-1, keepdims=True)
    var = jnp.square(xf - mu).mean(-1, keepdims=True)
    return ((xf - mu) * lax.rsqrt(var + LN_EPS) * g.astype(F32) + b.astype(F32)).astype(x.dtype)


def _modulation(cond, w, b):
    m = jax.nn.silu(cond) @ w + b
    return [p[:, None, :] for p in jnp.split(m, 6, axis=-1)]


def _project(h, w_in):
    bsz, L = h.shape[:2]
    pts = np.cumsum(_split_sizes())[:-1].tolist()
    qa, ka, va, qb, kb, vb, qc, kc, vc, gc = jnp.split(h @ w_in, pts, axis=-1)
    hd = lambda t, n: t.reshape(bsz, L, n, HEAD_DIM)
    return (hd(qa, H_A), hd(ka, KVH_A), hd(va, KVH_A), hd(qb, H_B), hd(kb, H_B), hd(vb, H_B),
            hd(qc, H_C), hd(kc, H_C), hd(vc, H_C), gc)


def _axial_rope(x):
    L = x.shape[1]
    t = np.arange(L)
    row, col = t // GRID_W, t % GRID_W
    half = HEAD_DIM // 2
    inv = 1.0 / (ROPE_BASE ** (np.arange(0, half, 2) / half))

    def rot(xp, pos):
        ang = jnp.asarray(pos[:, None] * inv[None, :], F32)
        cos, sin = jnp.cos(ang)[None, :, None, :], jnp.sin(ang)[None, :, None, :]
        x1, x2 = jnp.split(xp.astype(F32), 2, axis=-1)
        return jnp.concatenate([x1 * cos - x2 * sin, x1 * sin + x2 * cos], -1)

    xr, xc = jnp.split(x, 2, axis=-1)
    return jnp.concatenate([rot(xr, row), rot(xc, col)], -1).astype(x.dtype)


def _to_blocks(x):
    bsz, L = x.shape[:2]
    return jnp.moveaxis(x.reshape(bsz, L // BLOCK, BLOCK, *x.shape[2:]), 1, 0)


def _from_blocks(x):
    x = jnp.moveaxis(x, 0, 1)
    return x.reshape(x.shape[0], -1, *x.shape[3:])


def _gqa_attn(q, k, v, mask=None, sink=None):
    bsz, Q, H, d = q.shape
    kvh = k.shape[2]
    g = H // kvh
    qg = q.reshape(bsz, Q, kvh, g, d)
    s = jnp.einsum('bqhgd,bkhd->bhgqk', qg, k).astype(F32) * (d ** -0.5)
    if mask is not None:
        s = jnp.where(mask, s, -jnp.inf)
    m = s.max(-1, keepdims=True)
    if sink is not None:
        sk = sink.astype(F32).reshape(kvh, g, 1, 1)
        m = jnp.maximum(m, sk)
    p = jnp.exp(s - m)
    den = p.sum(-1, keepdims=True)
    if sink is not None:
        den = den + jnp.exp(sk - m)
    o = jnp.einsum('bhgqk,bkhd->bqhgd', (p / den).astype(v.dtype), v)
    return o.reshape(bsz, Q, H, d)


def _ctx_attn(q, k, v, sink=None):
    out = lax.map(lambda qb: _gqa_attn(qb, k, v, None, sink), _to_blocks(q))
    return _from_blocks(out)


def _window_attn_latent(q, k, v, kc, vc, sink):
    bsz, L = q.shape[:2]
    nb = L // BLOCK
    Lc = kc.shape[1]

    def windows(t):
        tb = jnp.pad(t, [(0, 0), (BLOCK, BLOCK), (0, 0), (0, 0)]).reshape(bsz, nb + 2, BLOCK, *t.shape[2:])
        return jnp.moveaxis(jnp.concatenate([tb[:, :-2], tb[:, 1:-1], tb[:, 2:]], axis=2), 1, 0)

    qi = np.arange(nb)[:, None, None] * BLOCK + np.arange(BLOCK)[None, :, None]
    kj = np.arange(nb)[:, None, None] * BLOCK - BLOCK + np.arange(3 * BLOCK)[None, None, :]
    band = (np.abs(qi - kj) <= WINDOW) & (kj >= 0) & (kj < L)
    full = jnp.asarray(np.concatenate([band, np.ones((nb, BLOCK, Lc), bool)], -1))

    def blk(args):
        qb, kb, vb, mb = args
        return _gqa_attn(qb, jnp.concatenate([kb, kc], 1), jnp.concatenate([vb, vc], 1), mb, sink)

    out = lax.map(blk, (_to_blocks(q), windows(k), windows(v), full))
    return _from_blocks(out)


def _na_indices(L):
    rows = L // GRID_W
    kh, kw = min(NA_ROWS, rows), min(NA_COLS, GRID_W)
    t = np.arange(L)
    r, c = t // GRID_W, t % GRID_W
    rs = np.clip(r - kh // 2, 0, rows - kh)
    cs = np.clip(c - kw // 2, 0, GRID_W - kw)
    kr = rs[:, None, None] + np.arange(kh)[None, :, None]
    kc = cs[:, None, None] + np.arange(kw)[None, None, :]
    kr, kc = np.broadcast_arrays(kr, kc)
    idx = (kr * GRID_W + kc).reshape(L, kh * kw)
    dr = (kr - r[:, None, None]).reshape(L, -1) + NA_ROWS - 1
    dc = (kc - c[:, None, None]).reshape(L, -1) + NA_COLS - 1
    return idx, dr, dc


def _na_latent(q, k, v, kc, vc, rpb):
    bsz, L, H, d = q.shape
    nb = L // BLOCK
    idx, dr, dc = _na_indices(L)
    K = idx.shape[1]
    bias = jnp.moveaxis(rpb[:, dr, dc].reshape(H, nb, BLOCK, K), 1, 0)
    scale = d ** -0.5

    def blk(args):
        qb, ib, bb = args
        kn, vn = k[:, ib], v[:, ib]
        s_nb = jnp.einsum('bqhd,bqkhd->bhqk', qb, kn).astype(F32) * scale + bb.astype(F32)
        s_cx = jnp.einsum('bqhd,bkhd->bhqk', qb, kc).astype(F32) * scale
        p = jax.nn.softmax(jnp.concatenate([s_nb, s_cx], -1), axis=-1).astype(v.dtype)
        return (jnp.einsum('bhqk,bqkhd->bqhd', p[..., :K], vn)
                + jnp.einsum('bhqk,bkhd->bqhd', p[..., K:], vc))

    out = lax.map(blk, (_to_blocks(q), jnp.asarray(idx.reshape(nb, BLOCK, K), jnp.int32), bias))
    return _from_blocks(out)


def _retention_dir(q, k, v, log_gamma, s0):
    j = np.arange(BLOCK).astype(np.float32)
    rel = j[:, None] - j[None, :]
    lg = log_gamma.astype(F32)
    dmat = jnp.where(jnp.asarray(rel >= 0), jnp.exp(lg[:, None, None] * np.maximum(rel, 0.0)), 0.0)
    xi = jnp.exp(lg[None, :] * (j[:, None] + 1.0))
    zeta = jnp.exp(lg[None, :] * (BLOCK - 1.0 - j)[:, None])
    g_chunk = jnp.exp(lg * BLOCK)

    def step(S, args):
        qc, kc, vc = args
        inner = jnp.einsum('bhij,bjhe->bihe', jnp.einsum('bihd,bjhd->bhij', qc, kc) * dmat, vc)
        cross = jnp.einsum('bihd,bhde->bihe', qc, S) * xi[None, :, :, None]
        S = S * g_chunk[None, :, None, None] + jnp.einsum('bjhd,bjhe->bhde', kc * zeta[None, :, :, None], vc)
        return S, inner + cross

    S, out = lax.scan(step, s0.astype(F32), (_to_blocks(q), _to_blocks(k), _to_blocks(v)))
    return _from_blocks(out), S


def _retention(q, k, v, lg, s0_f, s0_b):
    qf, kf, vf = q.astype(F32), k.astype(F32) * (HEAD_DIM ** -0.5), v.astype(F32)
    of, sf = _retention_dir(qf, kf, vf, lg[0], s0_f)
    ob, sb = _retention_dir(qf[:, ::-1], kf[:, ::-1], vf[:, ::-1], lg[1], s0_b)
    return of + ob[:, ::-1], sf, sb


def _retention_out(o, g, gn_w):
    mu = o.mean(-1, keepdims=True)
    var = jnp.square(o - mu).mean(-1, keepdims=True)
    on = ((o - mu) * lax.rsqrt(var + LN_EPS)).reshape(*g.shape) * gn_w.astype(F32)
    return (jax.nn.silu(g.astype(F32)) * on).astype(g.dtype)


def _mixer_context(h, w_in, w_out, sink, lg, gn_w):
    bsz, L = h.shape[:2]
    qa, ka, va, qb, kb, vb, qc, kc, vc, gc = _project(h, w_in)
    oa = _ctx_attn(qa, ka, va, sink)
    ob = _ctx_attn(qb, kb, vb)
    z = jnp.zeros((bsz, H_C, HEAD_DIM, HEAD_DIM), F32)
    oc, sf, sb = _retention(qc, kc, vc, lg, z, z)
    cat = jnp.concatenate([oa.reshape(bsz, L, -1), ob.reshape(bsz, L, -1), _retention_out(oc, gc, gn_w)], -1)
    return cat @ w_out, (ka, va, kb, vb, jnp.stack([sf, sb], 1))


def _mixer_latent(h, cak, cav, cbk, cbv, cst, w_in, w_out, sink, rpb, lg, gn_w):
    bsz, L = h.shape[:2]
    qa, ka, va, qb, kb, vb, qc, kc, vc, gc = _project(h, w_in)
    oa = _window_attn_latent(_axial_rope(qa), _axial_rope(ka), va, cak, cav, sink)
    ob = _na_latent(qb, kb, vb, cbk, cbv, rpb)
    oc, _, _ = _retention(qc, kc, vc, lg, cst[:, 0], cst[:, 1])
    cat = jnp.concatenate([oa.reshape(bsz, L, -1), ob.reshape(bsz, L, -1), _retention_out(oc, gc, gn_w)], -1)
    return cat @ w_out


def _expert_choice(h, w_router, w_gate_up, w_down):
    n = h.shape[1]
    cap = CAPACITY_FACTOR * n // N_EXPERTS
    aff = jax.nn.softmax((h @ w_router).astype(F32), axis=-1)
    gate, idx = lax.top_k(jnp.swapaxes(aff, 1, 2), cap)

    def per_request(hb, ib, gb):
        xe = hb[ib]
        a, b = jnp.split(jnp.einsum('ecd,edf->ecf', xe, w_gate_up), 2, axis=-1)
        ye = jnp.einsum('ecf,efd->ecd', jax.nn.silu(a) * b, w_down) * gb[..., None].astype(hb.dtype)
        return jnp.zeros_like(hb).at[ib.reshape(-1)].add(ye.reshape(-1, hb.shape[-1]))

    return jax.vmap(per_request)(h, idx, gate)


def setup_inputs(seed: int = 0) -> dict:
    key = jax.random.key(seed)
    ks = jax.random.split(key, 26)
    beta = (8 * DEPTH) ** -0.25
    d_in = sum(_split_sizes())
    nrm = lambda k, shape, s=1.0: jax.random.normal(k, shape, F32) * s
    dec0 = jnp.asarray(np.log(-np.log(1.0 - 2.0 ** (-5.0 - np.arange(H_C)))), F32)
    return {
        "x_prompt": nrm(ks[0], (BATCH, SEQ, D_MODEL)),
        "x_sample": nrm(ks[1], (DEC_BATCH, DEC_SEQ, D_MODEL)),
        "cache_attn_a_k": nrm(ks[2], (DEC_BATCH, DEPTH, PAST_LEN, KVH_A, HEAD_DIM)),
        "cache_attn_a_v": nrm(ks[3], (DEC_BATCH, DEPTH, PAST_LEN, KVH_A, HEAD_DIM)),
        "cache_attn_b_k": nrm(ks[4], (DEC_BATCH, DEPTH, PAST_LEN, H_B, HEAD_DIM)),
        "cache_attn_b_v": nrm(ks[5], (DEC_BATCH, DEPTH, PAST_LEN, H_B, HEAD_DIM)),
        "state_ret": nrm(ks[6], (DEC_BATCH, DEPTH, 2, H_C, HEAD_DIM, HEAD_DIM)),
        "c": nrm(ks[7], (DEC_BATCH, D_MODEL)),
        "c_ctx": nrm(ks[8], (D_MODEL,)),
        "w_ada": nrm(ks[9], (DEPTH, D_MODEL, 6 * D_MODEL), 0.5 * D_MODEL ** -0.5),
        "b_ada": nrm(ks[10], (DEPTH, 6 * D_MODEL), 0.01),
        "w_in": nrm(ks[11], (DEPTH, D_MODEL, d_in), D_MODEL ** -0.5),
        "w_out": nrm(ks[12], (DEPTH, D_MODEL, D_MODEL), beta * D_MODEL ** -0.5),
        "attn_sink": nrm(ks[13], (DEPTH, H_A), 0.5),
        "na_rpb": nrm(ks[14], (DEPTH, H_B, 2 * NA_ROWS - 1, 2 * NA_COLS - 1), 0.5),
        "ret_decay": jnp.broadcast_to(dec0, (DEPTH, 2, H_C)) + nrm(ks[15], (DEPTH, 2, H_C), 0.05),
        "ret_gn": 1.0 + nrm(ks[16], (DEPTH, H_C * HEAD_DIM), 0.02),
        "ln1_g": 1.0 + nrm(ks[17], (DEPTH, D_MODEL), 0.02),
        "ln1_b": nrm(ks[18], (DEPTH, D_MODEL), 0.01),
        "ln2_g": 1.0 + nrm(ks[19], (DEPTH, D_MODEL), 0.02),
        "ln2_b": nrm(ks[20], (DEPTH, D_MODEL), 0.01),
        "w_router": nrm(ks[21], (DEPTH, D_MODEL, N_EXPERTS), D_MODEL ** -0.5),
        "w_gate_up": nrm(ks[22], (DEPTH, N_EXPERTS, D_MODEL, 2 * D_FF), D_MODEL ** -0.5),
        "w_down": nrm(ks[23], (DEPTH, N_EXPERTS, D_FF, D_MODEL), beta * D_FF ** -0.5),
    }


def reference(x_prompt, x_sample, cache_attn_a_k, cache_attn_a_v, cache_attn_b_k, cache_attn_b_v, state_ret,
              c, c_ctx, w_ada, b_ada, w_in, w_out, attn_sink, na_rpb, ret_decay, ret_gn,
              ln1_g, ln1_b, ln2_g, ln2_b, w_router, w_gate_up, w_down):
    alpha = (2 * DEPTH) ** 0.25

    x = x_prompt
    a_k, a_v, b_k, b_v, st = [], [], [], [], []
    for l in range(DEPTH):
        sh1, sc1, g1, sh2, sc2, g2 = _modulation(c_ctx[None, :], w_ada[l], b_ada[l])
        lg = -jnp.exp(ret_decay[l].astype(F32))
        mix, (ka, va, kb, vb, s_l) = _mixer_context(x * (1 + sc1) + sh1, w_in[l], w_out[l], attn_sink[l], lg, ret_gn[l])
        x = _layernorm(alpha * x + g1 * mix, ln1_g[l], ln1_b[l])
        ff = _expert_choice(x * (1 + sc2) + sh2, w_router[l], w_gate_up[l], w_down[l])
        x = _layernorm(alpha * x + g2 * ff, ln2_g[l], ln2_b[l])
        a_k.append(ka); a_v.append(va); b_k.append(kb); b_v.append(vb); st.append(s_l)
    y_prompt = x
    new_attn_a_k = jnp.stack(a_k, 1)
    new_attn_a_v = jnp.stack(a_v, 1)
    new_attn_b_k = jnp.stack(b_k, 1)
    new_attn_b_v = jnp.stack(b_v, 1)
    new_state_ret = jnp.stack(st, 1)

    x = x_sample
    for l in range(DEPTH):
        sh1, sc1, g1, sh2, sc2, g2 = _modulation(c, w_ada[l], b_ada[l])
        lg = -jnp.exp(ret_decay[l].astype(F32))
        mix = _mixer_latent(x * (1 + sc1) + sh1, cache_attn_a_k[:, l], cache_attn_a_v[:, l],
                            cache_attn_b_k[:, l], cache_attn_b_v[:, l], state_ret[:, l],
                            w_in[l], w_out[l], attn_sink[l], na_rpb[l], lg, ret_gn[l])
        x = _layernorm(alpha * x + g1 * mix, ln1_g[l], ln1_b[l])
        ff = _expert_choice(x * (1 + sc2) + sh2, w_router[l], w_gate_up[l], w_down[l])
        x = _layernorm(alpha * x + g2 * ff, ln2_g[l], ln2_b[l])
    y_sample = x

    return (y_prompt, y_sample, new_attn_a_k, new_attn_a_v, new_attn_b_k, new_attn_b_v, new_state_ret)
```

```python
import functools

import numpy as np
import jax
import jax.numpy as jnp
from jax import lax
from jax.experimental import pallas as pl
from jax.experimental.pallas import tpu as pltpu

F32 = jnp.float32
BF16 = jnp.bfloat16
I32 = jnp.int32

D_MODEL = 1024
DEPTH = 2
GRID_W = 64
HEAD_DIM = 64
H_A, KVH_A, H_B, H_C = 8, 2, 4, 4
D_IN = 2560
BLOCK = 128
WINDOW = 128
NA_ROWS, NA_COLS = 8, 16
NA_WIN_BLOCKS = 5
ROPE_BASE = 10000.0
N_EXPERTS = 16
CAPACITY_FACTOR = 2
D_FF = 1024
LN_EPS = 1e-6
ALPHA = (2 * DEPTH) ** 0.25
NEG = -1e30
LANES = 128
QK_SCALE = HEAD_DIM ** -0.5
VMEM_LIMIT = 56 * 1024 * 1024

COL_QA, COL_KA, COL_VA = 0, 4, 5
COL_QB, COL_KB, COL_VB = 3, 4, 5
COL_QC, COL_KC, COL_VC, COL_GC = 6, 7, 8, 9


def _cp(n_axes):
    return pltpu.CompilerParams(dimension_semantics=("arbitrary",) * n_axes, vmem_limit_bytes=VMEM_LIMIT)


def _dot(a, b):
    return jnp.dot(a, b, preferred_element_type=F32)


def _dot_t(a, b):
    return lax.dot_general(a, b, (((1,), (1,)), ((), ())), preferred_element_type=F32)


def _silu(x):
    return x / (1.0 + jnp.exp(-x))


def _lane_low(shape):
    return lax.broadcasted_iota(I32, shape, len(shape) - 1) < HEAD_DIM


def _layernorm(y, g, b):
    mu = jnp.mean(y, axis=-1, keepdims=True)
    d = y - mu
    var = jnp.mean(d * d, axis=-1, keepdims=True)
    return d * lax.rsqrt(var + LN_EPS) * g + b


def _mod_kernel(c_ref, w_ref, b_ref, o_ref):
    s = _silu(c_ref[...]).astype(BF16)
    o_ref[...] = _dot(s, w_ref[...].astype(BF16)) + b_ref[...]


def _modulation(cond, w_ada, b_ada):
    tn = 1536
    return pl.pallas_call(
        _mod_kernel,
        out_shape=jax.ShapeDtypeStruct((DEPTH, 8, 6 * D_MODEL), F32),
        grid=(DEPTH, 6 * D_MODEL // tn),
        in_specs=[pl.BlockSpec((8, D_MODEL), lambda l, j: (0, 0)),
                  pl.BlockSpec((None, D_MODEL, tn), lambda l, j: (l, 0, j)),
                  pl.BlockSpec((None, 1, tn), lambda l, j: (l, 0, j))],
        out_specs=pl.BlockSpec((None, 8, tn), lambda l, j: (l, 0, j)),
        compiler_params=_cp(2), name="modulation",
    )(cond, w_ada, b_ada.reshape(DEPTH, 1, 6 * D_MODEL))


def _mod_spec(k, n_grid):
    if n_grid == 1:
        return pl.BlockSpec((None, 1, D_MODEL), lambda r: (r, 0, k))
    return pl.BlockSpec((None, 1, D_MODEL), lambda r, i: (r, 0, k))


def _rope_tables(L):
    t = np.arange(L)
    pos = (t // GRID_W, t % GRID_W)
    half = HEAD_DIM // 2
    inv = 1.0 / (ROPE_BASE ** (np.arange(0, half, 2) / half))
    cos = np.zeros((L, HEAD_DIM)); s_up = np.zeros((L, HEAD_DIM)); s_dn = np.zeros((L, HEAD_DIM))
    for part in range(2):
        ang = (pos[part][:, None] * inv[None, :]).astype(np.float32).astype(np.float64)
        q = half // 2
        base = part * half
        cos[:, base:base + q] = np.cos(ang); cos[:, base + q:base + half] = np.cos(ang)
        s_up[:, base:base + q] = -np.sin(ang)
        s_dn[:, base + q:base + half] = np.sin(ang)
    tile = lambda a: jnp.asarray(np.tile(a, (1, LANES // HEAD_DIM)), F32)
    return tile(cos), tile(s_up), tile(s_dn)


def _proj_kernel(*refs, rope):
    if rope:
        x_ref, sc_ref, sh_ref, w_ref, cos_ref, sup_ref, sdn_ref, o_ref = refs
    else:
        x_ref, sc_ref, sh_ref, w_ref, o_ref = refs
    h = (x_ref[...] * (1.0 + sc_ref[...]) + sh_ref[...]).astype(BF16)
    n_rope = (H_A + KVH_A) * HEAD_DIM
    step = 512
    for j in range(0, D_IN, step):
        acc = _dot(h, w_ref[:, j:j + step])
        if rope and j < n_rope:
            q = HEAD_DIM // 4
            for c in range(0, step, LANES):
                a = acc[:, c:c + LANES]
                if j + c < n_rope:
                    a = (a * cos_ref[...] + pltpu.roll(a, LANES - q, 1) * sup_ref[...]
                         + pltpu.roll(a, q, 1) * sdn_ref[...])
                o_ref[:, j + c:j + c + LANES] = a
        else:
            o_ref[:, j:j + step] = acc


def _project(x, mods, w_bf16, tm, rope_tabs=None):
    R, L, _ = x.shape
    rope = rope_tabs is not None
    in_specs = [pl.BlockSpec((None, tm, D_MODEL), lambda r, i: (r, i, 0)),
                _mod_spec(1, 2), _mod_spec(0, 2),
                pl.BlockSpec((D_MODEL, D_IN), lambda r, i: (0, 0))]
    args = [x, mods, mods, w_bf16]
    if rope:
        in_specs += [pl.BlockSpec((tm, LANES), lambda r, i: (i, 0))] * 3
        args += list(rope_tabs)
    return pl.pallas_call(
        functools.partial(_proj_kernel, rope=rope),
        out_shape=jax.ShapeDtypeStruct((R, L, D_IN), F32),
        grid=(R, L // tm), in_specs=in_specs,
        out_specs=pl.BlockSpec((None, tm, D_IN), lambda r, i: (r, i, 0)),
        compiler_params=_cp(2), name="in_proj",
    )(*args)


def _dup_head(x, head):
    sw = pltpu.roll(x, HEAD_DIM, 1)
    low = _lane_low(x.shape)
    return jnp.where(low, x, sw) if head == 0 else jnp.where(low, sw, x)


def _mask_head(x, sub):
    low = _lane_low(x.shape)
    return jnp.where(low, x, 0.0) if sub == 0 else jnp.where(low, 0.0, x)


def _softmax_pv(scores, values, sink=None):
    m = scores[0].max(axis=-1, keepdims=True)
    for s in scores[1:]:
        m = jnp.maximum(m, s.max(axis=-1, keepdims=True))
    if sink is not None:
        m = jnp.maximum(m, sink)
    den = jnp.exp(sink - m) if sink is not None else 0.0
    o = 0.0
    for s, v in zip(scores, values):
        p = jnp.exp(s - m)
        den = den + p.sum(axis=-1, keepdims=True)
        o = o + _dot(p.astype(BF16), v)
    return o / den


def _gqa_group(q_ref, heads, rows):
    parts = []
    for h in heads:
        qp = q_ref[:, LANES * (h // 2):LANES * (h // 2) + LANES] * QK_SCALE
        parts.append(_mask_head(qp, h % 2))
    return jnp.concatenate(parts, axis=0).astype(BF16) if len(parts) > 1 else parts[0].astype(BF16)


def _sink_col(sink_ref, heads, rows):
    return jnp.concatenate([jnp.full((rows, 1), sink_ref[h], F32) for h in heads], axis=0)


def _store_heads(o_ref, o, heads, rows):
    low = _lane_low((rows, LANES))
    for j in range(0, len(heads), 2):
        h = heads[j]
        pair = jnp.where(low, o[j * rows:(j + 1) * rows], o[(j + 1) * rows:(j + 2) * rows])
        o_ref[:, LANES * (h // 2):LANES * (h // 2) + LANES] = pair


def _attn_a_ctx_kernel(sink_ref, q_ref, k_ref, v_ref, o_ref):
    L = q_ref.shape[0]
    g = H_A // KVH_A
    for hk in range(KVH_A):
        heads = list(range(hk * g, (hk + 1) * g))
        kd = _dup_head(k_ref[...], hk).astype(BF16)
        vd = _dup_head(v_ref[...], hk).astype(BF16)
        qs = _gqa_group(q_ref, heads, L)
        o = _softmax_pv([_dot_t(qs, kd)], [vd], _sink_col(sink_ref, heads, L))
        _store_heads(o_ref, o, heads, L)


def _attn_a_lat_kernel(sink_ref, q_ref, k_ref, v_ref, kc_ref, vc_ref, o_ref, *, nb):
    i = pl.program_id(1)
    g = H_A // KVH_A
    wk = 3 * BLOCK
    start = pl.multiple_of(jnp.clip(i - 1, 0, nb - 3) * BLOCK, BLOCK)
    kw = k_ref[pl.ds(start, wk), :]
    vw = v_ref[pl.ds(start, wk), :]
    rows = g * BLOCK
    qi = i * BLOCK + (lax.broadcasted_iota(I32, (rows, wk), 0) & (BLOCK - 1))
    kj = start + lax.broadcasted_iota(I32, (rows, wk), 1)
    valid = jnp.abs(qi - kj) <= WINDOW
    for hk in range(KVH_A):
        heads = list(range(hk * g, (hk + 1) * g))
        qs = _gqa_group(q_ref, heads, BLOCK)
        s_w = jnp.where(valid, _dot_t(qs, _dup_head(kw, hk).astype(BF16)), NEG)
        s_c = _dot_t(qs, _dup_head(kc_ref[...], hk).astype(BF16))
        o = _softmax_pv([s_w, s_c],
                        [_dup_head(vw, hk).astype(BF16), _dup_head(vc_ref[...], hk).astype(BF16)],
                        _sink_col(sink_ref, heads, BLOCK))
        _store_heads(o_ref, o, heads, BLOCK)


def _smem_spec():
    return pl.BlockSpec(memory_space=pltpu.SMEM)


def _attn_a_ctx(qkv, sink):
    R, L, _ = qkv.shape
    wq, wkv = H_A * HEAD_DIM, KVH_A * HEAD_DIM
    return pl.pallas_call(
        _attn_a_ctx_kernel,
        out_shape=jax.ShapeDtypeStruct((R, L, wq), F32),
        grid=(R,),
        in_specs=[_smem_spec(),
                  pl.BlockSpec((None, L, wq), lambda r: (r, 0, COL_QA)),
                  pl.BlockSpec((None, L, wkv), lambda r: (r, 0, COL_KA)),
                  pl.BlockSpec((None, L, wkv), lambda r: (r, 0, COL_VA))],
        out_specs=pl.BlockSpec((None, L, wq), lambda r: (r, 0, 0)),
        compiler_params=_cp(1), name="attn_a_ctx",
    )(sink, qkv, qkv, qkv)


def _attn_a_lat(qkv, sink, cache_k, cache_v, layer):
    R, L, _ = qkv.shape
    nb = L // BLOCK
    P = cache_k.shape[2]
    wq, wkv = H_A * HEAD_DIM, KVH_A * HEAD_DIM
    cache_spec = pl.BlockSpec((None, None, P, wkv), lambda r, i: (r, layer, 0, 0))
    return pl.pallas_call(
        functools.partial(_attn_a_lat_kernel, nb=nb),
        out_shape=jax.ShapeDtypeStruct((R, L, wq), F32),
        grid=(R, nb),
        in_specs=[_smem_spec(),
                  pl.BlockSpec((None, BLOCK, wq), lambda r, i: (r, i, COL_QA)),
                  pl.BlockSpec((None, L, wkv), lambda r, i: (r, 0, COL_KA)),
                  pl.BlockSpec((None, L, wkv), lambda r, i: (r, 0, COL_VA)),
                  cache_spec, cache_spec],
        out_specs=pl.BlockSpec((None, BLOCK, wq), lambda r, i: (r, i, 0)),
        compiler_params=_cp(2), name="attn_a_lat",
    )(sink, qkv, qkv, qkv, cache_k, cache_v)


def _attn_b_ctx_kernel(q_ref, k_ref, v_ref, o_ref):
    L = q_ref.shape[0]
    low = _lane_low((L, LANES))
    for p in range(H_B // 2):
        cs = slice(LANES * p, LANES * p + LANES)
        q = q_ref[:, cs] * QK_SCALE
        k = k_ref[:, cs].astype(BF16)
        v = v_ref[:, cs].astype(BF16)
        outs = [_softmax_pv([_dot_t(_mask_head(q, sub).astype(BF16), k)], [v]) for sub in range(2)]
        o_ref[:, cs] = jnp.where(low, outs[0], outs[1])


def _bias_table_kernel(rpb_ref, o_ref):
    n_dr, n_dc = 2 * NA_ROWS - 1, 2 * NA_COLS - 1
    n_t = o_ref.shape[1]
    shape = (GRID_W, LANES)
    cq = lax.broadcasted_iota(I32, shape, 0)
    lane = lax.broadcasted_iota(I32, shape, 1)
    right = lane >= GRID_W
    ck = lane & (GRID_W - 1)
    cs = jnp.clip(cq - NA_COLS // 2, 0, GRID_W - NA_COLS)
    col_ok = (ck >= cs) & (ck < cs + NA_COLS)
    dc = ck - cq + (NA_COLS - 1)

    def body(t, carry, h):
        dr_l = t - (NA_ROWS + 1)
        acc = jnp.zeros(shape, F32)
        for d in range(n_dc):
            vals = []
            for dr in (dr_l, dr_l + 1):
                ok = (dr >= -(NA_ROWS - 1)) & (dr <= NA_ROWS - 1)
                idx = h * (n_dr * n_dc) + jnp.clip(dr + NA_ROWS - 1, 0, n_dr - 1) * n_dc + d
                vals.append(jnp.where(ok, rpb_ref[idx], 0.0))
            acc = jnp.where(dc == d, jnp.where(right, vals[1], vals[0]), acc)
        o_ref[h, t] = jnp.where(col_ok, acc, NEG)
        return carry

    for h in range(H_B):
        lax.fori_loop(0, n_t, functools.partial(body, h=h), 0)


def _bias_table(rpb_l):
    n_t = 2 * (NA_ROWS + 1)
    return pl.pallas_call(
        _bias_table_kernel,
        out_shape=jax.ShapeDtypeStruct((H_B, n_t, GRID_W, LANES), F32),
        in_specs=[_smem_spec()],
        out_specs=pl.BlockSpec(memory_space=pltpu.VMEM),
        name="na_bias_table",
    )(rpb_l.reshape(-1))


def _attn_b_lat_kernel(q_ref, k_ref, v_ref, kc_ref, vc_ref, tp_ref, o_ref, *, nb, rows_total):
    i = pl.program_id(1)
    wk = NA_WIN_BLOCKS * BLOCK
    sblk = jnp.clip(i - 2, 0, nb - NA_WIN_BLOCKS)
    start = pl.multiple_of(sblk * BLOCK, BLOCK)
    shape = (BLOCK, wk)
    rq = 2 * i + (lax.broadcasted_iota(I32, shape, 0) >> 6)
    rk = 2 * sblk + (lax.broadcasted_iota(I32, shape, 1) >> 6)
    rs = jnp.clip(rq - NA_ROWS // 2, 0, rows_total - NA_ROWS)
    row_ok = (rk >= rs) & (rk < rs + NA_ROWS)
    low = _lane_low((BLOCK, LANES))
    for p in range(H_B // 2):
        cs = slice(LANES * p, LANES * p + LANES)
        q = q_ref[:, cs] * QK_SCALE
        kw = k_ref[pl.ds(start, wk), cs].astype(BF16)
        vw = v_ref[pl.ds(start, wk), cs].astype(BF16)
        kc = kc_ref[:, cs].astype(BF16)
        vc = vc_ref[:, cs].astype(BF16)
        outs = []
        for sub in range(2):
            h = 2 * p + sub
            tiles = []
            for kb in range(NA_WIN_BLOCKS):
                t = 2 * (sblk + kb - i) + NA_ROWS + 1
                tiles.append(jnp.concatenate([tp_ref[h, t], tp_ref[h, t - 1]], axis=0))
            bias = jnp.concatenate(tiles, axis=1)
            qm = _mask_head(q, sub).astype(BF16)
            s_w = jnp.where(row_ok, _dot_t(qm, kw) + bias, NEG)
            outs.append(_softmax_pv([s_w, _dot_t(qm, kc)], [vw, vc]))
        o_ref[:, cs] = jnp.where(low, outs[0], outs[1])


def _attn_b_ctx(qkv):
    R, L, _ = qkv.shape
    w = H_B * HEAD_DIM
    return pl.pallas_call(
        _attn_b_ctx_kernel,
        out_shape=jax.ShapeDtypeStruct((R, L, w), F32),
        grid=(R,),
        in_specs=[pl.BlockSpec((None, L, w), lambda r: (r, 0, COL_QB)),
                  pl.BlockSpec((None, L, w), lambda r: (r, 0, COL_KB)),
                  pl.BlockSpec((None, L, w), lambda r: (r, 0, COL_VB))],
        out_specs=pl.BlockSpec((None, L, w), lambda r: (r, 0, 0)),
        compiler_params=_cp(1), name="attn_b_ctx",
    )(qkv, qkv, qkv)


def _attn_b_lat(qkv, cache_k, cache_v, bias_tab, layer):
    R, L, _ = qkv.shape
    nb = L // BLOCK
    P = cache_k.shape[2]
    w = H_B * HEAD_DIM
    cache_spec = pl.BlockSpec((None, None, P, w), lambda r, i: (r, layer, 0, 0))
    return pl.pallas_call(
        functools.partial(_attn_b_lat_kernel, nb=nb, rows_total=L // GRID_W),
        out_shape=jax.ShapeDtypeStruct((R, L, w), F32),
        grid=(R, nb),
        in_specs=[pl.BlockSpec((None, BLOCK, w), lambda r, i: (r, i, COL_QB)),
                  pl.BlockSpec((None, L, w), lambda r, i: (r, 0, COL_KB)),
                  pl.BlockSpec((None, L, w), lambda r, i: (r, 0, COL_VB)),
                  cache_spec, cache_spec,
                  pl.BlockSpec(bias_tab.shape, lambda r, i: (0, 0, 0, 0))],
        out_specs=pl.BlockSpec((None, BLOCK, w), lambda r, i: (r, i, 0)),
        compiler_params=_cp(2), name="attn_b_lat",
    )(qkv, qkv, qkv, cache_k, cache_v, bias_tab)


T_ZF, T_ZB, T_XF, T_XB, T_GF, T_GB, T_MA, T_MB = range(8)


def _ret_kernel(*refs, nc, has_s0, out_state):
    refs = list(refs)
    dec_ref, q_ref, k_ref, v_ref, g_ref, gn_ref = refs[:6]
    pos = 6
    s0_ref = None
    if has_s0:
        s0_ref = refs[pos]; pos += 1
    o_ref = refs[pos]; pos += 1
    st_ref = None
    if out_state:
        st_ref = refs[pos]; pos += 1
    sf_ref, tab_ref = refs[pos], refs[pos + 1]

    sq = (BLOCK, LANES)
    ri = lax.broadcasted_iota(I32, sq, 0).astype(F32)
    ci = lax.broadcasted_iota(I32, sq, 1).astype(F32)
    low = _lane_low(sq)
    row_low = lax.broadcasted_iota(I32, sq, 0) < HEAD_DIM
    same_head = low == row_low
    n_pairs = H_C // 2

    def log_gamma(d, h):
        return -jnp.exp(jnp.full(sq, dec_ref[d, h], F32))

    for p in range(n_pairs):
        lf = [log_gamma(0, 2 * p), log_gamma(0, 2 * p + 1)]
        lb = [log_gamma(1, 2 * p), log_gamma(1, 2 * p + 1)]
        lf_lane, lb_lane = jnp.where(low, lf[0], lf[1]), jnp.where(low, lb[0], lb[1])
        lf_row, lb_row = jnp.where(row_low, lf[0], lf[1]), jnp.where(row_low, lb[0], lb[1])
        tab_ref[p, T_ZF] = jnp.exp(lf_lane * (BLOCK - 1.0 - ri))
        tab_ref[p, T_ZB] = jnp.exp(lb_lane * ri)
        tab_ref[p, T_XF] = jnp.exp(lf_lane * (ri + 1.0))
        tab_ref[p, T_XB] = jnp.exp(lb_lane * (BLOCK - ri))
        tab_ref[p, T_GF] = jnp.exp(lf_row * float(BLOCK))
        tab_ref[p, T_GB] = jnp.exp(lb_row * float(BLOCK))
        rel = ri - ci
        for sub in range(2):
            fwd = jnp.where(rel >= 0, jnp.exp(lf[sub] * jnp.maximum(rel, 0.0)), 0.0)
            bwd = jnp.where(rel <= 0, jnp.exp(lb[sub] * jnp.maximum(-rel, 0.0)), 0.0)
            tab_ref[p, T_MA + sub] = fwd + bwd

    def chunk(ref, c, p):
        r0 = pl.multiple_of(c * BLOCK, BLOCK)
        return ref[pl.ds(r0, BLOCK), LANES * p:LANES * p + LANES]

    def state_update(S, k, v16, p, tz, tg):
        kt = (k * tab_ref[p, tz]).T.astype(BF16)
        return tab_ref[p, tg] * S + jnp.where(same_head, _dot(kt, v16), 0.0)

    def fwd_body(c, S):
        new = []
        for p in range(n_pairs):
            sf_ref[c, p] = S[p]
            k = chunk(k_ref, c, p) * QK_SCALE
            v16 = chunk(v_ref, c, p).astype(BF16)
            new.append(state_update(S[p], k, v16, p, T_ZF, T_GF))
        return tuple(new)

    zero = jnp.zeros(sq, F32)
    s_f0 = tuple(s0_ref[0, p] if has_s0 else zero for p in range(n_pairs))
    s_b0 = tuple(s0_ref[1, p] if has_s0 else zero for p in range(n_pairs))
    s_f = lax.fori_loop(0, nc, fwd_body, s_f0)

    def bwd_body(t, S):
        c = nc - 1 - t
        new = []
        for p in range(n_pairs):
            q = chunk(q_ref, c, p)
            k = chunk(k_ref, c, p) * QK_SCALE
            v16 = chunk(v_ref, c, p).astype(BF16)
            k16 = k.astype(BF16)
            inner = []
            for sub in range(2):
                a = _dot_t(_mask_head(q, sub).astype(BF16), k16) * tab_ref[p, T_MA + sub]
                inner.append(_dot(a.astype(BF16), v16))
            q16 = q.astype(BF16)
            o = (jnp.where(low, inner[0], inner[1])
                 + _dot(q16, sf_ref[c, p].astype(BF16)) * tab_ref[p, T_XF]
                 + _dot(q16, S[p].astype(BF16)) * tab_ref[p, T_XB])
            mu = jnp.where(low, _mask_head(o, 0).sum(-1, keepdims=True),
                           _mask_head(o, 1).sum(-1, keepdims=True)) * (1.0 / HEAD_DIM)
            d = o - mu
            d2 = d * d
            var = jnp.where(low, _mask_head(d2, 0).sum(-1, keepdims=True),
                            _mask_head(d2, 1).sum(-1, keepdims=True)) * (1.0 / HEAD_DIM)
            cs = slice(LANES * p, LANES * p + LANES)
            on = d * lax.rsqrt(var + LN_EPS) * gn_ref[:, cs]
            r0 = pl.multiple_of(c * BLOCK, BLOCK)
            o_ref[pl.ds(r0, BLOCK), cs] = _silu(chunk(g_ref, c, p)) * on
            new.append(state_update(S[p], k, v16, p, T_ZB, T_GB))
        return tuple(new)

    s_b = lax.fori_loop(0, nc, bwd_body, s_b0)
    if out_state:
        for p in range(n_pairs):
            st_ref[0, p] = s_f[p]
            st_ref[1, p] = s_b[p]


def _retention(qkv, decay, gn_w, s0=None, layer=0, out_state=False):
    R, L, _ = qkv.shape
    nc = L // BLOCK
    w = H_C * HEAD_DIM
    n_pairs = H_C // 2
    col = lambda cb: pl.BlockSpec((None, L, w), lambda r: (r, 0, cb))
    in_specs = [_smem_spec(), col(COL_QC), col(COL_KC), col(COL_VC), col(COL_GC),
                pl.BlockSpec((1, w), lambda r: (0, 0))]
    args = [decay, qkv, qkv, qkv, qkv, gn_w.reshape(1, w)]
    if s0 is not None:
        in_specs.append(pl.BlockSpec((None, None, 2, n_pairs, BLOCK, LANES), lambda r: (r, layer, 0, 0, 0, 0)))
        args.append(s0)
    out_shape = [jax.ShapeDtypeStruct((R, L, w), F32)]
    out_specs = [pl.BlockSpec((None, L, w), lambda r: (r, 0, 0))]
    if out_state:
        out_shape.append(jax.ShapeDtypeStruct((R, 2, n_pairs, BLOCK, LANES), F32))
        out_specs.append(pl.BlockSpec((None, 2, n_pairs, BLOCK, LANES), lambda r: (r, 0, 0, 0, 0)))
    res = pl.pallas_call(
        functools.partial(_ret_kernel, nc=nc, has_s0=s0 is not None, out_state=out_state),
        out_shape=out_shape, grid=(R,), in_specs=in_specs, out_specs=out_specs,
        scratch_shapes=[pltpu.VMEM((nc, n_pairs, BLOCK, LANES), F32),
                        pltpu.VMEM((n_pairs, 8, BLOCK, LANES), F32)],
        compiler_params=_cp(1), name="retention",
    )(*args)
    return res if out_state else res[0]


def _pair_states_to_heads(st):
    a = st[..., :HEAD_DIM, :HEAD_DIM]
    b = st[..., HEAD_DIM:, HEAD_DIM:]
    return jnp.stack([a, b], axis=3).reshape(st.shape[0], 2, H_C, HEAD_DIM, HEAD_DIM)


def _heads_to_pair_states(s):
    s = s.reshape(*s.shape[:3], H_C // 2, 2, HEAD_DIM, HEAD_DIM)
    z = jnp.zeros_like(s[..., 0, :, :])
    top = jnp.concatenate([s[..., 0, :, :], z], axis=-1)
    bot = jnp.concatenate([z, s[..., 1, :, :]], axis=-1)
    return jnp.concatenate([top, bot], axis=-2)


def _outproj_kernel(oa_ref, ob_ref, oc_ref, x_ref, g1_ref, sc2_ref, sh2_ref, w_ref, lg_ref, lb_ref, wr_ref,
                    x1_ref, h2_ref, aff_ref):
    wa, wb = H_A * HEAD_DIM, H_B * HEAD_DIM
    mix = (_dot(oa_ref[...].astype(BF16), w_ref[0:wa, :])
           + _dot(ob_ref[...].astype(BF16), w_ref[wa:wa + wb, :])
           + _dot(oc_ref[...].astype(BF16), w_ref[wa + wb:, :]))
    x1 = _layernorm(ALPHA * x_ref[...] + g1_ref[...] * mix, lg_ref[...], lb_ref[...])
    x1_ref[...] = x1
    h2 = (x1 * (1.0 + sc2_ref[...]) + sh2_ref[...]).astype(BF16)
    h2_ref[...] = h2
    logits = _dot(h2, wr_ref[...])
    e = jnp.exp(logits - logits.max(axis=-1, keepdims=True))
    aff_ref[...] = e / e.sum(axis=-1, keepdims=True)


def _outproj(oa, ob, oc, x, mods, w_out_bf16, ln_g, ln_b, w_router_bf16, tm):
    R, L, _ = x.shape
    row = lambda w: pl.BlockSpec((None, tm, w), lambda r, i: (r, i, 0))
    vec = pl.BlockSpec((1, D_MODEL), lambda r, i: (0, 0))
    return pl.pallas_call(
        _outproj_kernel,
        out_shape=[jax.ShapeDtypeStruct((R, L, D_MODEL), F32),
                   jax.ShapeDtypeStruct((R, L, D_MODEL), BF16),
                   jax.ShapeDtypeStruct((R, L, N_EXPERTS), F32)],
        grid=(R, L // tm),
        in_specs=[row(H_A * HEAD_DIM), row(H_B * HEAD_DIM), row(H_C * HEAD_DIM), row(D_MODEL),
                  _mod_spec(2, 2), _mod_spec(4, 2), _mod_spec(3, 2),
                  pl.BlockSpec((D_MODEL, D_MODEL), lambda r, i: (0, 0)), vec, vec,
                  pl.BlockSpec((D_MODEL, N_EXPERTS), lambda r, i: (0, 0))],
        out_specs=[row(D_MODEL), row(D_MODEL), row(N_EXPERTS)],
        compiler_params=_cp(2), name="out_proj_ln_router",
    )(oa, ob, oc, x, mods, mods, mods, w_out_bf16, ln_g.reshape(1, -1), ln_b.reshape(1, -1), w_router_bf16)


CUMSUM_CHUNK = 256


def _excl_cumsum_lanes(x):
    rows, n = x.shape
    w = min(CUMSUM_CHUNK, n)
    tri = (lax.broadcasted_iota(I32, (w, w), 0) < lax.broadcasted_iota(I32, (w, w), 1))
    tri = jnp.where(tri, 1.0, 0.0).astype(BF16)
    carry = jnp.zeros((rows, 1), F32)
    outs = []
    for c in range(0, n, w):
        xc = x[:, c:c + w]
        outs.append(_dot(xc.astype(BF16), tri) + carry)
        carry = carry + xc.sum(axis=-1, keepdims=True)
    return outs[0] if len(outs) == 1 else jnp.concatenate(outs, axis=1)


def _select_kernel(a_ref, pos_ref, *, cap):
    rb, ne, n = a_ref.shape
    rows = rb * ne
    a = a_ref[...].reshape(rows, n)
    capf = float(cap)

    def body(_, c):
        lo, hi = c
        mid = lo + ((hi - lo + 1) >> 1)
        cnt = jnp.where(a >= pltpu.bitcast(mid, F32), 1.0, 0.0).sum(axis=-1, keepdims=True)
        ok = cnt >= capf
        return jnp.where(ok, mid, lo), jnp.where(ok, hi, mid - 1)

    one_bits = 0x3F800000
    lo, _ = lax.fori_loop(0, 30, body, (jnp.zeros((rows, 1), I32), jnp.full((rows, 1), one_bits, I32)))
    thr = pltpu.bitcast(lo, F32)
    gt = jnp.where(a > thr, 1.0, 0.0)
    eq = jnp.where(a == thr, 1.0, 0.0)
    need = capf - gt.sum(axis=-1, keepdims=True)
    sel = gt + eq * jnp.where(_excl_cumsum_lanes(eq) < need, 1.0, 0.0)
    slot = _excl_cumsum_lanes(sel)
    pos_ref[...] = jnp.where(sel > 0.5, slot, -1.0).astype(I32).reshape(rb, ne, n)


def _select(aff_t, cap):
    R, ne, n = aff_t.shape
    rb = min(R, max(1, 128 * 2048 // (ne * n)))
    return pl.pallas_call(
        functools.partial(_select_kernel, cap=cap),
        out_shape=jax.ShapeDtypeStruct((R, ne, n), I32),
        grid=(R // rb,),
        in_specs=[pl.BlockSpec((rb, ne, n), lambda r: (r, 0, 0))],
        out_specs=pl.BlockSpec((rb, ne, n), lambda r: (r, 0, 0)),
        compiler_params=_cp(1), name="expert_select",
    )(aff_t)


def _gather_kernel(pos_ref, aff_ref, h_ref, x_ref, gate_ref, *, cap):
    eb, _, n = pos_ref.shape
    slot = lax.broadcasted_iota(I32, (cap, n), 0)
    parts = []
    for j in range(eb):
        hit = slot == pos_ref[j]
        parts.append(jnp.where(hit, 1.0, 0.0).astype(BF16))
        gate_ref[j] = jnp.where(hit, aff_ref[j], 0.0).sum(axis=-1, keepdims=True)
    onehot = parts[0] if eb == 1 else jnp.concatenate(parts, axis=0)
    xe = _dot(onehot, h_ref[...]).astype(BF16)
    for j in range(eb):
        x_ref[j] = xe[j * cap:(j + 1) * cap]


def _gather(pos, aff_t, h2, cap):
    R, ne, n = pos.shape
    eb = max(1, min(ne, 256 // cap))
    row = pl.BlockSpec((None, eb, 1, n), lambda r, e: (r, e, 0, 0))
    return pl.pallas_call(
        functools.partial(_gather_kernel, cap=cap),
        out_shape=[jax.ShapeDtypeStruct((ne, R * cap, D_MODEL), BF16),
                   jax.ShapeDtypeStruct((ne, R * cap, 1), F32)],
        grid=(R, ne // eb),
        in_specs=[row, row, pl.BlockSpec((None, n, D_MODEL), lambda r, e: (r, 0, 0))],
        out_specs=[pl.BlockSpec((eb, cap, D_MODEL), lambda r, e: (e, r, 0)),
                   pl.BlockSpec((eb, cap, 1), lambda r, e: (e, r, 0))],
        compiler_params=_cp(2), name="expert_gather",
    )(pos.reshape(R, ne, 1, n), aff_t.reshape(R, ne, 1, n), h2)


FF_TILE = 512


def _ffn_kernel(x_ref, wa_ref, wb_ref, wd_ref, gate_ref, y_ref, acc_ref):
    f = pl.program_id(1)

    @pl.when(f == 0)
    def _():
        acc_ref[...] = jnp.zeros_like(acc_ref)

    x = x_ref[...]
    a = _dot(x, wa_ref[...].astype(BF16))
    b = _dot(x, wb_ref[...].astype(BF16))
    acc_ref[...] += _dot((_silu(a) * b).astype(BF16), wd_ref[...].astype(BF16))

    @pl.when(f == pl.num_programs(1) - 1)
    def _():
        y_ref[...] = (acc_ref[...] * gate_ref[...]).astype(BF16)


def _experts(xe, gates, w_gate_up, w_down, layer):
    ne, S, _ = xe.shape
    nf = D_FF // FF_TILE
    return pl.pallas_call(
        _ffn_kernel,
        out_shape=jax.ShapeDtypeStruct((ne, S, D_MODEL), BF16),
        grid=(ne, nf),
        in_specs=[pl.BlockSpec((None, S, D_MODEL), lambda e, f: (e, 0, 0)),
                  pl.BlockSpec((None, None, D_MODEL, FF_TILE), lambda e, f: (layer, e, 0, f)),
                  pl.BlockSpec((None, None, D_MODEL, FF_TILE), lambda e, f: (layer, e, 0, nf + f)),
                  pl.BlockSpec((None, None, FF_TILE, D_MODEL), lambda e, f: (layer, e, f, 0)),
                  pl.BlockSpec((None, S, 1), lambda e, f: (e, 0, 0))],
        out_specs=pl.BlockSpec((None, S, D_MODEL), lambda e, f: (e, 0, 0)),
        scratch_shapes=[pltpu.VMEM((S, D_MODEL), F32)],
        compiler_params=_cp(2), name="expert_ffn",
    )(xe, w_gate_up, w_gate_up, w_down, gates)


def _scatter_kernel(pos_ref, y_ref, x1_ref, g2_ref, lg_ref, lb_ref, o_ref, *, cap):
    tm, ne = pos_ref.shape
    cw = max(cap, LANES)
    epc = cw // cap
    lane = lax.broadcasted_iota(I32, (tm, cw), 1)
    pos = pos_ref[...]
    acc = jnp.zeros((tm, D_MODEL), F32)
    for c in range(ne // epc):
        onehot = jnp.zeros((tm, cw), F32)
        for jj in range(epc):
            col = pos[:, c * epc + jj:c * epc + jj + 1]
            tgt = jnp.where(col >= 0, col + jj * cap, -1)
            onehot = onehot + jnp.where(lane == tgt, 1.0, 0.0)
        yc = y_ref[c] if epc == 1 else y_ref[c * epc:(c + 1) * epc].reshape(cw, D_MODEL)
        acc = acc + _dot(onehot.astype(BF16), yc)
    o_ref[...] = _layernorm(ALPHA * x1_ref[...] + g2_ref[...] * acc, lg_ref[...], lb_ref[...])


def _scatter(pos_tok, y, x1, mods, ln_g, ln_b, cap, tm):
    R, n, ne = pos_tok.shape
    vec = pl.BlockSpec((1, D_MODEL), lambda r, i: (0, 0))
    return pl.pallas_call(
        functools.partial(_scatter_kernel, cap=cap),
        out_shape=jax.ShapeDtypeStruct((R, n, D_MODEL), F32),
        grid=(R, n // tm),
        in_specs=[pl.BlockSpec((None, tm, ne), lambda r, i: (r, i, 0)),
                  pl.BlockSpec((ne, cap, D_MODEL), lambda r, i: (0, r, 0)),
                  pl.BlockSpec((None, tm, D_MODEL), lambda r, i: (r, i, 0)),
                  _mod_spec(5, 2), vec, vec],
        out_specs=pl.BlockSpec((None, tm, D_MODEL), lambda r, i: (r, i, 0)),
        compiler_params=_cp(2), name="expert_scatter_ln",
    )(pos_tok, y, x1, mods, ln_g.reshape(1, -1), ln_b.reshape(1, -1))


def _channel_mixer(x1, h2, aff, mods, l, w_gate_up, w_down, ln2_g, ln2_b, tm):
    R, n, _ = x1.shape
    cap = CAPACITY_FACTOR * n // N_EXPERTS
    aff_t = jnp.swapaxes(aff, 1, 2)
    pos = _select(aff_t, cap)
    xe, gates = _gather(pos, aff_t, h2, cap)
    y = _experts(xe, gates, w_gate_up, w_down, l)
    return _scatter(jnp.swapaxes(pos, 1, 2), y, x1, mods, ln2_g[l], ln2_b[l], cap, tm)


def kernel(x_prompt, x_sample, cache_attn_a_k, cache_attn_a_v, cache_attn_b_k, cache_attn_b_v, state_ret, c, c_ctx,
           w_ada, b_ada, w_in, w_out, attn_sink, na_rpb, ret_decay, ret_gn, ln1_g, ln1_b, ln2_g, ln2_b,
           w_router, w_gate_up, w_down):
    n_ctx, L_ctx, _ = x_prompt.shape
    n_lat, L_lat, _ = x_sample.shape
    P = cache_attn_a_k.shape[2]

    cond = jnp.concatenate([c_ctx[None, :], c, jnp.zeros((8 - 1 - n_lat, D_MODEL), F32)], axis=0)
    mod = _modulation(cond, w_ada, b_ada)
    w_in16, w_out16, w_r16 = w_in.astype(BF16), w_out.astype(BF16), w_router.astype(BF16)
    rope_tabs = _rope_tables(L_lat)
    cak = cache_attn_a_k.reshape(n_lat, DEPTH, P, KVH_A * HEAD_DIM)
    cav = cache_attn_a_v.reshape(n_lat, DEPTH, P, KVH_A * HEAD_DIM)
    cbk = cache_attn_b_k.reshape(n_lat, DEPTH, P, H_B * HEAD_DIM)
    cbv = cache_attn_b_v.reshape(n_lat, DEPTH, P, H_B * HEAD_DIM)
    s0_pairs = _heads_to_pair_states(state_ret)

    x = x_prompt
    tm = 256
    a_k, a_v, b_k, b_v, st = [], [], [], [], []
    for l in range(DEPTH):
        mods = jnp.broadcast_to(mod[l, 0:1][:, None, :], (n_ctx, 1, 6 * D_MODEL))
        qkv = _project(x, mods, w_in16[l], tm)
        oa = _attn_a_ctx(qkv, attn_sink[l])
        ob = _attn_b_ctx(qkv)
        oc, s_l = _retention(qkv, ret_decay[l], ret_gn[l], out_state=True)
        x1, h2, aff = _outproj(oa, ob, oc, x, mods, w_out16[l], ln1_g[l], ln1_b[l], w_r16[l], tm)
        x = _channel_mixer(x1, h2, aff, mods, l, w_gate_up, w_down, ln2_g, ln2_b, tm)
        wa, wkv, wb = H_A * HEAD_DIM, KVH_A * HEAD_DIM, H_B * HEAD_DIM
        a_k.append(qkv[..., wa:wa + wkv].reshape(n_ctx, L_ctx, KVH_A, HEAD_DIM))
        a_v.append(qkv[..., wa + wkv:wa + 2 * wkv].reshape(n_ctx, L_ctx, KVH_A, HEAD_DIM))
        ob0 = wa + 2 * wkv
        b_k.append(qkv[..., ob0 + wb:ob0 + 2 * wb].reshape(n_ctx, L_ctx, H_B, HEAD_DIM))
        b_v.append(qkv[..., ob0 + 2 * wb:ob0 + 3 * wb].reshape(n_ctx, L_ctx, H_B, HEAD_DIM))
        st.append(_pair_states_to_heads(s_l))
    y_prompt = x

    x = x_sample
    tm = 512
    for l in range(DEPTH):
        mods = mod[l, 1:1 + n_lat][:, None, :]
        qkv = _project(x, mods, w_in16[l], tm, rope_tabs)
        oa = _attn_a_lat(qkv, attn_sink[l], cak, cav, l)
        ob = _attn_b_lat(qkv, cbk, cbv, _bias_table(na_rpb[l]), l)
        oc = _retention(qkv, ret_decay[l], ret_gn[l], s0=s0_pairs, layer=l)
        x1, h2, aff = _outproj(oa, ob, oc, x, mods, w_out16[l], ln1_g[l], ln1_b[l], w_r16[l], tm)
        x = _channel_mixer(x1, h2, aff, mods, l, w_gate_up, w_down, ln2_g, ln2_b, tm)
    y_sample = x

    return (y_prompt, y_sample, jnp.stack(a_k, 1), jnp.stack(a_v, 1), jnp.stack(b_k, 1), jnp.stack(b_v, 1),
            jnp.stack(st, 1))
```

```python
import functools

import numpy as np
import jax
import jax.numpy as jnp
from jax import lax
from jax.experimental import pallas as pl
from jax.experimental.pallas import tpu as pltpu

F32 = jnp.float32
BF16 = jnp.bfloat16
I32 = jnp.int32

D_MODEL = 1024
DEPTH = 2
GRID_W = 64
HEAD_DIM = 64
H_A, KVH_A, H_B, H_C = 8, 2, 4, 4
D_IN = 2560
BLOCK = 128
WINDOW = 128
NA_ROWS, NA_COLS = 8, 16
NA_WIN_BLOCKS = 5
NA_QBLOCKS = 2
ROPE_BASE = 10000.0
N_EXPERTS = 16
CAPACITY_FACTOR = 2
D_FF = 1024
LN_EPS = 1e-6
ALPHA = (2 * DEPTH) ** 0.25
NEG = -1e30
LANES = 128
QK_SCALE = HEAD_DIM ** -0.5
VMEM_LIMIT = 56 * 1024 * 1024

COL_QA, COL_KA, COL_VA = 0, 4, 5
COL_QB, COL_KB, COL_VB = 3, 4, 5
COL_QC, COL_KC, COL_VC, COL_GC = 6, 7, 8, 9


def _cp(n_axes):
    return pltpu.CompilerParams(dimension_semantics=("arbitrary",) * n_axes, vmem_limit_bytes=VMEM_LIMIT)


def _dot(a, b):
    return jnp.dot(a, b, preferred_element_type=F32)


def _dot_t(a, b):
    return lax.dot_general(a, b, (((1,), (1,)), ((), ())), preferred_element_type=F32)


def _silu(x):
    return x / (1.0 + jnp.exp(-x))


def _lane_low(shape):
    return lax.broadcasted_iota(I32, shape, len(shape) - 1) < HEAD_DIM


def _layernorm(y, g, b):
    mu = jnp.mean(y, axis=-1, keepdims=True)
    d = y - mu
    var = jnp.mean(d * d, axis=-1, keepdims=True)
    return d * lax.rsqrt(var + LN_EPS) * g + b


def _mod_kernel(c_ref, w_ref, b_ref, o_ref):
    s = _silu(c_ref[...]).astype(BF16)
    o_ref[...] = _dot(s, w_ref[...].astype(BF16)) + b_ref[...]


def _modulation(cond, w_ada, b_ada):
    tn = 1536
    return pl.pallas_call(
        _mod_kernel,
        out_shape=jax.ShapeDtypeStruct((DEPTH, 8, 6 * D_MODEL), F32),
        grid=(DEPTH, 6 * D_MODEL // tn),
        in_specs=[pl.BlockSpec((8, D_MODEL), lambda l, j: (0, 0)),
                  pl.BlockSpec((None, D_MODEL, tn), lambda l, j: (l, 0, j)),
                  pl.BlockSpec((None, 1, tn), lambda l, j: (l, 0, j))],
        out_specs=pl.BlockSpec((None, 8, tn), lambda l, j: (l, 0, j)),
        compiler_params=_cp(2), name="modulation",
    )(cond, w_ada, b_ada.reshape(DEPTH, 1, 6 * D_MODEL))


def _mod_spec(k, n_grid):
    if n_grid == 1:
        return pl.BlockSpec((None, 1, D_MODEL), lambda r: (r, 0, k))
    return pl.BlockSpec((None, 1, D_MODEL), lambda r, i: (r, 0, k))


def _rope_tables(L):
    t = np.arange(L)
    pos = (t // GRID_W, t % GRID_W)
    half = HEAD_DIM // 2
    inv = 1.0 / (ROPE_BASE ** (np.arange(0, half, 2) / half))
    cos = np.zeros((L, HEAD_DIM)); s_up = np.zeros((L, HEAD_DIM)); s_dn = np.zeros((L, HEAD_DIM))
    for part in range(2):
        ang = (pos[part][:, None] * inv[None, :]).astype(np.float32).astype(np.float64)
        q = half // 2
        base = part * half
        cos[:, base:base + q] = np.cos(ang); cos[:, base + q:base + half] = np.cos(ang)
        s_up[:, base:base + q] = -np.sin(ang)
        s_dn[:, base + q:base + half] = np.sin(ang)
    tile = lambda a: jnp.asarray(np.tile(a, (1, LANES // HEAD_DIM)), F32)
    return tile(cos), tile(s_up), tile(s_dn)


def _proj_kernel(*refs, rope):
    if rope:
        x_ref, sc_ref, sh_ref, w_ref, cos_ref, sup_ref, sdn_ref, o_ref = refs
    else:
        x_ref, sc_ref, sh_ref, w_ref, o_ref = refs
    h = (x_ref[...] * (1.0 + sc_ref[...]) + sh_ref[...]).astype(BF16)
    n_rope = (H_A + KVH_A) * HEAD_DIM
    step = 512
    for j in range(0, D_IN, step):
        acc = _dot(h, w_ref[:, j:j + step])
        if rope and j < n_rope:
            q = HEAD_DIM // 4
            for c in range(0, step, LANES):
                a = acc[:, c:c + LANES]
                if j + c < n_rope:
                    a = (a * cos_ref[...] + pltpu.roll(a, LANES - q, 1) * sup_ref[...]
                         + pltpu.roll(a, q, 1) * sdn_ref[...])
                o_ref[:, j + c:j + c + LANES] = a
        else:
            o_ref[:, j:j + step] = acc


def _project(x, mods, w_bf16, tm, rope_tabs=None):
    R, L, _ = x.shape
    rope = rope_tabs is not None
    in_specs = [pl.BlockSpec((None, tm, D_MODEL), lambda r, i: (r, i, 0)),
                _mod_spec(1, 2), _mod_spec(0, 2),
                pl.BlockSpec((D_MODEL, D_IN), lambda r, i: (0, 0))]
    args = [x, mods, mods, w_bf16]
    if rope:
        in_specs += [pl.BlockSpec((tm, LANES), lambda r, i: (i, 0))] * 3
        args += list(rope_tabs)
    return pl.pallas_call(
        functools.partial(_proj_kernel, rope=rope),
        out_shape=jax.ShapeDtypeStruct((R, L, D_IN), F32),
        grid=(R, L // tm), in_specs=in_specs,
        out_specs=pl.BlockSpec((None, tm, D_IN), lambda r, i: (r, i, 0)),
        compiler_params=_cp(2), name="in_proj",
    )(*args)


def _mask_head(x, sub):
    low = _lane_low(x.shape)
    return jnp.where(low, x, 0.0) if sub == 0 else jnp.where(low, 0.0, x)


def _key_variant(k, sub):
    return _mask_head(k, sub).astype(BF16)


def _value_variant(v, sub):
    low = _lane_low(v.shape)
    return (jnp.where(low, v, 1.0) if sub == 0 else jnp.where(low, 1.0, v)).astype(BF16)


def _attend(lhs, keys, values, biases, sub, sink=None):
    scores = []
    for k, b in zip(keys, biases):
        s = _dot_t(lhs, k)
        scores.append(s if b is None else s + b)
    tiles = [s[:, c:c + LANES] for s in scores for c in range(0, s.shape[1], LANES)]
    while len(tiles) > 1:
        tiles = [jnp.maximum(a, b) for a, b in zip(tiles[0::2], tiles[1::2])] + ([tiles[-1]] if len(tiles) % 2 else [])
    m = tiles[0].max(axis=-1, keepdims=True)
    if sink is not None:
        m = jnp.maximum(m, sink)
    o = None
    for s, v in zip(scores, values):
        pv = _dot(jnp.exp(s - m).astype(BF16), v)
        o = pv if o is None else o + pv
    if sink is not None:
        low = _lane_low(o.shape)
        extra = jnp.exp(sink - m)
        o = o + (jnp.where(low, 0.0, extra) if sub == 0 else jnp.where(low, extra, 0.0))
    return o / pltpu.roll(o, HEAD_DIM, 1)


def _fill_gqa_variants(k_src, v_src, kv_ref, vv_ref, off, n):
    step = 256
    for c in range(0, n, step):
        k = k_src[c:c + step, :]
        v = v_src[c:c + step, :]
        rows = slice(off + c, off + c + step)
        swapped = (pltpu.roll(k, HEAD_DIM, 1), pltpu.roll(v, HEAD_DIM, 1))
        for hk in range(KVH_A):
            for sub in range(2):
                src = (k, v) if hk == sub else swapped
                kv_ref[hk, sub, rows, :] = _key_variant(src[0], sub)
                vv_ref[hk, sub, rows, :] = _value_variant(src[1], sub)


def _fill_mha_variants(k_src, v_src, kv_ref, vv_ref, off, n):
    step = 256
    for c in range(0, n, step):
        rows = slice(off + c, off + c + step)
        for p in range(H_B // 2):
            cs = slice(LANES * p, LANES * p + LANES)
            k = k_src[c:c + step, cs]
            v = v_src[c:c + step, cs]
            for sub in range(2):
                kv_ref[p, sub, rows, :] = _key_variant(k, sub)
                vv_ref[p, sub, rows, :] = _value_variant(v, sub)


def _sink_col(sink_ref, heads, rows):
    return jnp.concatenate([jnp.full((rows, 1), sink_ref[h], F32) for h in heads], axis=0)


def _gqa_attend(sink_ref, q_ref, o_ref, hk, key_blocks, value_blocks, biases):
    rows = q_ref.shape[0]
    c0 = 2 * LANES * hk
    lhs = jnp.concatenate([q_ref[:, c0:c0 + LANES], q_ref[:, c0 + LANES:c0 + 2 * LANES]], axis=0)
    lhs = (lhs * QK_SCALE).astype(BF16)
    g = H_A // KVH_A
    outs = [_attend(lhs, key_blocks[sub], value_blocks[sub], biases, sub,
                    _sink_col(sink_ref, [g * hk + sub, g * hk + 2 + sub], rows)) for sub in range(2)]
    res = jnp.where(_lane_low(outs[0].shape), outs[0], outs[1])
    o_ref[:, c0:c0 + LANES] = res[:rows]
    o_ref[:, c0 + LANES:c0 + 2 * LANES] = res[rows:]


def _attn_a_ctx_kernel(sink_ref, q_ref, k_ref, v_ref, o_ref):
    k, v = k_ref[...], v_ref[...]
    swapped = (pltpu.roll(k, HEAD_DIM, 1), pltpu.roll(v, HEAD_DIM, 1))
    for hk in range(KVH_A):
        ks, vs = [], []
        for sub in range(2):
            src = (k, v) if hk == sub else swapped
            ks.append([_key_variant(src[0], sub)])
            vs.append([_value_variant(src[1], sub)])
        _gqa_attend(sink_ref, q_ref, o_ref, hk, ks, vs, [None])


def _window_mask_table(nb):
    row = np.arange(2 * BLOCK)[:, None] % BLOCK
    col = np.arange(3 * BLOCK)[None, :]
    tabs = [np.where(np.abs(col - off * BLOCK - row) <= WINDOW, 0.0, NEG) for off in range(3)]
    return jnp.asarray(np.stack(tabs), F32)


def _attn_a_lat_kernel(sink_ref, q_ref, k_ref, v_ref, kc_ref, vc_ref, mask_ref, o_ref, kv_ref, vv_ref, *, nb):
    i = pl.program_id(1)
    L, P = k_ref.shape[0], kc_ref.shape[0]

    @pl.when(i == 0)
    def _():
        _fill_gqa_variants(k_ref, v_ref, kv_ref, vv_ref, 0, L)
        _fill_gqa_variants(kc_ref, vc_ref, kv_ref, vv_ref, L, P)

    wk = 3 * BLOCK
    sblk = jnp.clip(i - 1, 0, nb - 3)
    start = pl.multiple_of(sblk * BLOCK, BLOCK)
    bias = mask_ref[i - sblk]
    for hk in range(KVH_A):
        ks = [[kv_ref[hk, sub, pl.ds(start, wk), :], kv_ref[hk, sub, L:L + P, :]] for sub in range(2)]
        vs = [[vv_ref[hk, sub, pl.ds(start, wk), :], vv_ref[hk, sub, L:L + P, :]] for sub in range(2)]
        _gqa_attend(sink_ref, q_ref, o_ref, hk, ks, vs, [bias, None])


def _smem_spec():
    return pl.BlockSpec(memory_space=pltpu.SMEM)


def _attn_a_ctx(qkv, sink):
    R, L, _ = qkv.shape
    wq, wkv = H_A * HEAD_DIM, KVH_A * HEAD_DIM
    return pl.pallas_call(
        _attn_a_ctx_kernel,
        out_shape=jax.ShapeDtypeStruct((R, L, wq), F32),
        grid=(R,),
        in_specs=[_smem_spec(),
                  pl.BlockSpec((None, L, wq), lambda r: (r, 0, COL_QA)),
                  pl.BlockSpec((None, L, wkv), lambda r: (r, 0, COL_KA)),
                  pl.BlockSpec((None, L, wkv), lambda r: (r, 0, COL_VA))],
        out_specs=pl.BlockSpec((None, L, wq), lambda r: (r, 0, 0)),
        compiler_params=_cp(1), name="attn_a_ctx",
    )(sink, qkv, qkv, qkv)


def _attn_a_lat(qkv, sink, cache_k, cache_v, layer):
    R, L, _ = qkv.shape
    nb = L // BLOCK
    assert nb >= 3
    P = cache_k.shape[2]
    wq, wkv = H_A * HEAD_DIM, KVH_A * HEAD_DIM
    cache_spec = pl.BlockSpec((None, None, P, wkv), lambda r, i: (r, layer, 0, 0))
    masks = _window_mask_table(nb)
    return pl.pallas_call(
        functools.partial(_attn_a_lat_kernel, nb=nb),
        out_shape=jax.ShapeDtypeStruct((R, L, wq), F32),
        grid=(R, nb),
        in_specs=[_smem_spec(),
                  pl.BlockSpec((None, BLOCK, wq), lambda r, i: (r, i, COL_QA)),
                  pl.BlockSpec((None, L, wkv), lambda r, i: (r, 0, COL_KA)),
                  pl.BlockSpec((None, L, wkv), lambda r, i: (r, 0, COL_VA)),
                  cache_spec, cache_spec,
                  pl.BlockSpec(masks.shape, lambda r, i: (0, 0, 0))],
        out_specs=pl.BlockSpec((None, BLOCK, wq), lambda r, i: (r, i, 0)),
        scratch_shapes=[pltpu.VMEM((KVH_A, 2, L + P, LANES), BF16),
                        pltpu.VMEM((KVH_A, 2, L + P, LANES), BF16)],
        compiler_params=_cp(2), name="attn_a_lat",
    )(sink, qkv, qkv, qkv, cache_k, cache_v, masks)


def _attn_b_ctx_kernel(q_ref, k_ref, v_ref, o_ref):
    L = q_ref.shape[0]
    low = _lane_low((L, LANES))
    for p in range(H_B // 2):
        cs = slice(LANES * p, LANES * p + LANES)
        lhs = (q_ref[:, cs] * QK_SCALE).astype(BF16)
        k, v = k_ref[:, cs], v_ref[:, cs]
        outs = [_attend(lhs, [_key_variant(k, sub)], [_value_variant(v, sub)], [None], sub) for sub in range(2)]
        o_ref[:, cs] = jnp.where(low, outs[0], outs[1])


def _bias_table_kernel(rpb_ref, o_ref):
    n_dr, n_dc = 2 * NA_ROWS - 1, 2 * NA_COLS - 1
    n_t = o_ref.shape[1]
    shape = (GRID_W, LANES)
    cq = lax.broadcasted_iota(I32, shape, 0)
    lane = lax.broadcasted_iota(I32, shape, 1)
    right = lane >= GRID_W
    ck = lane & (GRID_W - 1)
    cs = jnp.clip(cq - NA_COLS // 2, 0, GRID_W - NA_COLS)
    col_ok = (ck >= cs) & (ck < cs + NA_COLS)
    dc = ck - cq + (NA_COLS - 1)

    def body(t, carry, h):
        dr_l = t - (NA_ROWS + 1)
        acc = jnp.zeros(shape, F32)
        for d in range(n_dc):
            vals = []
            for dr in (dr_l, dr_l + 1):
                ok = (dr >= -(NA_ROWS - 1)) & (dr <= NA_ROWS - 1)
                idx = h * (n_dr * n_dc) + jnp.clip(dr + NA_ROWS - 1, 0, n_dr - 1) * n_dc + d
                vals.append(jnp.where(ok, rpb_ref[idx], 0.0))
            acc = jnp.where(dc == d, jnp.where(right, vals[1], vals[0]), acc)
        o_ref[h, t] = jnp.where(col_ok, acc, NEG)
        return carry

    for h in range(H_B):
        lax.fori_loop(0, n_t, functools.partial(body, h=h), 0)


def _bias_table(rpb_l):
    n_t = 2 * (NA_ROWS + 1)
    return pl.pallas_call(
        _bias_table_kernel,
        out_shape=jax.ShapeDtypeStruct((H_B, n_t, GRID_W, LANES), F32),
        in_specs=[_smem_spec()],
        out_specs=pl.BlockSpec(memory_space=pltpu.VMEM),
        name="na_bias_table",
    )(rpb_l.reshape(-1))


def _attn_b_lat_kernel(q_ref, k_ref, v_ref, kc_ref, vc_ref, tp_ref, o_ref, kv_ref, vv_ref, *, nb, rows_total):
    i = pl.program_id(1)
    L, P = k_ref.shape[0], kc_ref.shape[0]

    @pl.when(i == 0)
    def _():
        _fill_mha_variants(k_ref, v_ref, kv_ref, vv_ref, 0, L)
        _fill_mha_variants(kc_ref, vc_ref, kv_ref, vv_ref, L, P)

    nq = NA_QBLOCKS
    nwin = NA_WIN_BLOCKS + nq - 1
    wk = nwin * BLOCK
    rows = nq * BLOCK
    sblk = jnp.clip(nq * i - 2, 0, nb - nwin)
    start = pl.multiple_of(sblk * BLOCK, BLOCK)
    shape = (rows, wk)
    rq = 2 * nq * i + (lax.broadcasted_iota(I32, shape, 0) >> 6)
    rk = 2 * sblk + (lax.broadcasted_iota(I32, shape, 1) >> 6)
    rs = jnp.clip(rq - NA_ROWS // 2, 0, rows_total - NA_ROWS)
    row_bias = jnp.where((rk >= rs) & (rk < rs + NA_ROWS), 0.0, NEG)
    n_t = tp_ref.shape[1]
    low = _lane_low((rows, LANES))
    for p in range(H_B // 2):
        cs = slice(LANES * p, LANES * p + LANES)
        lhs = (q_ref[:, cs] * QK_SCALE).astype(BF16)
        outs = []
        for sub in range(2):
            h = 2 * p + sub
            strips = []
            for qq in range(nq):
                tiles = []
                for kb in range(nwin):
                    t = jnp.clip(2 * (sblk + kb - (nq * i + qq)) + NA_ROWS + 1, 1, n_t - 1)
                    tiles.append(jnp.concatenate([tp_ref[h, t], tp_ref[h, t - 1]], axis=0))
                strips.append(jnp.concatenate(tiles, axis=1))
            bias = jnp.concatenate(strips, axis=0) + row_bias
            ks = [kv_ref[p, sub, pl.ds(start, wk), :], kv_ref[p, sub, L:L + P, :]]
            vs = [vv_ref[p, sub, pl.ds(start, wk), :], vv_ref[p, sub, L:L + P, :]]
            outs.append(_attend(lhs, ks, vs, [bias, None], sub))
        o_ref[:, cs] = jnp.where(low, outs[0], outs[1])


def _attn_b_ctx(qkv):
    R, L, _ = qkv.shape
    w = H_B * HEAD_DIM
    return pl.pallas_call(
        _attn_b_ctx_kernel,
        out_shape=jax.ShapeDtypeStruct((R, L, w), F32),
        grid=(R,),
        in_specs=[pl.BlockSpec((None, L, w), lambda r: (r, 0, COL_QB)),
                  pl.BlockSpec((None, L, w), lambda r: (r, 0, COL_KB)),
                  pl.BlockSpec((None, L, w), lambda r: (r, 0, COL_VB))],
        out_specs=pl.BlockSpec((None, L, w), lambda r: (r, 0, 0)),
        compiler_params=_cp(1), name="attn_b_ctx",
    )(qkv, qkv, qkv)


def _attn_b_lat(qkv, cache_k, cache_v, bias_tab, layer):
    R, L, _ = qkv.shape
    nb = L // BLOCK
    P = cache_k.shape[2]
    w = H_B * HEAD_DIM
    cache_spec = pl.BlockSpec((None, None, P, w), lambda r, i: (r, layer, 0, 0))
    rows = NA_QBLOCKS * BLOCK
    assert nb % NA_QBLOCKS == 0 and nb >= NA_WIN_BLOCKS + NA_QBLOCKS - 1
    return pl.pallas_call(
        functools.partial(_attn_b_lat_kernel, nb=nb, rows_total=L // GRID_W),
        out_shape=jax.ShapeDtypeStruct((R, L, w), F32),
        grid=(R, nb // NA_QBLOCKS),
        in_specs=[pl.BlockSpec((None, rows, w), lambda r, i: (r, i, COL_QB)),
                  pl.BlockSpec((None, L, w), lambda r, i: (r, 0, COL_KB)),
                  pl.BlockSpec((None, L, w), lambda r, i: (r, 0, COL_VB)),
                  cache_spec, cache_spec,
                  pl.BlockSpec(bias_tab.shape, lambda r, i: (0, 0, 0, 0))],
        out_specs=pl.BlockSpec((None, rows, w), lambda r, i: (r, i, 0)),
        scratch_shapes=[pltpu.VMEM((H_B // 2, 2, L + P, LANES), BF16),
                        pltpu.VMEM((H_B // 2, 2, L + P, LANES), BF16)],
        compiler_params=_cp(2), name="attn_b_lat",
    )(qkv, qkv, qkv, cache_k, cache_v, bias_tab)


T_ZF, T_ZB, T_XF, T_XB, T_GF, T_GB, T_MA, T_MB = range(8)


def _ret_kernel(*refs, nc, has_s0, out_state):
    refs = list(refs)
    dec_ref, q_ref, k_ref, v_ref, g_ref, gn_ref = refs[:6]
    pos = 6
    s0_ref = None
    if has_s0:
        s0_ref = refs[pos]; pos += 1
    o_ref = refs[pos]; pos += 1
    st_ref = None
    if out_state:
        st_ref = refs[pos]; pos += 1
    sf_ref, sb_ref, tab_ref = refs[pos], refs[pos + 1], refs[pos + 2]

    sq = (BLOCK, LANES)
    low = _lane_low(sq)
    row_low = lax.broadcasted_iota(I32, sq, 0) < HEAD_DIM
    same_head = low == row_low
    n_pairs = H_C // 2

    @pl.when(pl.program_id(0) == 0)
    def _():
        ri = lax.broadcasted_iota(I32, sq, 0).astype(F32)
        ci = lax.broadcasted_iota(I32, sq, 1).astype(F32)

        def log_gamma(d, h):
            return -jnp.exp(jnp.full(sq, dec_ref[d, h], F32))

        for p in range(n_pairs):
            lf = [log_gamma(0, 2 * p), log_gamma(0, 2 * p + 1)]
            lb = [log_gamma(1, 2 * p), log_gamma(1, 2 * p + 1)]
            lf_lane, lb_lane = jnp.where(low, lf[0], lf[1]), jnp.where(low, lb[0], lb[1])
            lf_row, lb_row = jnp.where(row_low, lf[0], lf[1]), jnp.where(row_low, lb[0], lb[1])
            tab_ref[p, T_ZF] = jnp.exp(lf_lane * (BLOCK - 1.0 - ri))
            tab_ref[p, T_ZB] = jnp.exp(lb_lane * ri)
            tab_ref[p, T_XF] = jnp.exp(lf_lane * (ri + 1.0))
            tab_ref[p, T_XB] = jnp.exp(lb_lane * (BLOCK - ri))
            tab_ref[p, T_GF] = jnp.exp(lf_row * float(BLOCK))
            tab_ref[p, T_GB] = jnp.exp(lb_row * float(BLOCK))
            rel = ri - ci
            for sub in range(2):
                fwd = jnp.where(rel >= 0, jnp.exp(lf[sub] * jnp.maximum(rel, 0.0)), 0.0)
                bwd = jnp.where(rel <= 0, jnp.exp(lb[sub] * jnp.maximum(-rel, 0.0)), 0.0)
                tab_ref[p, T_MA + sub] = fwd + bwd

    def chunk(ref, c, p):
        r0 = pl.multiple_of(c * BLOCK, BLOCK)
        return ref[pl.ds(r0, BLOCK), LANES * p:LANES * p + LANES]

    unroll = min(nc, 4)

    def inc_body(c, carry):
        for p in range(n_pairs):
            kt = (chunk(k_ref, c, p) * QK_SCALE).T.astype(BF16)
            v = chunk(v_ref, c, p)
            zv = jnp.concatenate([v * tab_ref[p, T_ZF], v * tab_ref[p, T_ZB]], axis=1).astype(BF16)
            inc = _dot(kt, zv)
            sf_ref[c, p] = jnp.where(same_head, inc[:, :LANES], 0.0)
            sb_ref[c, p] = jnp.where(same_head, inc[:, LANES:], 0.0)
        return carry

    lax.fori_loop(0, nc, inc_body, 0, unroll=unroll)

    def scan_body(t, S, ref, tg, reverse):
        c = nc - 1 - t if reverse else t
        new = []
        for p in range(n_pairs):
            inc = ref[c, p]
            ref[c, p] = S[p]
            new.append(tab_ref[p, tg] * S[p] + inc)
        return tuple(new)

    zero = jnp.zeros(sq, F32)
    s_f0 = tuple(s0_ref[0, p] if has_s0 else zero for p in range(n_pairs))
    s_b0 = tuple(s0_ref[1, p] if has_s0 else zero for p in range(n_pairs))
    s_f = lax.fori_loop(0, nc, functools.partial(scan_body, ref=sf_ref, tg=T_GF, reverse=False), s_f0)
    s_b = lax.fori_loop(0, nc, functools.partial(scan_body, ref=sb_ref, tg=T_GB, reverse=True), s_b0)
    if out_state:
        for p in range(n_pairs):
            st_ref[0, p] = s_f[p]
            st_ref[1, p] = s_b[p]

    def out_body(c, carry):
        for p in range(n_pairs):
            q = chunk(q_ref, c, p)
            k16 = (chunk(k_ref, c, p) * QK_SCALE).astype(BF16)
            v = chunk(v_ref, c, p)
            a = _dot_t(jnp.concatenate([_mask_head(q, 0), _mask_head(q, 1)], axis=0).astype(BF16), k16)
            lhs = jnp.concatenate([a[:BLOCK] * tab_ref[p, T_MA], a[BLOCK:] * tab_ref[p, T_MA + 1],
                                   q * tab_ref[p, T_XF], q * tab_ref[p, T_XB]], axis=1).astype(BF16)
            rhs = jnp.concatenate([_mask_head(v, 0), _mask_head(v, 1), sf_ref[c, p], sb_ref[c, p]],
                                  axis=0).astype(BF16)
            o = _dot(lhs, rhs)
            mu = jnp.where(low, _mask_head(o, 0).sum(-1, keepdims=True),
                           _mask_head(o, 1).sum(-1, keepdims=True)) * (1.0 / HEAD_DIM)
            d = o - mu
            d2 = d * d
            var = jnp.where(low, _mask_head(d2, 0).sum(-1, keepdims=True),
                            _mask_head(d2, 1).sum(-1, keepdims=True)) * (1.0 / HEAD_DIM)
            cs = slice(LANES * p, LANES * p + LANES)
            on = d * lax.rsqrt(var + LN_EPS) * gn_ref[:, cs]
            r0 = pl.multiple_of(c * BLOCK, BLOCK)
            o_ref[pl.ds(r0, BLOCK), cs] = _silu(chunk(g_ref, c, p)) * on
        return carry

    lax.fori_loop(0, nc, out_body, 0, unroll=unroll)


def _retention(qkv, decay, gn_w, s0=None, layer=0, out_state=False):
    R, L, _ = qkv.shape
    nc = L // BLOCK
    w = H_C * HEAD_DIM
    n_pairs = H_C // 2
    col = lambda cb: pl.BlockSpec((None, L, w), lambda r: (r, 0, cb))
    in_specs = [_smem_spec(), col(COL_QC), col(COL_KC), col(COL_VC), col(COL_GC),
                pl.BlockSpec((1, w), lambda r: (0, 0))]
    args = [decay, qkv, qkv, qkv, qkv, gn_w.reshape(1, w)]
    if s0 is not None:
        in_specs.append(pl.BlockSpec((None, None, 2, n_pairs, BLOCK, LANES), lambda r: (r, layer, 0, 0, 0, 0)))
        args.append(s0)
    out_shape = [jax.ShapeDtypeStruct((R, L, w), F32)]
    out_specs = [pl.BlockSpec((None, L, w), lambda r: (r, 0, 0))]
    if out_state:
        out_shape.append(jax.ShapeDtypeStruct((R, 2, n_pairs, BLOCK, LANES), F32))
        out_specs.append(pl.BlockSpec((None, 2, n_pairs, BLOCK, LANES), lambda r: (r, 0, 0, 0, 0)))
    res = pl.pallas_call(
        functools.partial(_ret_kernel, nc=nc, has_s0=s0 is not None, out_state=out_state),
        out_shape=out_shape, grid=(R,), in_specs=in_specs, out_specs=out_specs,
        scratch_shapes=[pltpu.VMEM((nc, n_pairs, BLOCK, LANES), F32),
                        pltpu.VMEM((nc, n_pairs, BLOCK, LANES), F32),
                        pltpu.VMEM((n_pairs, 8, BLOCK, LANES), F32)],
        compiler_params=_cp(1), name="retention",
    )(*args)
    return res if out_state else res[0]


def _pair_states_to_heads(st):
    a = st[..., :HEAD_DIM, :HEAD_DIM]
    b = st[..., HEAD_DIM:, HEAD_DIM:]
    return jnp.stack([a, b], axis=3).reshape(st.shape[0], 2, H_C, HEAD_DIM, HEAD_DIM)


def _heads_to_pair_states(s):
    s = s.reshape(*s.shape[:3], H_C // 2, 2, HEAD_DIM, HEAD_DIM)
    z = jnp.zeros_like(s[..., 0, :, :])
    top = jnp.concatenate([s[..., 0, :, :], z], axis=-1)
    bot = jnp.concatenate([z, s[..., 1, :, :]], axis=-1)
    return jnp.concatenate([top, bot], axis=-2)


def _outproj_kernel(oa_ref, ob_ref, oc_ref, x_ref, g1_ref, sc2_ref, sh2_ref, w_ref, lg_ref, lb_ref, wr_ref,
                    x1_ref, h2_ref, aff_ref):
    wa, wb = H_A * HEAD_DIM, H_B * HEAD_DIM
    mix = (_dot(oa_ref[...].astype(BF16), w_ref[0:wa, :])
           + _dot(ob_ref[...].astype(BF16), w_ref[wa:wa + wb, :])
           + _dot(oc_ref[...].astype(BF16), w_ref[wa + wb:, :]))
    x1 = _layernorm(ALPHA * x_ref[...] + g1_ref[...] * mix, lg_ref[...], lb_ref[...])
    x1_ref[...] = x1
    h2 = (x1 * (1.0 + sc2_ref[...]) + sh2_ref[...]).astype(BF16)
    h2_ref[...] = h2
    logits = _dot(h2, wr_ref[...])
    e = jnp.exp(logits - logits.max(axis=-1, keepdims=True))
    aff_ref[...] = e / e.sum(axis=-1, keepdims=True)


def _outproj(oa, ob, oc, x, mods, w_out_bf16, ln_g, ln_b, w_router_bf16, tm):
    R, L, _ = x.shape
    row = lambda w: pl.BlockSpec((None, tm, w), lambda r, i: (r, i, 0))
    vec = pl.BlockSpec((1, D_MODEL), lambda r, i: (0, 0))
    return pl.pallas_call(
        _outproj_kernel,
        out_shape=[jax.ShapeDtypeStruct((R, L, D_MODEL), F32),
                   jax.ShapeDtypeStruct((R, L, D_MODEL), BF16),
                   jax.ShapeDtypeStruct((R, L, N_EXPERTS), F32)],
        grid=(R, L // tm),
        in_specs=[row(H_A * HEAD_DIM), row(H_B * HEAD_DIM), row(H_C * HEAD_DIM), row(D_MODEL),
                  _mod_spec(2, 2), _mod_spec(4, 2), _mod_spec(3, 2),
                  pl.BlockSpec((D_MODEL, D_MODEL), lambda r, i: (0, 0)), vec, vec,
                  pl.BlockSpec((D_MODEL, N_EXPERTS), lambda r, i: (0, 0))],
        out_specs=[row(D_MODEL), row(D_MODEL), row(N_EXPERTS)],
        compiler_params=_cp(2), name="out_proj_ln_router",
    )(oa, ob, oc, x, mods, mods, mods, w_out_bf16, ln_g.reshape(1, -1), ln_b.reshape(1, -1), w_router_bf16)


CUMSUM_CHUNK = 256


def _excl_cumsum_lanes(x):
    rows, n = x.shape
    w = min(CUMSUM_CHUNK, n)
    tri = (lax.broadcasted_iota(I32, (w, w), 0) < lax.broadcasted_iota(I32, (w, w), 1))
    tri = jnp.where(tri, 1.0, 0.0).astype(BF16)
    carry = jnp.zeros((rows, 1), F32)
    outs = []
    for c in range(0, n, w):
        xc = x[:, c:c + w]
        outs.append(_dot(xc.astype(BF16), tri) + carry)
        carry = carry + xc.sum(axis=-1, keepdims=True)
    return outs[0] if len(outs) == 1 else jnp.concatenate(outs, axis=1)


def _select_kernel(a_ref, pos_ref, *, cap):
    rb, ne, n = a_ref.shape
    rows = rb * ne
    a = a_ref[...].reshape(rows, n)
    capf = float(cap)

    def body(_, c):
        lo, hi = c
        mid = lo + ((hi - lo + 1) >> 1)
        cnt = jnp.where(a >= pltpu.bitcast(mid, F32), 1.0, 0.0).sum(axis=-1, keepdims=True)
        ok = cnt >= capf
        return jnp.where(ok, mid, lo), jnp.where(ok, hi, mid - 1)

    one_bits = 0x3F800000
    lo, _ = lax.fori_loop(0, 30, body, (jnp.zeros((rows, 1), I32), jnp.full((rows, 1), one_bits, I32)))
    thr = pltpu.bitcast(lo, F32)
    gt = jnp.where(a > thr, 1.0, 0.0)
    eq = jnp.where(a == thr, 1.0, 0.0)
    need = capf - gt.sum(axis=-1, keepdims=True)
    sel = gt + eq * jnp.where(_excl_cumsum_lanes(eq) < need, 1.0, 0.0)
    slot = _excl_cumsum_lanes(sel)
    pos_ref[...] = jnp.where(sel > 0.5, slot, -1.0).astype(I32).reshape(rb, ne, n)


def _select(aff_t, cap):
    R, ne, n = aff_t.shape
    rb = min(R, max(1, 128 * 2048 // (ne * n)))
    return pl.pallas_call(
        functools.partial(_select_kernel, cap=cap),
        out_shape=jax.ShapeDtypeStruct((R, ne, n), I32),
        grid=(R // rb,),
        in_specs=[pl.BlockSpec((rb, ne, n), lambda r: (r, 0, 0))],
        out_specs=pl.BlockSpec((rb, ne, n), lambda r: (r, 0, 0)),
        compiler_params=_cp(1), name="expert_select",
    )(aff_t)


def _gather_kernel(pos_ref, aff_ref, h_ref, x_ref, gate_ref, *, cap):
    eb, _, n = pos_ref.shape
    slot = lax.broadcasted_iota(I32, (cap, n), 0)
    parts = []
    for j in range(eb):
        hit = slot == pos_ref[j]
        parts.append(jnp.where(hit, 1.0, 0.0).astype(BF16))
        gate_ref[j] = jnp.where(hit, aff_ref[j], 0.0).sum(axis=-1, keepdims=True)
    onehot = parts[0] if eb == 1 else jnp.concatenate(parts, axis=0)
    xe = _dot(onehot, h_ref[...]).astype(BF16)
    for j in range(eb):
        x_ref[j] = xe[j * cap:(j + 1) * cap]


def _gather(pos, aff_t, h2, cap):
    R, ne, n = pos.shape
    eb = max(1, min(ne, 256 // cap))
    row = pl.BlockSpec((None, eb, 1, n), lambda r, e: (r, e, 0, 0))
    return pl.pallas_call(
        functools.partial(_gather_kernel, cap=cap),
        out_shape=[jax.ShapeDtypeStruct((ne, R * cap, D_MODEL), BF16),
                   jax.ShapeDtypeStruct((ne, R * cap, 1), F32)],
        grid=(R, ne // eb),
        in_specs=[row, row, pl.BlockSpec((None, n, D_MODEL), lambda r, e: (r, 0, 0))],
        out_specs=[pl.BlockSpec((eb, cap, D_MODEL), lambda r, e: (e, r, 0)),
                   pl.BlockSpec((eb, cap, 1), lambda r, e: (e, r, 0))],
        compiler_params=_cp(2), name="expert_gather",
    )(pos.reshape(R, ne, 1, n), aff_t.reshape(R, ne, 1, n), h2)


FF_TILE = 512


def _ffn_kernel(x0_ref, x1_ref, wa_ref, wb_ref, wd_ref, g0_ref, g1_ref, y0_ref, y1_ref, acc_ref):
    f = pl.program_id(1)

    @pl.when(f == 0)
    def _():
        acc_ref[...] = jnp.zeros_like(acc_ref)

    x = jnp.concatenate([x0_ref[...], x1_ref[...]], axis=0)
    a = _dot(x, wa_ref[...].astype(BF16))
    b = _dot(x, wb_ref[...].astype(BF16))
    acc_ref[...] += _dot((_silu(a) * b).astype(BF16), wd_ref[...].astype(BF16))

    @pl.when(f == pl.num_programs(1) - 1)
    def _():
        s0 = x0_ref.shape[0]
        y0_ref[...] = (acc_ref[:s0, :] * g0_ref[...]).astype(BF16)
        y1_ref[...] = (acc_ref[s0:, :] * g1_ref[...]).astype(BF16)


def _experts(xe0, gates0, xe1, gates1, w_gate_up, w_down, layer):
    ne, S0, _ = xe0.shape
    S1 = xe1.shape[1]
    nf = D_FF // FF_TILE
    rows = lambda S, w: pl.BlockSpec((None, S, w), lambda e, f: (e, 0, 0))
    return pl.pallas_call(
        _ffn_kernel,
        out_shape=[jax.ShapeDtypeStruct((ne, S0, D_MODEL), BF16), jax.ShapeDtypeStruct((ne, S1, D_MODEL), BF16)],
        grid=(ne, nf),
        in_specs=[rows(S0, D_MODEL), rows(S1, D_MODEL),
                  pl.BlockSpec((None, None, D_MODEL, FF_TILE), lambda e, f: (layer, e, 0, f)),
                  pl.BlockSpec((None, None, D_MODEL, FF_TILE), lambda e, f: (layer, e, 0, nf + f)),
                  pl.BlockSpec((None, None, FF_TILE, D_MODEL), lambda e, f: (layer, e, f, 0)),
                  rows(S0, 1), rows(S1, 1)],
        out_specs=[rows(S0, D_MODEL), rows(S1, D_MODEL)],
        scratch_shapes=[pltpu.VMEM((S0 + S1, D_MODEL), F32)],
        compiler_params=_cp(2), name="expert_ffn",
    )(xe0, xe1, w_gate_up, w_gate_up, w_down, gates0, gates1)


def _scatter_kernel(pos_ref, y_ref, x1_ref, g2_ref, lg_ref, lb_ref, o_ref, *, cap):
    tm, ne = pos_ref.shape
    cw = max(cap, LANES)
    epc = cw // cap
    lane = lax.broadcasted_iota(I32, (tm, cw), 1)
    pos = pos_ref[...]
    acc = jnp.zeros((tm, D_MODEL), F32)
    for c in range(ne // epc):
        onehot = jnp.zeros((tm, cw), F32)
        for jj in range(epc):
            col = pos[:, c * epc + jj:c * epc + jj + 1]
            tgt = jnp.where(col >= 0, col + jj * cap, -1)
            onehot = onehot + jnp.where(lane == tgt, 1.0, 0.0)
        yc = y_ref[c] if epc == 1 else y_ref[c * epc:(c + 1) * epc].reshape(cw, D_MODEL)
        acc = acc + _dot(onehot.astype(BF16), yc)
    o_ref[...] = _layernorm(ALPHA * x1_ref[...] + g2_ref[...] * acc, lg_ref[...], lb_ref[...])


def _scatter(pos_tok, y, x1, mods, ln_g, ln_b, cap, tm):
    R, n, ne = pos_tok.shape
    vec = pl.BlockSpec((1, D_MODEL), lambda r, i: (0, 0))
    return pl.pallas_call(
        functools.partial(_scatter_kernel, cap=cap),
        out_shape=jax.ShapeDtypeStruct((R, n, D_MODEL), F32),
        grid=(R, n // tm),
        in_specs=[pl.BlockSpec((None, tm, ne), lambda r, i: (r, i, 0)),
                  pl.BlockSpec((ne, cap, D_MODEL), lambda r, i: (0, r, 0)),
                  pl.BlockSpec((None, tm, D_MODEL), lambda r, i: (r, i, 0)),
                  _mod_spec(5, 2), vec, vec],
        out_specs=pl.BlockSpec((None, tm, D_MODEL), lambda r, i: (r, i, 0)),
        compiler_params=_cp(2), name="expert_scatter_ln",
    )(pos_tok, y, x1, mods, ln_g.reshape(1, -1), ln_b.reshape(1, -1))


def _capacity(n):
    return CAPACITY_FACTOR * n // N_EXPERTS


def _route(h2, aff):
    cap = _capacity(h2.shape[1])
    aff_t = jnp.swapaxes(aff, 1, 2)
    pos = _select(aff_t, cap)
    xe, gates = _gather(pos, aff_t, h2, cap)
    return pos, xe, gates


def _channel_mixers(groups, l, w_gate_up, w_down, ln2_g, ln2_b):
    routed = [_route(g["h2"], g["aff"]) for g in groups]
    ys = _experts(routed[0][1], routed[0][2], routed[1][1], routed[1][2], w_gate_up, w_down, l)
    return [_scatter(jnp.swapaxes(r[0], 1, 2), y, g["x1"], g["mods"], ln2_g[l], ln2_b[l],
                     _capacity(g["h2"].shape[1]), g["tm"]) for g, r, y in zip(groups, routed, ys)]


def kernel(x_prompt, x_sample, cache_attn_a_k, cache_attn_a_v, cache_attn_b_k, cache_attn_b_v, state_ret, c, c_ctx,
           w_ada, b_ada, w_in, w_out, attn_sink, na_rpb, ret_decay, ret_gn, ln1_g, ln1_b, ln2_g, ln2_b,
           w_router, w_gate_up, w_down):
    n_ctx, L_ctx, _ = x_prompt.shape
    n_lat, L_lat, _ = x_sample.shape
    P = cache_attn_a_k.shape[2]

    cond = jnp.concatenate([c_ctx[None, :], c, jnp.zeros((8 - 1 - n_lat, D_MODEL), F32)], axis=0)
    mod = _modulation(cond, w_ada, b_ada)
    w_in16, w_out16, w_r16 = w_in.astype(BF16), w_out.astype(BF16), w_router.astype(BF16)
    rope_tabs = _rope_tables(L_lat)
    cak = cache_attn_a_k.reshape(n_lat, DEPTH, P, KVH_A * HEAD_DIM)
    cav = cache_attn_a_v.reshape(n_lat, DEPTH, P, KVH_A * HEAD_DIM)
    cbk = cache_attn_b_k.reshape(n_lat, DEPTH, P, H_B * HEAD_DIM)
    cbv = cache_attn_b_v.reshape(n_lat, DEPTH, P, H_B * HEAD_DIM)
    s0_pairs = _heads_to_pair_states(state_ret)

    xc, xs = x_prompt, x_sample
    tm_c, tm_s = 256, 512
    a_k, a_v, b_k, b_v, st = [], [], [], [], []
    wa, wkv, wb = H_A * HEAD_DIM, KVH_A * HEAD_DIM, H_B * HEAD_DIM
    for l in range(DEPTH):
        mods_c = jnp.broadcast_to(mod[l, 0:1][:, None, :], (n_ctx, 1, 6 * D_MODEL))
        qkv = _project(xc, mods_c, w_in16[l], tm_c)
        oa = _attn_a_ctx(qkv, attn_sink[l])
        ob = _attn_b_ctx(qkv)
        oc, s_l = _retention(qkv, ret_decay[l], ret_gn[l], out_state=True)
        x1, h2, aff = _outproj(oa, ob, oc, xc, mods_c, w_out16[l], ln1_g[l], ln1_b[l], w_r16[l], tm_c)
        ctx = dict(x1=x1, h2=h2, aff=aff, mods=mods_c, tm=tm_c)
        a_k.append(qkv[..., wa:wa + wkv].reshape(n_ctx, L_ctx, KVH_A, HEAD_DIM))
        a_v.append(qkv[..., wa + wkv:wa + 2 * wkv].reshape(n_ctx, L_ctx, KVH_A, HEAD_DIM))
        ob0 = wa + 2 * wkv
        b_k.append(qkv[..., ob0 + wb:ob0 + 2 * wb].reshape(n_ctx, L_ctx, H_B, HEAD_DIM))
        b_v.append(qkv[..., ob0 + 2 * wb:ob0 + 3 * wb].reshape(n_ctx, L_ctx, H_B, HEAD_DIM))
        st.append(_pair_states_to_heads(s_l))

        mods_s = mod[l, 1:1 + n_lat][:, None, :]
        qkv = _project(xs, mods_s, w_in16[l], tm_s, rope_tabs)
        oa = _attn_a_lat(qkv, attn_sink[l], cak, cav, l)
        ob = _attn_b_lat(qkv, cbk, cbv, _bias_table(na_rpb[l]), l)
        oc = _retention(qkv, ret_decay[l], ret_gn[l], s0=s0_pairs, layer=l)
        x1, h2, aff = _outproj(oa, ob, oc, xs, mods_s, w_out16[l], ln1_g[l], ln1_b[l], w_r16[l], tm_s)
        lat = dict(x1=x1, h2=h2, aff=aff, mods=mods_s, tm=tm_s)

        xc, xs = _channel_mixers([ctx, lat], l, w_gate_up, w_down, ln2_g, ln2_b)

    return (xc, xs, jnp.stack(a_k, 1), jnp.stack(a_v, 1), jnp.stack(b_k, 1), jnp.stack(b_v, 1),
            jnp.stack(st, 1))
```

```python
import functools

import numpy as np
import jax
import jax.numpy as jnp
from jax import lax
from jax.experimental import pallas as pl
from jax.experimental.pallas import tpu as pltpu

F32 = jnp.float32
BF16 = jnp.bfloat16
I32 = jnp.int32

D_MODEL = 1024
DEPTH = 2
GRID_W = 64
HEAD_DIM = 64
H_A, KVH_A, H_B, H_C = 8, 2, 4, 4
D_IN = 2560
BLOCK = 128
WINDOW = 128
NA_ROWS, NA_COLS = 8, 16
NA_WIN_BLOCKS = 5
NA_QBLOCKS = 4
ROPE_BASE = 10000.0
N_EXPERTS = 16
CAPACITY_FACTOR = 2
D_FF = 1024
LN_EPS = 1e-6
ALPHA = (2 * DEPTH) ** 0.25
NEG = -1e30
LANES = 128
QK_SCALE = HEAD_DIM ** -0.5
VMEM_LIMIT = 56 * 1024 * 1024
ROW_TILE = 1024

COL_QA, COL_KA, COL_VA = 0, 4, 5
COL_QB, COL_KB, COL_VB = 3, 4, 5
COL_QC, COL_KC, COL_VC, COL_GC = 6, 7, 8, 9


def _cp(n_axes):
    return pltpu.CompilerParams(dimension_semantics=("arbitrary",) * n_axes, vmem_limit_bytes=VMEM_LIMIT)


def _dot(a, b):
    return jnp.dot(a, b, preferred_element_type=F32)


def _dot_t(a, b):
    return lax.dot_general(a, b, (((1,), (1,)), ((), ())), preferred_element_type=F32)


def _silu(x):
    return x / (1.0 + jnp.exp(-x))


def _lane_low(shape):
    return lax.broadcasted_iota(I32, shape, len(shape) - 1) < HEAD_DIM


def _layernorm(y, g, b):
    mu = jnp.mean(y, axis=-1, keepdims=True)
    d = y - mu
    var = jnp.mean(d * d, axis=-1, keepdims=True)
    return d * lax.rsqrt(var + LN_EPS) * g + b


def _mod_kernel(c_ref, w_ref, b_ref, o_ref):
    s = _silu(c_ref[...]).astype(BF16)
    o_ref[...] = _dot(s, w_ref[...].astype(BF16)) + b_ref[...]


def _modulation(cond, w_ada, b_ada):
    tn = 1536
    return pl.pallas_call(
        _mod_kernel,
        out_shape=jax.ShapeDtypeStruct((DEPTH, 8, 6 * D_MODEL), F32),
        grid=(DEPTH, 6 * D_MODEL // tn),
        in_specs=[pl.BlockSpec((8, D_MODEL), lambda l, j: (0, 0)),
                  pl.BlockSpec((None, D_MODEL, tn), lambda l, j: (l, 0, j)),
                  pl.BlockSpec((None, 1, tn), lambda l, j: (l, 0, j))],
        out_specs=pl.BlockSpec((None, 8, tn), lambda l, j: (l, 0, j)),
        compiler_params=_cp(2), name="modulation",
    )(cond, w_ada, b_ada.reshape(DEPTH, 1, 6 * D_MODEL))


def _mod_spec(k, n_grid):
    if n_grid == 1:
        return pl.BlockSpec((None, 1, D_MODEL), lambda r: (r, 0, k))
    return pl.BlockSpec((None, 1, D_MODEL), lambda r, i: (r, 0, k))


def _rope_tables(L):
    t = np.arange(L)
    pos = (t // GRID_W, t % GRID_W)
    half = HEAD_DIM // 2
    inv = 1.0 / (ROPE_BASE ** (np.arange(0, half, 2) / half))
    cos = np.zeros((L, HEAD_DIM)); s_up = np.zeros((L, HEAD_DIM)); s_dn = np.zeros((L, HEAD_DIM))
    for part in range(2):
        ang = (pos[part][:, None] * inv[None, :]).astype(np.float32).astype(np.float64)
        q = half // 2
        base = part * half
        cos[:, base:base + q] = np.cos(ang); cos[:, base + q:base + half] = np.cos(ang)
        s_up[:, base:base + q] = -np.sin(ang)
        s_dn[:, base + q:base + half] = np.sin(ang)
    tile = lambda a: jnp.asarray(np.tile(a, (1, LANES // HEAD_DIM)), F32)
    return tile(cos), tile(s_up), tile(s_dn)


def _proj_kernel(*refs, rope):
    if rope:
        x_ref, sc_ref, sh_ref, w_ref, cos_ref, sup_ref, sdn_ref, o_ref = refs
    else:
        x_ref, sc_ref, sh_ref, w_ref, o_ref = refs
    h = (x_ref[...] * (1.0 + sc_ref[...]) + sh_ref[...]).astype(BF16)
    n_rope = (H_A + KVH_A) * HEAD_DIM
    step = 512
    for j in range(0, D_IN, step):
        acc = _dot(h, w_ref[:, j:j + step])
        if rope and j < n_rope:
            q = HEAD_DIM // 4
            for c in range(0, step, LANES):
                a = acc[:, c:c + LANES]
                if j + c < n_rope:
                    a = (a * cos_ref[...] + pltpu.roll(a, LANES - q, 1) * sup_ref[...]
                         + pltpu.roll(a, q, 1) * sdn_ref[...])
                o_ref[:, j + c:j + c + LANES] = a
        else:
            o_ref[:, j:j + step] = acc


def _project(x, mods, w_bf16, tm, rope_tabs=None):
    R, L, _ = x.shape
    rope = rope_tabs is not None
    in_specs = [pl.BlockSpec((None, tm, D_MODEL), lambda r, i: (r, i, 0)),
                _mod_spec(1, 2), _mod_spec(0, 2),
                pl.BlockSpec((D_MODEL, D_IN), lambda r, i: (0, 0))]
    args = [x, mods, mods, w_bf16]
    if rope:
        in_specs += [pl.BlockSpec((tm, LANES), lambda r, i: (i, 0))] * 3
        args += list(rope_tabs)
    return pl.pallas_call(
        functools.partial(_proj_kernel, rope=rope),
        out_shape=jax.ShapeDtypeStruct((R, L, D_IN), F32),
        grid=(R, L // tm), in_specs=in_specs,
        out_specs=pl.BlockSpec((None, tm, D_IN), lambda r, i: (r, i, 0)),
        compiler_params=_cp(2), name="in_proj",
    )(*args)


def _mask_head(x, sub):
    low = _lane_low(x.shape)
    return jnp.where(low, x, 0.0) if sub == 0 else jnp.where(low, 0.0, x)


def _key_variant(k, sub):
    return _mask_head(k, sub).astype(BF16)


def _value_variant(v, sub):
    low = _lane_low(v.shape)
    return (jnp.where(low, v, 1.0) if sub == 0 else jnp.where(low, 1.0, v)).astype(BF16)


def _attend(lhs, keys, values, biases, sub, sink=None):
    scores = []
    for k, b in zip(keys, biases):
        s = _dot_t(lhs, k)
        scores.append(s if b is None else s + b)
    tiles = [s[:, c:c + LANES] for s in scores for c in range(0, s.shape[1], LANES)]
    while len(tiles) > 1:
        tiles = [jnp.maximum(a, b) for a, b in zip(tiles[0::2], tiles[1::2])] + ([tiles[-1]] if len(tiles) % 2 else [])
    m = tiles[0].max(axis=-1, keepdims=True)
    if sink is not None:
        m = jnp.maximum(m, sink)
    o = None
    for s, v in zip(scores, values):
        pv = _dot(jnp.exp(s - m).astype(BF16), v)
        o = pv if o is None else o + pv
    if sink is not None:
        low = _lane_low(o.shape)
        extra = jnp.exp(sink - m)
        o = o + (jnp.where(low, 0.0, extra) if sub == 0 else jnp.where(low, extra, 0.0))
    return o / pltpu.roll(o, HEAD_DIM, 1)


def _gqa_variants(k, v, hk):
    low = _lane_low(k.shape)
    k_sw, v_sw = pltpu.roll(k, HEAD_DIM, 1), pltpu.roll(v, HEAD_DIM, 1)
    k_dup = jnp.where(low, k, k_sw) if hk == 0 else jnp.where(low, k_sw, k)
    return k_dup.astype(BF16), jnp.where(low, v if hk == 0 else v_sw, 1.0).astype(BF16)


def _fill_gqa_variants(k_src, v_src, kv_ref, vv_ref, off, n):
    step = 256
    for c in range(0, n, step):
        k = k_src[c:c + step, :]
        v = v_src[c:c + step, :]
        for hk in range(KVH_A):
            kd, va = _gqa_variants(k, v, hk)
            kv_ref[hk, off + c:off + c + step, :] = kd
            vv_ref[hk, off + c:off + c + step, :] = va


def _fill_mha_variants(k_src, v_src, kv_ref, vv_ref, off, n):
    step = 256
    for c in range(0, n, step):
        rows = slice(off + c, off + c + step)
        for p in range(H_B // 2):
            cs = slice(LANES * p, LANES * p + LANES)
            k = k_src[c:c + step, cs]
            v = v_src[c:c + step, cs]
            for sub in range(2):
                kv_ref[p, sub, rows, :] = _key_variant(k, sub)
                vv_ref[p, sub, rows, :] = _value_variant(v, sub)


def _sink_col(sink_ref, heads, rows):
    return jnp.concatenate([jnp.full((rows, 1), sink_ref[h], F32) for h in heads], axis=0)


def _gqa_attend(sink_ref, q_ref, o_ref, hk, keys, values, biases):
    rows = q_ref.shape[0]
    g = H_A // KVH_A
    heads = list(range(g * hk, g * hk + g))
    lhs = jnp.concatenate([_mask_head(q_ref[:, LANES * (h // 2):LANES * (h // 2) + LANES], h % 2) for h in heads],
                          axis=0)
    lhs = (lhs * QK_SCALE).astype(BF16)
    stacked = [None if b is None else jnp.concatenate([b] * g, axis=0) for b in biases]
    out = _attend(lhs, keys, values, stacked, 0, _sink_col(sink_ref, heads, rows))
    low = _lane_low((rows, LANES))
    for j in range(0, g, 2):
        h = heads[j]
        odd = pltpu.roll(out[(j + 1) * rows:(j + 2) * rows], HEAD_DIM, 1)
        o_ref[:, LANES * (h // 2):LANES * (h // 2) + LANES] = jnp.where(low, out[j * rows:(j + 1) * rows], odd)


def _attn_a_ctx_kernel(sink_ref, q_ref, k_ref, v_ref, o_ref):
    for hk in range(KVH_A):
        kd, va = _gqa_variants(k_ref[...], v_ref[...], hk)
        _gqa_attend(sink_ref, q_ref, o_ref, hk, [kd], [va], [None])


def _window_mask_table(nb):
    row = np.arange(BLOCK)[:, None]
    col = np.arange(3 * BLOCK)[None, :]
    tabs = [np.where(np.abs(col - off * BLOCK - row) <= WINDOW, 0.0, NEG) for off in range(3)]
    return jnp.asarray(np.stack(tabs), F32)


def _attn_a_lat_kernel(sink_ref, q_ref, k_ref, v_ref, kc_ref, vc_ref, mask_ref, o_ref, kv_ref, vv_ref, *, nb):
    i = pl.program_id(1)
    L, P = k_ref.shape[0], kc_ref.shape[0]

    @pl.when(i == 0)
    def _():
        _fill_gqa_variants(k_ref, v_ref, kv_ref, vv_ref, 0, L)
        _fill_gqa_variants(kc_ref, vc_ref, kv_ref, vv_ref, L, P)

    wk = 3 * BLOCK
    sblk = jnp.clip(i - 1, 0, nb - 3)
    start = pl.multiple_of(sblk * BLOCK, BLOCK)
    bias = mask_ref[i - sblk]
    for hk in range(KVH_A):
        ks = [kv_ref[hk, pl.ds(start, wk), :], kv_ref[hk, L:L + P, :]]
        vs = [vv_ref[hk, pl.ds(start, wk), :], vv_ref[hk, L:L + P, :]]
        _gqa_attend(sink_ref, q_ref, o_ref, hk, ks, vs, [bias, None])


def _smem_spec():
    return pl.BlockSpec(memory_space=pltpu.SMEM)


def _attn_a_ctx(qkv, sink):
    R, L, _ = qkv.shape
    wq, wkv = H_A * HEAD_DIM, KVH_A * HEAD_DIM
    return pl.pallas_call(
        _attn_a_ctx_kernel,
        out_shape=jax.ShapeDtypeStruct((R, L, wq), F32),
        grid=(R,),
        in_specs=[_smem_spec(),
                  pl.BlockSpec((None, L, wq), lambda r: (r, 0, COL_QA)),
                  pl.BlockSpec((None, L, wkv), lambda r: (r, 0, COL_KA)),
                  pl.BlockSpec((None, L, wkv), lambda r: (r, 0, COL_VA))],
        out_specs=pl.BlockSpec((None, L, wq), lambda r: (r, 0, 0)),
        compiler_params=_cp(1), name="attn_a_ctx",
    )(sink, qkv, qkv, qkv)


def _attn_a_lat(qkv, sink, cache_k, cache_v, layer):
    R, L, _ = qkv.shape
    nb = L // BLOCK
    assert nb >= 3
    P = cache_k.shape[2]
    wq, wkv = H_A * HEAD_DIM, KVH_A * HEAD_DIM
    cache_spec = pl.BlockSpec((None, None, P, wkv), lambda r, i: (r, layer, 0, 0))
    masks = _window_mask_table(nb)
    return pl.pallas_call(
        functools.partial(_attn_a_lat_kernel, nb=nb),
        out_shape=jax.ShapeDtypeStruct((R, L, wq), F32),
        grid=(R, nb),
        in_specs=[_smem_spec(),
                  pl.BlockSpec((None, BLOCK, wq), lambda r, i: (r, i, COL_QA)),
                  pl.BlockSpec((None, L, wkv), lambda r, i: (r, 0, COL_KA)),
                  pl.BlockSpec((None, L, wkv), lambda r, i: (r, 0, COL_VA)),
                  cache_spec, cache_spec,
                  pl.BlockSpec(masks.shape, lambda r, i: (0, 0, 0))],
        out_specs=pl.BlockSpec((None, BLOCK, wq), lambda r, i: (r, i, 0)),
        scratch_shapes=[pltpu.VMEM((KVH_A, L + P, LANES), BF16),
                        pltpu.VMEM((KVH_A, L + P, LANES), BF16)],
        compiler_params=_cp(2), name="attn_a_lat",
    )(sink, qkv, qkv, qkv, cache_k, cache_v, masks)


def _attn_b_ctx_kernel(q_ref, k_ref, v_ref, o_ref):
    L = q_ref.shape[0]
    low = _lane_low((L, LANES))
    for p in range(H_B // 2):
        cs = slice(LANES * p, LANES * p + LANES)
        lhs = (q_ref[:, cs] * QK_SCALE).astype(BF16)
        k, v = k_ref[:, cs], v_ref[:, cs]
        outs = [_attend(lhs, [_key_variant(k, sub)], [_value_variant(v, sub)], [None], sub) for sub in range(2)]
        o_ref[:, cs] = jnp.where(low, outs[0], outs[1])


def _bias_table_kernel(rpb_ref, o_ref):
    n_dr, n_dc = 2 * NA_ROWS - 1, 2 * NA_COLS - 1
    n_t = o_ref.shape[1]
    shape = (GRID_W, LANES)
    cq = lax.broadcasted_iota(I32, shape, 0)
    lane = lax.broadcasted_iota(I32, shape, 1)
    right = lane >= GRID_W
    ck = lane & (GRID_W - 1)
    cs = jnp.clip(cq - NA_COLS // 2, 0, GRID_W - NA_COLS)
    col_ok = (ck >= cs) & (ck < cs + NA_COLS)
    dc = ck - cq + (NA_COLS - 1)

    def body(t, carry, h):
        dr_l = t - (NA_ROWS + 1)
        acc = jnp.zeros(shape, F32)
        for d in range(n_dc):
            vals = []
            for dr in (dr_l, dr_l + 1):
                ok = (dr >= -(NA_ROWS - 1)) & (dr <= NA_ROWS - 1)
                idx = h * (n_dr * n_dc) + jnp.clip(dr + NA_ROWS - 1, 0, n_dr - 1) * n_dc + d
                vals.append(jnp.where(ok, rpb_ref[idx], 0.0))
            acc = jnp.where(dc == d, jnp.where(right, vals[1], vals[0]), acc)
        o_ref[h, t] = jnp.where(col_ok, acc, NEG)
        return carry

    for h in range(H_B):
        lax.fori_loop(0, n_t, functools.partial(body, h=h), 0)


def _bias_table(rpb_l):
    n_t = 2 * (NA_ROWS + 1)
    return pl.pallas_call(
        _bias_table_kernel,
        out_shape=jax.ShapeDtypeStruct((H_B, n_t, GRID_W, LANES), F32),
        in_specs=[_smem_spec()],
        out_specs=pl.BlockSpec(memory_space=pltpu.VMEM),
        name="na_bias_table",
    )(rpb_l.reshape(-1))


def _attn_b_lat_kernel(q_ref, k_ref, v_ref, kc_ref, vc_ref, tp_ref, o_ref, kv_ref, vv_ref, *, nb, rows_total):
    i = pl.program_id(1)
    L, P = k_ref.shape[0], kc_ref.shape[0]

    @pl.when(i == 0)
    def _():
        _fill_mha_variants(k_ref, v_ref, kv_ref, vv_ref, 0, L)
        _fill_mha_variants(kc_ref, vc_ref, kv_ref, vv_ref, L, P)

    nq = NA_QBLOCKS
    nwin = NA_WIN_BLOCKS + nq - 1
    wk = nwin * BLOCK
    rows = nq * BLOCK
    sblk = jnp.clip(nq * i - 2, 0, nb - nwin)
    start = pl.multiple_of(sblk * BLOCK, BLOCK)
    shape = (rows, wk)
    rq = 2 * nq * i + (lax.broadcasted_iota(I32, shape, 0) >> 6)
    rk = 2 * sblk + (lax.broadcasted_iota(I32, shape, 1) >> 6)
    rs = jnp.clip(rq - NA_ROWS // 2, 0, rows_total - NA_ROWS)
    row_bias = jnp.where((rk >= rs) & (rk < rs + NA_ROWS), 0.0, NEG)
    n_t = tp_ref.shape[1]
    low = _lane_low((rows, LANES))
    for p in range(H_B // 2):
        cs = slice(LANES * p, LANES * p + LANES)
        lhs = (q_ref[:, cs] * QK_SCALE).astype(BF16)
        outs = []
        for sub in range(2):
            h = 2 * p + sub
            strips = []
            for qq in range(nq):
                tiles = []
                for kb in range(nwin):
                    t = jnp.clip(2 * (sblk + kb - (nq * i + qq)) + NA_ROWS + 1, 1, n_t - 1)
                    tiles.append(jnp.concatenate([tp_ref[h, t], tp_ref[h, t - 1]], axis=0))
                strips.append(jnp.concatenate(tiles, axis=1))
            bias = jnp.concatenate(strips, axis=0) + row_bias
            ks = [kv_ref[p, sub, pl.ds(start, wk), :], kv_ref[p, sub, L:L + P, :]]
            vs = [vv_ref[p, sub, pl.ds(start, wk), :], vv_ref[p, sub, L:L + P, :]]
            outs.append(_attend(lhs, ks, vs, [bias, None], sub))
        o_ref[:, cs] = jnp.where(low, outs[0], outs[1])


def _attn_b_ctx(qkv):
    R, L, _ = qkv.shape
    w = H_B * HEAD_DIM
    return pl.pallas_call(
        _attn_b_ctx_kernel,
        out_shape=jax.ShapeDtypeStruct((R, L, w), F32),
        grid=(R,),
        in_specs=[pl.BlockSpec((None, L, w), lambda r: (r, 0, COL_QB)),
                  pl.BlockSpec((None, L, w), lambda r: (r, 0, COL_KB)),
                  pl.BlockSpec((None, L, w), lambda r: (r, 0, COL_VB))],
        out_specs=pl.BlockSpec((None, L, w), lambda r: (r, 0, 0)),
        compiler_params=_cp(1), name="attn_b_ctx",
    )(qkv, qkv, qkv)


def _attn_b_lat(qkv, cache_k, cache_v, bias_tab, layer):
    R, L, _ = qkv.shape
    nb = L // BLOCK
    P = cache_k.shape[2]
    w = H_B * HEAD_DIM
    cache_spec = pl.BlockSpec((None, None, P, w), lambda r, i: (r, layer, 0, 0))
    rows = NA_QBLOCKS * BLOCK
    assert nb % NA_QBLOCKS == 0 and nb >= NA_WIN_BLOCKS + NA_QBLOCKS - 1
    return pl.pallas_call(
        functools.partial(_attn_b_lat_kernel, nb=nb, rows_total=L // GRID_W),
        out_shape=jax.ShapeDtypeStruct((R, L, w), F32),
        grid=(R, nb // NA_QBLOCKS),
        in_specs=[pl.BlockSpec((None, rows, w), lambda r, i: (r, i, COL_QB)),
                  pl.BlockSpec((None, L, w), lambda r, i: (r, 0, COL_KB)),
                  pl.BlockSpec((None, L, w), lambda r, i: (r, 0, COL_VB)),
                  cache_spec, cache_spec,
                  pl.BlockSpec(bias_tab.shape, lambda r, i: (0, 0, 0, 0))],
        out_specs=pl.BlockSpec((None, rows, w), lambda r, i: (r, i, 0)),
        scratch_shapes=[pltpu.VMEM((H_B // 2, 2, L + P, LANES), BF16),
                        pltpu.VMEM((H_B // 2, 2, L + P, LANES), BF16)],
        compiler_params=_cp(2), name="attn_b_lat",
    )(qkv, qkv, qkv, cache_k, cache_v, bias_tab)


T_ZF, T_ZB, T_XF, T_XB, T_GF, T_GB, T_MA, T_MB = range(8)


def _ret_kernel(*refs, nc, has_s0, out_state):
    refs = list(refs)
    dec_ref, q_ref, k_ref, v_ref, g_ref, gn_ref = refs[:6]
    pos = 6
    s0_ref = None
    if has_s0:
        s0_ref = refs[pos]; pos += 1
    o_ref = refs[pos]; pos += 1
    st_ref = None
    if out_state:
        st_ref = refs[pos]; pos += 1
    sf_ref, sb_ref, tab_ref = refs[pos], refs[pos + 1], refs[pos + 2]

    sq = (BLOCK, LANES)
    low = _lane_low(sq)
    row_low = lax.broadcasted_iota(I32, sq, 0) < HEAD_DIM
    same_head = low == row_low
    n_pairs = H_C // 2

    @pl.when(pl.program_id(0) == 0)
    def _():
        ri = lax.broadcasted_iota(I32, sq, 0).astype(F32)
        ci = lax.broadcasted_iota(I32, sq, 1).astype(F32)

        def log_gamma(d, h):
            return -jnp.exp(jnp.full(sq, dec_ref[d, h], F32))

        for p in range(n_pairs):
            lf = [log_gamma(0, 2 * p), log_gamma(0, 2 * p + 1)]
            lb = [log_gamma(1, 2 * p), log_gamma(1, 2 * p + 1)]
            lf_lane, lb_lane = jnp.where(low, lf[0], lf[1]), jnp.where(low, lb[0], lb[1])
            lf_row, lb_row = jnp.where(row_low, lf[0], lf[1]), jnp.where(row_low, lb[0], lb[1])
            tab_ref[p, T_ZF] = jnp.exp(lf_lane * (BLOCK - 1.0 - ri))
            tab_ref[p, T_ZB] = jnp.exp(lb_lane * ri)
            tab_ref[p, T_XF] = jnp.exp(lf_lane * (ri + 1.0))
            tab_ref[p, T_XB] = jnp.exp(lb_lane * (BLOCK - ri))
            tab_ref[p, T_GF] = jnp.exp(lf_row * float(BLOCK))
            tab_ref[p, T_GB] = jnp.exp(lb_row * float(BLOCK))
            rel = ri - ci
            for sub in range(2):
                fwd = jnp.where(rel >= 0, jnp.exp(lf[sub] * jnp.maximum(rel, 0.0)), 0.0)
                bwd = jnp.where(rel <= 0, jnp.exp(lb[sub] * jnp.maximum(-rel, 0.0)), 0.0)
                tab_ref[p, T_MA + sub] = fwd + bwd

    def chunk(ref, c, p):
        r0 = pl.multiple_of(c * BLOCK, BLOCK)
        return ref[pl.ds(r0, BLOCK), LANES * p:LANES * p + LANES]

    unroll = min(nc, 4)

    def inc_body(c, carry):
        for p in range(n_pairs):
            kt = (chunk(k_ref, c, p) * QK_SCALE).T.astype(BF16)
            v = chunk(v_ref, c, p)
            zv = jnp.concatenate([v * tab_ref[p, T_ZF], v * tab_ref[p, T_ZB]], axis=1).astype(BF16)
            inc = _dot(kt, zv)
            sf_ref[c, p] = jnp.where(same_head, inc[:, :LANES], 0.0)
            sb_ref[c, p] = jnp.where(same_head, inc[:, LANES:], 0.0)
        return carry

    lax.fori_loop(0, nc, inc_body, 0, unroll=unroll)

    def scan_body(t, S, ref, tg, reverse):
        c = nc - 1 - t if reverse else t
        new = []
        for p in range(n_pairs):
            inc = ref[c, p]
            ref[c, p] = S[p]
            new.append(tab_ref[p, tg] * S[p] + inc)
        return tuple(new)

    zero = jnp.zeros(sq, F32)
    s_f0 = tuple(s0_ref[0, p] if has_s0 else zero for p in range(n_pairs))
    s_b0 = tuple(s0_ref[1, p] if has_s0 else zero for p in range(n_pairs))
    s_f = lax.fori_loop(0, nc, functools.partial(scan_body, ref=sf_ref, tg=T_GF, reverse=False), s_f0)
    s_b = lax.fori_loop(0, nc, functools.partial(scan_body, ref=sb_ref, tg=T_GB, reverse=True), s_b0)
    if out_state:
        for p in range(n_pairs):
            st_ref[0, p] = s_f[p]
            st_ref[1, p] = s_b[p]

    def out_body(c, carry):
        for p in range(n_pairs):
            q = chunk(q_ref, c, p)
            k16 = (chunk(k_ref, c, p) * QK_SCALE).astype(BF16)
            v = chunk(v_ref, c, p)
            a = _dot_t(jnp.concatenate([_mask_head(q, 0), _mask_head(q, 1)], axis=0).astype(BF16), k16)
            lhs = jnp.concatenate([a[:BLOCK] * tab_ref[p, T_MA], a[BLOCK:] * tab_ref[p, T_MA + 1],
                                   q * tab_ref[p, T_XF], q * tab_ref[p, T_XB]], axis=1).astype(BF16)
            rhs = jnp.concatenate([_mask_head(v, 0), _mask_head(v, 1), sf_ref[c, p], sb_ref[c, p]],
                                  axis=0).astype(BF16)
            o = _dot(lhs, rhs)
            mu = jnp.where(low, _mask_head(o, 0).sum(-1, keepdims=True),
                           _mask_head(o, 1).sum(-1, keepdims=True)) * (1.0 / HEAD_DIM)
            d = o - mu
            d2 = d * d
            var = jnp.where(low, _mask_head(d2, 0).sum(-1, keepdims=True),
                            _mask_head(d2, 1).sum(-1, keepdims=True)) * (1.0 / HEAD_DIM)
            cs = slice(LANES * p, LANES * p + LANES)
            on = d * lax.rsqrt(var + LN_EPS) * gn_ref[:, cs]
            r0 = pl.multiple_of(c * BLOCK, BLOCK)
            o_ref[pl.ds(r0, BLOCK), cs] = _silu(chunk(g_ref, c, p)) * on
        return carry

    lax.fori_loop(0, nc, out_body, 0, unroll=unroll)


def _retention(qkv, decay, gn_w, s0=None, layer=0, out_state=False):
    R, L, _ = qkv.shape
    nc = L // BLOCK
    w = H_C * HEAD_DIM
    n_pairs = H_C // 2
    col = lambda cb: pl.BlockSpec((None, L, w), lambda r: (r, 0, cb))
    in_specs = [_smem_spec(), col(COL_QC), col(COL_KC), col(COL_VC), col(COL_GC),
                pl.BlockSpec((1, w), lambda r: (0, 0))]
    args = [decay, qkv, qkv, qkv, qkv, gn_w.reshape(1, w)]
    if s0 is not None:
        in_specs.append(pl.BlockSpec((None, None, 2, n_pairs, BLOCK, LANES), lambda r: (r, layer, 0, 0, 0, 0)))
        args.append(s0)
    out_shape = [jax.ShapeDtypeStruct((R, L, w), F32)]
    out_specs = [pl.BlockSpec((None, L, w), lambda r: (r, 0, 0))]
    if out_state:
        out_shape.append(jax.ShapeDtypeStruct((R, 2, n_pairs, BLOCK, LANES), F32))
        out_specs.append(pl.BlockSpec((None, 2, n_pairs, BLOCK, LANES), lambda r: (r, 0, 0, 0, 0)))
    res = pl.pallas_call(
        functools.partial(_ret_kernel, nc=nc, has_s0=s0 is not None, out_state=out_state),
        out_shape=out_shape, grid=(R,), in_specs=in_specs, out_specs=out_specs,
        scratch_shapes=[pltpu.VMEM((nc, n_pairs, BLOCK, LANES), F32),
                        pltpu.VMEM((nc, n_pairs, BLOCK, LANES), F32),
                        pltpu.VMEM((n_pairs, 8, BLOCK, LANES), F32)],
        compiler_params=_cp(1), name="retention",
    )(*args)
    return res if out_state else res[0]


def _pair_states_to_heads(st):
    a = st[..., :HEAD_DIM, :HEAD_DIM]
    b = st[..., HEAD_DIM:, HEAD_DIM:]
    return jnp.stack([a, b], axis=3).reshape(st.shape[0], 2, H_C, HEAD_DIM, HEAD_DIM)


def _heads_to_pair_states(s):
    s = s.reshape(*s.shape[:3], H_C // 2, 2, HEAD_DIM, HEAD_DIM)
    z = jnp.zeros_like(s[..., 0, :, :])
    top = jnp.concatenate([s[..., 0, :, :], z], axis=-1)
    bot = jnp.concatenate([z, s[..., 1, :, :]], axis=-1)
    return jnp.concatenate([top, bot], axis=-2)


def _outproj_kernel(oa_ref, ob_ref, oc_ref, x_ref, g1_ref, sc2_ref, sh2_ref, w_ref, lg_ref, lb_ref, wr_ref,
                    x1_ref, h2_ref, aff_ref):
    wa, wb = H_A * HEAD_DIM, H_B * HEAD_DIM
    mix = (_dot(oa_ref[...].astype(BF16), w_ref[0:wa, :])
           + _dot(ob_ref[...].astype(BF16), w_ref[wa:wa + wb, :])
           + _dot(oc_ref[...].astype(BF16), w_ref[wa + wb:, :]))
    x1 = _layernorm(ALPHA * x_ref[...] + g1_ref[...] * mix, lg_ref[...], lb_ref[...])
    x1_ref[...] = x1
    h2 = (x1 * (1.0 + sc2_ref[...]) + sh2_ref[...]).astype(BF16)
    h2_ref[...] = h2
    logits = _dot(h2, wr_ref[...])
    e = jnp.exp(logits - logits.max(axis=-1, keepdims=True))
    aff_ref[...] = e / e.sum(axis=-1, keepdims=True)


def _outproj(oa, ob, oc, x, mods, w_out_bf16, ln_g, ln_b, w_router_bf16, tm):
    R, L, _ = x.shape
    row = lambda w: pl.BlockSpec((None, tm, w), lambda r, i: (r, i, 0))
    vec = pl.BlockSpec((1, D_MODEL), lambda r, i: (0, 0))
    return pl.pallas_call(
        _outproj_kernel,
        out_shape=[jax.ShapeDtypeStruct((R, L, D_MODEL), F32),
                   jax.ShapeDtypeStruct((R, L, D_MODEL), BF16),
                   jax.ShapeDtypeStruct((R, L, N_EXPERTS), F32)],
        grid=(R, L // tm),
        in_specs=[row(H_A * HEAD_DIM), row(H_B * HEAD_DIM), row(H_C * HEAD_DIM), row(D_MODEL),
                  _mod_spec(2, 2), _mod_spec(4, 2), _mod_spec(3, 2),
                  pl.BlockSpec((D_MODEL, D_MODEL), lambda r, i: (0, 0)), vec, vec,
                  pl.BlockSpec((D_MODEL, N_EXPERTS), lambda r, i: (0, 0))],
        out_specs=[row(D_MODEL), row(D_MODEL), row(N_EXPERTS)],
        compiler_params=_cp(2), name="out_proj_ln_router",
    )(oa, ob, oc, x, mods, mods, mods, w_out_bf16, ln_g.reshape(1, -1), ln_b.reshape(1, -1), w_router_bf16)


CUMSUM_CHUNK = 256


def _excl_cumsum_lanes(x):
    rows, n = x.shape
    w = min(CUMSUM_CHUNK, n)
    tri = (lax.broadcasted_iota(I32, (w, w), 0) < lax.broadcasted_iota(I32, (w, w), 1))
    tri = jnp.where(tri, 1.0, 0.0).astype(BF16)
    carry = jnp.zeros((rows, 1), F32)
    outs = []
    for c in range(0, n, w):
        xc = x[:, c:c + w]
        outs.append(_dot(xc.astype(BF16), tri) + carry)
        carry = carry + xc.sum(axis=-1, keepdims=True)
    return outs[0] if len(outs) == 1 else jnp.concatenate(outs, axis=1)


def _select_kernel(a_ref, pos_ref, *, cap):
    rb, ne, n = a_ref.shape
    rows = rb * ne
    a = a_ref[...].reshape(rows, n)
    capf = float(cap)

    def body(_, c):
        lo, hi = c
        mid = lo + ((hi - lo + 1) >> 1)
        cnt = jnp.where(a >= pltpu.bitcast(mid, F32), 1.0, 0.0).sum(axis=-1, keepdims=True)
        ok = cnt >= capf
        return jnp.where(ok, mid, lo), jnp.where(ok, hi, mid - 1)

    one_bits = 0x3F800000
    lo, _ = lax.fori_loop(0, 30, body, (jnp.zeros((rows, 1), I32), jnp.full((rows, 1), one_bits, I32)))
    thr = pltpu.bitcast(lo, F32)
    gt = jnp.where(a > thr, 1.0, 0.0)
    eq = jnp.where(a == thr, 1.0, 0.0)
    need = capf - gt.sum(axis=-1, keepdims=True)
    sel = gt + eq * jnp.where(_excl_cumsum_lanes(eq) < need, 1.0, 0.0)
    slot = _excl_cumsum_lanes(sel)
    pos_ref[...] = jnp.where(sel > 0.5, slot, -1.0).astype(I32).reshape(rb, ne, n)


def _select(aff_t, cap):
    R, ne, n = aff_t.shape
    rb = min(R, max(1, 128 * 2048 // (ne * n)))
    return pl.pallas_call(
        functools.partial(_select_kernel, cap=cap),
        out_shape=jax.ShapeDtypeStruct((R, ne, n), I32),
        grid=(R // rb,),
        in_specs=[pl.BlockSpec((rb, ne, n), lambda r: (r, 0, 0))],
        out_specs=pl.BlockSpec((rb, ne, n), lambda r: (r, 0, 0)),
        compiler_params=_cp(1), name="expert_select",
    )(aff_t)


def _gather_kernel(pos_ref, aff_ref, h_ref, x_ref, gate_ref, *, cap):
    eb, _, n = pos_ref.shape
    slot = lax.broadcasted_iota(I32, (cap, n), 0)
    parts = []
    for j in range(eb):
        hit = slot == pos_ref[j]
        parts.append(jnp.where(hit, 1.0, 0.0).astype(BF16))
        gate_ref[j] = jnp.where(hit, aff_ref[j], 0.0).sum(axis=-1, keepdims=True)
    onehot = parts[0] if eb == 1 else jnp.concatenate(parts, axis=0)
    xe = _dot(onehot, h_ref[...]).astype(BF16)
    for j in range(eb):
        x_ref[j] = xe[j * cap:(j + 1) * cap]


def _gather(pos, aff_t, h2, cap):
    R, ne, n = pos.shape
    eb = max(1, min(ne, ROW_TILE // cap))
    row = pl.BlockSpec((None, eb, 1, n), lambda r, e: (r, e, 0, 0))
    return pl.pallas_call(
        functools.partial(_gather_kernel, cap=cap),
        out_shape=[jax.ShapeDtypeStruct((ne, R * cap, D_MODEL), BF16),
                   jax.ShapeDtypeStruct((ne, R * cap, 1), F32)],
        grid=(R, ne // eb),
        in_specs=[row, row, pl.BlockSpec((None, n, D_MODEL), lambda r, e: (r, 0, 0))],
        out_specs=[pl.BlockSpec((eb, cap, D_MODEL), lambda r, e: (e, r, 0)),
                   pl.BlockSpec((eb, cap, 1), lambda r, e: (e, r, 0))],
        compiler_params=_cp(2), name="expert_gather",
    )(pos.reshape(R, ne, 1, n), aff_t.reshape(R, ne, 1, n), h2)


FF_TILE = 512


def _ffn_kernel(x0_ref, x1_ref, wa_ref, wb_ref, wd_ref, g0_ref, g1_ref, y0_ref, y1_ref, acc_ref):
    f = pl.program_id(1)

    @pl.when(f == 0)
    def _():
        acc_ref[...] = jnp.zeros_like(acc_ref)

    x = jnp.concatenate([x0_ref[...], x1_ref[...]], axis=0)
    a = _dot(x, wa_ref[...].astype(BF16))
    b = _dot(x, wb_ref[...].astype(BF16))
    acc_ref[...] += _dot((_silu(a) * b).astype(BF16), wd_ref[...].astype(BF16))

    @pl.when(f == pl.num_programs(1) - 1)
    def _():
        s0 = x0_ref.shape[0]
        y0_ref[...] = (acc_ref[:s0, :] * g0_ref[...]).astype(BF16)
        y1_ref[...] = (acc_ref[s0:, :] * g1_ref[...]).astype(BF16)


def _experts(xe0, gates0, xe1, gates1, w_gate_up, w_down, layer):
    ne, S0, _ = xe0.shape
    S1 = xe1.shape[1]
    nf = D_FF // FF_TILE
    rows = lambda S, w: pl.BlockSpec((None, S, w), lambda e, f: (e, 0, 0))
    return pl.pallas_call(
        _ffn_kernel,
        out_shape=[jax.ShapeDtypeStruct((ne, S0, D_MODEL), BF16), jax.ShapeDtypeStruct((ne, S1, D_MODEL), BF16)],
        grid=(ne, nf),
        in_specs=[rows(S0, D_MODEL), rows(S1, D_MODEL),
                  pl.BlockSpec((None, None, D_MODEL, FF_TILE), lambda e, f: (layer, e, 0, f)),
                  pl.BlockSpec((None, None, D_MODEL, FF_TILE), lambda e, f: (layer, e, 0, nf + f)),
                  pl.BlockSpec((None, None, FF_TILE, D_MODEL), lambda e, f: (layer, e, f, 0)),
                  rows(S0, 1), rows(S1, 1)],
        out_specs=[rows(S0, D_MODEL), rows(S1, D_MODEL)],
        scratch_shapes=[pltpu.VMEM((S0 + S1, D_MODEL), F32)],
        compiler_params=_cp(2), name="expert_ffn",
    )(xe0, xe1, w_gate_up, w_gate_up, w_down, gates0, gates1)


def _scatter_kernel(pos_ref, y_ref, x1_ref, g2_ref, lg_ref, lb_ref, o_ref, *, cap):
    tm, ne = pos_ref.shape
    cw = max(cap, LANES)
    epc = cw // cap
    lane = lax.broadcasted_iota(I32, (tm, cw), 1)
    pos = pos_ref[...]
    acc = jnp.zeros((tm, D_MODEL), F32)
    for c in range(ne // epc):
        onehot = jnp.zeros((tm, cw), F32)
        for jj in range(epc):
            col = pos[:, c * epc + jj:c * epc + jj + 1]
            tgt = jnp.where(col >= 0, col + jj * cap, -1)
            onehot = onehot + jnp.where(lane == tgt, 1.0, 0.0)
        yc = y_ref[c] if epc == 1 else y_ref[c * epc:(c + 1) * epc].reshape(cw, D_MODEL)
        acc = acc + _dot(onehot.astype(BF16), yc)
    o_ref[...] = _layernorm(ALPHA * x1_ref[...] + g2_ref[...] * acc, lg_ref[...], lb_ref[...])


def _scatter(pos_tok, y, x1, mods, ln_g, ln_b, cap, tm):
    R, n, ne = pos_tok.shape
    vec = pl.BlockSpec((1, D_MODEL), lambda r, i: (0, 0))
    return pl.pallas_call(
        functools.partial(_scatter_kernel, cap=cap),
        out_shape=jax.ShapeDtypeStruct((R, n, D_MODEL), F32),
        grid=(R, n // tm),
        in_specs=[pl.BlockSpec((None, tm, ne), lambda r, i: (r, i, 0)),
                  pl.BlockSpec((ne, cap, D_MODEL), lambda r, i: (0, r, 0)),
                  pl.BlockSpec((None, tm, D_MODEL), lambda r, i: (r, i, 0)),
                  _mod_spec(5, 2), vec, vec],
        out_specs=pl.BlockSpec((None, tm, D_MODEL), lambda r, i: (r, i, 0)),
        compiler_params=_cp(2), name="expert_scatter_ln",
    )(pos_tok, y, x1, mods, ln_g.reshape(1, -1), ln_b.reshape(1, -1))


def _capacity(n):
    return CAPACITY_FACTOR * n // N_EXPERTS


def _route(h2, aff):
    cap = _capacity(h2.shape[1])
    aff_t = jnp.swapaxes(aff, 1, 2)
    pos = _select(aff_t, cap)
    xe, gates = _gather(pos, aff_t, h2, cap)
    return pos, xe, gates


def _channel_mixers(groups, l, w_gate_up, w_down, ln2_g, ln2_b):
    routed = [_route(g["h2"], g["aff"]) for g in groups]
    ys = _experts(routed[0][1], routed[0][2], routed[1][1], routed[1][2], w_gate_up, w_down, l)
    return [_scatter(jnp.swapaxes(r[0], 1, 2), y, g["x1"], g["mods"], ln2_g[l], ln2_b[l],
                     _capacity(g["h2"].shape[1]), g["tm"]) for g, r, y in zip(groups, routed, ys)]


def kernel(x_prompt, x_sample, cache_attn_a_k, cache_attn_a_v, cache_attn_b_k, cache_attn_b_v, state_ret, c, c_ctx,
           w_ada, b_ada, w_in, w_out, attn_sink, na_rpb, ret_decay, ret_gn, ln1_g, ln1_b, ln2_g, ln2_b,
           w_router, w_gate_up, w_down):
    n_ctx, L_ctx, _ = x_prompt.shape
    n_lat, L_lat, _ = x_sample.shape
    P = cache_attn_a_k.shape[2]

    cond = jnp.concatenate([c_ctx[None, :], c, jnp.zeros((8 - 1 - n_lat, D_MODEL), F32)], axis=0)
    mod = _modulation(cond, w_ada, b_ada)
    w_in16, w_out16, w_r16 = w_in.astype(BF16), w_out.astype(BF16), w_router.astype(BF16)
    rope_tabs = _rope_tables(L_lat)
    cak = cache_attn_a_k.reshape(n_lat, DEPTH, P, KVH_A * HEAD_DIM)
    cav = cache_attn_a_v.reshape(n_lat, DEPTH, P, KVH_A * HEAD_DIM)
    cbk = cache_attn_b_k.reshape(n_lat, DEPTH, P, H_B * HEAD_DIM)
    cbv = cache_attn_b_v.reshape(n_lat, DEPTH, P, H_B * HEAD_DIM)
    s0_pairs = _heads_to_pair_states(state_ret)

    xc, xs = x_prompt, x_sample
    tm = ROW_TILE
    a_k, a_v, b_k, b_v, st = [], [], [], [], []
    wa, wkv, wb = H_A * HEAD_DIM, KVH_A * HEAD_DIM, H_B * HEAD_DIM
    flat = lambda a: a.reshape(1, n_ctx * L_ctx, a.shape[-1])
    unflat = lambda a: a.reshape(n_ctx, L_ctx, a.shape[-1])
    for l in range(DEPTH):
        mod_c = mod[l, 0:1][:, None, :]
        mods_c = jnp.broadcast_to(mod_c, (n_ctx, 1, 6 * D_MODEL))
        qkv = unflat(_project(flat(xc), mod_c, w_in16[l], tm))
        oa = _attn_a_ctx(qkv, attn_sink[l])
        ob = _attn_b_ctx(qkv)
        oc, s_l = _retention(qkv, ret_decay[l], ret_gn[l], out_state=True)
        x1, h2, aff = [unflat(t) for t in _outproj(flat(oa), flat(ob), flat(oc), flat(xc), mod_c, w_out16[l],
                                                   ln1_g[l], ln1_b[l], w_r16[l], tm)]
        ctx = dict(x1=x1, h2=h2, aff=aff, mods=mods_c, tm=L_ctx)
        a_k.append(qkv[..., wa:wa + wkv].reshape(n_ctx, L_ctx, KVH_A, HEAD_DIM))
        a_v.append(qkv[..., wa + wkv:wa + 2 * wkv].reshape(n_ctx, L_ctx, KVH_A, HEAD_DIM))
        ob0 = wa + 2 * wkv
        b_k.append(qkv[..., ob0 + wb:ob0 + 2 * wb].reshape(n_ctx, L_ctx, H_B, HEAD_DIM))
        b_v.append(qkv[..., ob0 + 2 * wb:ob0 + 3 * wb].reshape(n_ctx, L_ctx, H_B, HEAD_DIM))
        st.append(_pair_states_to_heads(s_l))

        mods_s = mod[l, 1:1 + n_lat][:, None, :]
        qkv = _project(xs, mods_s, w_in16[l], tm, rope_tabs)
        oa = _attn_a_lat(qkv, attn_sink[l], cak, cav, l)
        ob = _attn_b_lat(qkv, cbk, cbv, _bias_table(na_rpb[l]), l)
        oc = _retention(qkv, ret_decay[l], ret_gn[l], s0=s0_pairs, layer=l)
        x1, h2, aff = _outproj(oa, ob, oc, xs, mods_s, w_out16[l], ln1_g[l], ln1_b[l], w_r16[l], tm)
        lat = dict(x1=x1, h2=h2, aff=aff, mods=mods_s, tm=tm)

        xc, xs = _channel_mixers([ctx, lat], l, w_gate_up, w_down, ln2_g, ln2_b)

    return (xc, xs, jnp.stack(a_k, 1), jnp.stack(a_v, 1), jnp.stack(b_k, 1), jnp.stack(b_v, 1),
            jnp.stack(st, 1))
```

```python
import functools

import numpy as np
import jax
import jax.numpy as jnp
from jax import lax
from jax.experimental import pallas as pl
from jax.experimental.pallas import tpu as pltpu

F32 = jnp.float32
BF16 = jnp.bfloat16
I32 = jnp.int32

D_MODEL = 1024
DEPTH = 2
GRID_W = 64
HEAD_DIM = 64
H_A, KVH_A, H_B, H_C = 8, 2, 4, 4
D_IN = 2560
BLOCK = 128
WINDOW = 128
NA_ROWS, NA_COLS = 8, 16
NA_WIN_BLOCKS = 5
NA_QBLOCKS = 4
ROPE_BASE = 10000.0
N_EXPERTS = 16
CAPACITY_FACTOR = 2
D_FF = 1024
LN_EPS = 1e-6
ALPHA = (2 * DEPTH) ** 0.25
NEG = -1e30
LANES = 128
QK_SCALE = HEAD_DIM ** -0.5
VMEM_LIMIT = 56 * 1024 * 1024
ROW_TILE = 1024

COL_QA, COL_KA, COL_VA = 0, 4, 5
COL_QB, COL_KB, COL_VB = 3, 4, 5
COL_QC, COL_KC, COL_VC, COL_GC = 6, 7, 8, 9


def _cp(n_axes):
    return pltpu.CompilerParams(dimension_semantics=("arbitrary",) * n_axes, vmem_limit_bytes=VMEM_LIMIT)


def _dot(a, b):
    return jnp.dot(a, b, preferred_element_type=F32)


def _dot_t(a, b):
    return lax.dot_general(a, b, (((1,), (1,)), ((), ())), preferred_element_type=F32)


def _silu(x):
    return x / (1.0 + jnp.exp(-x))


def _lane_low(shape):
    return lax.broadcasted_iota(I32, shape, len(shape) - 1) < HEAD_DIM


def _layernorm(y, g, b):
    mu = jnp.mean(y, axis=-1, keepdims=True)
    d = y - mu
    var = jnp.mean(d * d, axis=-1, keepdims=True)
    return d * lax.rsqrt(var + LN_EPS) * g + b


def _mod_kernel(c_ref, w_ref, b_ref, o_ref):
    s = _silu(c_ref[...]).astype(BF16)
    o_ref[...] = _dot(s, w_ref[...].astype(BF16)) + b_ref[...]


def _modulation(cond, w_ada, b_ada):
    tn = 1536
    return pl.pallas_call(
        _mod_kernel,
        out_shape=jax.ShapeDtypeStruct((DEPTH, 8, 6 * D_MODEL), F32),
        grid=(DEPTH, 6 * D_MODEL // tn),
        in_specs=[pl.BlockSpec((8, D_MODEL), lambda l, j: (0, 0)),
                  pl.BlockSpec((None, D_MODEL, tn), lambda l, j: (l, 0, j)),
                  pl.BlockSpec((None, 1, tn), lambda l, j: (l, 0, j))],
        out_specs=pl.BlockSpec((None, 8, tn), lambda l, j: (l, 0, j)),
        compiler_params=_cp(2), name="modulation",
    )(cond, w_ada, b_ada.reshape(DEPTH, 1, 6 * D_MODEL))


def _mod_spec(k, n_grid):
    if n_grid == 1:
        return pl.BlockSpec((None, 1, D_MODEL), lambda r: (r, 0, k))
    return pl.BlockSpec((None, 1, D_MODEL), lambda r, i: (r, 0, k))


def _rope_tables(L):
    t = np.arange(L)
    pos = (t // GRID_W, t % GRID_W)
    half = HEAD_DIM // 2
    inv = 1.0 / (ROPE_BASE ** (np.arange(0, half, 2) / half))
    cos = np.zeros((L, HEAD_DIM)); s_up = np.zeros((L, HEAD_DIM)); s_dn = np.zeros((L, HEAD_DIM))
    for part in range(2):
        ang = (pos[part][:, None] * inv[None, :]).astype(np.float32).astype(np.float64)
        q = half // 2
        base = part * half
        cos[:, base:base + q] = np.cos(ang); cos[:, base + q:base + half] = np.cos(ang)
        s_up[:, base:base + q] = -np.sin(ang)
        s_dn[:, base + q:base + half] = np.sin(ang)
    tile = lambda a: jnp.asarray(np.tile(a, (1, LANES // HEAD_DIM)), F32)
    return tile(cos), tile(s_up), tile(s_dn)


def _proj_kernel(*refs, rope):
    if rope:
        x_ref, sc_ref, sh_ref, w_ref, cos_ref, sup_ref, sdn_ref, o_ref = refs
    else:
        x_ref, sc_ref, sh_ref, w_ref, o_ref = refs
    h = (x_ref[...] * (1.0 + sc_ref[...]) + sh_ref[...]).astype(BF16)
    n_rope = (H_A + KVH_A) * HEAD_DIM
    step = 512
    for j in range(0, D_IN, step):
        acc = _dot(h, w_ref[:, j:j + step])
        if rope and j < n_rope:
            q = HEAD_DIM // 4
            for c in range(0, step, LANES):
                a = acc[:, c:c + LANES]
                if j + c < n_rope:
                    a = (a * cos_ref[...] + pltpu.roll(a, LANES - q, 1) * sup_ref[...]
                         + pltpu.roll(a, q, 1) * sdn_ref[...])
                o_ref[:, j + c:j + c + LANES] = a
        else:
            o_ref[:, j:j + step] = acc


def _project(x, mods, w_bf16, tm, rope_tabs=None):
    R, L, _ = x.shape
    rope = rope_tabs is not None
    in_specs = [pl.BlockSpec((None, tm, D_MODEL), lambda r, i: (r, i, 0)),
                _mod_spec(1, 2), _mod_spec(0, 2),
                pl.BlockSpec((D_MODEL, D_IN), lambda r, i: (0, 0))]
    args = [x, mods, mods, w_bf16]
    if rope:
        in_specs += [pl.BlockSpec((tm, LANES), lambda r, i: (i, 0))] * 3
        args += list(rope_tabs)
    return pl.pallas_call(
        functools.partial(_proj_kernel, rope=rope),
        out_shape=jax.ShapeDtypeStruct((R, L, D_IN), F32),
        grid=(R, L // tm), in_specs=in_specs,
        out_specs=pl.BlockSpec((None, tm, D_IN), lambda r, i: (r, i, 0)),
        compiler_params=_cp(2), name="in_proj",
    )(*args)


def _mask_head(x, sub):
    low = _lane_low(x.shape)
    return jnp.where(low, x, 0.0) if sub == 0 else jnp.where(low, 0.0, x)


def _key_variant(k, sub):
    return _mask_head(k, sub).T.astype(BF16)


def _value_variant(v, sub):
    low = _lane_low(v.shape)
    return (jnp.where(low, v, 1.0) if sub == 0 else jnp.where(low, 1.0, v)).astype(BF16)


def _attend(lhs, keys, values, biases, sub, sink=None):
    scores = []
    for k, b in zip(keys, biases):
        s = _dot(lhs, k)
        scores.append(s if b is None else s + b)
    tiles = [s[:, c:c + LANES] for s in scores for c in range(0, s.shape[1], LANES)]
    while len(tiles) > 1:
        tiles = [jnp.maximum(a, b) for a, b in zip(tiles[0::2], tiles[1::2])] + ([tiles[-1]] if len(tiles) % 2 else [])
    m = jnp.broadcast_to(tiles[0].max(axis=-1, keepdims=True), tiles[0].shape)
    if sink is not None:
        m = jnp.maximum(m, sink)
    o = None
    for s, v in zip(scores, values):
        p = [jnp.exp(s[:, c:c + LANES] - m).astype(BF16) for c in range(0, s.shape[1], LANES)]
        pv = _dot(p[0] if len(p) == 1 else jnp.concatenate(p, axis=1), v)
        o = pv if o is None else o + pv
    if sink is not None:
        low = _lane_low(o.shape)
        extra = jnp.exp(sink - m)
        o = o + (jnp.where(low, 0.0, extra) if sub == 0 else jnp.where(low, extra, 0.0))
    return o / pltpu.roll(o, HEAD_DIM, 1)


def _gqa_variants(k, v, hk):
    low = _lane_low(k.shape)
    k_sw, v_sw = pltpu.roll(k, HEAD_DIM, 1), pltpu.roll(v, HEAD_DIM, 1)
    k_dup = jnp.where(low, k, k_sw) if hk == 0 else jnp.where(low, k_sw, k)
    return k_dup.T.astype(BF16), jnp.where(low, v if hk == 0 else v_sw, 1.0).astype(BF16)


def _fill_gqa_variants(k_src, v_src, kv_ref, vv_ref, off, n):
    step = 256
    for c in range(0, n, step):
        k = k_src[c:c + step, :]
        v = v_src[c:c + step, :]
        for hk in range(KVH_A):
            kd, va = _gqa_variants(k, v, hk)
            kv_ref[hk, :, off + c:off + c + step] = kd
            vv_ref[hk, off + c:off + c + step, :] = va


def _fill_mha_variants(k_src, v_src, kv_ref, vv_ref, off, n):
    step = 256
    for c in range(0, n, step):
        rows = slice(off + c, off + c + step)
        for p in range(H_B // 2):
            cs = slice(LANES * p, LANES * p + LANES)
            k = k_src[c:c + step, cs]
            v = v_src[c:c + step, cs]
            for sub in range(2):
                kv_ref[p, sub, :, rows] = _key_variant(k, sub)
                vv_ref[p, sub, rows, :] = _value_variant(v, sub)


def _sink_col(sink_ref, heads, rows):
    return jnp.concatenate([jnp.full((rows, LANES), sink_ref[h], F32) for h in heads], axis=0)


def _gqa_attend(sink_ref, q_ref, o_ref, hk, keys, values, biases):
    rows = q_ref.shape[0]
    g = H_A // KVH_A
    heads = list(range(g * hk, g * hk + g))
    lhs = jnp.concatenate([_mask_head(q_ref[:, LANES * (h // 2):LANES * (h // 2) + LANES], h % 2) for h in heads],
                          axis=0)
    lhs = (lhs * QK_SCALE).astype(BF16)
    stacked = [None if b is None else jnp.concatenate([b] * g, axis=0) for b in biases]
    out = _attend(lhs, keys, values, stacked, 0, _sink_col(sink_ref, heads, rows))
    low = _lane_low((rows, LANES))
    for j in range(0, g, 2):
        h = heads[j]
        odd = pltpu.roll(out[(j + 1) * rows:(j + 2) * rows], HEAD_DIM, 1)
        o_ref[:, LANES * (h // 2):LANES * (h // 2) + LANES] = jnp.where(low, out[j * rows:(j + 1) * rows], odd)


def _attn_a_ctx_kernel(sink_ref, q_ref, k_ref, v_ref, o_ref):
    for hk in range(KVH_A):
        kd, va = _gqa_variants(k_ref[...], v_ref[...], hk)
        _gqa_attend(sink_ref, q_ref, o_ref, hk, [kd], [va], [None])


def _window_mask_table(nb):
    row = np.arange(BLOCK)[:, None]
    col = np.arange(3 * BLOCK)[None, :]
    tabs = [np.where(np.abs(col - off * BLOCK - row) <= WINDOW, 0.0, NEG) for off in range(3)]
    return jnp.asarray(np.stack(tabs), F32)


def _attn_a_lat_kernel(sink_ref, q_ref, k_ref, v_ref, kc_ref, vc_ref, mask_ref, o_ref, kv_ref, vv_ref, *, nb):
    i = pl.program_id(1)
    L, P = k_ref.shape[0], kc_ref.shape[0]

    @pl.when(i == 0)
    def _():
        _fill_gqa_variants(k_ref, v_ref, kv_ref, vv_ref, 0, L)
        _fill_gqa_variants(kc_ref, vc_ref, kv_ref, vv_ref, L, P)

    wk = 3 * BLOCK
    sblk = jnp.clip(i - 1, 0, nb - 3)
    start = pl.multiple_of(sblk * BLOCK, BLOCK)
    bias = mask_ref[i - sblk]
    for hk in range(KVH_A):
        ks = [kv_ref[hk, :, pl.ds(start, wk)], kv_ref[hk, :, L:L + P]]
        vs = [vv_ref[hk, pl.ds(start, wk), :], vv_ref[hk, L:L + P, :]]
        _gqa_attend(sink_ref, q_ref, o_ref, hk, ks, vs, [bias, None])


def _smem_spec():
    return pl.BlockSpec(memory_space=pltpu.SMEM)


def _attn_a_ctx(qkv, sink):
    R, L, _ = qkv.shape
    wq, wkv = H_A * HEAD_DIM, KVH_A * HEAD_DIM
    return pl.pallas_call(
        _attn_a_ctx_kernel,
        out_shape=jax.ShapeDtypeStruct((R, L, wq), F32),
        grid=(R,),
        in_specs=[_smem_spec(),
                  pl.BlockSpec((None, L, wq), lambda r: (r, 0, COL_QA)),
                  pl.BlockSpec((None, L, wkv), lambda r: (r, 0, COL_KA)),
                  pl.BlockSpec((None, L, wkv), lambda r: (r, 0, COL_VA))],
        out_specs=pl.BlockSpec((None, L, wq), lambda r: (r, 0, 0)),
        compiler_params=_cp(1), name="attn_a_ctx",
    )(sink, qkv, qkv, qkv)


def _attn_a_lat(qkv, sink, cache_k, cache_v, layer):
    R, L, _ = qkv.shape
    nb = L // BLOCK
    assert nb >= 3
    P = cache_k.shape[2]
    wq, wkv = H_A * HEAD_DIM, KVH_A * HEAD_DIM
    cache_spec = pl.BlockSpec((None, None, P, wkv), lambda r, i: (r, layer, 0, 0))
    masks = _window_mask_table(nb)
    return pl.pallas_call(
        functools.partial(_attn_a_lat_kernel, nb=nb),
        out_shape=jax.ShapeDtypeStruct((R, L, wq), F32),
        grid=(R, nb),
        in_specs=[_smem_spec(),
                  pl.BlockSpec((None, BLOCK, wq), lambda r, i: (r, i, COL_QA)),
                  pl.BlockSpec((None, L, wkv), lambda r, i: (r, 0, COL_KA)),
                  pl.BlockSpec((None, L, wkv), lambda r, i: (r, 0, COL_VA)),
                  cache_spec, cache_spec,
                  pl.BlockSpec(masks.shape, lambda r, i: (0, 0, 0))],
        out_specs=pl.BlockSpec((None, BLOCK, wq), lambda r, i: (r, i, 0)),
        scratch_shapes=[pltpu.VMEM((KVH_A, LANES, L + P), BF16),
                        pltpu.VMEM((KVH_A, L + P, LANES), BF16)],
        compiler_params=_cp(2), name="attn_a_lat",
    )(sink, qkv, qkv, qkv, cache_k, cache_v, masks)


def _attn_b_ctx_kernel(q_ref, k_ref, v_ref, o_ref):
    L = q_ref.shape[0]
    low = _lane_low((L, LANES))
    for p in range(H_B // 2):
        cs = slice(LANES * p, LANES * p + LANES)
        lhs = (q_ref[:, cs] * QK_SCALE).astype(BF16)
        k, v = k_ref[:, cs], v_ref[:, cs]
        outs = [_attend(lhs, [_key_variant(k, sub)], [_value_variant(v, sub)], [None], sub) for sub in range(2)]
        o_ref[:, cs] = jnp.where(low, outs[0], outs[1])


def _bias_table_kernel(rpb_ref, o_ref):
    n_dr, n_dc = 2 * NA_ROWS - 1, 2 * NA_COLS - 1
    n_t = o_ref.shape[1]
    shape = (GRID_W, LANES)
    cq = lax.broadcasted_iota(I32, shape, 0)
    lane = lax.broadcasted_iota(I32, shape, 1)
    right = lane >= GRID_W
    ck = lane & (GRID_W - 1)
    cs = jnp.clip(cq - NA_COLS // 2, 0, GRID_W - NA_COLS)
    col_ok = (ck >= cs) & (ck < cs + NA_COLS)
    dc = ck - cq + (NA_COLS - 1)

    def body(t, carry, h):
        dr_l = t - (NA_ROWS + 1)
        acc = jnp.zeros(shape, F32)
        for d in range(n_dc):
            vals = []
            for dr in (dr_l, dr_l + 1):
                ok = (dr >= -(NA_ROWS - 1)) & (dr <= NA_ROWS - 1)
                idx = h * (n_dr * n_dc) + jnp.clip(dr + NA_ROWS - 1, 0, n_dr - 1) * n_dc + d
                vals.append(jnp.where(ok, rpb_ref[idx], 0.0))
            acc = jnp.where(dc == d, jnp.where(right, vals[1], vals[0]), acc)
        o_ref[h, t] = jnp.where(col_ok, acc, NEG)
        return carry

    for h in range(H_B):
        lax.fori_loop(0, n_t, functools.partial(body, h=h), 0)


def _bias_table(rpb_l):
    n_t = 2 * (NA_ROWS + 1)
    return pl.pallas_call(
        _bias_table_kernel,
        out_shape=jax.ShapeDtypeStruct((H_B, n_t, GRID_W, LANES), F32),
        in_specs=[_smem_spec()],
        out_specs=pl.BlockSpec(memory_space=pltpu.VMEM),
        name="na_bias_table",
    )(rpb_l.reshape(-1))


def _attn_b_lat_kernel(q_ref, k_ref, v_ref, kc_ref, vc_ref, tp_ref, o_ref, kv_ref, vv_ref, *, nb, rows_total):
    i = pl.program_id(1)
    L, P = k_ref.shape[0], kc_ref.shape[0]

    @pl.when(i == 0)
    def _():
        _fill_mha_variants(k_ref, v_ref, kv_ref, vv_ref, 0, L)
        _fill_mha_variants(kc_ref, vc_ref, kv_ref, vv_ref, L, P)

    nq = NA_QBLOCKS
    nwin = NA_WIN_BLOCKS + nq - 1
    wk = nwin * BLOCK
    rows = nq * BLOCK
    sblk = jnp.clip(nq * i - 2, 0, nb - nwin)
    start = pl.multiple_of(sblk * BLOCK, BLOCK)
    shape = (rows, wk)
    rq = 2 * nq * i + (lax.broadcasted_iota(I32, shape, 0) >> 6)
    rk = 2 * sblk + (lax.broadcasted_iota(I32, shape, 1) >> 6)
    rs = jnp.clip(rq - NA_ROWS // 2, 0, rows_total - NA_ROWS)
    row_bias = jnp.where((rk >= rs) & (rk < rs + NA_ROWS), 0.0, NEG)
    n_t = tp_ref.shape[1]
    low = _lane_low((rows, LANES))
    for p in range(H_B // 2):
        cs = slice(LANES * p, LANES * p + LANES)
        lhs = (q_ref[:, cs] * QK_SCALE).astype(BF16)
        outs = []
        for sub in range(2):
            h = 2 * p + sub
            strips = []
            for qq in range(nq):
                tiles = []
                for kb in range(nwin):
                    t = jnp.clip(2 * (sblk + kb - (nq * i + qq)) + NA_ROWS + 1, 1, n_t - 1)
                    tiles.append(jnp.concatenate([tp_ref[h, t], tp_ref[h, t - 1]], axis=0))
                strips.append(jnp.concatenate(tiles, axis=1))
            bias = jnp.concatenate(strips, axis=0) + row_bias
            ks = [kv_ref[p, sub, :, pl.ds(start, wk)], kv_ref[p, sub, :, L:L + P]]
            vs = [vv_ref[p, sub, pl.ds(start, wk), :], vv_ref[p, sub, L:L + P, :]]
            outs.append(_attend(lhs, ks, vs, [bias, None], sub))
        o_ref[:, cs] = jnp.where(low, outs[0], outs[1])


def _attn_b_ctx(qkv):
    R, L, _ = qkv.shape
    w = H_B * HEAD_DIM
    return pl.pallas_call(
        _attn_b_ctx_kernel,
        out_shape=jax.ShapeDtypeStruct((R, L, w), F32),
        grid=(R,),
        in_specs=[pl.BlockSpec((None, L, w), lambda r: (r, 0, COL_QB)),
                  pl.BlockSpec((None, L, w), lambda r: (r, 0, COL_KB)),
                  pl.BlockSpec((None, L, w), lambda r: (r, 0, COL_VB))],
        out_specs=pl.BlockSpec((None, L, w), lambda r: (r, 0, 0)),
        compiler_params=_cp(1), name="attn_b_ctx",
    )(qkv, qkv, qkv)


def _attn_b_lat(qkv, cache_k, cache_v, bias_tab, layer):
    R, L, _ = qkv.shape
    nb = L // BLOCK
    P = cache_k.shape[2]
    w = H_B * HEAD_DIM
    cache_spec = pl.BlockSpec((None, None, P, w), lambda r, i: (r, layer, 0, 0))
    rows = NA_QBLOCKS * BLOCK
    assert nb % NA_QBLOCKS == 0 and nb >= NA_WIN_BLOCKS + NA_QBLOCKS - 1
    return pl.pallas_call(
        functools.partial(_attn_b_lat_kernel, nb=nb, rows_total=L // GRID_W),
        out_shape=jax.ShapeDtypeStruct((R, L, w), F32),
        grid=(R, nb // NA_QBLOCKS),
        in_specs=[pl.BlockSpec((None, rows, w), lambda r, i: (r, i, COL_QB)),
                  pl.BlockSpec((None, L, w), lambda r, i: (r, 0, COL_KB)),
                  pl.BlockSpec((None, L, w), lambda r, i: (r, 0, COL_VB)),
                  cache_spec, cache_spec,
                  pl.BlockSpec(bias_tab.shape, lambda r, i: (0, 0, 0, 0))],
        out_specs=pl.BlockSpec((None, rows, w), lambda r, i: (r, i, 0)),
        scratch_shapes=[pltpu.VMEM((H_B // 2, 2, LANES, L + P), BF16),
                        pltpu.VMEM((H_B // 2, 2, L + P, LANES), BF16)],
        compiler_params=_cp(2), name="attn_b_lat",
    )(qkv, qkv, qkv, cache_k, cache_v, bias_tab)


T_ZF, T_ZB, T_XF, T_XB, T_GF, T_GB, T_MA, T_MB = range(8)


def _ret_kernel(*refs, nc, has_s0, out_state):
    refs = list(refs)
    dec_ref, q_ref, k_ref, v_ref, g_ref, gn_ref = refs[:6]
    pos = 6
    s0_ref = None
    if has_s0:
        s0_ref = refs[pos]; pos += 1
    o_ref = refs[pos]; pos += 1
    st_ref = None
    if out_state:
        st_ref = refs[pos]; pos += 1
    sf_ref, sb_ref, tab_ref = refs[pos], refs[pos + 1], refs[pos + 2]

    sq = (BLOCK, LANES)
    low = _lane_low(sq)
    row_low = lax.broadcasted_iota(I32, sq, 0) < HEAD_DIM
    same_head = low == row_low
    n_pairs = H_C // 2

    @pl.when(pl.program_id(0) == 0)
    def _():
        ri = lax.broadcasted_iota(I32, sq, 0).astype(F32)
        ci = lax.broadcasted_iota(I32, sq, 1).astype(F32)

        def log_gamma(d, h):
            return -jnp.exp(jnp.full(sq, dec_ref[d, h], F32))

        for p in range(n_pairs):
            lf = [log_gamma(0, 2 * p), log_gamma(0, 2 * p + 1)]
            lb = [log_gamma(1, 2 * p), log_gamma(1, 2 * p + 1)]
            lf_lane, lb_lane = jnp.where(low, lf[0], lf[1]), jnp.where(low, lb[0], lb[1])
            lf_row, lb_row = jnp.where(row_low, lf[0], lf[1]), jnp.where(row_low, lb[0], lb[1])
            tab_ref[p, T_ZF] = jnp.exp(lf_lane * (BLOCK - 1.0 - ri))
            tab_ref[p, T_ZB] = jnp.exp(lb_lane * ri)
            tab_ref[p, T_XF] = jnp.exp(lf_lane * (ri + 1.0))
            tab_ref[p, T_XB] = jnp.exp(lb_lane * (BLOCK - ri))
            tab_ref[p, T_GF] = jnp.exp(lf_row * float(BLOCK))
            tab_ref[p, T_GB] = jnp.exp(lb_row * float(BLOCK))
            rel = ri - ci
            for sub in range(2):
                fwd = jnp.where(rel >= 0, jnp.exp(lf[sub] * jnp.maximum(rel, 0.0)), 0.0)
                bwd = jnp.where(rel <= 0, jnp.exp(lb[sub] * jnp.maximum(-rel, 0.0)), 0.0)
                tab_ref[p, T_MA + sub] = fwd + bwd

    def chunk(ref, c, p):
        r0 = pl.multiple_of(c * BLOCK, BLOCK)
        return ref[pl.ds(r0, BLOCK), LANES * p:LANES * p + LANES]

    unroll = min(nc, 4)

    def inc_body(c, carry):
        for p in range(n_pairs):
            kt = (chunk(k_ref, c, p) * QK_SCALE).T.astype(BF16)
            v = chunk(v_ref, c, p)
            zv = jnp.concatenate([v * tab_ref[p, T_ZF], v * tab_ref[p, T_ZB]], axis=1).astype(BF16)
            inc = _dot(kt, zv)
            sf_ref[c, p] = jnp.where(same_head, inc[:, :LANES], 0.0)
            sb_ref[c, p] = jnp.where(same_head, inc[:, LANES:], 0.0)
        return carry

    lax.fori_loop(0, nc, inc_body, 0, unroll=unroll)

    def scan_body(t, S, ref, tg, reverse):
        c = nc - 1 - t if reverse else t
        new = []
        for p in range(n_pairs):
            inc = ref[c, p]
            ref[c, p] = S[p]
            new.append(tab_ref[p, tg] * S[p] + inc)
        return tuple(new)

    zero = jnp.zeros(sq, F32)
    s_f0 = tuple(s0_ref[0, p] if has_s0 else zero for p in range(n_pairs))
    s_b0 = tuple(s0_ref[1, p] if has_s0 else zero for p in range(n_pairs))
    s_f = lax.fori_loop(0, nc, functools.partial(scan_body, ref=sf_ref, tg=T_GF, reverse=False), s_f0)
    s_b = lax.fori_loop(0, nc, functools.partial(scan_body, ref=sb_ref, tg=T_GB, reverse=True), s_b0)
    if out_state:
        for p in range(n_pairs):
            st_ref[0, p] = s_f[p]
            st_ref[1, p] = s_b[p]

    def out_body(c, carry):
        for p in range(n_pairs):
            q = chunk(q_ref, c, p)
            k16 = (chunk(k_ref, c, p) * QK_SCALE).astype(BF16)
            v = chunk(v_ref, c, p)
            a = _dot_t(jnp.concatenate([_mask_head(q, 0), _mask_head(q, 1)], axis=0).astype(BF16), k16)
            lhs = jnp.concatenate([a[:BLOCK] * tab_ref[p, T_MA], a[BLOCK:] * tab_ref[p, T_MA + 1],
                                   q * tab_ref[p, T_XF], q * tab_ref[p, T_XB]], axis=1).astype(BF16)
            rhs = jnp.concatenate([_mask_head(v, 0), _mask_head(v, 1), sf_ref[c, p], sb_ref[c, p]],
                                  axis=0).astype(BF16)
            o = _dot(lhs, rhs)
            mu = jnp.where(low, _mask_head(o, 0).sum(-1, keepdims=True),
                           _mask_head(o, 1).sum(-1, keepdims=True)) * (1.0 / HEAD_DIM)
            d = o - mu
            d2 = d * d
            var = jnp.where(low, _mask_head(d2, 0).sum(-1, keepdims=True),
                            _mask_head(d2, 1).sum(-1, keepdims=True)) * (1.0 / HEAD_DIM)
            cs = slice(LANES * p, LANES * p + LANES)
            on = d * lax.rsqrt(var + LN_EPS) * gn_ref[:, cs]
            r0 = pl.multiple_of(c * BLOCK, BLOCK)
            o_ref[pl.ds(r0, BLOCK), cs] = _silu(chunk(g_ref, c, p)) * on
        return carry

    lax.fori_loop(0, nc, out_body, 0, unroll=unroll)


def _retention(qkv, decay, gn_w, s0=None, layer=0, out_state=False):
    R, L, _ = qkv.shape
    nc = L // BLOCK
    w = H_C * HEAD_DIM
    n_pairs = H_C // 2
    col = lambda cb: pl.BlockSpec((None, L, w), lambda r: (r, 0, cb))
    in_specs = [_smem_spec(), col(COL_QC), col(COL_KC), col(COL_VC), col(COL_GC),
                pl.BlockSpec((1, w), lambda r: (0, 0))]
    args = [decay, qkv, qkv, qkv, qkv, gn_w.reshape(1, w)]
    if s0 is not None:
        in_specs.append(pl.BlockSpec((None, None, 2, n_pairs, BLOCK, LANES), lambda r: (r, layer, 0, 0, 0, 0)))
        args.append(s0)
    out_shape = [jax.ShapeDtypeStruct((R, L, w), F32)]
    out_specs = [pl.BlockSpec((None, L, w), lambda r: (r, 0, 0))]
    if out_state:
        out_shape.append(jax.ShapeDtypeStruct((R, 2, n_pairs, BLOCK, LANES), F32))
        out_specs.append(pl.BlockSpec((None, 2, n_pairs, BLOCK, LANES), lambda r: (r, 0, 0, 0, 0)))
    res = pl.pallas_call(
        functools.partial(_ret_kernel, nc=nc, has_s0=s0 is not None, out_state=out_state),
        out_shape=out_shape, grid=(R,), in_specs=in_specs, out_specs=out_specs,
        scratch_shapes=[pltpu.VMEM((nc, n_pairs, BLOCK, LANES), F32),
                        pltpu.VMEM((nc, n_pairs, BLOCK, LANES), F32),
                        pltpu.VMEM((n_pairs, 8, BLOCK, LANES), F32)],
        compiler_params=_cp(1), name="retention",
    )(*args)
    return res if out_state else res[0]


def _pair_states_to_heads(st):
    a = st[..., :HEAD_DIM, :HEAD_DIM]
    b = st[..., HEAD_DIM:, HEAD_DIM:]
    return jnp.stack([a, b], axis=3).reshape(st.shape[0], 2, H_C, HEAD_DIM, HEAD_DIM)


def _heads_to_pair_states(s):
    s = s.reshape(*s.shape[:3], H_C // 2, 2, HEAD_DIM, HEAD_DIM)
    z = jnp.zeros_like(s[..., 0, :, :])
    top = jnp.concatenate([s[..., 0, :, :], z], axis=-1)
    bot = jnp.concatenate([z, s[..., 1, :, :]], axis=-1)
    return jnp.concatenate([top, bot], axis=-2)


def _outproj_kernel(oa_ref, ob_ref, oc_ref, x_ref, g1_ref, sc2_ref, sh2_ref, w_ref, lg_ref, lb_ref, wr_ref,
                    x1_ref, h2_ref, aff_ref):
    wa, wb = H_A * HEAD_DIM, H_B * HEAD_DIM
    mix = (_dot(oa_ref[...].astype(BF16), w_ref[0:wa, :])
           + _dot(ob_ref[...].astype(BF16), w_ref[wa:wa + wb, :])
           + _dot(oc_ref[...].astype(BF16), w_ref[wa + wb:, :]))
    x1 = _layernorm(ALPHA * x_ref[...] + g1_ref[...] * mix, lg_ref[...], lb_ref[...])
    x1_ref[...] = x1
    h2 = (x1 * (1.0 + sc2_ref[...]) + sh2_ref[...]).astype(BF16)
    h2_ref[...] = h2
    logits = _dot(h2, wr_ref[...])
    e = jnp.exp(logits - logits.max(axis=-1, keepdims=True))
    aff_ref[...] = e / e.sum(axis=-1, keepdims=True)


def _outproj(oa, ob, oc, x, mods, w_out_bf16, ln_g, ln_b, w_router_bf16, tm):
    R, L, _ = x.shape
    row = lambda w: pl.BlockSpec((None, tm, w), lambda r, i: (r, i, 0))
    vec = pl.BlockSpec((1, D_MODEL), lambda r, i: (0, 0))
    return pl.pallas_call(
        _outproj_kernel,
        out_shape=[jax.ShapeDtypeStruct((R, L, D_MODEL), F32),
                   jax.ShapeDtypeStruct((R, L, D_MODEL), BF16),
                   jax.ShapeDtypeStruct((R, L, N_EXPERTS), F32)],
        grid=(R, L // tm),
        in_specs=[row(H_A * HEAD_DIM), row(H_B * HEAD_DIM), row(H_C * HEAD_DIM), row(D_MODEL),
                  _mod_spec(2, 2), _mod_spec(4, 2), _mod_spec(3, 2),
                  pl.BlockSpec((D_MODEL, D_MODEL), lambda r, i: (0, 0)), vec, vec,
                  pl.BlockSpec((D_MODEL, N_EXPERTS), lambda r, i: (0, 0))],
        out_specs=[row(D_MODEL), row(D_MODEL), row(N_EXPERTS)],
        compiler_params=_cp(2), name="out_proj_ln_router",
    )(oa, ob, oc, x, mods, mods, mods, w_out_bf16, ln_g.reshape(1, -1), ln_b.reshape(1, -1), w_router_bf16)


CUMSUM_CHUNK = 256


def _excl_cumsum_lanes(x):
    rows, n = x.shape
    w = min(CUMSUM_CHUNK, n)
    tri = (lax.broadcasted_iota(I32, (w, w), 0) < lax.broadcasted_iota(I32, (w, w), 1))
    tri = jnp.where(tri, 1.0, 0.0).astype(BF16)
    carry = jnp.zeros((rows, 1), F32)
    outs = []
    for c in range(0, n, w):
        xc = x[:, c:c + w]
        outs.append(_dot(xc.astype(BF16), tri) + carry)
        carry = carry + xc.sum(axis=-1, keepdims=True)
    return outs[0] if len(outs) == 1 else jnp.concatenate(outs, axis=1)


def _select_kernel(a_ref, pos_ref, *, cap):
    rb, ne, n = a_ref.shape
    rows = rb * ne
    a = a_ref[...].reshape(rows, n)
    capf = float(cap)

    def body(_, c):
        lo, hi = c
        mid = lo + ((hi - lo + 1) >> 1)
        cnt = jnp.where(a >= pltpu.bitcast(mid, F32), 1.0, 0.0).sum(axis=-1, keepdims=True)
        ok = cnt >= capf
        return jnp.where(ok, mid, lo), jnp.where(ok, hi, mid - 1)

    one_bits = 0x3F800000
    lo, _ = lax.fori_loop(0, 30, body, (jnp.zeros((rows, 1), I32), jnp.full((rows, 1), one_bits, I32)))
    thr = pltpu.bitcast(lo, F32)
    gt = jnp.where(a > thr, 1.0, 0.0)
    eq = jnp.where(a == thr, 1.0, 0.0)
    need = capf - gt.sum(axis=-1, keepdims=True)
    sel = gt + eq * jnp.where(_excl_cumsum_lanes(eq) < need, 1.0, 0.0)
    slot = _excl_cumsum_lanes(sel)
    pos_ref[...] = jnp.where(sel > 0.5, slot, -1.0).astype(I32).reshape(rb, ne, n)


def _select(aff_t, cap):
    R, ne, n = aff_t.shape
    rb = min(R, max(1, 128 * 2048 // (ne * n)))
    return pl.pallas_call(
        functools.partial(_select_kernel, cap=cap),
        out_shape=jax.ShapeDtypeStruct((R, ne, n), I32),
        grid=(R // rb,),
        in_specs=[pl.BlockSpec((rb, ne, n), lambda r: (r, 0, 0))],
        out_specs=pl.BlockSpec((rb, ne, n), lambda r: (r, 0, 0)),
        compiler_params=_cp(1), name="expert_select",
    )(aff_t)


def _gather_kernel(pos_ref, aff_ref, h_ref, x_ref, gate_ref, *, cap):
    eb, _, n = pos_ref.shape
    slot = lax.broadcasted_iota(I32, (cap, n), 0)
    parts = []
    for j in range(eb):
        hit = slot == pos_ref[j]
        parts.append(jnp.where(hit, 1.0, 0.0).astype(BF16))
        gate_ref[j] = jnp.where(hit, aff_ref[j], 0.0).sum(axis=-1, keepdims=True)
    onehot = parts[0] if eb == 1 else jnp.concatenate(parts, axis=0)
    xe = _dot(onehot, h_ref[...]).astype(BF16)
    for j in range(eb):
        x_ref[j] = xe[j * cap:(j + 1) * cap]


def _gather(pos, aff_t, h2, cap):
    R, ne, n = pos.shape
    eb = max(1, min(ne, ROW_TILE // cap))
    row = pl.BlockSpec((None, eb, 1, n), lambda r, e: (r, e, 0, 0))
    return pl.pallas_call(
        functools.partial(_gather_kernel, cap=cap),
        out_shape=[jax.ShapeDtypeStruct((ne, R * cap, D_MODEL), BF16),
                   jax.ShapeDtypeStruct((ne, R * cap, 1), F32)],
        grid=(R, ne // eb),
        in_specs=[row, row, pl.BlockSpec((None, n, D_MODEL), lambda r, e: (r, 0, 0))],
        out_specs=[pl.BlockSpec((eb, cap, D_MODEL), lambda r, e: (e, r, 0)),
                   pl.BlockSpec((eb, cap, 1), lambda r, e: (e, r, 0))],
        compiler_params=_cp(2), name="expert_gather",
    )(pos.reshape(R, ne, 1, n), aff_t.reshape(R, ne, 1, n), h2)


FF_TILE = 512


def _ffn_kernel(x0_ref, x1_ref, wa_ref, wb_ref, wd_ref, g0_ref, g1_ref, y0_ref, y1_ref, acc_ref):
    f = pl.program_id(1)

    @pl.when(f == 0)
    def _():
        acc_ref[...] = jnp.zeros_like(acc_ref)

    x = jnp.concatenate([x0_ref[...], x1_ref[...]], axis=0)
    a = _dot(x, wa_ref[...].astype(BF16))
    b = _dot(x, wb_ref[...].astype(BF16))
    acc_ref[...] += _dot((_silu(a) * b).astype(BF16), wd_ref[...].astype(BF16))

    @pl.when(f == pl.num_programs(1) - 1)
    def _():
        s0 = x0_ref.shape[0]
        y0_ref[...] = (acc_ref[:s0, :] * g0_ref[...]).astype(BF16)
        y1_ref[...] = (acc_ref[s0:, :] * g1_ref[...]).astype(BF16)


def _experts(xe0, gates0, xe1, gates1, w_gate_up, w_down, layer):
    ne, S0, _ = xe0.shape
    S1 = xe1.shape[1]
    nf = D_FF // FF_TILE
    rows = lambda S, w: pl.BlockSpec((None, S, w), lambda e, f: (e, 0, 0))
    return pl.pallas_call(
        _ffn_kernel,
        out_shape=[jax.ShapeDtypeStruct((ne, S0, D_MODEL), BF16), jax.ShapeDtypeStruct((ne, S1, D_MODEL), BF16)],
        grid=(ne, nf),
        in_specs=[rows(S0, D_MODEL), rows(S1, D_MODEL),
                  pl.BlockSpec((None, None, D_MODEL, FF_TILE), lambda e, f: (layer, e, 0, f)),
                  pl.BlockSpec((None, None, D_MODEL, FF_TILE), lambda e, f: (layer, e, 0, nf + f)),
                  pl.BlockSpec((None, None, FF_TILE, D_MODEL), lambda e, f: (layer, e, f, 0)),
                  rows(S0, 1), rows(S1, 1)],
        out_specs=[rows(S0, D_MODEL), rows(S1, D_MODEL)],
        scratch_shapes=[pltpu.VMEM((S0 + S1, D_MODEL), F32)],
        compiler_params=_cp(2), name="expert_ffn",
    )(xe0, xe1, w_gate_up, w_gate_up, w_down, gates0, gates1)


def _scatter_kernel(pos_ref, y_ref, x1_ref, g2_ref, lg_ref, lb_ref, o_ref, *, cap):
    tm, ne = pos_ref.shape
    cw = max(cap, LANES)
    epc = cw // cap
    lane = lax.broadcasted_iota(I32, (tm, cw), 1)
    pos = pos_ref[...]
    acc = jnp.zeros((tm, D_MODEL), F32)
    for c in range(ne // epc):
        onehot = jnp.zeros((tm, cw), F32)
        for jj in range(epc):
            col = pos[:, c * epc + jj:c * epc + jj + 1]
            tgt = jnp.where(col >= 0, col + jj * cap, -1)
            onehot = onehot + jnp.where(lane == tgt, 1.0, 0.0)
        yc = y_ref[c] if epc == 1 else y_ref[c * epc:(c + 1) * epc].reshape(cw, D_MODEL)
        acc = acc + _dot(onehot.astype(BF16), yc)
    o_ref[...] = _layernorm(ALPHA * x1_ref[...] + g2_ref[...] * acc, lg_ref[...], lb_ref[...])


def _scatter(pos_tok, y, x1, mods, ln_g, ln_b, cap, tm):
    R, n, ne = pos_tok.shape
    vec = pl.BlockSpec((1, D_MODEL), lambda r, i: (0, 0))
    return pl.pallas_call(
        functools.partial(_scatter_kernel, cap=cap),
        out_shape=jax.ShapeDtypeStruct((R, n, D_MODEL), F32),
        grid=(R, n // tm),
        in_specs=[pl.BlockSpec((None, tm, ne), lambda r, i: (r, i, 0)),
                  pl.BlockSpec((ne, cap, D_MODEL), lambda r, i: (0, r, 0)),
                  pl.BlockSpec((None, tm, D_MODEL), lambda r, i: (r, i, 0)),
                  _mod_spec(5, 2), vec, vec],
        out_specs=pl.BlockSpec((None, tm, D_MODEL), lambda r, i: (r, i, 0)),
        compiler_params=_cp(2), name="expert_scatter_ln",
    )(pos_tok, y, x1, mods, ln_g.reshape(1, -1), ln_b.reshape(1, -1))


def _capacity(n):
    return CAPACITY_FACTOR * n // N_EXPERTS


def _route(h2, aff):
    cap = _capacity(h2.shape[1])
    aff_t = jnp.swapaxes(aff, 1, 2)
    pos = _select(aff_t, cap)
    xe, gates = _gather(pos, aff_t, h2, cap)
    return pos, xe, gates


def _channel_mixers(groups, l, w_gate_up, w_down, ln2_g, ln2_b):
    routed = [_route(g["h2"], g["aff"]) for g in groups]
    ys = _experts(routed[0][1], routed[0][2], routed[1][1], routed[1][2], w_gate_up, w_down, l)
    return [_scatter(jnp.swapaxes(r[0], 1, 2), y, g["x1"], g["mods"], ln2_g[l], ln2_b[l],
                     _capacity(g["h2"].shape[1]), g["tm"]) for g, r, y in zip(groups, routed, ys)]


def kernel(x_prompt, x_sample, cache_attn_a_k, cache_attn_a_v, cache_attn_b_k, cache_attn_b_v, state_ret, c, c_ctx,
           w_ada, b_ada, w_in, w_out, attn_sink, na_rpb, ret_decay, ret_gn, ln1_g, ln1_b, ln2_g, ln2_b,
           w_router, w_gate_up, w_down):
    n_ctx, L_ctx, _ = x_prompt.shape
    n_lat, L_lat, _ = x_sample.shape
    P = cache_attn_a_k.shape[2]

    cond = jnp.concatenate([c_ctx[None, :], c, jnp.zeros((8 - 1 - n_lat, D_MODEL), F32)], axis=0)
    mod = _modulation(cond, w_ada, b_ada)
    w_in16, w_out16, w_r16 = w_in.astype(BF16), w_out.astype(BF16), w_router.astype(BF16)
    rope_tabs = _rope_tables(L_lat)
    cak = cache_attn_a_k.reshape(n_lat, DEPTH, P, KVH_A * HEAD_DIM)
    cav = cache_attn_a_v.reshape(n_lat, DEPTH, P, KVH_A * HEAD_DIM)
    cbk = cache_attn_b_k.reshape(n_lat, DEPTH, P, H_B * HEAD_DIM)
    cbv = cache_attn_b_v.reshape(n_lat, DEPTH, P, H_B * HEAD_DIM)
    s0_pairs = _heads_to_pair_states(state_ret)

    xc, xs = x_prompt, x_sample
    tm = ROW_TILE
    a_k, a_v, b_k, b_v, st = [], [], [], [], []
    wa, wkv, wb = H_A * HEAD_DIM, KVH_A * HEAD_DIM, H_B * HEAD_DIM
    flat = lambda a: a.reshape(1, n_ctx * L_ctx, a.shape[-1])
    unflat = lambda a: a.reshape(n_ctx, L_ctx, a.shape[-1])
    for l in range(DEPTH):
        mod_c = mod[l, 0:1][:, None, :]
        mods_c = jnp.broadcast_to(mod_c, (n_ctx, 1, 6 * D_MODEL))
        qkv = unflat(_project(flat(xc), mod_c, w_in16[l], tm))
        oa = _attn_a_ctx(qkv, attn_sink[l])
        ob = _attn_b_ctx(qkv)
        oc, s_l = _retention(qkv, ret_decay[l], ret_gn[l], out_state=True)
        x1, h2, aff = [unflat(t) for t in _outproj(flat(oa), flat(ob), flat(oc), flat(xc), mod_c, w_out16[l],
                                                   ln1_g[l], ln1_b[l], w_r16[l], tm)]
        ctx = dict(x1=x1, h2=h2, aff=aff, mods=mods_c, tm=L_ctx)
        a_k.append(qkv[..., wa:wa + wkv].reshape(n_ctx, L_ctx, KVH_A, HEAD_DIM))
        a_v.append(qkv[..., wa + wkv:wa + 2 * wkv].reshape(n_ctx, L_ctx, KVH_A, HEAD_DIM))
        ob0 = wa + 2 * wkv
        b_k.append(qkv[..., ob0 + wb:ob0 + 2 * wb].reshape(n_ctx, L_ctx, H_B, HEAD_DIM))
        b_v.append(qkv[..., ob0 + 2 * wb:ob0 + 3 * wb].reshape(n_ctx, L_ctx, H_B, HEAD_DIM))
        st.append(_pair_states_to_heads(s_l))

        mods_s = mod[l, 1:1 + n_lat][:, None, :]
        qkv = _project(xs, mods_s, w_in16[l], tm, rope_tabs)
        oa = _attn_a_lat(qkv, attn_sink[l], cak, cav, l)
        ob = _attn_b_lat(qkv, cbk, cbv, _bias_table(na_rpb[l]), l)
        oc = _retention(qkv, ret_decay[l], ret_gn[l], s0=s0_pairs, layer=l)
        x1, h2, aff = _outproj(oa, ob, oc, xs, mods_s, w_out16[l], ln1_g[l], ln1_b[l], w_r16[l], tm)
        lat = dict(x1=x1, h2=h2, aff=aff, mods=mods_s, tm=tm)

        xc, xs = _channel_mixers([ctx, lat], l, w_gate_up, w_down, ln2_g, ln2_b)

    return (xc, xs, jnp.stack(a_k, 1), jnp.stack(a_v, 1), jnp.stack(b_k, 1), jnp.stack(b_v, 1),
            jnp.stack(st, 1))
```

```python
import functools

import numpy as np
import jax
import jax.numpy as jnp
from jax import lax
from jax.experimental import pallas as pl
from jax.experimental.pallas import tpu as pltpu

F32 = jnp.float32
BF16 = jnp.bfloat16
I32 = jnp.int32

D_MODEL = 1024
DEPTH = 2
GRID_W = 64
HEAD_DIM = 64
H_A, KVH_A, H_B, H_C = 8, 2, 4, 4
D_IN = 2560
BLOCK = 128
WINDOW = 128
NA_ROWS, NA_COLS = 8, 16
NA_WIN_BLOCKS = 5
NA_QBLOCKS = 4
WIN_QBLOCKS = 2
ROPE_BASE = 10000.0
N_EXPERTS = 16
CAPACITY_FACTOR = 2
D_FF = 1024
LN_EPS = 1e-6
ALPHA = (2 * DEPTH) ** 0.25
NEG = -1e30
LANES = 128
QK_SCALE = HEAD_DIM ** -0.5
VMEM_LIMIT = 56 * 1024 * 1024
ROW_TILE = 1024

COL_QA, COL_KA, COL_VA = 0, 4, 5
COL_QB, COL_KB, COL_VB = 3, 4, 5
COL_QC, COL_KC, COL_VC, COL_GC = 6, 7, 8, 9


def _cp(n_axes):
    return pltpu.CompilerParams(dimension_semantics=("arbitrary",) * n_axes, vmem_limit_bytes=VMEM_LIMIT)


def _dot(a, b):
    return jnp.dot(a, b, preferred_element_type=F32)


def _dot_t(a, b):
    return lax.dot_general(a, b, (((1,), (1,)), ((), ())), preferred_element_type=F32)


def _silu(x):
    return x / (1.0 + jnp.exp(-x))


def _lane_low(shape):
    return lax.broadcasted_iota(I32, shape, len(shape) - 1) < HEAD_DIM


def _layernorm(y, g, b):
    mu = jnp.mean(y, axis=-1, keepdims=True)
    d = y - mu
    var = jnp.mean(d * d, axis=-1, keepdims=True)
    return d * lax.rsqrt(var + LN_EPS) * g + b


def _mod_kernel(c_ref, w_ref, b_ref, o_ref):
    s = _silu(c_ref[...]).astype(BF16)
    o_ref[...] = _dot(s, w_ref[...].astype(BF16)) + b_ref[...]


def _modulation(cond, w_ada, b_ada):
    tn = 1536
    return pl.pallas_call(
        _mod_kernel,
        out_shape=jax.ShapeDtypeStruct((DEPTH, 8, 6 * D_MODEL), F32),
        grid=(DEPTH, 6 * D_MODEL // tn),
        in_specs=[pl.BlockSpec((8, D_MODEL), lambda l, j: (0, 0)),
                  pl.BlockSpec((None, D_MODEL, tn), lambda l, j: (l, 0, j)),
                  pl.BlockSpec((None, 1, tn), lambda l, j: (l, 0, j))],
        out_specs=pl.BlockSpec((None, 8, tn), lambda l, j: (l, 0, j)),
        compiler_params=_cp(2), name="modulation",
    )(cond, w_ada, b_ada.reshape(DEPTH, 1, 6 * D_MODEL))


def _mod_spec(k, n_grid):
    if n_grid == 1:
        return pl.BlockSpec((None, 1, D_MODEL), lambda r: (r, 0, k))
    return pl.BlockSpec((None, 1, D_MODEL), lambda r, i: (r, 0, k))


def _rope_tables(L):
    t = np.arange(L)
    pos = (t // GRID_W, t % GRID_W)
    half = HEAD_DIM // 2
    inv = 1.0 / (ROPE_BASE ** (np.arange(0, half, 2) / half))
    cos = np.zeros((L, HEAD_DIM)); s_up = np.zeros((L, HEAD_DIM)); s_dn = np.zeros((L, HEAD_DIM))
    for part in range(2):
        ang = (pos[part][:, None] * inv[None, :]).astype(np.float32).astype(np.float64)
        q = half // 2
        base = part * half
        cos[:, base:base + q] = np.cos(ang); cos[:, base + q:base + half] = np.cos(ang)
        s_up[:, base:base + q] = -np.sin(ang)
        s_dn[:, base + q:base + half] = np.sin(ang)
    tile = lambda a: jnp.asarray(np.tile(a, (1, LANES // HEAD_DIM)), F32)
    return tile(cos), tile(s_up), tile(s_dn)


KV_COLUMNS = ((512, 128), (640, 128), (1024, 256), (1280, 256))


def _proj_kernel(*refs, rope, emit_kv):
    kv_refs = ()
    if emit_kv:
        refs, kv_refs = refs[:-len(KV_COLUMNS)], refs[-len(KV_COLUMNS):]
    if rope:
        x_ref, sc_ref, sh_ref, w_ref, cos_ref, sup_ref, sdn_ref, o_ref = refs
    else:
        x_ref, sc_ref, sh_ref, w_ref, o_ref = refs
    h = (x_ref[...] * (1.0 + sc_ref[...]) + sh_ref[...]).astype(BF16)
    n_rope = (H_A + KVH_A) * HEAD_DIM
    step = 512
    for j in range(0, D_IN, step):
        acc = _dot(h, w_ref[:, j:j + step])
        if rope and j < n_rope:
            q = HEAD_DIM // 4
            for c in range(0, step, LANES):
                a = acc[:, c:c + LANES]
                if j + c < n_rope:
                    a = (a * cos_ref[...] + pltpu.roll(a, LANES - q, 1) * sup_ref[...]
                         + pltpu.roll(a, q, 1) * sdn_ref[...])
                o_ref[:, j + c:j + c + LANES] = a
        else:
            o_ref[:, j:j + step] = acc
        for ref, (c0, w) in zip(kv_refs, KV_COLUMNS):
            if j <= c0 < j + step:
                ref[...] = acc[:, c0 - j:c0 - j + w]


def _project(x, mods, w_bf16, layer, tm, rope_tabs=None, emit_kv=False):
    R, L, _ = x.shape
    rope = rope_tabs is not None
    assert not (rope and emit_kv)
    out_shape = [jax.ShapeDtypeStruct((R, L, D_IN), F32)]
    out_specs = [pl.BlockSpec((None, tm, D_IN), lambda r, i: (r, i, 0))]
    if emit_kv:
        out_shape += [jax.ShapeDtypeStruct((R, L, w), F32) for _, w in KV_COLUMNS]
        out_specs += [pl.BlockSpec((None, tm, w), lambda r, i: (r, i, 0)) for _, w in KV_COLUMNS]
    in_specs = [pl.BlockSpec((None, tm, D_MODEL), lambda r, i: (r, i, 0)),
                _mod_spec(1, 2), _mod_spec(0, 2),
                pl.BlockSpec((None, D_MODEL, D_IN), lambda r, i: (layer, 0, 0))]
    args = [x, mods, mods, w_bf16]
    if rope:
        in_specs += [pl.BlockSpec((tm, LANES), lambda r, i: (i, 0))] * 3
        args += list(rope_tabs)
    res = pl.pallas_call(
        functools.partial(_proj_kernel, rope=rope, emit_kv=emit_kv),
        out_shape=out_shape, grid=(R, L // tm), in_specs=in_specs, out_specs=out_specs,
        compiler_params=_cp(2), name="in_proj",
    )(*args)
    return res if emit_kv else res[0]


def _mask_head(x, sub):
    low = _lane_low(x.shape)
    return jnp.where(low, x, 0.0) if sub == 0 else jnp.where(low, 0.0, x)


def _key_variant(k, sub):
    return _mask_head(k, sub).astype(BF16)


def _value_variant(v, sub):
    low = _lane_low(v.shape)
    return (jnp.where(low, v, 1.0) if sub == 0 else jnp.where(low, 1.0, v)).astype(BF16)


def _attend(lhs, keys, values, biases, sub, sink=None):
    scores = []
    for k, b in zip(keys, biases):
        s = _dot_t(lhs, k)
        scores.append(s if b is None else s + b)
    tiles = [s[:, c:c + LANES] for s in scores for c in range(0, s.shape[1], LANES)]
    while len(tiles) > 1:
        tiles = [jnp.maximum(a, b) for a, b in zip(tiles[0::2], tiles[1::2])] + ([tiles[-1]] if len(tiles) % 2 else [])
    m = jnp.broadcast_to(tiles[0].max(axis=-1, keepdims=True), tiles[0].shape)
    if sink is not None:
        m = jnp.maximum(m, sink)
    o = None
    for s, v in zip(scores, values):
        p = [jnp.exp(s[:, c:c + LANES] - m).astype(BF16) for c in range(0, s.shape[1], LANES)]
        pv = _dot(p[0] if len(p) == 1 else jnp.concatenate(p, axis=1), v)
        o = pv if o is None else o + pv
    if sink is not None:
        low = _lane_low(o.shape)
        extra = jnp.exp(sink - m)
        o = o + (jnp.where(low, 0.0, extra) if sub == 0 else jnp.where(low, extra, 0.0))
    return o / pltpu.roll(o, HEAD_DIM, 1)


def _gqa_variants(k, v, hk):
    low = _lane_low(k.shape)
    k_sw, v_sw = pltpu.roll(k, HEAD_DIM, 1), pltpu.roll(v, HEAD_DIM, 1)
    k_dup = jnp.where(low, k, k_sw) if hk == 0 else jnp.where(low, k_sw, k)
    return k_dup.astype(BF16), jnp.where(low, v if hk == 0 else v_sw, 1.0).astype(BF16)


def _fill_gqa_variants(k_src, v_src, kv_ref, vv_ref, off, n):
    step = 256
    for c in range(0, n, step):
        k = k_src[c:c + step, :]
        v = v_src[c:c + step, :]
        for hk in range(KVH_A):
            kd, va = _gqa_variants(k, v, hk)
            kv_ref[hk, off + c:off + c + step, :] = kd
            vv_ref[hk, off + c:off + c + step, :] = va


def _fill_mha_variants(k_src, v_src, kv_ref, vv_ref, off, n):
    step = 256
    for c in range(0, n, step):
        rows = slice(off + c, off + c + step)
        for p in range(H_B // 2):
            cs = slice(LANES * p, LANES * p + LANES)
            k = k_src[c:c + step, cs]
            v = v_src[c:c + step, cs]
            for sub in range(2):
                kv_ref[p, sub, rows, :] = _key_variant(k, sub)
                vv_ref[p, sub, rows, :] = _value_variant(v, sub)


def _sink_col(sink_ref, heads, rows):
    return jnp.concatenate([jnp.full((rows, LANES), sink_ref[h], F32) for h in heads], axis=0)


def _gqa_attend(sink_ref, q_ref, o_ref, hk, keys, values, biases):
    rows = q_ref.shape[0]
    g = H_A // KVH_A
    heads = list(range(g * hk, g * hk + g))
    lhs = jnp.concatenate([_mask_head(q_ref[:, LANES * (h // 2):LANES * (h // 2) + LANES], h % 2) for h in heads],
                          axis=0)
    lhs = (lhs * QK_SCALE).astype(BF16)
    stacked = [None if b is None else jnp.concatenate([b] * g, axis=0) for b in biases]
    out = _attend(lhs, keys, values, stacked, 0, _sink_col(sink_ref, heads, rows))
    low = _lane_low((rows, LANES))
    for j in range(0, g, 2):
        h = heads[j]
        odd = pltpu.roll(out[(j + 1) * rows:(j + 2) * rows], HEAD_DIM, 1)
        o_ref[:, LANES * (h // 2):LANES * (h // 2) + LANES] = jnp.where(low, out[j * rows:(j + 1) * rows], odd)


def _attn_a_ctx_kernel(sink_ref, q_ref, k_ref, v_ref, o_ref):
    for hk in range(KVH_A):
        kd, va = _gqa_variants(k_ref[...], v_ref[...], hk)
        _gqa_attend(sink_ref, q_ref, o_ref, hk, [kd], [va], [None])


def _window_mask_table():
    nq = WIN_QBLOCKS
    row = np.arange(nq * BLOCK)[:, None]
    col = np.arange((nq + 2) * BLOCK)[None, :]
    tabs = [np.where(np.abs(col - off * BLOCK - row) <= WINDOW, 0.0, NEG) for off in range(3)]
    return jnp.asarray(np.stack(tabs), F32)


def _attn_a_lat_kernel(sink_ref, q_ref, k_ref, v_ref, kc_ref, vc_ref, mask_ref, o_ref, kv_ref, vv_ref, *, nb):
    i = pl.program_id(1)
    L, P = k_ref.shape[0], kc_ref.shape[0]

    @pl.when(i == 0)
    def _():
        _fill_gqa_variants(k_ref, v_ref, kv_ref, vv_ref, 0, L)
        _fill_gqa_variants(kc_ref, vc_ref, kv_ref, vv_ref, L, P)

    nq = WIN_QBLOCKS
    wk = (nq + 2) * BLOCK
    sblk = jnp.clip(nq * i - 1, 0, nb - (nq + 2))
    start = pl.multiple_of(sblk * BLOCK, BLOCK)
    bias = mask_ref[nq * i - sblk]
    for hk in range(KVH_A):
        ks = [kv_ref[hk, pl.ds(start, wk), :], kv_ref[hk, L:L + P, :]]
        vs = [vv_ref[hk, pl.ds(start, wk), :], vv_ref[hk, L:L + P, :]]
        _gqa_attend(sink_ref, q_ref, o_ref, hk, ks, vs, [bias, None])


def _smem_spec():
    return pl.BlockSpec(memory_space=pltpu.SMEM)


def _attn_a_ctx(qkv, sink):
    R, L, _ = qkv.shape
    wq, wkv = H_A * HEAD_DIM, KVH_A * HEAD_DIM
    return pl.pallas_call(
        _attn_a_ctx_kernel,
        out_shape=jax.ShapeDtypeStruct((R, L, wq), F32),
        grid=(R,),
        in_specs=[_smem_spec(),
                  pl.BlockSpec((None, L, wq), lambda r: (r, 0, COL_QA)),
                  pl.BlockSpec((None, L, wkv), lambda r: (r, 0, COL_KA)),
                  pl.BlockSpec((None, L, wkv), lambda r: (r, 0, COL_VA))],
        out_specs=pl.BlockSpec((None, L, wq), lambda r: (r, 0, 0)),
        compiler_params=_cp(1), name="attn_a_ctx",
    )(sink, qkv, qkv, qkv)


def _attn_a_lat(qkv, sink, cache_k, cache_v, layer):
    R, L, _ = qkv.shape
    nb = L // BLOCK
    nq = WIN_QBLOCKS
    assert nb % nq == 0 and nb >= nq + 2
    P = cache_k.shape[2]
    wq, wkv = H_A * HEAD_DIM, KVH_A * HEAD_DIM
    cache_spec = pl.BlockSpec((None, None, P, wkv), lambda r, i: (r, layer, 0, 0))
    masks = _window_mask_table()
    return pl.pallas_call(
        functools.partial(_attn_a_lat_kernel, nb=nb),
        out_shape=jax.ShapeDtypeStruct((R, L, wq), F32),
        grid=(R, nb // nq),
        in_specs=[_smem_spec(),
                  pl.BlockSpec((None, nq * BLOCK, wq), lambda r, i: (r, i, COL_QA)),
                  pl.BlockSpec((None, L, wkv), lambda r, i: (r, 0, COL_KA)),
                  pl.BlockSpec((None, L, wkv), lambda r, i: (r, 0, COL_VA)),
                  cache_spec, cache_spec,
                  pl.BlockSpec(masks.shape, lambda r, i: (0, 0, 0))],
        out_specs=pl.BlockSpec((None, nq * BLOCK, wq), lambda r, i: (r, i, 0)),
        scratch_shapes=[pltpu.VMEM((KVH_A, L + P, LANES), BF16),
                        pltpu.VMEM((KVH_A, L + P, LANES), BF16)],
        compiler_params=_cp(2), name="attn_a_lat",
    )(sink, qkv, qkv, qkv, cache_k, cache_v, masks)


def _attn_b_ctx_kernel(q_ref, k_ref, v_ref, o_ref):
    L = q_ref.shape[0]
    low = _lane_low((L, LANES))
    for p in range(H_B // 2):
        cs = slice(LANES * p, LANES * p + LANES)
        lhs = (q_ref[:, cs] * QK_SCALE).astype(BF16)
        k, v = k_ref[:, cs], v_ref[:, cs]
        outs = [_attend(lhs, [_key_variant(k, sub)], [_value_variant(v, sub)], [None], sub) for sub in range(2)]
        o_ref[:, cs] = jnp.where(low, outs[0], outs[1])


def _bias_table_kernel(rpb_ref, o_ref):
    n_dr, n_dc = 2 * NA_ROWS - 1, 2 * NA_COLS - 1
    n_t = o_ref.shape[1]
    shape = (GRID_W, LANES)
    cq = lax.broadcasted_iota(I32, shape, 0)
    lane = lax.broadcasted_iota(I32, shape, 1)
    right = lane >= GRID_W
    ck = lane & (GRID_W - 1)
    cs = jnp.clip(cq - NA_COLS // 2, 0, GRID_W - NA_COLS)
    col_ok = (ck >= cs) & (ck < cs + NA_COLS)
    dc = ck - cq + (NA_COLS - 1)

    def body(t, carry, h):
        dr_l = t - (NA_ROWS + 1)
        acc = jnp.zeros(shape, F32)
        for d in range(n_dc):
            vals = []
            for dr in (dr_l, dr_l + 1):
                ok = (dr >= -(NA_ROWS - 1)) & (dr <= NA_ROWS - 1)
                idx = h * (n_dr * n_dc) + jnp.clip(dr + NA_ROWS - 1, 0, n_dr - 1) * n_dc + d
                vals.append(jnp.where(ok, rpb_ref[idx], 0.0))
            acc = jnp.where(dc == d, jnp.where(right, vals[1], vals[0]), acc)
        o_ref[h, t] = jnp.where(col_ok, acc, NEG)
        return carry

    for h in range(H_B):
        lax.fori_loop(0, n_t, functools.partial(body, h=h), 0)


def _bias_table(rpb_l):
    n_t = 2 * (NA_ROWS + 1)
    return pl.pallas_call(
        _bias_table_kernel,
        out_shape=jax.ShapeDtypeStruct((H_B, n_t, GRID_W, LANES), F32),
        in_specs=[_smem_spec()],
        out_specs=pl.BlockSpec(memory_space=pltpu.VMEM),
        name="na_bias_table",
    )(rpb_l.reshape(-1))


def _attn_b_lat_kernel(q_ref, k_ref, v_ref, kc_ref, vc_ref, tp_ref, o_ref, kv_ref, vv_ref, *, nb, rows_total):
    i = pl.program_id(1)
    L, P = k_ref.shape[0], kc_ref.shape[0]

    @pl.when(i == 0)
    def _():
        _fill_mha_variants(k_ref, v_ref, kv_ref, vv_ref, 0, L)
        _fill_mha_variants(kc_ref, vc_ref, kv_ref, vv_ref, L, P)

    nq = NA_QBLOCKS
    nwin = NA_WIN_BLOCKS + nq - 1
    wk = nwin * BLOCK
    rows = nq * BLOCK
    sblk = jnp.clip(nq * i - 2, 0, nb - nwin)
    start = pl.multiple_of(sblk * BLOCK, BLOCK)
    shape = (rows, wk)
    rq = 2 * nq * i + (lax.broadcasted_iota(I32, shape, 0) >> 6)
    rk = 2 * sblk + (lax.broadcasted_iota(I32, shape, 1) >> 6)
    rs = jnp.clip(rq - NA_ROWS // 2, 0, rows_total - NA_ROWS)
    row_bias = jnp.where((rk >= rs) & (rk < rs + NA_ROWS), 0.0, NEG)
    n_t = tp_ref.shape[1]
    low = _lane_low((rows, LANES))
    for p in range(H_B // 2):
        cs = slice(LANES * p, LANES * p + LANES)
        lhs = (q_ref[:, cs] * QK_SCALE).astype(BF16)
        outs = []
        for sub in range(2):
            h = 2 * p + sub
            strips = []
            for qq in range(nq):
                tiles = []
                for kb in range(nwin):
                    t = jnp.clip(2 * (sblk + kb - (nq * i + qq)) + NA_ROWS + 1, 1, n_t - 1)
                    tiles.append(jnp.concatenate([tp_ref[h, t], tp_ref[h, t - 1]], axis=0))
                strips.append(jnp.concatenate(tiles, axis=1))
            bias = jnp.concatenate(strips, axis=0) + row_bias
            ks = [kv_ref[p, sub, pl.ds(start, wk), :], kv_ref[p, sub, L:L + P, :]]
            vs = [vv_ref[p, sub, pl.ds(start, wk), :], vv_ref[p, sub, L:L + P, :]]
            outs.append(_attend(lhs, ks, vs, [bias, None], sub))
        o_ref[:, cs] = jnp.where(low, outs[0], outs[1])


def _attn_b_ctx(qkv):
    R, L, _ = qkv.shape
    w = H_B * HEAD_DIM
    return pl.pallas_call(
        _attn_b_ctx_kernel,
        out_shape=jax.ShapeDtypeStruct((R, L, w), F32),
        grid=(R,),
        in_specs=[pl.BlockSpec((None, L, w), lambda r: (r, 0, COL_QB)),
                  pl.BlockSpec((None, L, w), lambda r: (r, 0, COL_KB)),
                  pl.BlockSpec((None, L, w), lambda r: (r, 0, COL_VB))],
        out_specs=pl.BlockSpec((None, L, w), lambda r: (r, 0, 0)),
        compiler_params=_cp(1), name="attn_b_ctx",
    )(qkv, qkv, qkv)


def _attn_b_lat(qkv, cache_k, cache_v, bias_tab, layer):
    R, L, _ = qkv.shape
    nb = L // BLOCK
    P = cache_k.shape[2]
    w = H_B * HEAD_DIM
    cache_spec = pl.BlockSpec((None, None, P, w), lambda r, i: (r, layer, 0, 0))
    rows = NA_QBLOCKS * BLOCK
    assert nb % NA_QBLOCKS == 0 and nb >= NA_WIN_BLOCKS + NA_QBLOCKS - 1
    return pl.pallas_call(
        functools.partial(_attn_b_lat_kernel, nb=nb, rows_total=L // GRID_W),
        out_shape=jax.ShapeDtypeStruct((R, L, w), F32),
        grid=(R, nb // NA_QBLOCKS),
        in_specs=[pl.BlockSpec((None, rows, w), lambda r, i: (r, i, COL_QB)),
                  pl.BlockSpec((None, L, w), lambda r, i: (r, 0, COL_KB)),
                  pl.BlockSpec((None, L, w), lambda r, i: (r, 0, COL_VB)),
                  cache_spec, cache_spec,
                  pl.BlockSpec(bias_tab.shape, lambda r, i: (0, 0, 0, 0))],
        out_specs=pl.BlockSpec((None, rows, w), lambda r, i: (r, i, 0)),
        scratch_shapes=[pltpu.VMEM((H_B // 2, 2, L + P, LANES), BF16),
                        pltpu.VMEM((H_B // 2, 2, L + P, LANES), BF16)],
        compiler_params=_cp(2), name="attn_b_lat",
    )(qkv, qkv, qkv, cache_k, cache_v, bias_tab)


T_ZF, T_ZB, T_XF, T_XB, T_GF, T_GB, T_MA, T_MB = range(8)


def _ret_kernel(*refs, nc, has_s0, out_state):
    refs = list(refs)
    dec_ref, q_ref, k_ref, v_ref, g_ref, gn_ref = refs[:6]
    pos = 6
    s0_ref = None
    if has_s0:
        s0_ref = refs[pos]; pos += 1
    o_ref = refs[pos]; pos += 1
    st_ref = None
    if out_state:
        st_ref = refs[pos]; pos += 1
    sf_ref, sb_ref, tab_ref = refs[pos], refs[pos + 1], refs[pos + 2]

    sq = (BLOCK, LANES)
    low = _lane_low(sq)
    row_low = lax.broadcasted_iota(I32, sq, 0) < HEAD_DIM
    same_head = low == row_low
    n_pairs = H_C // 2

    @pl.when(pl.program_id(0) == 0)
    def _():
        ri = lax.broadcasted_iota(I32, sq, 0).astype(F32)
        ci = lax.broadcasted_iota(I32, sq, 1).astype(F32)

        def log_gamma(d, h):
            return -jnp.exp(jnp.full(sq, dec_ref[d, h], F32))

        for p in range(n_pairs):
            lf = [log_gamma(0, 2 * p), log_gamma(0, 2 * p + 1)]
            lb = [log_gamma(1, 2 * p), log_gamma(1, 2 * p + 1)]
            lf_lane, lb_lane = jnp.where(low, lf[0], lf[1]), jnp.where(low, lb[0], lb[1])
            lf_row, lb_row = jnp.where(row_low, lf[0], lf[1]), jnp.where(row_low, lb[0], lb[1])
            tab_ref[p, T_ZF] = jnp.exp(lf_lane * (BLOCK - 1.0 - ri))
            tab_ref[p, T_ZB] = jnp.exp(lb_lane * ri)
            tab_ref[p, T_XF] = jnp.exp(lf_lane * (ri + 1.0))
            tab_ref[p, T_XB] = jnp.exp(lb_lane * (BLOCK - ri))
            tab_ref[p, T_GF] = jnp.exp(lf_row * float(BLOCK))
            tab_ref[p, T_GB] = jnp.exp(lb_row * float(BLOCK))
            rel = ri - ci
            for sub in range(2):
                fwd = jnp.where(rel >= 0, jnp.exp(lf[sub] * jnp.maximum(rel, 0.0)), 0.0)
                bwd = jnp.where(rel <= 0, jnp.exp(lb[sub] * jnp.maximum(-rel, 0.0)), 0.0)
                tab_ref[p, T_MA + sub] = fwd + bwd

    def chunk(ref, c, p):
        r0 = pl.multiple_of(c * BLOCK, BLOCK)
        return ref[pl.ds(r0, BLOCK), LANES * p:LANES * p + LANES]

    unroll = min(nc, 4)

    def inc_body(c, carry):
        for p in range(n_pairs):
            kt = (chunk(k_ref, c, p) * QK_SCALE).T.astype(BF16)
            v = chunk(v_ref, c, p)
            zv = jnp.concatenate([v * tab_ref[p, T_ZF], v * tab_ref[p, T_ZB]], axis=1).astype(BF16)
            inc = _dot(kt, zv)
            sf_ref[c, p] = jnp.where(same_head, inc[:, :LANES], 0.0)
            sb_ref[c, p] = jnp.where(same_head, inc[:, LANES:], 0.0)
        return carry

    lax.fori_loop(0, nc, inc_body, 0, unroll=unroll)

    def scan_body(t, S, ref, tg, reverse):
        c = nc - 1 - t if reverse else t
        new = []
        for p in range(n_pairs):
            inc = ref[c, p]
            ref[c, p] = S[p]
            new.append(tab_ref[p, tg] * S[p] + inc)
        return tuple(new)

    zero = jnp.zeros(sq, F32)
    s_f0 = tuple(s0_ref[0, p] if has_s0 else zero for p in range(n_pairs))
    s_b0 = tuple(s0_ref[1, p] if has_s0 else zero for p in range(n_pairs))
    s_f = lax.fori_loop(0, nc, functools.partial(scan_body, ref=sf_ref, tg=T_GF, reverse=False), s_f0)
    s_b = lax.fori_loop(0, nc, functools.partial(scan_body, ref=sb_ref, tg=T_GB, reverse=True), s_b0)
    if out_state:
        for p in range(n_pairs):
            st_ref[0, p] = s_f[p]
            st_ref[1, p] = s_b[p]

    def out_body(c, carry):
        for p in range(n_pairs):
            q = chunk(q_ref, c, p)
            k16 = (chunk(k_ref, c, p) * QK_SCALE).astype(BF16)
            v = chunk(v_ref, c, p)
            a = _dot_t(jnp.concatenate([_mask_head(q, 0), _mask_head(q, 1)], axis=0).astype(BF16), k16)
            lhs = jnp.concatenate([a[:BLOCK] * tab_ref[p, T_MA], a[BLOCK:] * tab_ref[p, T_MA + 1],
                                   q * tab_ref[p, T_XF], q * tab_ref[p, T_XB]], axis=1).astype(BF16)
            rhs = jnp.concatenate([_mask_head(v, 0), _mask_head(v, 1), sf_ref[c, p], sb_ref[c, p]],
                                  axis=0).astype(BF16)
            o = _dot(lhs, rhs)
            mu = jnp.where(low, _mask_head(o, 0).sum(-1, keepdims=True),
                           _mask_head(o, 1).sum(-1, keepdims=True)) * (1.0 / HEAD_DIM)
            d = o - mu
            d2 = d * d
            var = jnp.where(low, _mask_head(d2, 0).sum(-1, keepdims=True),
                            _mask_head(d2, 1).sum(-1, keepdims=True)) * (1.0 / HEAD_DIM)
            cs = slice(LANES * p, LANES * p + LANES)
            on = d * lax.rsqrt(var + LN_EPS) * gn_ref[:, cs]
            r0 = pl.multiple_of(c * BLOCK, BLOCK)
            o_ref[pl.ds(r0, BLOCK), cs] = _silu(chunk(g_ref, c, p)) * on
        return carry

    lax.fori_loop(0, nc, out_body, 0, unroll=unroll)


def _retention(qkv, decay, gn_w, s0=None, layer=0, out_state=False):
    R, L, _ = qkv.shape
    nc = L // BLOCK
    w = H_C * HEAD_DIM
    n_pairs = H_C // 2
    col = lambda cb: pl.BlockSpec((None, L, w), lambda r: (r, 0, cb))
    in_specs = [_smem_spec(), col(COL_QC), col(COL_KC), col(COL_VC), col(COL_GC),
                pl.BlockSpec((1, w), lambda r: (0, 0))]
    args = [decay, qkv, qkv, qkv, qkv, gn_w.reshape(1, w)]
    if s0 is not None:
        in_specs.append(pl.BlockSpec((None, None, 2, n_pairs, BLOCK, LANES), lambda r: (r, layer, 0, 0, 0, 0)))
        args.append(s0)
    out_shape = [jax.ShapeDtypeStruct((R, L, w), F32)]
    out_specs = [pl.BlockSpec((None, L, w), lambda r: (r, 0, 0))]
    if out_state:
        out_shape.append(jax.ShapeDtypeStruct((R, 2, n_pairs, BLOCK, LANES), F32))
        out_specs.append(pl.BlockSpec((None, 2, n_pairs, BLOCK, LANES), lambda r: (r, 0, 0, 0, 0)))
    res = pl.pallas_call(
        functools.partial(_ret_kernel, nc=nc, has_s0=s0 is not None, out_state=out_state),
        out_shape=out_shape, grid=(R,), in_specs=in_specs, out_specs=out_specs,
        scratch_shapes=[pltpu.VMEM((nc, n_pairs, BLOCK, LANES), F32),
                        pltpu.VMEM((nc, n_pairs, BLOCK, LANES), F32),
                        pltpu.VMEM((n_pairs, 8, BLOCK, LANES), F32)],
        compiler_params=_cp(1), name="retention",
    )(*args)
    return res if out_state else res[0]


def _pair_states_to_heads(st):
    a = st[..., :HEAD_DIM, :HEAD_DIM]
    b = st[..., HEAD_DIM:, HEAD_DIM:]
    return jnp.stack([a, b], axis=3).reshape(st.shape[0], 2, H_C, HEAD_DIM, HEAD_DIM)


def _heads_to_pair_states(s):
    s = s.reshape(*s.shape[:3], H_C // 2, 2, HEAD_DIM, HEAD_DIM)
    z = jnp.zeros_like(s[..., 0, :, :])
    top = jnp.concatenate([s[..., 0, :, :], z], axis=-1)
    bot = jnp.concatenate([z, s[..., 1, :, :]], axis=-1)
    return jnp.concatenate([top, bot], axis=-2)


def _outproj_kernel(oa_ref, ob_ref, oc_ref, x_ref, g1_ref, sc2_ref, sh2_ref, w_ref, lg_ref, lb_ref, wr_ref,
                    x1_ref, h2_ref, aff_ref):
    wa, wb = H_A * HEAD_DIM, H_B * HEAD_DIM
    mix = (_dot(oa_ref[...].astype(BF16), w_ref[0:wa, :])
           + _dot(ob_ref[...].astype(BF16), w_ref[wa:wa + wb, :])
           + _dot(oc_ref[...].astype(BF16), w_ref[wa + wb:, :]))
    x1 = _layernorm(ALPHA * x_ref[...] + g1_ref[...] * mix, lg_ref[...], lb_ref[...])
    x1_ref[...] = x1
    h2 = (x1 * (1.0 + sc2_ref[...]) + sh2_ref[...]).astype(BF16)
    h2_ref[...] = h2
    logits = _dot(h2, wr_ref[...])
    e = jnp.exp(logits - logits.max(axis=-1, keepdims=True))
    aff_ref[...] = e / e.sum(axis=-1, keepdims=True)


def _outproj(oa, ob, oc, x, mods, w_out_bf16, ln_g, ln_b, w_router_bf16, layer, tm):
    R, L, _ = x.shape
    row = lambda w: pl.BlockSpec((None, tm, w), lambda r, i: (r, i, 0))
    vec = pl.BlockSpec((1, D_MODEL), lambda r, i: (0, 0))
    return pl.pallas_call(
        _outproj_kernel,
        out_shape=[jax.ShapeDtypeStruct((R, L, D_MODEL), F32),
                   jax.ShapeDtypeStruct((R, L, D_MODEL), BF16),
                   jax.ShapeDtypeStruct((R, L, N_EXPERTS), F32)],
        grid=(R, L // tm),
        in_specs=[row(H_A * HEAD_DIM), row(H_B * HEAD_DIM), row(H_C * HEAD_DIM), row(D_MODEL),
                  _mod_spec(2, 2), _mod_spec(4, 2), _mod_spec(3, 2),
                  pl.BlockSpec((None, D_MODEL, D_MODEL), lambda r, i: (layer, 0, 0)), vec, vec,
                  pl.BlockSpec((None, D_MODEL, N_EXPERTS), lambda r, i: (layer, 0, 0))],
        out_specs=[row(D_MODEL), row(D_MODEL), row(N_EXPERTS)],
        compiler_params=_cp(2), name="out_proj_ln_router",
    )(oa, ob, oc, x, mods, mods, mods, w_out_bf16, ln_g.reshape(1, -1), ln_b.reshape(1, -1), w_router_bf16)


CUMSUM_CHUNK = 256


def _excl_cumsum_lanes(x):
    rows, n = x.shape
    w = min(CUMSUM_CHUNK, n)
    tri = (lax.broadcasted_iota(I32, (w, w), 0) < lax.broadcasted_iota(I32, (w, w), 1))
    tri = jnp.where(tri, 1.0, 0.0).astype(BF16)
    carry = jnp.zeros((rows, 1), F32)
    outs = []
    for c in range(0, n, w):
        xc = x[:, c:c + w]
        outs.append(_dot(xc.astype(BF16), tri) + carry)
        carry = carry + xc.sum(axis=-1, keepdims=True)
    return outs[0] if len(outs) == 1 else jnp.concatenate(outs, axis=1)


def _select_kernel(a_ref, pos_ref, *, cap):
    rb, ne, n = a_ref.shape
    rows = rb * ne
    a = a_ref[...].reshape(rows, n)
    capf = float(cap)

    def body(_, c):
        lo, hi = c
        mid = lo + ((hi - lo + 1) >> 1)
        cnt = jnp.where(a >= pltpu.bitcast(mid, F32), 1.0, 0.0).sum(axis=-1, keepdims=True)
        ok = cnt >= capf
        return jnp.where(ok, mid, lo), jnp.where(ok, hi, mid - 1)

    one_bits = 0x3F800000
    lo, _ = lax.fori_loop(0, 30, body, (jnp.zeros((rows, 1), I32), jnp.full((rows, 1), one_bits, I32)))
    thr = pltpu.bitcast(lo, F32)
    gt = jnp.where(a > thr, 1.0, 0.0)
    eq = jnp.where(a == thr, 1.0, 0.0)
    need = capf - gt.sum(axis=-1, keepdims=True)
    sel = gt + eq * jnp.where(_excl_cumsum_lanes(eq) < need, 1.0, 0.0)
    slot = _excl_cumsum_lanes(sel)
    pos_ref[...] = jnp.where(sel > 0.5, slot, -1.0).astype(I32).reshape(rb, ne, n)


def _select(aff_t, cap):
    R, ne, n = aff_t.shape
    rb = min(R, max(1, 128 * 2048 // (ne * n)))
    return pl.pallas_call(
        functools.partial(_select_kernel, cap=cap),
        out_shape=jax.ShapeDtypeStruct((R, ne, n), I32),
        grid=(R // rb,),
        in_specs=[pl.BlockSpec((rb, ne, n), lambda r: (r, 0, 0))],
        out_specs=pl.BlockSpec((rb, ne, n), lambda r: (r, 0, 0)),
        compiler_params=_cp(1), name="expert_select",
    )(aff_t)


def _gather_kernel(pos_ref, aff_ref, h_ref, x_ref, gate_ref, *, cap):
    eb, _, n = pos_ref.shape
    slot = lax.broadcasted_iota(I32, (cap, n), 0)
    parts = []
    for j in range(eb):
        hit = slot == pos_ref[j]
        parts.append(jnp.where(hit, 1.0, 0.0).astype(BF16))
        gate_ref[j] = jnp.where(hit, aff_ref[j], 0.0).sum(axis=-1, keepdims=True)
    onehot = parts[0] if eb == 1 else jnp.concatenate(parts, axis=0)
    xe = _dot(onehot, h_ref[...]).astype(BF16)
    for j in range(eb):
        x_ref[j] = xe[j * cap:(j + 1) * cap]


def _gather(pos, aff_t, h2, cap):
    R, ne, n = pos.shape
    eb = max(1, min(ne, ROW_TILE // cap))
    row = pl.BlockSpec((None, eb, 1, n), lambda r, e: (r, e, 0, 0))
    return pl.pallas_call(
        functools.partial(_gather_kernel, cap=cap),
        out_shape=[jax.ShapeDtypeStruct((ne, R * cap, D_MODEL), BF16),
                   jax.ShapeDtypeStruct((ne, R * cap, 1), F32)],
        grid=(R, ne // eb),
        in_specs=[row, row, pl.BlockSpec((None, n, D_MODEL), lambda r, e: (r, 0, 0))],
        out_specs=[pl.BlockSpec((eb, cap, D_MODEL), lambda r, e: (e, r, 0)),
                   pl.BlockSpec((eb, cap, 1), lambda r, e: (e, r, 0))],
        compiler_params=_cp(2), name="expert_gather",
    )(pos.reshape(R, ne, 1, n), aff_t.reshape(R, ne, 1, n), h2)


FF_TILE = 512


def _ffn_kernel(x0_ref, x1_ref, wa_ref, wb_ref, wd_ref, g0_ref, g1_ref, y0_ref, y1_ref, acc_ref):
    f = pl.program_id(1)

    @pl.when(f == 0)
    def _():
        acc_ref[...] = jnp.zeros_like(acc_ref)

    x = jnp.concatenate([x0_ref[...], x1_ref[...]], axis=0)
    a = _dot(x, wa_ref[...].astype(BF16))
    b = _dot(x, wb_ref[...].astype(BF16))
    acc_ref[...] += _dot((_silu(a) * b).astype(BF16), wd_ref[...].astype(BF16))

    @pl.when(f == pl.num_programs(1) - 1)
    def _():
        s0 = x0_ref.shape[0]
        y0_ref[...] = (acc_ref[:s0, :] * g0_ref[...]).astype(BF16)
        y1_ref[...] = (acc_ref[s0:, :] * g1_ref[...]).astype(BF16)


def _experts(xe0, gates0, xe1, gates1, w_gate_up, w_down, layer):
    ne, S0, _ = xe0.shape
    S1 = xe1.shape[1]
    nf = D_FF // FF_TILE
    rows = lambda S, w: pl.BlockSpec((None, S, w), lambda e, f: (e, 0, 0))
    return pl.pallas_call(
        _ffn_kernel,
        out_shape=[jax.ShapeDtypeStruct((ne, S0, D_MODEL), BF16), jax.ShapeDtypeStruct((ne, S1, D_MODEL), BF16)],
        grid=(ne, nf),
        in_specs=[rows(S0, D_MODEL), rows(S1, D_MODEL),
                  pl.BlockSpec((None, None, D_MODEL, FF_TILE), lambda e, f: (layer, e, 0, f)),
                  pl.BlockSpec((None, None, D_MODEL, FF_TILE), lambda e, f: (layer, e, 0, nf + f)),
                  pl.BlockSpec((None, None, FF_TILE, D_MODEL), lambda e, f: (layer, e, f, 0)),
                  rows(S0, 1), rows(S1, 1)],
        out_specs=[rows(S0, D_MODEL), rows(S1, D_MODEL)],
        scratch_shapes=[pltpu.VMEM((S0 + S1, D_MODEL), F32)],
        compiler_params=_cp(2), name="expert_ffn",
    )(xe0, xe1, w_gate_up, w_gate_up, w_down, gates0, gates1)


def _scatter_kernel(pos_ref, y_ref, x1_ref, g2_ref, lg_ref, lb_ref, o_ref, *, cap):
    tm, ne = pos_ref.shape
    cw = max(cap, LANES)
    epc = cw // cap
    lane = lax.broadcasted_iota(I32, (tm, cw), 1)
    pos = pos_ref[...]
    acc = jnp.zeros((tm, D_MODEL), F32)
    for c in range(ne // epc):
        onehot = jnp.zeros((tm, cw), F32)
        for jj in range(epc):
            col = pos[:, c * epc + jj:c * epc + jj + 1]
            tgt = jnp.where(col >= 0, col + jj * cap, -1)
            onehot = onehot + jnp.where(lane == tgt, 1.0, 0.0)
        yc = y_ref[c] if epc == 1 else y_ref[c * epc:(c + 1) * epc].reshape(cw, D_MODEL)
        acc = acc + _dot(onehot.astype(BF16), yc)
    o_ref[...] = _layernorm(ALPHA * x1_ref[...] + g2_ref[...] * acc, lg_ref[...], lb_ref[...])


def _scatter(pos_tok, y, x1, mods, ln_g, ln_b, cap, tm):
    R, n, ne = pos_tok.shape
    vec = pl.BlockSpec((1, D_MODEL), lambda r, i: (0, 0))
    return pl.pallas_call(
        functools.partial(_scatter_kernel, cap=cap),
        out_shape=jax.ShapeDtypeStruct((R, n, D_MODEL), F32),
        grid=(R, n // tm),
        in_specs=[pl.BlockSpec((None, tm, ne), lambda r, i: (r, i, 0)),
                  pl.BlockSpec((ne, cap, D_MODEL), lambda r, i: (0, r, 0)),
                  pl.BlockSpec((None, tm, D_MODEL), lambda r, i: (r, i, 0)),
                  _mod_spec(5, 2), vec, vec],
        out_specs=pl.BlockSpec((None, tm, D_MODEL), lambda r, i: (r, i, 0)),
        compiler_params=_cp(2), name="expert_scatter_ln",
    )(pos_tok, y, x1, mods, ln_g.reshape(1, -1), ln_b.reshape(1, -1))


def _capacity(n):
    return CAPACITY_FACTOR * n // N_EXPERTS


def _route(h2, aff):
    cap = _capacity(h2.shape[1])
    aff_t = jnp.swapaxes(aff, 1, 2)
    pos = _select(aff_t, cap)
    xe, gates = _gather(pos, aff_t, h2, cap)
    return pos, xe, gates


def _channel_mixers(groups, l, w_gate_up, w_down, ln2_g, ln2_b):
    routed = [_route(g["h2"], g["aff"]) for g in groups]
    ys = _experts(routed[0][1], routed[0][2], routed[1][1], routed[1][2], w_gate_up, w_down, l)
    return [_scatter(jnp.swapaxes(r[0], 1, 2), y, g["x1"], g["mods"], ln2_g[l], ln2_b[l],
                     _capacity(g["h2"].shape[1]), g["tm"]) for g, r, y in zip(groups, routed, ys)]


def kernel(x_prompt, x_sample, cache_attn_a_k, cache_attn_a_v, cache_attn_b_k, cache_attn_b_v, state_ret, c, c_ctx,
           w_ada, b_ada, w_in, w_out, attn_sink, na_rpb, ret_decay, ret_gn, ln1_g, ln1_b, ln2_g, ln2_b,
           w_router, w_gate_up, w_down):
    n_ctx, L_ctx, _ = x_prompt.shape
    n_lat, L_lat, _ = x_sample.shape
    P = cache_attn_a_k.shape[2]

    cond = jnp.concatenate([c_ctx[None, :], c, jnp.zeros((8 - 1 - n_lat, D_MODEL), F32)], axis=0)
    mod = _modulation(cond, w_ada, b_ada)
    w_in16, w_out16, w_r16 = w_in.astype(BF16), w_out.astype(BF16), w_router.astype(BF16)
    rope_tabs = _rope_tables(L_lat)
    cak = cache_attn_a_k.reshape(n_lat, DEPTH, P, KVH_A * HEAD_DIM)
    cav = cache_attn_a_v.reshape(n_lat, DEPTH, P, KVH_A * HEAD_DIM)
    cbk = cache_attn_b_k.reshape(n_lat, DEPTH, P, H_B * HEAD_DIM)
    cbv = cache_attn_b_v.reshape(n_lat, DEPTH, P, H_B * HEAD_DIM)
    s0_pairs = _heads_to_pair_states(state_ret)

    xc, xs = x_prompt, x_sample
    tm = ROW_TILE
    a_k, a_v, b_k, b_v, st = [], [], [], [], []
    flat = lambda a: a.reshape(1, n_ctx * L_ctx, a.shape[-1])
    unflat = lambda a: a.reshape(n_ctx, L_ctx, a.shape[-1])
    for l in range(DEPTH):
        mod_c = mod[l, 0:1][:, None, :]
        mods_c = jnp.broadcast_to(mod_c, (n_ctx, 1, 6 * D_MODEL))
        qkv, ka, va, kb, vb = [unflat(t) for t in _project(flat(xc), mod_c, w_in16, l, tm, emit_kv=True)]
        oa = _attn_a_ctx(qkv, attn_sink[l])
        ob = _attn_b_ctx(qkv)
        oc, s_l = _retention(qkv, ret_decay[l], ret_gn[l], out_state=True)
        x1, h2, aff = [unflat(t) for t in _outproj(flat(oa), flat(ob), flat(oc), flat(xc), mod_c, w_out16,
                                                   ln1_g[l], ln1_b[l], w_r16, l, tm)]
        ctx = dict(x1=x1, h2=h2, aff=aff, mods=mods_c, tm=L_ctx)
        a_k.append(ka.reshape(n_ctx, L_ctx, KVH_A, HEAD_DIM))
        a_v.append(va.reshape(n_ctx, L_ctx, KVH_A, HEAD_DIM))
        b_k.append(kb.reshape(n_ctx, L_ctx, H_B, HEAD_DIM))
        b_v.append(vb.reshape(n_ctx, L_ctx, H_B, HEAD_DIM))
        st.append(_pair_states_to_heads(s_l))

        mods_s = mod[l, 1:1 + n_lat][:, None, :]
        qkv = _project(xs, mods_s, w_in16, l, tm, rope_tabs)
        oa = _attn_a_lat(qkv, attn_sink[l], cak, cav, l)
        ob = _attn_b_lat(qkv, cbk, cbv, _bias_table(na_rpb[l]), l)
        oc = _retention(qkv, ret_decay[l], ret_gn[l], s0=s0_pairs, layer=l)
        x1, h2, aff = _outproj(oa, ob, oc, xs, mods_s, w_out16, ln1_g[l], ln1_b[l], w_r16, l, tm)
        lat = dict(x1=x1, h2=h2, aff=aff, mods=mods_s, tm=tm)

        xc, xs = _channel_mixers([ctx, lat], l, w_gate_up, w_down, ln2_g, ln2_b)

    return (xc, xs, jnp.stack(a_k, 1), jnp.stack(a_v, 1), jnp.stack(b_k, 1), jnp.stack(b_v, 1),
            jnp.stack(st, 1))
```

```python
import functools

import numpy as np
import jax
import jax.numpy as jnp
from jax import lax
from jax.experimental import pallas as pl
from jax.experimental.pallas import tpu as pltpu

F32 = jnp.float32
BF16 = jnp.bfloat16
I32 = jnp.int32

D_MODEL = 1024
DEPTH = 2
GRID_W = 64
HEAD_DIM = 64
H_A, KVH_A, H_B, H_C = 8, 2, 4, 4
D_IN = 2560
BLOCK = 128
WINDOW = 128
NA_ROWS, NA_COLS = 8, 16
NA_WIN_BLOCKS = 5
NA_QBLOCKS = 4
WIN_QBLOCKS = 2
CTX_GROUP = 2
ROPE_BASE = 10000.0
N_EXPERTS = 16
CAPACITY_FACTOR = 2
D_FF = 1024
LN_EPS = 1e-6
ALPHA = (2 * DEPTH) ** 0.25
NEG = -1e30
LANES = 128
QK_SCALE = HEAD_DIM ** -0.5
LOG2E = float(np.log2(np.e))
ATTN_SCALE = QK_SCALE * LOG2E
VMEM_LIMIT = 56 * 1024 * 1024
ROW_TILE = 1024

COL_QA, COL_KA, COL_VA = 0, 4, 5
COL_QB, COL_KB, COL_VB = 3, 4, 5
COL_QC, COL_KC, COL_VC, COL_GC = 6, 7, 8, 9


def _cp(n_axes):
    return pltpu.CompilerParams(dimension_semantics=("arbitrary",) * n_axes, vmem_limit_bytes=VMEM_LIMIT)


def _dot(a, b):
    return jnp.dot(a, b, preferred_element_type=F32)


def _dot_t(a, b):
    return lax.dot_general(a, b, (((1,), (1,)), ((), ())), preferred_element_type=F32)


def _silu(x):
    return x / (1.0 + jnp.exp(-x))


def _lane_low(shape):
    return lax.broadcasted_iota(I32, shape, len(shape) - 1) < HEAD_DIM


def _layernorm(y, g, b):
    mu = jnp.mean(y, axis=-1, keepdims=True)
    d = y - mu
    var = jnp.mean(d * d, axis=-1, keepdims=True)
    return d * lax.rsqrt(var + LN_EPS) * g + b


def _mod_kernel(c_ref, w_ref, b_ref, o_ref):
    s = _silu(c_ref[...]).astype(BF16)
    o_ref[...] = _dot(s, w_ref[...].astype(BF16)) + b_ref[...]


def _modulation(cond, w_ada, b_ada):
    tn = 1536
    return pl.pallas_call(
        _mod_kernel,
        out_shape=jax.ShapeDtypeStruct((DEPTH, 8, 6 * D_MODEL), F32),
        grid=(DEPTH, 6 * D_MODEL // tn),
        in_specs=[pl.BlockSpec((8, D_MODEL), lambda l, j: (0, 0)),
                  pl.BlockSpec((None, D_MODEL, tn), lambda l, j: (l, 0, j)),
                  pl.BlockSpec((None, 1, tn), lambda l, j: (l, 0, j))],
        out_specs=pl.BlockSpec((None, 8, tn), lambda l, j: (l, 0, j)),
        compiler_params=_cp(2), name="modulation",
    )(cond, w_ada, b_ada.reshape(DEPTH, 1, 6 * D_MODEL))


def _mod_spec(k, n_grid):
    if n_grid == 1:
        return pl.BlockSpec((None, 1, D_MODEL), lambda r: (r, 0, k))
    return pl.BlockSpec((None, 1, D_MODEL), lambda r, i: (r, 0, k))


def _rope_tables(L):
    t = np.arange(L)
    pos = (t // GRID_W, t % GRID_W)
    half = HEAD_DIM // 2
    inv = 1.0 / (ROPE_BASE ** (np.arange(0, half, 2) / half))
    cos = np.zeros((L, HEAD_DIM)); s_up = np.zeros((L, HEAD_DIM)); s_dn = np.zeros((L, HEAD_DIM))
    for part in range(2):
        ang = (pos[part][:, None] * inv[None, :]).astype(np.float32).astype(np.float64)
        q = half // 2
        base = part * half
        cos[:, base:base + q] = np.cos(ang); cos[:, base + q:base + half] = np.cos(ang)
        s_up[:, base:base + q] = -np.sin(ang)
        s_dn[:, base + q:base + half] = np.sin(ang)
    tile = lambda a: jnp.asarray(np.tile(a, (1, LANES // HEAD_DIM)), F32)
    return tile(cos), tile(s_up), tile(s_dn)


KV_COLUMNS = ((512, 128), (640, 128), (1024, 256), (1280, 256))


def _proj_kernel(*refs, rope, emit_kv):
    kv_refs = ()
    if emit_kv:
        refs, kv_refs = refs[:-len(KV_COLUMNS)], refs[-len(KV_COLUMNS):]
    if rope:
        x_ref, sc_ref, sh_ref, w_ref, cos_ref, sup_ref, sdn_ref, o_ref = refs
    else:
        x_ref, sc_ref, sh_ref, w_ref, o_ref = refs
    h = (x_ref[...] * (1.0 + sc_ref[...]) + sh_ref[...]).astype(BF16)
    n_rope = (H_A + KVH_A) * HEAD_DIM
    step = 512
    for j in range(0, D_IN, step):
        acc = _dot(h, w_ref[:, j:j + step])
        if rope and j < n_rope:
            q = HEAD_DIM // 4
            for c in range(0, step, LANES):
                a = acc[:, c:c + LANES]
                if j + c < n_rope:
                    a = (a * cos_ref[...] + pltpu.roll(a, LANES - q, 1) * sup_ref[...]
                         + pltpu.roll(a, q, 1) * sdn_ref[...])
                o_ref[:, j + c:j + c + LANES] = a
        else:
            o_ref[:, j:j + step] = acc
        for ref, (c0, w) in zip(kv_refs, KV_COLUMNS):
            if j <= c0 < j + step:
                ref[...] = acc[:, c0 - j:c0 - j + w]


def _project(x, mods, w_bf16, layer, tm, rope_tabs=None, emit_kv=False):
    R, L, _ = x.shape
    rope = rope_tabs is not None
    assert not (rope and emit_kv)
    out_shape = [jax.ShapeDtypeStruct((R, L, D_IN), F32)]
    out_specs = [pl.BlockSpec((None, tm, D_IN), lambda r, i: (r, i, 0))]
    if emit_kv:
        out_shape += [jax.ShapeDtypeStruct((R, L, w), F32) for _, w in KV_COLUMNS]
        out_specs += [pl.BlockSpec((None, tm, w), lambda r, i: (r, i, 0)) for _, w in KV_COLUMNS]
    in_specs = [pl.BlockSpec((None, tm, D_MODEL), lambda r, i: (r, i, 0)),
                _mod_spec(1, 2), _mod_spec(0, 2),
                pl.BlockSpec((None, D_MODEL, D_IN), lambda r, i: (layer, 0, 0))]
    args = [x, mods, mods, w_bf16]
    if rope:
        in_specs += [pl.BlockSpec((tm, LANES), lambda r, i: (i, 0))] * 3
        args += list(rope_tabs)
    res = pl.pallas_call(
        functools.partial(_proj_kernel, rope=rope, emit_kv=emit_kv),
        out_shape=out_shape, grid=(R, L // tm), in_specs=in_specs, out_specs=out_specs,
        compiler_params=_cp(2), name="in_proj",
    )(*args)
    return res if emit_kv else res[0]


def _mask_head(x, sub):
    low = _lane_low(x.shape)
    return jnp.where(low, x, 0.0) if sub == 0 else jnp.where(low, 0.0, x)


def _key_variant(k, sub):
    return _mask_head(k, sub).astype(BF16)


def _value_variant(v, sub):
    low = _lane_low(v.shape)
    return (jnp.where(low, v, 1.0) if sub == 0 else jnp.where(low, 1.0, v)).astype(BF16)


def _attend(lhs, keys, values, biases, sub, sink=None):
    scores = []
    for k, b in zip(keys, biases):
        s = _dot_t(lhs, k)
        scores.append(s if b is None else s + b)
    tiles = [s[:, c:c + LANES] for s in scores for c in range(0, s.shape[1], LANES)]
    while len(tiles) > 1:
        tiles = [jnp.maximum(a, b) for a, b in zip(tiles[0::2], tiles[1::2])] + ([tiles[-1]] if len(tiles) % 2 else [])
    m = jnp.broadcast_to(tiles[0].max(axis=-1, keepdims=True), tiles[0].shape)
    if sink is not None:
        m = jnp.maximum(m, sink)
    o = None
    for s, v in zip(scores, values):
        p = [jnp.exp2(s[:, c:c + LANES] - m).astype(BF16) for c in range(0, s.shape[1], LANES)]
        pv = _dot(p[0] if len(p) == 1 else jnp.concatenate(p, axis=1), v)
        o = pv if o is None else o + pv
    if sink is not None:
        low = _lane_low(o.shape)
        extra = jnp.exp2(sink - m)
        o = o + (jnp.where(low, 0.0, extra) if sub == 0 else jnp.where(low, extra, 0.0))
    return o / pltpu.roll(o, HEAD_DIM, 1)


def _gqa_variants(k, v, hk):
    low = _lane_low(k.shape)
    k_sw, v_sw = pltpu.roll(k, HEAD_DIM, 1), pltpu.roll(v, HEAD_DIM, 1)
    k_dup = jnp.where(low, k, k_sw) if hk == 0 else jnp.where(low, k_sw, k)
    return k_dup.astype(BF16), jnp.where(low, v if hk == 0 else v_sw, 1.0).astype(BF16)


def _fill_gqa_variants(k_src, v_src, kv_ref, vv_ref, off, n):
    step = 256
    for c in range(0, n, step):
        k = k_src[c:c + step, :]
        v = v_src[c:c + step, :]
        for hk in range(KVH_A):
            kd, va = _gqa_variants(k, v, hk)
            kv_ref[hk, off + c:off + c + step, :] = kd
            vv_ref[hk, off + c:off + c + step, :] = va


def _fill_mha_variants(k_src, v_src, kv_ref, vv_ref, off, n):
    step = 256
    for c in range(0, n, step):
        rows = slice(off + c, off + c + step)
        for p in range(H_B // 2):
            cs = slice(LANES * p, LANES * p + LANES)
            k = k_src[c:c + step, cs]
            v = v_src[c:c + step, cs]
            for sub in range(2):
                kv_ref[p, sub, rows, :] = _key_variant(k, sub)
                vv_ref[p, sub, rows, :] = _value_variant(v, sub)


def _sink_col(sink_ref, heads, rows):
    return jnp.concatenate([jnp.full((rows, LANES), sink_ref[h] * LOG2E, F32) for h in heads], axis=0)


def _gqa_attend(sink_ref, q_ref, o_ref, hk, keys, values, biases):
    rows = q_ref.shape[0]
    g = H_A // KVH_A
    heads = list(range(g * hk, g * hk + g))
    lhs = jnp.concatenate([_mask_head(q_ref[:, LANES * (h // 2):LANES * (h // 2) + LANES], h % 2) for h in heads],
                          axis=0)
    lhs = (lhs * ATTN_SCALE).astype(BF16)
    stacked = [None if b is None else jnp.concatenate([b] * g, axis=0) for b in biases]
    out = _attend(lhs, keys, values, stacked, 0, _sink_col(sink_ref, heads, rows))
    low = _lane_low((rows, LANES))
    for j in range(0, g, 2):
        h = heads[j]
        odd = pltpu.roll(out[(j + 1) * rows:(j + 2) * rows], HEAD_DIM, 1)
        o_ref[:, LANES * (h // 2):LANES * (h // 2) + LANES] = jnp.where(low, out[j * rows:(j + 1) * rows], odd)


def _attn_a_ctx_kernel(sink_ref, q_ref, k_ref, v_ref, o_ref):
    for g in range(q_ref.shape[0]):
        for hk in range(KVH_A):
            kd, va = _gqa_variants(k_ref[g], v_ref[g], hk)
            _gqa_attend(sink_ref, q_ref.at[g], o_ref.at[g], hk, [kd], [va], [None])


def _window_mask_table():
    nq = WIN_QBLOCKS
    row = np.arange(nq * BLOCK)[:, None]
    col = np.arange((nq + 2) * BLOCK)[None, :]
    tabs = [np.where(np.abs(col - off * BLOCK - row) <= WINDOW, 0.0, NEG) for off in range(3)]
    return jnp.asarray(np.stack(tabs), F32)


def _attn_a_lat_kernel(sink_ref, q_ref, k_ref, v_ref, kc_ref, vc_ref, mask_ref, o_ref, kv_ref, vv_ref, *, nb):
    i = pl.program_id(1)
    L, P = k_ref.shape[0], kc_ref.shape[0]

    @pl.when(i == 0)
    def _():
        _fill_gqa_variants(k_ref, v_ref, kv_ref, vv_ref, 0, L)
        _fill_gqa_variants(kc_ref, vc_ref, kv_ref, vv_ref, L, P)

    nq = WIN_QBLOCKS
    wk = (nq + 2) * BLOCK
    sblk = jnp.clip(nq * i - 1, 0, nb - (nq + 2))
    start = pl.multiple_of(sblk * BLOCK, BLOCK)
    bias = mask_ref[nq * i - sblk]
    for hk in range(KVH_A):
        ks = [kv_ref[hk, pl.ds(start, wk), :], kv_ref[hk, L:L + P, :]]
        vs = [vv_ref[hk, pl.ds(start, wk), :], vv_ref[hk, L:L + P, :]]
        _gqa_attend(sink_ref, q_ref, o_ref, hk, ks, vs, [bias, None])


def _smem_spec():
    return pl.BlockSpec(memory_space=pltpu.SMEM)


def _attn_a_ctx(qkv, sink):
    R, L, _ = qkv.shape
    wq, wkv = H_A * HEAD_DIM, KVH_A * HEAD_DIM
    G = CTX_GROUP
    assert R % G == 0
    return pl.pallas_call(
        _attn_a_ctx_kernel,
        out_shape=jax.ShapeDtypeStruct((R, L, wq), F32),
        grid=(R // G,),
        in_specs=[_smem_spec(),
                  pl.BlockSpec((G, L, wq), lambda r: (r, 0, COL_QA)),
                  pl.BlockSpec((G, L, wkv), lambda r: (r, 0, COL_KA)),
                  pl.BlockSpec((G, L, wkv), lambda r: (r, 0, COL_VA))],
        out_specs=pl.BlockSpec((G, L, wq), lambda r: (r, 0, 0)),
        compiler_params=_cp(1), name="attn_a_ctx",
    )(sink, qkv, qkv, qkv)


def _attn_a_lat(qkv, sink, cache_k, cache_v, layer):
    R, L, _ = qkv.shape
    nb = L // BLOCK
    nq = WIN_QBLOCKS
    assert nb % nq == 0 and nb >= nq + 2
    P = cache_k.shape[2]
    wq, wkv = H_A * HEAD_DIM, KVH_A * HEAD_DIM
    cache_spec = pl.BlockSpec((None, None, P, wkv), lambda r, i: (r, layer, 0, 0))
    masks = _window_mask_table()
    return pl.pallas_call(
        functools.partial(_attn_a_lat_kernel, nb=nb),
        out_shape=jax.ShapeDtypeStruct((R, L, wq), F32),
        grid=(R, nb // nq),
        in_specs=[_smem_spec(),
                  pl.BlockSpec((None, nq * BLOCK, wq), lambda r, i: (r, i, COL_QA)),
                  pl.BlockSpec((None, L, wkv), lambda r, i: (r, 0, COL_KA)),
                  pl.BlockSpec((None, L, wkv), lambda r, i: (r, 0, COL_VA)),
                  cache_spec, cache_spec,
                  pl.BlockSpec(masks.shape, lambda r, i: (0, 0, 0))],
        out_specs=pl.BlockSpec((None, nq * BLOCK, wq), lambda r, i: (r, i, 0)),
        scratch_shapes=[pltpu.VMEM((KVH_A, L + P, LANES), BF16),
                        pltpu.VMEM((KVH_A, L + P, LANES), BF16)],
        compiler_params=_cp(2), name="attn_a_lat",
    )(sink, qkv, qkv, qkv, cache_k, cache_v, masks)


def _attn_b_ctx_kernel(q_ref, k_ref, v_ref, o_ref):
    G, L = q_ref.shape[:2]
    low = _lane_low((L, LANES))
    for g in range(G):
        for p in range(H_B // 2):
            cs = slice(LANES * p, LANES * p + LANES)
            lhs = (q_ref[g, :, cs] * ATTN_SCALE).astype(BF16)
            k, v = k_ref[g, :, cs], v_ref[g, :, cs]
            outs = [_attend(lhs, [_key_variant(k, sub)], [_value_variant(v, sub)], [None], sub)
                    for sub in range(2)]
            o_ref[g, :, cs] = jnp.where(low, outs[0], outs[1])


def _bias_table_kernel(rpb_ref, o_ref, t_ref):
    n_dr, n_dc = 2 * NA_ROWS - 1, 2 * NA_COLS - 1
    n_t = o_ref.shape[1]
    shape = (GRID_W, LANES)
    cq = lax.broadcasted_iota(I32, shape, 0)
    lane = lax.broadcasted_iota(I32, shape, 1)
    right = lane >= GRID_W
    ck = lane & (GRID_W - 1)
    cs = jnp.clip(cq - NA_COLS // 2, 0, GRID_W - NA_COLS)
    col_ok = (ck >= cs) & (ck < cs + NA_COLS)
    dc = ck - cq + (NA_COLS - 1)
    for j in (0, 1, n_t - 1, n_t):
        t_ref[j] = jnp.zeros(shape, F32)

    def build(r, carry, h):
        acc = jnp.zeros(shape, F32)
        for d in range(n_dc):
            acc = jnp.where(dc == d, rpb_ref[(h * n_dr + r) * n_dc + d] * LOG2E, acc)
        t_ref[r + 2] = jnp.where(col_ok, acc, NEG)
        return carry

    def compose(t, carry, h):
        o_ref[h, t] = jnp.where(right, t_ref[t + 1], t_ref[t])
        return carry

    for h in range(H_B):
        lax.fori_loop(0, n_dr, functools.partial(build, h=h), 0)
        lax.fori_loop(0, n_t, functools.partial(compose, h=h), 0)


def _bias_table(rpb_l):
    n_t = 2 * (NA_ROWS + 1)
    return pl.pallas_call(
        _bias_table_kernel,
        out_shape=jax.ShapeDtypeStruct((H_B, n_t, GRID_W, LANES), F32),
        in_specs=[_smem_spec()],
        out_specs=pl.BlockSpec(memory_space=pltpu.VMEM),
        scratch_shapes=[pltpu.VMEM((n_t + 1, GRID_W, LANES), F32)],
        name="na_bias_table",
    )(rpb_l.reshape(-1))


def _attn_b_lat_kernel(q_ref, k_ref, v_ref, kc_ref, vc_ref, tp_ref, o_ref, kv_ref, vv_ref, *, nb, rows_total):
    i = pl.program_id(1)
    L, P = k_ref.shape[0], kc_ref.shape[0]

    @pl.when(i == 0)
    def _():
        _fill_mha_variants(k_ref, v_ref, kv_ref, vv_ref, 0, L)
        _fill_mha_variants(kc_ref, vc_ref, kv_ref, vv_ref, L, P)

    nq = NA_QBLOCKS
    nwin = NA_WIN_BLOCKS + nq - 1
    wk = nwin * BLOCK
    rows = nq * BLOCK
    sblk = jnp.clip(nq * i - 2, 0, nb - nwin)
    start = pl.multiple_of(sblk * BLOCK, BLOCK)
    shape = (rows, wk)
    rq = 2 * nq * i + (lax.broadcasted_iota(I32, shape, 0) >> 6)
    rk = 2 * sblk + (lax.broadcasted_iota(I32, shape, 1) >> 6)
    rs = jnp.clip(rq - NA_ROWS // 2, 0, rows_total - NA_ROWS)
    row_bias = jnp.where((rk >= rs) & (rk < rs + NA_ROWS), 0.0, NEG)
    n_t = tp_ref.shape[1]
    low = _lane_low((rows, LANES))
    for p in range(H_B // 2):
        cs = slice(LANES * p, LANES * p + LANES)
        lhs = (q_ref[:, cs] * ATTN_SCALE).astype(BF16)
        outs = []
        for sub in range(2):
            h = 2 * p + sub
            strips = []
            for qq in range(nq):
                tiles = []
                for kb in range(nwin):
                    t = jnp.clip(2 * (sblk + kb - (nq * i + qq)) + NA_ROWS + 1, 1, n_t - 1)
                    tiles.append(jnp.concatenate([tp_ref[h, t], tp_ref[h, t - 1]], axis=0))
                strips.append(jnp.concatenate(tiles, axis=1))
            bias = jnp.concatenate(strips, axis=0) + row_bias
            ks = [kv_ref[p, sub, pl.ds(start, wk), :], kv_ref[p, sub, L:L + P, :]]
            vs = [vv_ref[p, sub, pl.ds(start, wk), :], vv_ref[p, sub, L:L + P, :]]
            outs.append(_attend(lhs, ks, vs, [bias, None], sub))
        o_ref[:, cs] = jnp.where(low, outs[0], outs[1])


def _attn_b_ctx(qkv):
    R, L, _ = qkv.shape
    w = H_B * HEAD_DIM
    G = CTX_GROUP
    assert R % G == 0
    return pl.pallas_call(
        _attn_b_ctx_kernel,
        out_shape=jax.ShapeDtypeStruct((R, L, w), F32),
        grid=(R // G,),
        in_specs=[pl.BlockSpec((G, L, w), lambda r: (r, 0, COL_QB)),
                  pl.BlockSpec((G, L, w), lambda r: (r, 0, COL_KB)),
                  pl.BlockSpec((G, L, w), lambda r: (r, 0, COL_VB))],
        out_specs=pl.BlockSpec((G, L, w), lambda r: (r, 0, 0)),
        compiler_params=_cp(1), name="attn_b_ctx",
    )(qkv, qkv, qkv)


def _attn_b_lat(qkv, cache_k, cache_v, bias_tab, layer):
    R, L, _ = qkv.shape
    nb = L // BLOCK
    P = cache_k.shape[2]
    w = H_B * HEAD_DIM
    cache_spec = pl.BlockSpec((None, None, P, w), lambda r, i: (r, layer, 0, 0))
    rows = NA_QBLOCKS * BLOCK
    assert nb % NA_QBLOCKS == 0 and nb >= NA_WIN_BLOCKS + NA_QBLOCKS - 1
    return pl.pallas_call(
        functools.partial(_attn_b_lat_kernel, nb=nb, rows_total=L // GRID_W),
        out_shape=jax.ShapeDtypeStruct((R, L, w), F32),
        grid=(R, nb // NA_QBLOCKS),
        in_specs=[pl.BlockSpec((None, rows, w), lambda r, i: (r, i, COL_QB)),
                  pl.BlockSpec((None, L, w), lambda r, i: (r, 0, COL_KB)),
                  pl.BlockSpec((None, L, w), lambda r, i: (r, 0, COL_VB)),
                  cache_spec, cache_spec,
                  pl.BlockSpec(bias_tab.shape, lambda r, i: (0, 0, 0, 0))],
        out_specs=pl.BlockSpec((None, rows, w), lambda r, i: (r, i, 0)),
        scratch_shapes=[pltpu.VMEM((H_B // 2, 2, L + P, LANES), BF16),
                        pltpu.VMEM((H_B // 2, 2, L + P, LANES), BF16)],
        compiler_params=_cp(2), name="attn_b_lat",
    )(qkv, qkv, qkv, cache_k, cache_v, bias_tab)


T_ZF, T_ZB, T_XF, T_XB, T_GF, T_GB, T_MA, T_MB = range(8)


def _ret_kernel(*refs, nc, has_s0, out_state):
    refs = list(refs)
    dec_ref, q_ref, k_ref, v_ref, g_ref, gn_ref = refs[:6]
    pos = 6
    s0_ref = None
    if has_s0:
        s0_ref = refs[pos]; pos += 1
    o_ref = refs[pos]; pos += 1
    st_ref = None
    if out_state:
        st_ref = refs[pos]; pos += 1
    sf_ref, sb_ref, tab_ref = refs[pos], refs[pos + 1], refs[pos + 2]

    sq = (BLOCK, LANES)
    low = _lane_low(sq)
    row_low = lax.broadcasted_iota(I32, sq, 0) < HEAD_DIM
    same_head = low == row_low
    n_pairs = H_C // 2

    @pl.when(pl.program_id(0) == 0)
    def _():
        ri = lax.broadcasted_iota(I32, sq, 0).astype(F32)
        ci = lax.broadcasted_iota(I32, sq, 1).astype(F32)

        def log_gamma(d, h):
            return -jnp.exp(jnp.full(sq, dec_ref[d, h], F32))

        for p in range(n_pairs):
            lf = [log_gamma(0, 2 * p), log_gamma(0, 2 * p + 1)]
            lb = [log_gamma(1, 2 * p), log_gamma(1, 2 * p + 1)]
            lf_lane, lb_lane = jnp.where(low, lf[0], lf[1]), jnp.where(low, lb[0], lb[1])
            lf_row, lb_row = jnp.where(row_low, lf[0], lf[1]), jnp.where(row_low, lb[0], lb[1])
            tab_ref[p, T_ZF] = jnp.exp(lf_lane * (BLOCK - 1.0 - ri))
            tab_ref[p, T_ZB] = jnp.exp(lb_lane * ri)
            tab_ref[p, T_XF] = jnp.exp(lf_lane * (ri + 1.0))
            tab_ref[p, T_XB] = jnp.exp(lb_lane * (BLOCK - ri))
            tab_ref[p, T_GF] = jnp.exp(lf_row * float(BLOCK))
            tab_ref[p, T_GB] = jnp.exp(lb_row * float(BLOCK))
            rel = ri - ci
            for sub in range(2):
                fwd = jnp.where(rel >= 0, jnp.exp(lf[sub] * jnp.maximum(rel, 0.0)), 0.0)
                bwd = jnp.where(rel <= 0, jnp.exp(lb[sub] * jnp.maximum(-rel, 0.0)), 0.0)
                tab_ref[p, T_MA + sub] = fwd + bwd

    def chunk(ref, c, p):
        r0 = pl.multiple_of(c * BLOCK, BLOCK)
        return ref[pl.ds(r0, BLOCK), LANES * p:LANES * p + LANES]

    unroll = min(nc, 4)

    def inc_body(c, carry):
        for p in range(n_pairs):
            kt = (chunk(k_ref, c, p) * QK_SCALE).T.astype(BF16)
            v = chunk(v_ref, c, p)
            zv = jnp.concatenate([v * tab_ref[p, T_ZF], v * tab_ref[p, T_ZB]], axis=1).astype(BF16)
            inc = _dot(kt, zv)
            sf_ref[c, p] = jnp.where(same_head, inc[:, :LANES], 0.0)
            sb_ref[c, p] = jnp.where(same_head, inc[:, LANES:], 0.0)
        return carry

    lax.fori_loop(0, nc, inc_body, 0, unroll=unroll)

    def scan_body(t, S, ref, tg, reverse):
        c = nc - 1 - t if reverse else t
        new = []
        for p in range(n_pairs):
            inc = ref[c, p]
            ref[c, p] = S[p]
            new.append(tab_ref[p, tg] * S[p] + inc)
        return tuple(new)

    zero = jnp.zeros(sq, F32)
    s_f0 = tuple(s0_ref[0, p] if has_s0 else zero for p in range(n_pairs))
    s_b0 = tuple(s0_ref[1, p] if has_s0 else zero for p in range(n_pairs))
    s_f = lax.fori_loop(0, nc, functools.partial(scan_body, ref=sf_ref, tg=T_GF, reverse=False), s_f0)
    s_b = lax.fori_loop(0, nc, functools.partial(scan_body, ref=sb_ref, tg=T_GB, reverse=True), s_b0)
    if out_state:
        for p in range(n_pairs):
            st_ref[0, p] = s_f[p]
            st_ref[1, p] = s_b[p]

    def out_body(c, carry):
        for p in range(n_pairs):
            q = chunk(q_ref, c, p)
            k16 = (chunk(k_ref, c, p) * QK_SCALE).astype(BF16)
            v = chunk(v_ref, c, p)
            a = _dot_t(jnp.concatenate([_mask_head(q, 0), _mask_head(q, 1)], axis=0).astype(BF16), k16)
            lhs = jnp.concatenate([a[:BLOCK] * tab_ref[p, T_MA], a[BLOCK:] * tab_ref[p, T_MA + 1],
                                   q * tab_ref[p, T_XF], q * tab_ref[p, T_XB]], axis=1).astype(BF16)
            rhs = jnp.concatenate([_mask_head(v, 0), _mask_head(v, 1), sf_ref[c, p], sb_ref[c, p]],
                                  axis=0).astype(BF16)
            o = _dot(lhs, rhs)
            mu = jnp.where(low, _mask_head(o, 0).sum(-1, keepdims=True),
                           _mask_head(o, 1).sum(-1, keepdims=True)) * (1.0 / HEAD_DIM)
            d = o - mu
            d2 = d * d
            var = jnp.where(low, _mask_head(d2, 0).sum(-1, keepdims=True),
                            _mask_head(d2, 1).sum(-1, keepdims=True)) * (1.0 / HEAD_DIM)
            cs = slice(LANES * p, LANES * p + LANES)
            on = d * lax.rsqrt(var + LN_EPS) * gn_ref[:, cs]
            r0 = pl.multiple_of(c * BLOCK, BLOCK)
            o_ref[pl.ds(r0, BLOCK), cs] = _silu(chunk(g_ref, c, p)) * on
        return carry

    lax.fori_loop(0, nc, out_body, 0, unroll=unroll)


def _retention(qkv, decay, gn_w, s0=None, layer=0, out_state=False):
    R, L, _ = qkv.shape
    nc = L // BLOCK
    w = H_C * HEAD_DIM
    n_pairs = H_C // 2
    col = lambda cb: pl.BlockSpec((None, L, w), lambda r: (r, 0, cb))
    in_specs = [_smem_spec(), col(COL_QC), col(COL_KC), col(COL_VC), col(COL_GC),
                pl.BlockSpec((1, w), lambda r: (0, 0))]
    args = [decay, qkv, qkv, qkv, qkv, gn_w.reshape(1, w)]
    if s0 is not None:
        in_specs.append(pl.BlockSpec((None, None, 2, n_pairs, BLOCK, LANES), lambda r: (r, layer, 0, 0, 0, 0)))
        args.append(s0)
    out_shape = [jax.ShapeDtypeStruct((R, L, w), F32)]
    out_specs = [pl.BlockSpec((None, L, w), lambda r: (r, 0, 0))]
    if out_state:
        out_shape.append(jax.ShapeDtypeStruct((R, 2, n_pairs, BLOCK, LANES), F32))
        out_specs.append(pl.BlockSpec((None, 2, n_pairs, BLOCK, LANES), lambda r: (r, 0, 0, 0, 0)))
    res = pl.pallas_call(
        functools.partial(_ret_kernel, nc=nc, has_s0=s0 is not None, out_state=out_state),
        out_shape=out_shape, grid=(R,), in_specs=in_specs, out_specs=out_specs,
        scratch_shapes=[pltpu.VMEM((nc, n_pairs, BLOCK, LANES), F32),
                        pltpu.VMEM((nc, n_pairs, BLOCK, LANES), F32),
                        pltpu.VMEM((n_pairs, 8, BLOCK, LANES), F32)],
        compiler_params=_cp(1), name="retention",
    )(*args)
    return res if out_state else res[0]


def _pair_states_to_heads(st):
    a = st[..., :HEAD_DIM, :HEAD_DIM]
    b = st[..., HEAD_DIM:, HEAD_DIM:]
    return jnp.stack([a, b], axis=3).reshape(st.shape[0], 2, H_C, HEAD_DIM, HEAD_DIM)


def _heads_to_pair_states(s):
    s = s.reshape(*s.shape[:3], H_C // 2, 2, HEAD_DIM, HEAD_DIM)
    z = jnp.zeros_like(s[..., 0, :, :])
    top = jnp.concatenate([s[..., 0, :, :], z], axis=-1)
    bot = jnp.concatenate([z, s[..., 1, :, :]], axis=-1)
    return jnp.concatenate([top, bot], axis=-2)


def _outproj_kernel(oa_ref, ob_ref, oc_ref, x_ref, g1_ref, sc2_ref, sh2_ref, w_ref, lg_ref, lb_ref, wr_ref,
                    x1_ref, h2_ref, aff_ref):
    wa, wb = H_A * HEAD_DIM, H_B * HEAD_DIM
    mix = (_dot(oa_ref[...].astype(BF16), w_ref[0:wa, :])
           + _dot(ob_ref[...].astype(BF16), w_ref[wa:wa + wb, :])
           + _dot(oc_ref[...].astype(BF16), w_ref[wa + wb:, :]))
    x1 = _layernorm(ALPHA * x_ref[...] + g1_ref[...] * mix, lg_ref[...], lb_ref[...])
    x1_ref[...] = x1
    h2 = (x1 * (1.0 + sc2_ref[...]) + sh2_ref[...]).astype(BF16)
    h2_ref[...] = h2
    logits = _dot(h2, wr_ref[...])
    e = jnp.exp(logits - logits.max(axis=-1, keepdims=True))
    aff_ref[...] = e / e.sum(axis=-1, keepdims=True)


def _outproj(oa, ob, oc, x, mods, w_out_bf16, ln_g, ln_b, w_router_bf16, layer, tm):
    R, L, _ = x.shape
    row = lambda w: pl.BlockSpec((None, tm, w), lambda r, i: (r, i, 0))
    vec = pl.BlockSpec((1, D_MODEL), lambda r, i: (0, 0))
    return pl.pallas_call(
        _outproj_kernel,
        out_shape=[jax.ShapeDtypeStruct((R, L, D_MODEL), F32),
                   jax.ShapeDtypeStruct((R, L, D_MODEL), BF16),
                   jax.ShapeDtypeStruct((R, L, N_EXPERTS), F32)],
        grid=(R, L // tm),
        in_specs=[row(H_A * HEAD_DIM), row(H_B * HEAD_DIM), row(H_C * HEAD_DIM), row(D_MODEL),
                  _mod_spec(2, 2), _mod_spec(4, 2), _mod_spec(3, 2),
                  pl.BlockSpec((None, D_MODEL, D_MODEL), lambda r, i: (layer, 0, 0)), vec, vec,
                  pl.BlockSpec((None, D_MODEL, N_EXPERTS), lambda r, i: (layer, 0, 0))],
        out_specs=[row(D_MODEL), row(D_MODEL), row(N_EXPERTS)],
        compiler_params=_cp(2), name="out_proj_ln_router",
    )(oa, ob, oc, x, mods, mods, mods, w_out_bf16, ln_g.reshape(1, -1), ln_b.reshape(1, -1), w_router_bf16)


CUMSUM_CHUNK = 256


def _excl_cumsum_lanes(x):
    rows, n = x.shape
    w = min(CUMSUM_CHUNK, n)
    tri = (lax.broadcasted_iota(I32, (w, w), 0) < lax.broadcasted_iota(I32, (w, w), 1))
    tri = jnp.where(tri, 1.0, 0.0).astype(BF16)
    carry = jnp.zeros((rows, 1), F32)
    outs = []
    for c in range(0, n, w):
        xc = x[:, c:c + w]
        outs.append(_dot(xc.astype(BF16), tri) + carry)
        carry = carry + xc.sum(axis=-1, keepdims=True)
    return outs[0] if len(outs) == 1 else jnp.concatenate(outs, axis=1)


def _select_kernel(a_ref, pos_ref, *, cap):
    rb, ne, n = a_ref.shape
    rows = rb * ne
    a = a_ref[...].reshape(rows, n)
    capf = float(cap)

    def body(_, c):
        lo, hi = c
        mid = lo + ((hi - lo + 1) >> 1)
        cnt = jnp.where(a >= pltpu.bitcast(mid, F32), 1.0, 0.0).sum(axis=-1, keepdims=True)
        ok = cnt >= capf
        return jnp.where(ok, mid, lo), jnp.where(ok, hi, mid - 1)

    one_bits = 0x3F800000
    lo, _ = lax.fori_loop(0, 30, body, (jnp.zeros((rows, 1), I32), jnp.full((rows, 1), one_bits, I32)))
    thr = pltpu.bitcast(lo, F32)
    gt = jnp.where(a > thr, 1.0, 0.0)
    eq = jnp.where(a == thr, 1.0, 0.0)
    need = capf - gt.sum(axis=-1, keepdims=True)
    sel = gt + eq * jnp.where(_excl_cumsum_lanes(eq) < need, 1.0, 0.0)
    slot = _excl_cumsum_lanes(sel)
    pos_ref[...] = jnp.where(sel > 0.5, slot, -1.0).astype(I32).reshape(rb, ne, n)


def _select(aff_t, cap):
    R, ne, n = aff_t.shape
    rb = min(R, max(1, 128 * 2048 // (ne * n)))
    return pl.pallas_call(
        functools.partial(_select_kernel, cap=cap),
        out_shape=jax.ShapeDtypeStruct((R, ne, n), I32),
        grid=(R // rb,),
        in_specs=[pl.BlockSpec((rb, ne, n), lambda r: (r, 0, 0))],
        out_specs=pl.BlockSpec((rb, ne, n), lambda r: (r, 0, 0)),
        compiler_params=_cp(1), name="expert_select",
    )(aff_t)


def _gather_kernel(pos_ref, aff_ref, h_ref, x_ref, gate_ref, *, cap):
    eb, _, n = pos_ref.shape
    slot = lax.broadcasted_iota(I32, (cap, n), 0)
    parts = []
    for j in range(eb):
        hit = slot == pos_ref[j]
        parts.append(jnp.where(hit, 1.0, 0.0).astype(BF16))
        gate_ref[j] = jnp.where(hit, aff_ref[j], 0.0).sum(axis=-1, keepdims=True)
    onehot = parts[0] if eb == 1 else jnp.concatenate(parts, axis=0)
    xe = _dot(onehot, h_ref[...]).astype(BF16)
    for j in range(eb):
        x_ref[j] = xe[j * cap:(j + 1) * cap]


def _gather(pos, aff_t, h2, cap):
    R, ne, n = pos.shape
    eb = max(1, min(ne, ROW_TILE // cap))
    row = pl.BlockSpec((None, eb, 1, n), lambda r, e: (r, e, 0, 0))
    return pl.pallas_call(
        functools.partial(_gather_kernel, cap=cap),
        out_shape=[jax.ShapeDtypeStruct((ne, R * cap, D_MODEL), BF16),
                   jax.ShapeDtypeStruct((ne, R * cap, 1), F32)],
        grid=(R, ne // eb),
        in_specs=[row, row, pl.BlockSpec((None, n, D_MODEL), lambda r, e: (r, 0, 0))],
        out_specs=[pl.BlockSpec((eb, cap, D_MODEL), lambda r, e: (e, r, 0)),
                   pl.BlockSpec((eb, cap, 1), lambda r, e: (e, r, 0))],
        compiler_params=_cp(2), name="expert_gather",
    )(pos.reshape(R, ne, 1, n), aff_t.reshape(R, ne, 1, n), h2)


FF_TILE = 512


def _ffn_kernel(x0_ref, x1_ref, wa_ref, wb_ref, wd_ref, g0_ref, g1_ref, y0_ref, y1_ref, acc_ref):
    f = pl.program_id(1)

    @pl.when(f == 0)
    def _():
        acc_ref[...] = jnp.zeros_like(acc_ref)

    x = jnp.concatenate([x0_ref[...], x1_ref[...]], axis=0)
    a = _dot(x, wa_ref[...].astype(BF16))
    b = _dot(x, wb_ref[...].astype(BF16))
    acc_ref[...] += _dot((_silu(a) * b).astype(BF16), wd_ref[...].astype(BF16))

    @pl.when(f == pl.num_programs(1) - 1)
    def _():
        s0 = x0_ref.shape[0]
        y0_ref[...] = (acc_ref[:s0, :] * g0_ref[...]).astype(BF16)
        y1_ref[...] = (acc_ref[s0:, :] * g1_ref[...]).astype(BF16)


def _experts(xe0, gates0, xe1, gates1, w_gate_up, w_down, layer):
    ne, S0, _ = xe0.shape
    S1 = xe1.shape[1]
    nf = D_FF // FF_TILE
    rows = lambda S, w: pl.BlockSpec((None, S, w), lambda e, f: (e, 0, 0))
    return pl.pallas_call(
        _ffn_kernel,
        out_shape=[jax.ShapeDtypeStruct((ne, S0, D_MODEL), BF16), jax.ShapeDtypeStruct((ne, S1, D_MODEL), BF16)],
        grid=(ne, nf),
        in_specs=[rows(S0, D_MODEL), rows(S1, D_MODEL),
                  pl.BlockSpec((None, None, D_MODEL, FF_TILE), lambda e, f: (layer, e, 0, f)),
                  pl.BlockSpec((None, None, D_MODEL, FF_TILE), lambda e, f: (layer, e, 0, nf + f)),
                  pl.BlockSpec((None, None, FF_TILE, D_MODEL), lambda e, f: (layer, e, f, 0)),
                  rows(S0, 1), rows(S1, 1)],
        out_specs=[rows(S0, D_MODEL), rows(S1, D_MODEL)],
        scratch_shapes=[pltpu.VMEM((S0 + S1, D_MODEL), F32)],
        compiler_params=_cp(2), name="expert_ffn",
    )(xe0, xe1, w_gate_up, w_gate_up, w_down, gates0, gates1)


def _scatter_kernel(pos_ref, y_ref, x1_ref, g2_ref, lg_ref, lb_ref, o_ref, *, cap):
    tm, ne = pos_ref.shape
    cw = max(cap, LANES)
    epc = cw // cap
    lane = lax.broadcasted_iota(I32, (tm, cw), 1)
    pos = pos_ref[...]
    acc = jnp.zeros((tm, D_MODEL), F32)
    for c in range(ne // epc):
        onehot = jnp.zeros((tm, cw), F32)
        for jj in range(epc):
            col = pos[:, c * epc + jj:c * epc + jj + 1]
            tgt = jnp.where(col >= 0, col + jj * cap, -1)
            onehot = onehot + jnp.where(lane == tgt, 1.0, 0.0)
        yc = y_ref[c] if epc == 1 else y_ref[c * epc:(c + 1) * epc].reshape(cw, D_MODEL)
        acc = acc + _dot(onehot.astype(BF16), yc)
    o_ref[...] = _layernorm(ALPHA * x1_ref[...] + g2_ref[...] * acc, lg_ref[...], lb_ref[...])


def _scatter(pos_tok, y, x1, mods, ln_g, ln_b, cap, tm):
    R, n, ne = pos_tok.shape
    vec = pl.BlockSpec((1, D_MODEL), lambda r, i: (0, 0))
    return pl.pallas_call(
        functools.partial(_scatter_kernel, cap=cap),
        out_shape=jax.ShapeDtypeStruct((R, n, D_MODEL), F32),
        grid=(R, n // tm),
        in_specs=[pl.BlockSpec((None, tm, ne), lambda r, i: (r, i, 0)),
                  pl.BlockSpec((ne, cap, D_MODEL), lambda r, i: (0, r, 0)),
                  pl.BlockSpec((None, tm, D_MODEL), lambda r, i: (r, i, 0)),
                  _mod_spec(5, 2), vec, vec],
        out_specs=pl.BlockSpec((None, tm, D_MODEL), lambda r, i: (r, i, 0)),
        compiler_params=_cp(2), name="expert_scatter_ln",
    )(pos_tok, y, x1, mods, ln_g.reshape(1, -1), ln_b.reshape(1, -1))


def _capacity(n):
    return CAPACITY_FACTOR * n // N_EXPERTS


def _route(h2, aff):
    cap = _capacity(h2.shape[1])
    aff_t = jnp.swapaxes(aff, 1, 2)
    pos = _select(aff_t, cap)
    xe, gates = _gather(pos, aff_t, h2, cap)
    return pos, xe, gates


def _channel_mixers(groups, l, w_gate_up, w_down, ln2_g, ln2_b):
    routed = [_route(g["h2"], g["aff"]) for g in groups]
    ys = _experts(routed[0][1], routed[0][2], routed[1][1], routed[1][2], w_gate_up, w_down, l)
    return [_scatter(jnp.swapaxes(r[0], 1, 2), y, g["x1"], g["mods"], ln2_g[l], ln2_b[l],
                     _capacity(g["h2"].shape[1]), g["tm"]) for g, r, y in zip(groups, routed, ys)]


def kernel(x_prompt, x_sample, cache_attn_a_k, cache_attn_a_v, cache_attn_b_k, cache_attn_b_v, state_ret, c, c_ctx,
           w_ada, b_ada, w_in, w_out, attn_sink, na_rpb, ret_decay, ret_gn, ln1_g, ln1_b, ln2_g, ln2_b,
           w_router, w_gate_up, w_down):
    n_ctx, L_ctx, _ = x_prompt.shape
    n_lat, L_lat, _ = x_sample.shape
    P = cache_attn_a_k.shape[2]

    cond = jnp.concatenate([c_ctx[None, :], c, jnp.zeros((8 - 1 - n_lat, D_MODEL), F32)], axis=0)
    mod = _modulation(cond, w_ada, b_ada)
    w_in16, w_out16, w_r16 = w_in.astype(BF16), w_out.astype(BF16), w_router.astype(BF16)
    rope_tabs = _rope_tables(L_lat)
    cak = cache_attn_a_k.reshape(n_lat, DEPTH, P, KVH_A * HEAD_DIM)
    cav = cache_attn_a_v.reshape(n_lat, DEPTH, P, KVH_A * HEAD_DIM)
    cbk = cache_attn_b_k.reshape(n_lat, DEPTH, P, H_B * HEAD_DIM)
    cbv = cache_attn_b_v.reshape(n_lat, DEPTH, P, H_B * HEAD_DIM)
    s0_pairs = _heads_to_pair_states(state_ret)

    xc, xs = x_prompt, x_sample
    tm = ROW_TILE
    a_k, a_v, b_k, b_v, st = [], [], [], [], []
    flat = lambda a: a.reshape(1, n_ctx * L_ctx, a.shape[-1])
    unflat = lambda a: a.reshape(n_ctx, L_ctx, a.shape[-1])
    for l in range(DEPTH):
        mod_c = mod[l, 0:1][:, None, :]
        mods_c = jnp.broadcast_to(mod_c, (n_ctx, 1, 6 * D_MODEL))
        qkv, ka, va, kb, vb = [unflat(t) for t in _project(flat(xc), mod_c, w_in16, l, tm, emit_kv=True)]
        oa = _attn_a_ctx(qkv, attn_sink[l])
        ob = _attn_b_ctx(qkv)
        oc, s_l = _retention(qkv, ret_decay[l], ret_gn[l], out_state=True)
        x1, h2, aff = [unflat(t) for t in _outproj(flat(oa), flat(ob), flat(oc), flat(xc), mod_c, w_out16,
                                                   ln1_g[l], ln1_b[l], w_r16, l, tm)]
        ctx = dict(x1=x1, h2=h2, aff=aff, mods=mods_c, tm=L_ctx)
        a_k.append(ka.reshape(n_ctx, L_ctx, KVH_A, HEAD_DIM))
        a_v.append(va.reshape(n_ctx, L_ctx, KVH_A, HEAD_DIM))
        b_k.append(kb.reshape(n_ctx, L_ctx, H_B, HEAD_DIM))
        b_v.append(vb.reshape(n_ctx, L_ctx, H_B, HEAD_DIM))
        st.append(_pair_states_to_heads(s_l))

        mods_s = mod[l, 1:1 + n_lat][:, None, :]
        qkv = _project(xs, mods_s, w_in16, l, tm, rope_tabs)
        oa = _attn_a_lat(qkv, attn_sink[l], cak, cav, l)
        ob = _attn_b_lat(qkv, cbk, cbv, _bias_table(na_rpb[l]), l)
        oc = _retention(qkv, ret_decay[l], ret_gn[l], s0=s0_pairs, layer=l)
        x1, h2, aff = _outproj(oa, ob, oc, xs, mods_s, w_out16, ln1_g[l], ln1_b[l], w_r16, l, tm)
        lat = dict(x1=x1, h2=h2, aff=aff, mods=mods_s, tm=tm)

        xc, xs = _channel_mixers([ctx, lat], l, w_gate_up, w_down, ln2_g, ln2_b)

    return (xc, xs, jnp.stack(a_k, 1), jnp.stack(a_v, 1), jnp.stack(b_k, 1), jnp.stack(b_v, 1),
            jnp.stack(st, 1))
```

```python
import functools

import numpy as np
import jax
import jax.numpy as jnp
from jax import lax
from jax.experimental import pallas as pl
from jax.experimental.pallas import tpu as pltpu

F32 = jnp.float32
BF16 = jnp.bfloat16
I32 = jnp.int32

D_MODEL = 1024
DEPTH = 2
GRID_W = 64
HEAD_DIM = 64
H_A, KVH_A, H_B, H_C = 8, 2, 4, 4
D_IN = 2560
BLOCK = 128
WINDOW = 128
NA_ROWS, NA_COLS = 8, 16
NA_WIN_BLOCKS = 5
NA_QBLOCKS = 4
WIN_QBLOCKS = 2
CTX_GROUP_A = 4
CTX_GROUP_B = 2
ROPE_BASE = 10000.0
N_EXPERTS = 16
CAPACITY_FACTOR = 2
D_FF = 1024
LN_EPS = 1e-6
ALPHA = (2 * DEPTH) ** 0.25
NEG = -1e30
LANES = 128
QK_SCALE = HEAD_DIM ** -0.5
LOG2E = float(np.log2(np.e))
ATTN_SCALE = QK_SCALE * LOG2E
VMEM_LIMIT = 56 * 1024 * 1024
ROW_TILE = 1024
OUTPROJ_PARTS = 2

COL_QA, COL_KA, COL_VA = 0, 4, 5
COL_QB, COL_KB, COL_VB = 3, 4, 5
COL_QC, COL_KC, COL_VC, COL_GC = 6, 7, 8, 9


def _cp(n_axes):
    return pltpu.CompilerParams(dimension_semantics=("arbitrary",) * n_axes, vmem_limit_bytes=VMEM_LIMIT)


def _dot(a, b):
    return jnp.dot(a, b, preferred_element_type=F32)


def _dot_t(a, b):
    return lax.dot_general(a, b, (((1,), (1,)), ((), ())), preferred_element_type=F32)


def _silu(x):
    return x / (1.0 + jnp.exp(-x))


def _lane_low(shape):
    return lax.broadcasted_iota(I32, shape, len(shape) - 1) < HEAD_DIM


def _layernorm(y, g, b):
    mu = jnp.mean(y, axis=-1, keepdims=True)
    d = y - mu
    var = jnp.mean(d * d, axis=-1, keepdims=True)
    return d * lax.rsqrt(var + LN_EPS) * g + b


def _mod_kernel(c_ref, w_ref, b_ref, o_ref):
    s = _silu(c_ref[...]).astype(BF16)
    o_ref[...] = _dot(s, w_ref[...].astype(BF16)) + b_ref[...]


def _modulation(cond, w_ada, b_ada):
    tn = 1536
    return pl.pallas_call(
        _mod_kernel,
        out_shape=jax.ShapeDtypeStruct((DEPTH, 8, 6 * D_MODEL), F32),
        grid=(DEPTH, 6 * D_MODEL // tn),
        in_specs=[pl.BlockSpec((8, D_MODEL), lambda l, j: (0, 0)),
                  pl.BlockSpec((None, D_MODEL, tn), lambda l, j: (l, 0, j)),
                  pl.BlockSpec((None, 1, tn), lambda l, j: (l, 0, j))],
        out_specs=pl.BlockSpec((None, 8, tn), lambda l, j: (l, 0, j)),
        compiler_params=_cp(2), name="modulation",
    )(cond, w_ada, b_ada.reshape(DEPTH, 1, 6 * D_MODEL))


def _mod_spec(k, n_grid):
    if n_grid == 1:
        return pl.BlockSpec((None, 1, D_MODEL), lambda r: (r, 0, k))
    return pl.BlockSpec((None, 1, D_MODEL), lambda r, i: (r, 0, k))


def _rope_tables(L):
    t = np.arange(L)
    pos = (t // GRID_W, t % GRID_W)
    half = HEAD_DIM // 2
    inv = 1.0 / (ROPE_BASE ** (np.arange(0, half, 2) / half))
    cos = np.zeros((L, HEAD_DIM)); s_up = np.zeros((L, HEAD_DIM)); s_dn = np.zeros((L, HEAD_DIM))
    for part in range(2):
        ang = (pos[part][:, None] * inv[None, :]).astype(np.float32).astype(np.float64)
        q = half // 2
        base = part * half
        cos[:, base:base + q] = np.cos(ang); cos[:, base + q:base + half] = np.cos(ang)
        s_up[:, base:base + q] = -np.sin(ang)
        s_dn[:, base + q:base + half] = np.sin(ang)
    tile = lambda a: jnp.asarray(np.tile(a, (1, LANES // HEAD_DIM)), F32)
    return tile(cos), tile(s_up), tile(s_dn)


KV_COLUMNS = ((512, 128), (640, 128), (1024, 256), (1280, 256))


def _proj_kernel(*refs, rope, emit_kv):
    kv_refs = ()
    if emit_kv:
        refs, kv_refs = refs[:-len(KV_COLUMNS)], refs[-len(KV_COLUMNS):]
    if rope:
        x_ref, sc_ref, sh_ref, w_ref, cos_ref, sup_ref, sdn_ref, o_ref = refs
    else:
        x_ref, sc_ref, sh_ref, w_ref, o_ref = refs
    h = (x_ref[...] * (1.0 + sc_ref[...]) + sh_ref[...]).astype(BF16)
    n_rope = (H_A + KVH_A) * HEAD_DIM
    step = 512
    for j in range(0, D_IN, step):
        acc = _dot(h, w_ref[:, j:j + step])
        if rope and j < n_rope:
            q = HEAD_DIM // 4
            for c in range(0, step, LANES):
                a = acc[:, c:c + LANES]
                if j + c < n_rope:
                    a = (a * cos_ref[...] + pltpu.roll(a, LANES - q, 1) * sup_ref[...]
                         + pltpu.roll(a, q, 1) * sdn_ref[...])
                o_ref[:, j + c:j + c + LANES] = a
        else:
            o_ref[:, j:j + step] = acc
        for ref, (c0, w) in zip(kv_refs, KV_COLUMNS):
            if j <= c0 < j + step:
                ref[...] = acc[:, c0 - j:c0 - j + w]


def _project(x, mods, w_bf16, layer, tm, rope_tabs=None, emit_kv=False):
    R, L, _ = x.shape
    rope = rope_tabs is not None
    assert not (rope and emit_kv)
    out_shape = [jax.ShapeDtypeStruct((R, L, D_IN), F32)]
    out_specs = [pl.BlockSpec((None, tm, D_IN), lambda r, i: (r, i, 0))]
    if emit_kv:
        out_shape += [jax.ShapeDtypeStruct((R, L, w), F32) for _, w in KV_COLUMNS]
        out_specs += [pl.BlockSpec((None, tm, w), lambda r, i: (r, i, 0)) for _, w in KV_COLUMNS]
    in_specs = [pl.BlockSpec((None, tm, D_MODEL), lambda r, i: (r, i, 0)),
                _mod_spec(1, 2), _mod_spec(0, 2),
                pl.BlockSpec((None, D_MODEL, D_IN), lambda r, i: (layer, 0, 0))]
    args = [x, mods, mods, w_bf16]
    if rope:
        in_specs += [pl.BlockSpec((tm, LANES), lambda r, i: (i, 0))] * 3
        args += list(rope_tabs)
    res = pl.pallas_call(
        functools.partial(_proj_kernel, rope=rope, emit_kv=emit_kv),
        out_shape=out_shape, grid=(R, L // tm), in_specs=in_specs, out_specs=out_specs,
        compiler_params=_cp(2), name="in_proj",
    )(*args)
    return res if emit_kv else res[0]


def _mask_head(x, sub):
    low = _lane_low(x.shape)
    return jnp.where(low, x, 0.0) if sub == 0 else jnp.where(low, 0.0, x)


def _key_variant(k, sub):
    return _mask_head(k, sub).astype(BF16)


def _value_variant(v, sub):
    low = _lane_low(v.shape)
    return (jnp.where(low, v, 1.0) if sub == 0 else jnp.where(low, 1.0, v)).astype(BF16)


def _attend(lhs, keys, values, biases, sub, sink=None):
    scores = []
    for k, b in zip(keys, biases):
        s = _dot_t(lhs, k)
        scores.append(s if b is None else s + b)
    tiles = [s[:, c:c + LANES] for s in scores for c in range(0, s.shape[1], LANES)]
    while len(tiles) > 1:
        tiles = [jnp.maximum(a, b) for a, b in zip(tiles[0::2], tiles[1::2])] + ([tiles[-1]] if len(tiles) % 2 else [])
    m = jnp.broadcast_to(tiles[0].max(axis=-1, keepdims=True), tiles[0].shape)
    if sink is not None:
        m = jnp.maximum(m, sink)
    o = None
    for s, v in zip(scores, values):
        p = [jnp.exp2(s[:, c:c + LANES] - m).astype(BF16) for c in range(0, s.shape[1], LANES)]
        pv = _dot(p[0] if len(p) == 1 else jnp.concatenate(p, axis=1), v)
        o = pv if o is None else o + pv
    if sink is not None:
        low = _lane_low(o.shape)
        extra = jnp.exp2(sink - m)
        o = o + (jnp.where(low, 0.0, extra) if sub == 0 else jnp.where(low, extra, 0.0))
    return o / pltpu.roll(o, HEAD_DIM, 1)


def _gqa_variants(k, v, hk):
    low = _lane_low(k.shape)
    k_sw, v_sw = pltpu.roll(k, HEAD_DIM, 1), pltpu.roll(v, HEAD_DIM, 1)
    k_dup = jnp.where(low, k, k_sw) if hk == 0 else jnp.where(low, k_sw, k)
    return k_dup.astype(BF16), jnp.where(low, v if hk == 0 else v_sw, 1.0).astype(BF16)


def _fill_gqa_variants(k_src, v_src, kv_ref, vv_ref, off, n):
    step = 256
    for c in range(0, n, step):
        k = k_src[c:c + step, :]
        v = v_src[c:c + step, :]
        for hk in range(KVH_A):
            kd, va = _gqa_variants(k, v, hk)
            kv_ref[hk, off + c:off + c + step, :] = kd
            vv_ref[hk, off + c:off + c + step, :] = va


def _fill_mha_variants(k_src, v_src, kv_ref, vv_ref, off, n):
    step = 256
    for c in range(0, n, step):
        rows = slice(off + c, off + c + step)
        for p in range(H_B // 2):
            cs = slice(LANES * p, LANES * p + LANES)
            k = k_src[c:c + step, cs]
            v = v_src[c:c + step, cs]
            for sub in range(2):
                kv_ref[p, sub, rows, :] = _key_variant(k, sub)
                vv_ref[p, sub, rows, :] = _value_variant(v, sub)


def _sink_col(sink_ref, heads, rows):
    return jnp.concatenate([jnp.full((rows, LANES), sink_ref[h] * LOG2E, F32) for h in heads], axis=0)


def _gqa_attend(sink_ref, q_ref, o_ref, hk, keys, values, biases):
    rows = q_ref.shape[0]
    g = H_A // KVH_A
    heads = list(range(g * hk, g * hk + g))
    lhs = jnp.concatenate([_mask_head(q_ref[:, LANES * (h // 2):LANES * (h // 2) + LANES], h % 2) for h in heads],
                          axis=0)
    lhs = (lhs * ATTN_SCALE).astype(BF16)
    stacked = [None if b is None else jnp.concatenate([b] * g, axis=0) for b in biases]
    out = _attend(lhs, keys, values, stacked, 0, _sink_col(sink_ref, heads, rows))
    low = _lane_low((rows, LANES))
    for j in range(0, g, 2):
        h = heads[j]
        odd = pltpu.roll(out[(j + 1) * rows:(j + 2) * rows], HEAD_DIM, 1)
        o_ref[:, LANES * (h // 2):LANES * (h // 2) + LANES] = jnp.where(low, out[j * rows:(j + 1) * rows], odd)


def _attn_a_ctx_kernel(sink_ref, q_ref, k_ref, v_ref, o_ref):
    for g in range(q_ref.shape[0]):
        for hk in range(KVH_A):
            kd, va = _gqa_variants(k_ref[g], v_ref[g], hk)
            _gqa_attend(sink_ref, q_ref.at[g], o_ref.at[g], hk, [kd], [va], [None])


def _window_mask_table():
    nq = WIN_QBLOCKS
    row = np.arange(nq * BLOCK)[:, None]
    col = np.arange((nq + 2) * BLOCK)[None, :]
    tabs = [np.where(np.abs(col - off * BLOCK - row) <= WINDOW, 0.0, NEG) for off in range(3)]
    return jnp.asarray(np.stack(tabs), F32)


def _attn_a_lat_kernel(sink_ref, q_ref, k_ref, v_ref, kc_ref, vc_ref, mask_ref, o_ref, kv_ref, vv_ref, *, nb):
    i = pl.program_id(1)
    L, P = k_ref.shape[0], kc_ref.shape[0]

    @pl.when(i == 0)
    def _():
        _fill_gqa_variants(k_ref, v_ref, kv_ref, vv_ref, 0, L)
        _fill_gqa_variants(kc_ref, vc_ref, kv_ref, vv_ref, L, P)

    nq = WIN_QBLOCKS
    wk = (nq + 2) * BLOCK
    sblk = jnp.clip(nq * i - 1, 0, nb - (nq + 2))
    start = pl.multiple_of(sblk * BLOCK, BLOCK)
    bias = mask_ref[nq * i - sblk]
    for hk in range(KVH_A):
        ks = [kv_ref[hk, pl.ds(start, wk), :], kv_ref[hk, L:L + P, :]]
        vs = [vv_ref[hk, pl.ds(start, wk), :], vv_ref[hk, L:L + P, :]]
        _gqa_attend(sink_ref, q_ref, o_ref, hk, ks, vs, [bias, None])


def _smem_spec():
    return pl.BlockSpec(memory_space=pltpu.SMEM)


def _attn_a_ctx(qkv, sink):
    R, L, _ = qkv.shape
    wq, wkv = H_A * HEAD_DIM, KVH_A * HEAD_DIM
    G = CTX_GROUP_A
    assert R % G == 0
    return pl.pallas_call(
        _attn_a_ctx_kernel,
        out_shape=jax.ShapeDtypeStruct((R, L, wq), F32),
        grid=(R // G,),
        in_specs=[_smem_spec(),
                  pl.BlockSpec((G, L, wq), lambda r: (r, 0, COL_QA)),
                  pl.BlockSpec((G, L, wkv), lambda r: (r, 0, COL_KA)),
                  pl.BlockSpec((G, L, wkv), lambda r: (r, 0, COL_VA))],
        out_specs=pl.BlockSpec((G, L, wq), lambda r: (r, 0, 0)),
        compiler_params=_cp(1), name="attn_a_ctx",
    )(sink, qkv, qkv, qkv)


def _attn_a_lat(qkv, sink, cache_k, cache_v, layer):
    R, L, _ = qkv.shape
    nb = L // BLOCK
    nq = WIN_QBLOCKS
    assert nb % nq == 0 and nb >= nq + 2
    P = cache_k.shape[2]
    wq, wkv = H_A * HEAD_DIM, KVH_A * HEAD_DIM
    cache_spec = pl.BlockSpec((None, None, P, wkv), lambda r, i: (r, layer, 0, 0))
    masks = _window_mask_table()
    return pl.pallas_call(
        functools.partial(_attn_a_lat_kernel, nb=nb),
        out_shape=jax.ShapeDtypeStruct((R, L, wq), F32),
        grid=(R, nb // nq),
        in_specs=[_smem_spec(),
                  pl.BlockSpec((None, nq * BLOCK, wq), lambda r, i: (r, i, COL_QA)),
                  pl.BlockSpec((None, L, wkv), lambda r, i: (r, 0, COL_KA)),
                  pl.BlockSpec((None, L, wkv), lambda r, i: (r, 0, COL_VA)),
                  cache_spec, cache_spec,
                  pl.BlockSpec(masks.shape, lambda r, i: (0, 0, 0))],
        out_specs=pl.BlockSpec((None, nq * BLOCK, wq), lambda r, i: (r, i, 0)),
        scratch_shapes=[pltpu.VMEM((KVH_A, L + P, LANES), BF16),
                        pltpu.VMEM((KVH_A, L + P, LANES), BF16)],
        compiler_params=_cp(2), name="attn_a_lat",
    )(sink, qkv, qkv, qkv, cache_k, cache_v, masks)


def _attn_b_ctx_kernel(q_ref, k_ref, v_ref, o_ref):
    G, L = q_ref.shape[:2]
    low = _lane_low((L, LANES))
    for g in range(G):
        for p in range(H_B // 2):
            cs = slice(LANES * p, LANES * p + LANES)
            lhs = (q_ref[g, :, cs] * ATTN_SCALE).astype(BF16)
            k, v = k_ref[g, :, cs], v_ref[g, :, cs]
            outs = [_attend(lhs, [_key_variant(k, sub)], [_value_variant(v, sub)], [None], sub)
                    for sub in range(2)]
            o_ref[g, :, cs] = jnp.where(low, outs[0], outs[1])


def _bias_table_kernel(rpb_ref, o_ref, t_ref):
    n_dr, n_dc = 2 * NA_ROWS - 1, 2 * NA_COLS - 1
    n_t = o_ref.shape[1]
    shape = (GRID_W, LANES)
    cq = lax.broadcasted_iota(I32, shape, 0)
    lane = lax.broadcasted_iota(I32, shape, 1)
    right = lane >= GRID_W
    ck = lane & (GRID_W - 1)
    cs = jnp.clip(cq - NA_COLS // 2, 0, GRID_W - NA_COLS)
    col_ok = (ck >= cs) & (ck < cs + NA_COLS)
    dc = ck - cq + (NA_COLS - 1)
    for j in (0, 1, n_t - 1, n_t):
        t_ref[j] = jnp.zeros(shape, F32)

    def build(r, carry, h):
        acc = jnp.zeros(shape, F32)
        for d in range(n_dc):
            acc = jnp.where(dc == d, rpb_ref[(h * n_dr + r) * n_dc + d] * LOG2E, acc)
        t_ref[r + 2] = jnp.where(col_ok, acc, NEG)
        return carry

    def compose(t, carry, h):
        o_ref[h, t] = jnp.where(right, t_ref[t + 1], t_ref[t])
        return carry

    for h in range(H_B):
        lax.fori_loop(0, n_dr, functools.partial(build, h=h), 0)
        lax.fori_loop(0, n_t, functools.partial(compose, h=h), 0)


def _bias_table(rpb_l):
    n_t = 2 * (NA_ROWS + 1)
    return pl.pallas_call(
        _bias_table_kernel,
        out_shape=jax.ShapeDtypeStruct((H_B, n_t, GRID_W, LANES), F32),
        in_specs=[_smem_spec()],
        out_specs=pl.BlockSpec(memory_space=pltpu.VMEM),
        scratch_shapes=[pltpu.VMEM((n_t + 1, GRID_W, LANES), F32)],
        name="na_bias_table",
    )(rpb_l.reshape(-1))


def _attn_b_lat_kernel(q_ref, k_ref, v_ref, kc_ref, vc_ref, tp_ref, o_ref, kv_ref, vv_ref, *, nb, rows_total):
    i = pl.program_id(1)
    L, P = k_ref.shape[0], kc_ref.shape[0]

    @pl.when(i == 0)
    def _():
        _fill_mha_variants(k_ref, v_ref, kv_ref, vv_ref, 0, L)
        _fill_mha_variants(kc_ref, vc_ref, kv_ref, vv_ref, L, P)

    nq = NA_QBLOCKS
    nwin = NA_WIN_BLOCKS + nq - 1
    wk = nwin * BLOCK
    rows = nq * BLOCK
    sblk = jnp.clip(nq * i - 2, 0, nb - nwin)
    start = pl.multiple_of(sblk * BLOCK, BLOCK)
    shape = (rows, wk)
    rq = 2 * nq * i + (lax.broadcasted_iota(I32, shape, 0) >> 6)
    rk = 2 * sblk + (lax.broadcasted_iota(I32, shape, 1) >> 6)
    rs = jnp.clip(rq - NA_ROWS // 2, 0, rows_total - NA_ROWS)
    row_bias = jnp.where((rk >= rs) & (rk < rs + NA_ROWS), 0.0, NEG)
    n_t = tp_ref.shape[1]
    low = _lane_low((rows, LANES))
    for p in range(H_B // 2):
        cs = slice(LANES * p, LANES * p + LANES)
        lhs = (q_ref[:, cs] * ATTN_SCALE).astype(BF16)
        outs = []
        for sub in range(2):
            h = 2 * p + sub
            strips = []
            for qq in range(nq):
                tiles = []
                for kb in range(nwin):
                    t = jnp.clip(2 * (sblk + kb - (nq * i + qq)) + NA_ROWS + 1, 1, n_t - 1)
                    tiles.append(jnp.concatenate([tp_ref[h, t], tp_ref[h, t - 1]], axis=0))
                strips.append(jnp.concatenate(tiles, axis=1))
            bias = jnp.concatenate(strips, axis=0) + row_bias
            ks = [kv_ref[p, sub, pl.ds(start, wk), :], kv_ref[p, sub, L:L + P, :]]
            vs = [vv_ref[p, sub, pl.ds(start, wk), :], vv_ref[p, sub, L:L + P, :]]
            outs.append(_attend(lhs, ks, vs, [bias, None], sub))
        o_ref[:, cs] = jnp.where(low, outs[0], outs[1])


def _attn_b_ctx(qkv):
    R, L, _ = qkv.shape
    w = H_B * HEAD_DIM
    G = CTX_GROUP_B
    assert R % G == 0
    return pl.pallas_call(
        _attn_b_ctx_kernel,
        out_shape=jax.ShapeDtypeStruct((R, L, w), F32),
        grid=(R // G,),
        in_specs=[pl.BlockSpec((G, L, w), lambda r: (r, 0, COL_QB)),
                  pl.BlockSpec((G, L, w), lambda r: (r, 0, COL_KB)),
                  pl.BlockSpec((G, L, w), lambda r: (r, 0, COL_VB))],
        out_specs=pl.BlockSpec((G, L, w), lambda r: (r, 0, 0)),
        compiler_params=_cp(1), name="attn_b_ctx",
    )(qkv, qkv, qkv)


def _attn_b_lat(qkv, cache_k, cache_v, bias_tab, layer):
    R, L, _ = qkv.shape
    nb = L // BLOCK
    P = cache_k.shape[2]
    w = H_B * HEAD_DIM
    cache_spec = pl.BlockSpec((None, None, P, w), lambda r, i: (r, layer, 0, 0))
    rows = NA_QBLOCKS * BLOCK
    assert nb % NA_QBLOCKS == 0 and nb >= NA_WIN_BLOCKS + NA_QBLOCKS - 1
    return pl.pallas_call(
        functools.partial(_attn_b_lat_kernel, nb=nb, rows_total=L // GRID_W),
        out_shape=jax.ShapeDtypeStruct((R, L, w), F32),
        grid=(R, nb // NA_QBLOCKS),
        in_specs=[pl.BlockSpec((None, rows, w), lambda r, i: (r, i, COL_QB)),
                  pl.BlockSpec((None, L, w), lambda r, i: (r, 0, COL_KB)),
                  pl.BlockSpec((None, L, w), lambda r, i: (r, 0, COL_VB)),
                  cache_spec, cache_spec,
                  pl.BlockSpec(bias_tab.shape, lambda r, i: (0, 0, 0, 0))],
        out_specs=pl.BlockSpec((None, rows, w), lambda r, i: (r, i, 0)),
        scratch_shapes=[pltpu.VMEM((H_B // 2, 2, L + P, LANES), BF16),
                        pltpu.VMEM((H_B // 2, 2, L + P, LANES), BF16)],
        compiler_params=_cp(2), name="attn_b_lat",
    )(qkv, qkv, qkv, cache_k, cache_v, bias_tab)


T_ZF, T_ZB, T_XF, T_XB, T_GF, T_GB, T_MA, T_MB = range(8)


def _ret_kernel(*refs, nc, has_s0, out_state):
    refs = list(refs)
    dec_ref, q_ref, k_ref, v_ref, g_ref, gn_ref = refs[:6]
    pos = 6
    s0_ref = None
    if has_s0:
        s0_ref = refs[pos]; pos += 1
    o_ref = refs[pos]; pos += 1
    st_ref = None
    if out_state:
        st_ref = refs[pos]; pos += 1
    sf_ref, sb_ref, tab_ref = refs[pos], refs[pos + 1], refs[pos + 2]

    sq = (BLOCK, LANES)
    low = _lane_low(sq)
    row_low = lax.broadcasted_iota(I32, sq, 0) < HEAD_DIM
    same_head = low == row_low
    n_pairs = H_C // 2

    @pl.when(pl.program_id(0) == 0)
    def _():
        ri = lax.broadcasted_iota(I32, sq, 0).astype(F32)
        ci = lax.broadcasted_iota(I32, sq, 1).astype(F32)

        def log_gamma(d, h):
            return -jnp.exp(jnp.full(sq, dec_ref[d, h], F32))

        for p in range(n_pairs):
            lf = [log_gamma(0, 2 * p), log_gamma(0, 2 * p + 1)]
            lb = [log_gamma(1, 2 * p), log_gamma(1, 2 * p + 1)]
            lf_lane, lb_lane = jnp.where(low, lf[0], lf[1]), jnp.where(low, lb[0], lb[1])
            lf_row, lb_row = jnp.where(row_low, lf[0], lf[1]), jnp.where(row_low, lb[0], lb[1])
            tab_ref[p, T_ZF] = jnp.exp(lf_lane * (BLOCK - 1.0 - ri))
            tab_ref[p, T_ZB] = jnp.exp(lb_lane * ri)
            tab_ref[p, T_XF] = jnp.exp(lf_lane * (ri + 1.0))
            tab_ref[p, T_XB] = jnp.exp(lb_lane * (BLOCK - ri))
            tab_ref[p, T_GF] = jnp.exp(lf_row * float(BLOCK))
            tab_ref[p, T_GB] = jnp.exp(lb_row * float(BLOCK))
            rel = ri - ci
            for sub in range(2):
                fwd = jnp.where(rel >= 0, jnp.exp(lf[sub] * jnp.maximum(rel, 0.0)), 0.0)
                bwd = jnp.where(rel <= 0, jnp.exp(lb[sub] * jnp.maximum(-rel, 0.0)), 0.0)
                tab_ref[p, T_MA + sub] = fwd + bwd

    def chunk(ref, c, p):
        r0 = pl.multiple_of(c * BLOCK, BLOCK)
        return ref[pl.ds(r0, BLOCK), LANES * p:LANES * p + LANES]

    unroll = min(nc, 8)

    def inc_body(c, carry):
        for p in range(n_pairs):
            kt = (chunk(k_ref, c, p) * QK_SCALE).T.astype(BF16)
            v = chunk(v_ref, c, p)
            zv = jnp.concatenate([v * tab_ref[p, T_ZF], v * tab_ref[p, T_ZB]], axis=1).astype(BF16)
            inc = _dot(kt, zv)
            sf_ref[c, p] = jnp.where(same_head, inc[:, :LANES], 0.0)
            sb_ref[c, p] = jnp.where(same_head, inc[:, LANES:], 0.0)
        return carry

    lax.fori_loop(0, nc, inc_body, 0, unroll=unroll)

    def scan_body(t, S, ref, tg, reverse):
        c = nc - 1 - t if reverse else t
        new = []
        for p in range(n_pairs):
            inc = ref[c, p]
            ref[c, p] = S[p]
            new.append(tab_ref[p, tg] * S[p] + inc)
        return tuple(new)

    zero = jnp.zeros(sq, F32)
    s_f0 = tuple(s0_ref[0, p] if has_s0 else zero for p in range(n_pairs))
    s_b0 = tuple(s0_ref[1, p] if has_s0 else zero for p in range(n_pairs))
    s_f = lax.fori_loop(0, nc, functools.partial(scan_body, ref=sf_ref, tg=T_GF, reverse=False), s_f0)
    s_b = lax.fori_loop(0, nc, functools.partial(scan_body, ref=sb_ref, tg=T_GB, reverse=True), s_b0)
    if out_state:
        for p in range(n_pairs):
            st_ref[0, p] = s_f[p]
            st_ref[1, p] = s_b[p]

    def out_body(c, carry):
        for p in range(n_pairs):
            q = chunk(q_ref, c, p)
            k16 = (chunk(k_ref, c, p) * QK_SCALE).astype(BF16)
            v = chunk(v_ref, c, p)
            a = _dot_t(jnp.concatenate([_mask_head(q, 0), _mask_head(q, 1)], axis=0).astype(BF16), k16)
            lhs = jnp.concatenate([a[:BLOCK] * tab_ref[p, T_MA], a[BLOCK:] * tab_ref[p, T_MA + 1],
                                   q * tab_ref[p, T_XF], q * tab_ref[p, T_XB]], axis=1).astype(BF16)
            rhs = jnp.concatenate([_mask_head(v, 0), _mask_head(v, 1), sf_ref[c, p], sb_ref[c, p]],
                                  axis=0).astype(BF16)
            o = _dot(lhs, rhs)
            mu = jnp.where(low, _mask_head(o, 0).sum(-1, keepdims=True),
                           _mask_head(o, 1).sum(-1, keepdims=True)) * (1.0 / HEAD_DIM)
            d = o - mu
            d2 = d * d
            var = jnp.where(low, _mask_head(d2, 0).sum(-1, keepdims=True),
                            _mask_head(d2, 1).sum(-1, keepdims=True)) * (1.0 / HEAD_DIM)
            cs = slice(LANES * p, LANES * p + LANES)
            on = d * lax.rsqrt(var + LN_EPS) * gn_ref[:, cs]
            r0 = pl.multiple_of(c * BLOCK, BLOCK)
            o_ref[pl.ds(r0, BLOCK), cs] = _silu(chunk(g_ref, c, p)) * on
        return carry

    lax.fori_loop(0, nc, out_body, 0, unroll=unroll)


def _retention(qkv, decay, gn_w, s0=None, layer=0, out_state=False):
    R, L, _ = qkv.shape
    nc = L // BLOCK
    w = H_C * HEAD_DIM
    n_pairs = H_C // 2
    col = lambda cb: pl.BlockSpec((None, L, w), lambda r: (r, 0, cb))
    in_specs = [_smem_spec(), col(COL_QC), col(COL_KC), col(COL_VC), col(COL_GC),
                pl.BlockSpec((1, w), lambda r: (0, 0))]
    args = [decay, qkv, qkv, qkv, qkv, gn_w.reshape(1, w)]
    if s0 is not None:
        in_specs.append(pl.BlockSpec((None, None, 2, n_pairs, BLOCK, LANES), lambda r: (r, layer, 0, 0, 0, 0)))
        args.append(s0)
    out_shape = [jax.ShapeDtypeStruct((R, L, w), F32)]
    out_specs = [pl.BlockSpec((None, L, w), lambda r: (r, 0, 0))]
    if out_state:
        out_shape.append(jax.ShapeDtypeStruct((R, 2, n_pairs, BLOCK, LANES), F32))
        out_specs.append(pl.BlockSpec((None, 2, n_pairs, BLOCK, LANES), lambda r: (r, 0, 0, 0, 0)))
    res = pl.pallas_call(
        functools.partial(_ret_kernel, nc=nc, has_s0=s0 is not None, out_state=out_state),
        out_shape=out_shape, grid=(R,), in_specs=in_specs, out_specs=out_specs,
        scratch_shapes=[pltpu.VMEM((nc, n_pairs, BLOCK, LANES), F32),
                        pltpu.VMEM((nc, n_pairs, BLOCK, LANES), F32),
                        pltpu.VMEM((n_pairs, 8, BLOCK, LANES), F32)],
        compiler_params=_cp(1), name="retention",
    )(*args)
    return res if out_state else res[0]


def _pair_states_to_heads(st):
    a = st[..., :HEAD_DIM, :HEAD_DIM]
    b = st[..., HEAD_DIM:, HEAD_DIM:]
    return jnp.stack([a, b], axis=3).reshape(st.shape[0], 2, H_C, HEAD_DIM, HEAD_DIM)


def _heads_to_pair_states(s):
    s = s.reshape(*s.shape[:3], H_C // 2, 2, HEAD_DIM, HEAD_DIM)
    z = jnp.zeros_like(s[..., 0, :, :])
    top = jnp.concatenate([s[..., 0, :, :], z], axis=-1)
    bot = jnp.concatenate([z, s[..., 1, :, :]], axis=-1)
    return jnp.concatenate([top, bot], axis=-2)


def _outproj_kernel(oa_ref, ob_ref, oc_ref, x_ref, g1_ref, sc2_ref, sh2_ref, w_ref, lg_ref, lb_ref, wr_ref,
                    x1_ref, h2_ref, aff_ref):
    wa, wb = H_A * HEAD_DIM, H_B * HEAD_DIM
    tm = x_ref.shape[0]
    part = tm // OUTPROJ_PARTS
    for r0 in range(0, tm, part):
        rs = slice(r0, r0 + part)
        mix = (_dot(oa_ref[rs, :].astype(BF16), w_ref[0:wa, :])
               + _dot(ob_ref[rs, :].astype(BF16), w_ref[wa:wa + wb, :])
               + _dot(oc_ref[rs, :].astype(BF16), w_ref[wa + wb:, :]))
        x1 = _layernorm(ALPHA * x_ref[rs, :] + g1_ref[...] * mix, lg_ref[...], lb_ref[...])
        x1_ref[rs, :] = x1
        h2 = (x1 * (1.0 + sc2_ref[...]) + sh2_ref[...]).astype(BF16)
        h2_ref[rs, :] = h2
        logits = _dot(h2, wr_ref[...])
        e = jnp.exp(logits - logits.max(axis=-1, keepdims=True))
        aff_ref[rs, :] = e / e.sum(axis=-1, keepdims=True)


def _outproj(oa, ob, oc, x, mods, w_out_bf16, ln_g, ln_b, w_router_bf16, layer, tm):
    R, L, _ = x.shape
    row = lambda w: pl.BlockSpec((None, tm, w), lambda r, i: (r, i, 0))
    vec = pl.BlockSpec((1, D_MODEL), lambda r, i: (0, 0))
    return pl.pallas_call(
        _outproj_kernel,
        out_shape=[jax.ShapeDtypeStruct((R, L, D_MODEL), F32),
                   jax.ShapeDtypeStruct((R, L, D_MODEL), BF16),
                   jax.ShapeDtypeStruct((R, L, N_EXPERTS), F32)],
        grid=(R, L // tm),
        in_specs=[row(H_A * HEAD_DIM), row(H_B * HEAD_DIM), row(H_C * HEAD_DIM), row(D_MODEL),
                  _mod_spec(2, 2), _mod_spec(4, 2), _mod_spec(3, 2),
                  pl.BlockSpec((None, D_MODEL, D_MODEL), lambda r, i: (layer, 0, 0)), vec, vec,
                  pl.BlockSpec((None, D_MODEL, N_EXPERTS), lambda r, i: (layer, 0, 0))],
        out_specs=[row(D_MODEL), row(D_MODEL), row(N_EXPERTS)],
        compiler_params=_cp(2), name="out_proj_ln_router",
    )(oa, ob, oc, x, mods, mods, mods, w_out_bf16, ln_g.reshape(1, -1), ln_b.reshape(1, -1), w_router_bf16)


CUMSUM_CHUNK = 256


def _excl_cumsum_lanes(x):
    rows, n = x.shape
    w = min(CUMSUM_CHUNK, n)
    tri = (lax.broadcasted_iota(I32, (w, w), 0) < lax.broadcasted_iota(I32, (w, w), 1))
    tri = jnp.where(tri, 1.0, 0.0).astype(BF16)
    carry = jnp.zeros((rows, 1), F32)
    outs = []
    for c in range(0, n, w):
        xc = x[:, c:c + w]
        outs.append(_dot(xc.astype(BF16), tri) + carry)
        carry = carry + xc.sum(axis=-1, keepdims=True)
    return outs[0] if len(outs) == 1 else jnp.concatenate(outs, axis=1)


def _select_kernel(a_ref, pos_ref, *, cap):
    rb, ne, n = a_ref.shape
    rows = rb * ne
    a = a_ref[...].reshape(rows, n)
    capf = float(cap)

    def body(_, c):
        lo, hi = c
        mid = lo + ((hi - lo + 1) >> 1)
        cnt = jnp.where(a >= pltpu.bitcast(mid, F32), 1.0, 0.0).sum(axis=-1, keepdims=True)
        ok = cnt >= capf
        return jnp.where(ok, mid, lo), jnp.where(ok, hi, mid - 1)

    one_bits = 0x3F800000
    lo, _ = lax.fori_loop(0, 30, body, (jnp.zeros((rows, 1), I32), jnp.full((rows, 1), one_bits, I32)))
    thr = pltpu.bitcast(lo, F32)
    gt = jnp.where(a > thr, 1.0, 0.0)
    eq = jnp.where(a == thr, 1.0, 0.0)
    need = capf - gt.sum(axis=-1, keepdims=True)
    sel = gt + eq * jnp.where(_excl_cumsum_lanes(eq) < need, 1.0, 0.0)
    slot = _excl_cumsum_lanes(sel)
    pos_ref[...] = jnp.where(sel > 0.5, slot, -1.0).astype(I32).reshape(rb, ne, n)


def _select(aff_t, cap):
    R, ne, n = aff_t.shape
    rb = min(R, max(1, 128 * 2048 // (ne * n)))
    return pl.pallas_call(
        functools.partial(_select_kernel, cap=cap),
        out_shape=jax.ShapeDtypeStruct((R, ne, n), I32),
        grid=(R // rb,),
        in_specs=[pl.BlockSpec((rb, ne, n), lambda r: (r, 0, 0))],
        out_specs=pl.BlockSpec((rb, ne, n), lambda r: (r, 0, 0)),
        compiler_params=_cp(1), name="expert_select",
    )(aff_t)


def _gather_kernel(pos_ref, aff_ref, h_ref, x_ref, gate_ref, *, cap):
    eb, _, n = pos_ref.shape
    slot = lax.broadcasted_iota(I32, (cap, n), 0)
    parts = []
    for j in range(eb):
        hit = slot == pos_ref[j]
        parts.append(jnp.where(hit, 1.0, 0.0).astype(BF16))
        gate_ref[j] = jnp.where(hit, aff_ref[j], 0.0).sum(axis=-1, keepdims=True)
    onehot = parts[0] if eb == 1 else jnp.concatenate(parts, axis=0)
    xe = _dot(onehot, h_ref[...]).astype(BF16)
    for j in range(eb):
        x_ref[j] = xe[j * cap:(j + 1) * cap]


def _gather(pos, aff_t, h2, cap):
    R, ne, n = pos.shape
    eb = max(1, min(ne, ROW_TILE // cap))
    row = pl.BlockSpec((None, eb, 1, n), lambda r, e: (r, e, 0, 0))
    return pl.pallas_call(
        functools.partial(_gather_kernel, cap=cap),
        out_shape=[jax.ShapeDtypeStruct((ne, R * cap, D_MODEL), BF16),
                   jax.ShapeDtypeStruct((ne, R * cap, 1), F32)],
        grid=(R, ne // eb),
        in_specs=[row, row, pl.BlockSpec((None, n, D_MODEL), lambda r, e: (r, 0, 0))],
        out_specs=[pl.BlockSpec((eb, cap, D_MODEL), lambda r, e: (e, r, 0)),
                   pl.BlockSpec((eb, cap, 1), lambda r, e: (e, r, 0))],
        compiler_params=_cp(2), name="expert_gather",
    )(pos.reshape(R, ne, 1, n), aff_t.reshape(R, ne, 1, n), h2)


FF_TILE = 512


def _ffn_kernel(x0_ref, x1_ref, wa_ref, wb_ref, wd_ref, g0_ref, g1_ref, y0_ref, y1_ref, acc_ref):
    f = pl.program_id(1)

    @pl.when(f == 0)
    def _():
        acc_ref[...] = jnp.zeros_like(acc_ref)

    x = jnp.concatenate([x0_ref[...], x1_ref[...]], axis=0)
    a = _dot(x, wa_ref[...].astype(BF16))
    b = _dot(x, wb_ref[...].astype(BF16))
    acc_ref[...] += _dot((_silu(a) * b).astype(BF16), wd_ref[...].astype(BF16))

    @pl.when(f == pl.num_programs(1) - 1)
    def _():
        s0 = x0_ref.shape[0]
        y0_ref[...] = (acc_ref[:s0, :] * g0_ref[...]).astype(BF16)
        y1_ref[...] = (acc_ref[s0:, :] * g1_ref[...]).astype(BF16)


def _experts(xe0, gates0, xe1, gates1, w_gate_up, w_down, layer):
    ne, S0, _ = xe0.shape
    S1 = xe1.shape[1]
    nf = D_FF // FF_TILE
    rows = lambda S, w: pl.BlockSpec((None, S, w), lambda e, f: (e, 0, 0))
    return pl.pallas_call(
        _ffn_kernel,
        out_shape=[jax.ShapeDtypeStruct((ne, S0, D_MODEL), BF16), jax.ShapeDtypeStruct((ne, S1, D_MODEL), BF16)],
        grid=(ne, nf),
        in_specs=[rows(S0, D_MODEL), rows(S1, D_MODEL),
                  pl.BlockSpec((None, None, D_MODEL, FF_TILE), lambda e, f: (layer, e, 0, f)),
                  pl.BlockSpec((None, None, D_MODEL, FF_TILE), lambda e, f: (layer, e, 0, nf + f)),
                  pl.BlockSpec((None, None, FF_TILE, D_MODEL), lambda e, f: (layer, e, f, 0)),
                  rows(S0, 1), rows(S1, 1)],
        out_specs=[rows(S0, D_MODEL), rows(S1, D_MODEL)],
        scratch_shapes=[pltpu.VMEM((S0 + S1, D_MODEL), F32)],
        compiler_params=_cp(2), name="expert_ffn",
    )(xe0, xe1, w_gate_up, w_gate_up, w_down, gates0, gates1)


def _scatter_kernel(pos_ref, y_ref, x1_ref, g2_ref, lg_ref, lb_ref, o_ref, *, cap):
    tm, ne = pos_ref.shape
    cw = max(cap, LANES)
    epc = cw // cap
    lane = lax.broadcasted_iota(I32, (tm, cw), 1)
    pos = pos_ref[...]
    def onehot_chunk(c):
        onehot = jnp.zeros((tm, cw), F32)
        for jj in range(epc):
            col = pos[:, c * epc + jj:c * epc + jj + 1]
            tgt = jnp.where(col >= 0, col + jj * cap, -1)
            onehot = onehot + jnp.where(lane == tgt, 1.0, 0.0)
        return onehot.astype(BF16)

    n_chunks = ne // epc
    if ne * cap <= ROW_TILE:
        acc = _dot(jnp.concatenate([onehot_chunk(c) for c in range(n_chunks)], axis=1),
                   y_ref[...].reshape(ne * cap, D_MODEL))
    else:
        acc = jnp.zeros((tm, D_MODEL), F32)
        for c in range(n_chunks):
            yc = y_ref[c] if epc == 1 else y_ref[c * epc:(c + 1) * epc].reshape(cw, D_MODEL)
            acc = acc + _dot(onehot_chunk(c), yc)
    o_ref[...] = _layernorm(ALPHA * x1_ref[...] + g2_ref[...] * acc, lg_ref[...], lb_ref[...])


def _scatter(pos_tok, y, x1, mods, ln_g, ln_b, cap, tm):
    R, n, ne = pos_tok.shape
    vec = pl.BlockSpec((1, D_MODEL), lambda r, i: (0, 0))
    return pl.pallas_call(
        functools.partial(_scatter_kernel, cap=cap),
        out_shape=jax.ShapeDtypeStruct((R, n, D_MODEL), F32),
        grid=(R, n // tm),
        in_specs=[pl.BlockSpec((None, tm, ne), lambda r, i: (r, i, 0)),
                  pl.BlockSpec((ne, cap, D_MODEL), lambda r, i: (0, r, 0)),
                  pl.BlockSpec((None, tm, D_MODEL), lambda r, i: (r, i, 0)),
                  _mod_spec(5, 2), vec, vec],
        out_specs=pl.BlockSpec((None, tm, D_MODEL), lambda r, i: (r, i, 0)),
        compiler_params=_cp(2), name="expert_scatter_ln",
    )(pos_tok, y, x1, mods, ln_g.reshape(1, -1), ln_b.reshape(1, -1))


def _capacity(n):
    return CAPACITY_FACTOR * n // N_EXPERTS


def _route(h2, aff):
    cap = _capacity(h2.shape[1])
    aff_t = jnp.swapaxes(aff, 1, 2)
    pos = _select(aff_t, cap)
    xe, gates = _gather(pos, aff_t, h2, cap)
    return pos, xe, gates


def _channel_mixers(groups, l, w_gate_up, w_down, ln2_g, ln2_b):
    routed = [_route(g["h2"], g["aff"]) for g in groups]
    ys = _experts(routed[0][1], routed[0][2], routed[1][1], routed[1][2], w_gate_up, w_down, l)
    return [_scatter(jnp.swapaxes(r[0], 1, 2), y, g["x1"], g["mods"], ln2_g[l], ln2_b[l],
                     _capacity(g["h2"].shape[1]), g["tm"]) for g, r, y in zip(groups, routed, ys)]


def kernel(x_prompt, x_sample, cache_attn_a_k, cache_attn_a_v, cache_attn_b_k, cache_attn_b_v, state_ret, c, c_ctx,
           w_ada, b_ada, w_in, w_out, attn_sink, na_rpb, ret_decay, ret_gn, ln1_g, ln1_b, ln2_g, ln2_b,
           w_router, w_gate_up, w_down):
    n_ctx, L_ctx, _ = x_prompt.shape
    n_lat, L_lat, _ = x_sample.shape
    P = cache_attn_a_k.shape[2]

    cond = jnp.concatenate([c_ctx[None, :], c, jnp.zeros((8 - 1 - n_lat, D_MODEL), F32)], axis=0)
    mod = _modulation(cond, w_ada, b_ada)
    w_in16, w_out16, w_r16 = w_in.astype(BF16), w_out.astype(BF16), w_router.astype(BF16)
    rope_tabs = _rope_tables(L_lat)
    cak = cache_attn_a_k.reshape(n_lat, DEPTH, P, KVH_A * HEAD_DIM)
    cav = cache_attn_a_v.reshape(n_lat, DEPTH, P, KVH_A * HEAD_DIM)
    cbk = cache_attn_b_k.reshape(n_lat, DEPTH, P, H_B * HEAD_DIM)
    cbv = cache_attn_b_v.reshape(n_lat, DEPTH, P, H_B * HEAD_DIM)
    s0_pairs = _heads_to_pair_states(state_ret)

    xc, xs = x_prompt, x_sample
    tm = ROW_TILE
    a_k, a_v, b_k, b_v, st = [], [], [], [], []
    flat = lambda a: a.reshape(1, n_ctx * L_ctx, a.shape[-1])
    unflat = lambda a: a.reshape(n_ctx, L_ctx, a.shape[-1])
    for l in range(DEPTH):
        mod_c = mod[l, 0:1][:, None, :]
        mods_c = jnp.broadcast_to(mod_c, (n_ctx, 1, 6 * D_MODEL))
        qkv, ka, va, kb, vb = [unflat(t) for t in _project(flat(xc), mod_c, w_in16, l, tm, emit_kv=True)]
        oa = _attn_a_ctx(qkv, attn_sink[l])
        ob = _attn_b_ctx(qkv)
        oc, s_l = _retention(qkv, ret_decay[l], ret_gn[l], out_state=True)
        x1, h2, aff = [unflat(t) for t in _outproj(flat(oa), flat(ob), flat(oc), flat(xc), mod_c, w_out16,
                                                   ln1_g[l], ln1_b[l], w_r16, l, tm)]
        ctx = dict(x1=x1, h2=h2, aff=aff, mods=mods_c, tm=L_ctx)
        a_k.append(ka.reshape(n_ctx, L_ctx, KVH_A, HEAD_DIM))
        a_v.append(va.reshape(n_ctx, L_ctx, KVH_A, HEAD_DIM))
        b_k.append(kb.reshape(n_ctx, L_ctx, H_B, HEAD_DIM))
        b_v.append(vb.reshape(n_ctx, L_ctx, H_B, HEAD_DIM))
        st.append(_pair_states_to_heads(s_l))

        mods_s = mod[l, 1:1 + n_lat][:, None, :]
        qkv = _project(xs, mods_s, w_in16, l, tm, rope_tabs)
        oa = _attn_a_lat(qkv, attn_sink[l], cak, cav, l)
        ob = _attn_b_lat(qkv, cbk, cbv, _bias_table(na_rpb[l]), l)
        oc = _retention(qkv, ret_decay[l], ret_gn[l], s0=s0_pairs, layer=l)
        x1, h2, aff = _outproj(oa, ob, oc, xs, mods_s, w_out16, ln1_g[l], ln1_b[l], w_r16, l, tm)
        lat = dict(x1=x1, h2=h2, aff=aff, mods=mods_s, tm=tm)

        xc, xs = _channel_mixers([ctx, lat], l, w_gate_up, w_down, ln2_g, ln2_b)

    return (xc, xs, jnp.stack(a_k, 1), jnp.stack(a_v, 1), jnp.stack(b_k, 1), jnp.stack(b_v, 1),
            jnp.stack(st, 1))
```

```python
import functools

import numpy as np
import jax
import jax.numpy as jnp
from jax import lax
from jax.experimental import pallas as pl
from jax.experimental.pallas import tpu as pltpu

F32 = jnp.float32
BF16 = jnp.bfloat16
I32 = jnp.int32

D_MODEL = 1024
DEPTH = 2
GRID_W = 64
HEAD_DIM = 64
H_A, KVH_A, H_B, H_C = 8, 2, 4, 4
D_IN = 2560
BLOCK = 128
WINDOW = 128
NA_ROWS, NA_COLS = 8, 16
NA_WIN_BLOCKS = 5
NA_QBLOCKS = 4
WIN_QBLOCKS = 2
CTX_GROUP_A = 4
CTX_GROUP_B = 2
ROPE_BASE = 10000.0
N_EXPERTS = 16
CAPACITY_FACTOR = 2
D_FF = 1024
LN_EPS = 1e-6
ALPHA = (2 * DEPTH) ** 0.25
NEG = -1e30
LANES = 128
QK_SCALE = HEAD_DIM ** -0.5
LOG2E = float(np.log2(np.e))
ATTN_SCALE = QK_SCALE * LOG2E
VMEM_LIMIT = 56 * 1024 * 1024
ROW_TILE = 1024
OUTPROJ_PARTS = 2

COL_QA, COL_KA, COL_VA = 0, 4, 5
COL_QB, COL_KB, COL_VB = 3, 4, 5
COL_QC, COL_KC, COL_VC, COL_GC = 6, 7, 8, 9


def _cp(n_axes):
    return pltpu.CompilerParams(dimension_semantics=("arbitrary",) * n_axes, vmem_limit_bytes=VMEM_LIMIT)


def _dot(a, b):
    return jnp.dot(a, b, preferred_element_type=F32)


def _dot_t(a, b):
    return lax.dot_general(a, b, (((1,), (1,)), ((), ())), preferred_element_type=F32)


def _silu(x):
    return x / (1.0 + jnp.exp(-x))


def _lane_low(shape):
    return lax.broadcasted_iota(I32, shape, len(shape) - 1) < HEAD_DIM


def _layernorm(y, g, b):
    mu = jnp.mean(y, axis=-1, keepdims=True)
    d = y - mu
    var = jnp.mean(d * d, axis=-1, keepdims=True)
    return d * lax.rsqrt(var + LN_EPS) * g + b


def _mod_kernel(c_ref, w_ref, b_ref, o_ref):
    s = _silu(c_ref[...]).astype(BF16)
    o_ref[...] = _dot(s, w_ref[...].astype(BF16)) + b_ref[...]


def _modulation(cond, w_ada, b_ada):
    tn = 1536
    return pl.pallas_call(
        _mod_kernel,
        out_shape=jax.ShapeDtypeStruct((DEPTH, 8, 6 * D_MODEL), F32),
        grid=(DEPTH, 6 * D_MODEL // tn),
        in_specs=[pl.BlockSpec((8, D_MODEL), lambda l, j: (0, 0)),
                  pl.BlockSpec((None, D_MODEL, tn), lambda l, j: (l, 0, j)),
                  pl.BlockSpec((None, 1, tn), lambda l, j: (l, 0, j))],
        out_specs=pl.BlockSpec((None, 8, tn), lambda l, j: (l, 0, j)),
        compiler_params=_cp(2), name="modulation",
    )(cond, w_ada, b_ada.reshape(DEPTH, 1, 6 * D_MODEL))


def _mod_spec(k, n_grid):
    if n_grid == 1:
        return pl.BlockSpec((None, 1, D_MODEL), lambda r: (r, 0, k))
    return pl.BlockSpec((None, 1, D_MODEL), lambda r, i: (r, 0, k))


def _rope_tables(L):
    t = np.arange(L)
    pos = (t // GRID_W, t % GRID_W)
    half = HEAD_DIM // 2
    inv = 1.0 / (ROPE_BASE ** (np.arange(0, half, 2) / half))
    cos = np.zeros((L, HEAD_DIM)); s_up = np.zeros((L, HEAD_DIM)); s_dn = np.zeros((L, HEAD_DIM))
    for part in range(2):
        ang = (pos[part][:, None] * inv[None, :]).astype(np.float32).astype(np.float64)
        q = half // 2
        base = part * half
        cos[:, base:base + q] = np.cos(ang); cos[:, base + q:base + half] = np.cos(ang)
        s_up[:, base:base + q] = -np.sin(ang)
        s_dn[:, base + q:base + half] = np.sin(ang)
    tile = lambda a: jnp.asarray(np.tile(a, (1, LANES // HEAD_DIM)), F32)
    return tile(cos), tile(s_up), tile(s_dn)


KV_COLUMNS = ((512, 128), (640, 128), (1024, 256), (1280, 256))


def _proj_kernel(*refs, rope, emit_kv):
    kv_refs = ()
    if emit_kv:
        refs, kv_refs = refs[:-len(KV_COLUMNS)], refs[-len(KV_COLUMNS):]
    if rope:
        x_ref, sc_ref, sh_ref, w_ref, cos_ref, sup_ref, sdn_ref, o_ref = refs
    else:
        x_ref, sc_ref, sh_ref, w_ref, o_ref = refs
    h = (x_ref[...] * (1.0 + sc_ref[...]) + sh_ref[...]).astype(BF16)
    n_rope = (H_A + KVH_A) * HEAD_DIM
    step = 512
    for j in range(0, D_IN, step):
        acc = _dot(h, w_ref[:, j:j + step])
        if rope and j < n_rope:
            q = HEAD_DIM // 4
            for c in range(0, step, LANES):
                a = acc[:, c:c + LANES]
                if j + c < n_rope:
                    a = (a * cos_ref[...] + pltpu.roll(a, LANES - q, 1) * sup_ref[...]
                         + pltpu.roll(a, q, 1) * sdn_ref[...])
                o_ref[:, j + c:j + c + LANES] = a
        else:
            o_ref[:, j:j + step] = acc
        for ref, (c0, w) in zip(kv_refs, KV_COLUMNS):
            if j <= c0 < j + step:
                n_req, _, seq = ref.shape
                for g in range(n_req):
                    ref[g] = acc[g * seq:(g + 1) * seq, c0 - j:c0 - j + w].T


def _project(x, mods, w_bf16, layer, tm, rope_tabs=None, kv_seq=None):
    R, L, _ = x.shape
    rope = rope_tabs is not None
    emit_kv = kv_seq is not None
    assert not (rope and emit_kv)
    out_shape = [jax.ShapeDtypeStruct((R, L, D_IN), F32)]
    out_specs = [pl.BlockSpec((None, tm, D_IN), lambda r, i: (r, i, 0))]
    if emit_kv:
        assert R == 1 and tm % kv_seq == 0
        out_shape += [jax.ShapeDtypeStruct((L // kv_seq, w, kv_seq), F32) for _, w in KV_COLUMNS]
        out_specs += [pl.BlockSpec((tm // kv_seq, w, kv_seq), lambda r, i: (i, 0, 0)) for _, w in KV_COLUMNS]
    in_specs = [pl.BlockSpec((None, tm, D_MODEL), lambda r, i: (r, i, 0)),
                _mod_spec(1, 2), _mod_spec(0, 2),
                pl.BlockSpec((None, D_MODEL, D_IN), lambda r, i: (layer, 0, 0))]
    args = [x, mods, mods, w_bf16]
    if rope:
        in_specs += [pl.BlockSpec((tm, LANES), lambda r, i: (i, 0))] * 3
        args += list(rope_tabs)
    res = pl.pallas_call(
        functools.partial(_proj_kernel, rope=rope, emit_kv=emit_kv),
        out_shape=out_shape, grid=(R, L // tm), in_specs=in_specs, out_specs=out_specs,
        compiler_params=_cp(2), name="in_proj",
    )(*args)
    return res if emit_kv else res[0]


def _mask_head(x, sub):
    low = _lane_low(x.shape)
    return jnp.where(low, x, 0.0) if sub == 0 else jnp.where(low, 0.0, x)


def _key_variant(k, sub):
    return _mask_head(k, sub).astype(BF16)


def _value_variant(v, sub):
    low = _lane_low(v.shape)
    return (jnp.where(low, v, 1.0) if sub == 0 else jnp.where(low, 1.0, v)).astype(BF16)


def _attend(lhs, keys, values, biases, sub, sink=None):
    scores = []
    for k, b in zip(keys, biases):
        s = _dot_t(lhs, k)
        scores.append(s if b is None else s + b)
    tiles = [s[:, c:c + LANES] for s in scores for c in range(0, s.shape[1], LANES)]
    while len(tiles) > 1:
        tiles = [jnp.maximum(a, b) for a, b in zip(tiles[0::2], tiles[1::2])] + ([tiles[-1]] if len(tiles) % 2 else [])
    m = jnp.broadcast_to(tiles[0].max(axis=-1, keepdims=True), tiles[0].shape)
    if sink is not None:
        m = jnp.maximum(m, sink)
    o = None
    for s, v in zip(scores, values):
        p = [jnp.exp2(s[:, c:c + LANES] - m).astype(BF16) for c in range(0, s.shape[1], LANES)]
        pv = _dot(p[0] if len(p) == 1 else jnp.concatenate(p, axis=1), v)
        o = pv if o is None else o + pv
    if sink is not None:
        low = _lane_low(o.shape)
        extra = jnp.exp2(sink - m)
        o = o + (jnp.where(low, 0.0, extra) if sub == 0 else jnp.where(low, extra, 0.0))
    return o / pltpu.roll(o, HEAD_DIM, 1)


def _gqa_variants(k, v, hk):
    low = _lane_low(k.shape)
    k_sw, v_sw = pltpu.roll(k, HEAD_DIM, 1), pltpu.roll(v, HEAD_DIM, 1)
    k_dup = jnp.where(low, k, k_sw) if hk == 0 else jnp.where(low, k_sw, k)
    return k_dup.astype(BF16), jnp.where(low, v if hk == 0 else v_sw, 1.0).astype(BF16)


def _fill_gqa_variants(k_src, v_src, kv_ref, vv_ref, off, n, feature_major=False):
    step = 256
    for c in range(0, n, step):
        k = k_src[:, c:c + step].T if feature_major else k_src[c:c + step, :]
        v = v_src[:, c:c + step].T if feature_major else v_src[c:c + step, :]
        for hk in range(KVH_A):
            kd, va = _gqa_variants(k, v, hk)
            kv_ref[hk, off + c:off + c + step, :] = kd
            vv_ref[hk, off + c:off + c + step, :] = va


def _fill_mha_variants(k_src, v_src, kv_ref, vv_ref, off, n, feature_major=False):
    step = 256
    for c in range(0, n, step):
        rows = slice(off + c, off + c + step)
        for p in range(H_B // 2):
            cs = slice(LANES * p, LANES * p + LANES)
            k = k_src[cs, c:c + step].T if feature_major else k_src[c:c + step, cs]
            v = v_src[cs, c:c + step].T if feature_major else v_src[c:c + step, cs]
            for sub in range(2):
                kv_ref[p, sub, rows, :] = _key_variant(k, sub)
                vv_ref[p, sub, rows, :] = _value_variant(v, sub)


def _sink_col(sink_ref, heads, rows):
    return jnp.concatenate([jnp.full((rows, LANES), sink_ref[h] * LOG2E, F32) for h in heads], axis=0)


def _gqa_attend(sink_ref, q_ref, o_ref, hk, keys, values, biases):
    rows = q_ref.shape[0]
    g = H_A // KVH_A
    heads = list(range(g * hk, g * hk + g))
    lhs = jnp.concatenate([_mask_head(q_ref[:, LANES * (h // 2):LANES * (h // 2) + LANES], h % 2) for h in heads],
                          axis=0)
    lhs = (lhs * ATTN_SCALE).astype(BF16)
    stacked = [None if b is None else jnp.concatenate([b] * g, axis=0) for b in biases]
    out = _attend(lhs, keys, values, stacked, 0, _sink_col(sink_ref, heads, rows))
    low = _lane_low((rows, LANES))
    for j in range(0, g, 2):
        h = heads[j]
        odd = pltpu.roll(out[(j + 1) * rows:(j + 2) * rows], HEAD_DIM, 1)
        pair = jnp.where(low, out[j * rows:(j + 1) * rows], odd)
        o_ref[:, LANES * (h // 2):LANES * (h // 2) + LANES] = pair.astype(o_ref.dtype)


def _attn_a_ctx_kernel(sink_ref, q_ref, k_ref, v_ref, o_ref):
    for g in range(q_ref.shape[0]):
        for hk in range(KVH_A):
            kd, va = _gqa_variants(k_ref[g], v_ref[g], hk)
            _gqa_attend(sink_ref, q_ref.at[g], o_ref.at[g], hk, [kd], [va], [None])


def _window_mask_table():
    nq = WIN_QBLOCKS
    row = np.arange(nq * BLOCK)[:, None]
    col = np.arange((nq + 2) * BLOCK)[None, :]
    tabs = [np.where(np.abs(col - off * BLOCK - row) <= WINDOW, 0.0, NEG) for off in range(3)]
    return jnp.asarray(np.stack(tabs), F32)


def _attn_a_lat_kernel(sink_ref, q_ref, k_ref, v_ref, kc_ref, vc_ref, mask_ref, o_ref, kv_ref, vv_ref, *, nb):
    i = pl.program_id(1)
    L, P = k_ref.shape[0], kc_ref.shape[1]

    @pl.when(i == 0)
    def _():
        _fill_gqa_variants(k_ref, v_ref, kv_ref, vv_ref, 0, L)
        _fill_gqa_variants(kc_ref, vc_ref, kv_ref, vv_ref, L, P, feature_major=True)

    nq = WIN_QBLOCKS
    wk = (nq + 2) * BLOCK
    sblk = jnp.clip(nq * i - 1, 0, nb - (nq + 2))
    start = pl.multiple_of(sblk * BLOCK, BLOCK)
    bias = mask_ref[nq * i - sblk]
    for hk in range(KVH_A):
        ks = [kv_ref[hk, pl.ds(start, wk), :], kv_ref[hk, L:L + P, :]]
        vs = [vv_ref[hk, pl.ds(start, wk), :], vv_ref[hk, L:L + P, :]]
        _gqa_attend(sink_ref, q_ref, o_ref, hk, ks, vs, [bias, None])


def _smem_spec():
    return pl.BlockSpec(memory_space=pltpu.SMEM)


def _attn_a_ctx(qkv, sink):
    R, L, _ = qkv.shape
    wq, wkv = H_A * HEAD_DIM, KVH_A * HEAD_DIM
    G = CTX_GROUP_A
    assert R % G == 0
    return pl.pallas_call(
        _attn_a_ctx_kernel,
        out_shape=jax.ShapeDtypeStruct((R, L, wq), F32),
        grid=(R // G,),
        in_specs=[_smem_spec(),
                  pl.BlockSpec((G, L, wq), lambda r: (r, 0, COL_QA)),
                  pl.BlockSpec((G, L, wkv), lambda r: (r, 0, COL_KA)),
                  pl.BlockSpec((G, L, wkv), lambda r: (r, 0, COL_VA))],
        out_specs=pl.BlockSpec((G, L, wq), lambda r: (r, 0, 0)),
        compiler_params=_cp(1), name="attn_a_ctx",
    )(sink, qkv, qkv, qkv)


def _attn_a_lat(qkv, sink, cache_k, cache_v, layer):
    R, L, _ = qkv.shape
    nb = L // BLOCK
    nq = WIN_QBLOCKS
    assert nb % nq == 0 and nb >= nq + 2
    P = cache_k.shape[3]
    wq, wkv = H_A * HEAD_DIM, KVH_A * HEAD_DIM
    cache_spec = pl.BlockSpec((None, None, wkv, P), lambda r, i: (r, layer, 0, 0))
    masks = _window_mask_table()
    return pl.pallas_call(
        functools.partial(_attn_a_lat_kernel, nb=nb),
        out_shape=jax.ShapeDtypeStruct((R, L, wq), F32),
        grid=(R, nb // nq),
        in_specs=[_smem_spec(),
                  pl.BlockSpec((None, nq * BLOCK, wq), lambda r, i: (r, i, COL_QA)),
                  pl.BlockSpec((None, L, wkv), lambda r, i: (r, 0, COL_KA)),
                  pl.BlockSpec((None, L, wkv), lambda r, i: (r, 0, COL_VA)),
                  cache_spec, cache_spec,
                  pl.BlockSpec(masks.shape, lambda r, i: (0, 0, 0))],
        out_specs=pl.BlockSpec((None, nq * BLOCK, wq), lambda r, i: (r, i, 0)),
        scratch_shapes=[pltpu.VMEM((KVH_A, L + P, LANES), BF16),
                        pltpu.VMEM((KVH_A, L + P, LANES), BF16)],
        compiler_params=_cp(2), name="attn_a_lat",
    )(sink, qkv, qkv, qkv, cache_k, cache_v, masks)


def _attn_b_ctx_kernel(q_ref, k_ref, v_ref, o_ref):
    G, L = q_ref.shape[:2]
    low = _lane_low((L, LANES))
    for g in range(G):
        for p in range(H_B // 2):
            cs = slice(LANES * p, LANES * p + LANES)
            lhs = (q_ref[g, :, cs] * ATTN_SCALE).astype(BF16)
            k, v = k_ref[g, :, cs], v_ref[g, :, cs]
            outs = [_attend(lhs, [_key_variant(k, sub)], [_value_variant(v, sub)], [None], sub)
                    for sub in range(2)]
            o_ref[g, :, cs] = jnp.where(low, outs[0], outs[1]).astype(o_ref.dtype)


def _bias_table_kernel(rpb_ref, o_ref, t_ref):
    n_dr, n_dc = 2 * NA_ROWS - 1, 2 * NA_COLS - 1
    n_t = o_ref.shape[1]
    shape = (GRID_W, LANES)
    cq = lax.broadcasted_iota(I32, shape, 0)
    lane = lax.broadcasted_iota(I32, shape, 1)
    right = lane >= GRID_W
    ck = lane & (GRID_W - 1)
    cs = jnp.clip(cq - NA_COLS // 2, 0, GRID_W - NA_COLS)
    col_ok = (ck >= cs) & (ck < cs + NA_COLS)
    dc = ck - cq + (NA_COLS - 1)
    for j in (0, 1, n_t - 1, n_t):
        t_ref[j] = jnp.zeros(shape, F32)

    def build(r, carry, h):
        acc = jnp.zeros(shape, F32)
        for d in range(n_dc):
            acc = jnp.where(dc == d, rpb_ref[(h * n_dr + r) * n_dc + d] * LOG2E, acc)
        t_ref[r + 2] = jnp.where(col_ok, acc, NEG)
        return carry

    def compose(t, carry, h):
        o_ref[h, t] = jnp.where(right, t_ref[t + 1], t_ref[t])
        return carry

    for h in range(H_B):
        lax.fori_loop(0, n_dr, functools.partial(build, h=h), 0)
        lax.fori_loop(0, n_t, functools.partial(compose, h=h), 0)


def _bias_table(rpb_l):
    n_t = 2 * (NA_ROWS + 1)
    return pl.pallas_call(
        _bias_table_kernel,
        out_shape=jax.ShapeDtypeStruct((H_B, n_t, GRID_W, LANES), F32),
        in_specs=[_smem_spec()],
        out_specs=pl.BlockSpec(memory_space=pltpu.VMEM),
        scratch_shapes=[pltpu.VMEM((n_t + 1, GRID_W, LANES), F32)],
        name="na_bias_table",
    )(rpb_l.reshape(-1))


def _attn_b_lat_kernel(q_ref, k_ref, v_ref, kc_ref, vc_ref, tp_ref, o_ref, kv_ref, vv_ref, *, nb, rows_total):
    i = pl.program_id(1)
    L, P = k_ref.shape[0], kc_ref.shape[1]

    @pl.when(i == 0)
    def _():
        _fill_mha_variants(k_ref, v_ref, kv_ref, vv_ref, 0, L)
        _fill_mha_variants(kc_ref, vc_ref, kv_ref, vv_ref, L, P, feature_major=True)

    nq = NA_QBLOCKS
    nwin = NA_WIN_BLOCKS + nq - 1
    wk = nwin * BLOCK
    rows = nq * BLOCK
    sblk = jnp.clip(nq * i - 2, 0, nb - nwin)
    start = pl.multiple_of(sblk * BLOCK, BLOCK)
    shape = (rows, wk)
    rq = 2 * nq * i + (lax.broadcasted_iota(I32, shape, 0) >> 6)
    rk = 2 * sblk + (lax.broadcasted_iota(I32, shape, 1) >> 6)
    rs = jnp.clip(rq - NA_ROWS // 2, 0, rows_total - NA_ROWS)
    row_bias = jnp.where((rk >= rs) & (rk < rs + NA_ROWS), 0.0, NEG)
    n_t = tp_ref.shape[1]
    low = _lane_low((rows, LANES))
    for p in range(H_B // 2):
        cs = slice(LANES * p, LANES * p + LANES)
        lhs = (q_ref[:, cs] * ATTN_SCALE).astype(BF16)
        outs = []
        for sub in range(2):
            h = 2 * p + sub
            strips = []
            for qq in range(nq):
                tiles = []
                for kb in range(nwin):
                    t = jnp.clip(2 * (sblk + kb - (nq * i + qq)) + NA_ROWS + 1, 1, n_t - 1)
                    tiles.append(jnp.concatenate([tp_ref[h, t], tp_ref[h, t - 1]], axis=0))
                strips.append(jnp.concatenate(tiles, axis=1))
            bias = jnp.concatenate(strips, axis=0) + row_bias
            ks = [kv_ref[p, sub, pl.ds(start, wk), :], kv_ref[p, sub, L:L + P, :]]
            vs = [vv_ref[p, sub, pl.ds(start, wk), :], vv_ref[p, sub, L:L + P, :]]
            outs.append(_attend(lhs, ks, vs, [bias, None], sub))
        o_ref[:, cs] = jnp.where(low, outs[0], outs[1]).astype(o_ref.dtype)


def _attn_b_ctx(qkv):
    R, L, _ = qkv.shape
    w = H_B * HEAD_DIM
    G = CTX_GROUP_B
    assert R % G == 0
    return pl.pallas_call(
        _attn_b_ctx_kernel,
        out_shape=jax.ShapeDtypeStruct((R, L, w), F32),
        grid=(R // G,),
        in_specs=[pl.BlockSpec((G, L, w), lambda r: (r, 0, COL_QB)),
                  pl.BlockSpec((G, L, w), lambda r: (r, 0, COL_KB)),
                  pl.BlockSpec((G, L, w), lambda r: (r, 0, COL_VB))],
        out_specs=pl.BlockSpec((G, L, w), lambda r: (r, 0, 0)),
        compiler_params=_cp(1), name="attn_b_ctx",
    )(qkv, qkv, qkv)


def _attn_b_lat(qkv, cache_k, cache_v, bias_tab, layer):
    R, L, _ = qkv.shape
    nb = L // BLOCK
    P = cache_k.shape[3]
    w = H_B * HEAD_DIM
    cache_spec = pl.BlockSpec((None, None, w, P), lambda r, i: (r, layer, 0, 0))
    rows = NA_QBLOCKS * BLOCK
    assert nb % NA_QBLOCKS == 0 and nb >= NA_WIN_BLOCKS + NA_QBLOCKS - 1
    return pl.pallas_call(
        functools.partial(_attn_b_lat_kernel, nb=nb, rows_total=L // GRID_W),
        out_shape=jax.ShapeDtypeStruct((R, L, w), F32),
        grid=(R, nb // NA_QBLOCKS),
        in_specs=[pl.BlockSpec((None, rows, w), lambda r, i: (r, i, COL_QB)),
                  pl.BlockSpec((None, L, w), lambda r, i: (r, 0, COL_KB)),
                  pl.BlockSpec((None, L, w), lambda r, i: (r, 0, COL_VB)),
                  cache_spec, cache_spec,
                  pl.BlockSpec(bias_tab.shape, lambda r, i: (0, 0, 0, 0))],
        out_specs=pl.BlockSpec((None, rows, w), lambda r, i: (r, i, 0)),
        scratch_shapes=[pltpu.VMEM((H_B // 2, 2, L + P, LANES), BF16),
                        pltpu.VMEM((H_B // 2, 2, L + P, LANES), BF16)],
        compiler_params=_cp(2), name="attn_b_lat",
    )(qkv, qkv, qkv, cache_k, cache_v, bias_tab)


T_ZF, T_ZB, T_XF, T_XB, T_GF, T_GB, T_MA, T_MB = range(8)


def _ret_kernel(*refs, nc, has_s0, out_state):
    refs = list(refs)
    dec_ref, q_ref, k_ref, v_ref, g_ref, gn_ref = refs[:6]
    pos = 6
    s0_ref = None
    if has_s0:
        s0_ref = refs[pos]; pos += 1
    o_ref = refs[pos]; pos += 1
    st_ref = None
    if out_state:
        st_ref = refs[pos]; pos += 1
    sf_ref, sb_ref, tab_ref = refs[pos], refs[pos + 1], refs[pos + 2]

    sq = (BLOCK, LANES)
    low = _lane_low(sq)
    row_low = lax.broadcasted_iota(I32, sq, 0) < HEAD_DIM
    same_head = low == row_low
    n_pairs = H_C // 2

    @pl.when(pl.program_id(0) == 0)
    def _():
        ri = lax.broadcasted_iota(I32, sq, 0).astype(F32)
        ci = lax.broadcasted_iota(I32, sq, 1).astype(F32)

        def log_gamma(d, h):
            return -jnp.exp(jnp.full(sq, dec_ref[d, h], F32))

        for p in range(n_pairs):
            lf = [log_gamma(0, 2 * p), log_gamma(0, 2 * p + 1)]
            lb = [log_gamma(1, 2 * p), log_gamma(1, 2 * p + 1)]
            lf_lane, lb_lane = jnp.where(low, lf[0], lf[1]), jnp.where(low, lb[0], lb[1])
            lf_row, lb_row = jnp.where(row_low, lf[0], lf[1]), jnp.where(row_low, lb[0], lb[1])
            tab_ref[p, T_ZF] = jnp.exp(lf_lane * (BLOCK - 1.0 - ri))
            tab_ref[p, T_ZB] = jnp.exp(lb_lane * ri)
            tab_ref[p, T_XF] = jnp.exp(lf_lane * (ri + 1.0))
            tab_ref[p, T_XB] = jnp.exp(lb_lane * (BLOCK - ri))
            tab_ref[p, T_GF] = jnp.exp(lf_row * float(BLOCK))
            tab_ref[p, T_GB] = jnp.exp(lb_row * float(BLOCK))
            rel = ri - ci
            for sub in range(2):
                fwd = jnp.where(rel >= 0, jnp.exp(lf[sub] * jnp.maximum(rel, 0.0)), 0.0)
                bwd = jnp.where(rel <= 0, jnp.exp(lb[sub] * jnp.maximum(-rel, 0.0)), 0.0)
                tab_ref[p, T_MA + sub] = fwd + bwd

    def chunk(ref, c, p):
        r0 = pl.multiple_of(c * BLOCK, BLOCK)
        return ref[pl.ds(r0, BLOCK), LANES * p:LANES * p + LANES]

    unroll = min(nc, 8)

    def inc_body(c, carry):
        for p in range(n_pairs):
            kt = (chunk(k_ref, c, p) * QK_SCALE).T.astype(BF16)
            v = chunk(v_ref, c, p)
            zv = jnp.concatenate([v * tab_ref[p, T_ZF], v * tab_ref[p, T_ZB]], axis=1).astype(BF16)
            inc = _dot(kt, zv)
            sf_ref[c, p] = jnp.where(same_head, inc[:, :LANES], 0.0)
            sb_ref[c, p] = jnp.where(same_head, inc[:, LANES:], 0.0)
        return carry

    lax.fori_loop(0, nc, inc_body, 0, unroll=unroll)

    def scan_body(t, S, ref, tg, reverse):
        c = nc - 1 - t if reverse else t
        new = []
        for p in range(n_pairs):
            inc = ref[c, p]
            ref[c, p] = S[p]
            new.append(tab_ref[p, tg] * S[p] + inc)
        return tuple(new)

    zero = jnp.zeros(sq, F32)
    s_f0 = tuple(s0_ref[0, p] if has_s0 else zero for p in range(n_pairs))
    s_b0 = tuple(s0_ref[1, p] if has_s0 else zero for p in range(n_pairs))
    s_f = lax.fori_loop(0, nc, functools.partial(scan_body, ref=sf_ref, tg=T_GF, reverse=False), s_f0)
    s_b = lax.fori_loop(0, nc, functools.partial(scan_body, ref=sb_ref, tg=T_GB, reverse=True), s_b0)
    if out_state:
        for p in range(n_pairs):
            st_ref[0, p] = s_f[p]
            st_ref[1, p] = s_b[p]

    def out_body(c, carry):
        for p in range(n_pairs):
            q = chunk(q_ref, c, p)
            k16 = (chunk(k_ref, c, p) * QK_SCALE).astype(BF16)
            v = chunk(v_ref, c, p)
            a = _dot_t(jnp.concatenate([_mask_head(q, 0), _mask_head(q, 1)], axis=0).astype(BF16), k16)
            lhs = jnp.concatenate([a[:BLOCK] * tab_ref[p, T_MA], a[BLOCK:] * tab_ref[p, T_MA + 1],
                                   q * tab_ref[p, T_XF], q * tab_ref[p, T_XB]], axis=1).astype(BF16)
            rhs = jnp.concatenate([_mask_head(v, 0), _mask_head(v, 1), sf_ref[c, p], sb_ref[c, p]],
                                  axis=0).astype(BF16)
            o = _dot(lhs, rhs)
            mu = jnp.where(low, _mask_head(o, 0).sum(-1, keepdims=True),
                           _mask_head(o, 1).sum(-1, keepdims=True)) * (1.0 / HEAD_DIM)
            d = o - mu
            d2 = d * d
            var = jnp.where(low, _mask_head(d2, 0).sum(-1, keepdims=True),
                            _mask_head(d2, 1).sum(-1, keepdims=True)) * (1.0 / HEAD_DIM)
            cs = slice(LANES * p, LANES * p + LANES)
            on = d * lax.rsqrt(var + LN_EPS) * gn_ref[:, cs]
            r0 = pl.multiple_of(c * BLOCK, BLOCK)
            o_ref[pl.ds(r0, BLOCK), cs] = (_silu(chunk(g_ref, c, p)) * on).astype(o_ref.dtype)
        return carry

    lax.fori_loop(0, nc, out_body, 0, unroll=unroll)


def _retention(qkv, decay, gn_w, s0=None, layer=0, out_state=False):
    R, L, _ = qkv.shape
    nc = L // BLOCK
    w = H_C * HEAD_DIM
    n_pairs = H_C // 2
    col = lambda cb: pl.BlockSpec((None, L, w), lambda r: (r, 0, cb))
    in_specs = [_smem_spec(), col(COL_QC), col(COL_KC), col(COL_VC), col(COL_GC),
                pl.BlockSpec((1, w), lambda r: (0, 0))]
    args = [decay, qkv, qkv, qkv, qkv, gn_w.reshape(1, w)]
    if s0 is not None:
        in_specs.append(pl.BlockSpec((None, None, 2, n_pairs, BLOCK, LANES), lambda r: (r, layer, 0, 0, 0, 0)))
        args.append(s0)
    out_shape = [jax.ShapeDtypeStruct((R, L, w), F32)]
    out_specs = [pl.BlockSpec((None, L, w), lambda r: (r, 0, 0))]
    if out_state:
        out_shape.append(jax.ShapeDtypeStruct((R, 2, n_pairs, BLOCK, LANES), F32))
        out_specs.append(pl.BlockSpec((None, 2, n_pairs, BLOCK, LANES), lambda r: (r, 0, 0, 0, 0)))
    res = pl.pallas_call(
        functools.partial(_ret_kernel, nc=nc, has_s0=s0 is not None, out_state=out_state),
        out_shape=out_shape, grid=(R,), in_specs=in_specs, out_specs=out_specs,
        scratch_shapes=[pltpu.VMEM((nc, n_pairs, BLOCK, LANES), F32),
                        pltpu.VMEM((nc, n_pairs, BLOCK, LANES), F32),
                        pltpu.VMEM((n_pairs, 8, BLOCK, LANES), F32)],
        compiler_params=_cp(1), name="retention",
    )(*args)
    return res if out_state else res[0]


def _pair_states_to_heads(st):
    a = st[..., :HEAD_DIM, :HEAD_DIM]
    b = st[..., HEAD_DIM:, HEAD_DIM:]
    return jnp.stack([a, b], axis=3).reshape(st.shape[0], 2, H_C, HEAD_DIM, HEAD_DIM)


def _heads_to_pair_states(s):
    s = s.reshape(*s.shape[:3], H_C // 2, 2, HEAD_DIM, HEAD_DIM)
    z = jnp.zeros_like(s[..., 0, :, :])
    top = jnp.concatenate([s[..., 0, :, :], z], axis=-1)
    bot = jnp.concatenate([z, s[..., 1, :, :]], axis=-1)
    return jnp.concatenate([top, bot], axis=-2)


def _outproj_kernel(oa_ref, ob_ref, oc_ref, x_ref, g1_ref, sc2_ref, sh2_ref, w_ref, lg_ref, lb_ref, wr_ref,
                    x1_ref, h2_ref, aff_ref):
    wa, wb = H_A * HEAD_DIM, H_B * HEAD_DIM
    tm = x_ref.shape[0]
    part = tm // OUTPROJ_PARTS
    for r0 in range(0, tm, part):
        rs = slice(r0, r0 + part)
        mix = (_dot(oa_ref[rs, :].astype(BF16), w_ref[0:wa, :])
               + _dot(ob_ref[rs, :].astype(BF16), w_ref[wa:wa + wb, :])
               + _dot(oc_ref[rs, :].astype(BF16), w_ref[wa + wb:, :]))
        x1 = _layernorm(ALPHA * x_ref[rs, :] + g1_ref[...] * mix, lg_ref[...], lb_ref[...])
        x1_ref[rs, :] = x1
        h2 = (x1 * (1.0 + sc2_ref[...]) + sh2_ref[...]).astype(BF16)
        h2_ref[rs, :] = h2
        logits = _dot(h2, wr_ref[...])
        e = jnp.exp(logits - logits.max(axis=-1, keepdims=True))
        aff_ref[rs, :] = e / e.sum(axis=-1, keepdims=True)


def _outproj(oa, ob, oc, x, mods, w_out_bf16, ln_g, ln_b, w_router_bf16, layer, tm):
    R, L, _ = x.shape
    row = lambda w: pl.BlockSpec((None, tm, w), lambda r, i: (r, i, 0))
    vec = pl.BlockSpec((1, D_MODEL), lambda r, i: (0, 0))
    return pl.pallas_call(
        _outproj_kernel,
        out_shape=[jax.ShapeDtypeStruct((R, L, D_MODEL), F32),
                   jax.ShapeDtypeStruct((R, L, D_MODEL), BF16),
                   jax.ShapeDtypeStruct((R, L, N_EXPERTS), F32)],
        grid=(R, L // tm),
        in_specs=[row(H_A * HEAD_DIM), row(H_B * HEAD_DIM), row(H_C * HEAD_DIM), row(D_MODEL),
                  _mod_spec(2, 2), _mod_spec(4, 2), _mod_spec(3, 2),
                  pl.BlockSpec((None, D_MODEL, D_MODEL), lambda r, i: (layer, 0, 0)), vec, vec,
                  pl.BlockSpec((None, D_MODEL, N_EXPERTS), lambda r, i: (layer, 0, 0))],
        out_specs=[row(D_MODEL), row(D_MODEL), row(N_EXPERTS)],
        compiler_params=_cp(2), name="out_proj_ln_router",
    )(oa, ob, oc, x, mods, mods, mods, w_out_bf16, ln_g.reshape(1, -1), ln_b.reshape(1, -1), w_router_bf16)


CUMSUM_CHUNK = 256


def _excl_cumsum_lanes(x):
    rows, n = x.shape
    w = min(CUMSUM_CHUNK, n)
    tri = (lax.broadcasted_iota(I32, (w, w), 0) < lax.broadcasted_iota(I32, (w, w), 1))
    tri = jnp.where(tri, 1.0, 0.0).astype(BF16)
    carry = jnp.zeros((rows, 1), F32)
    outs = []
    for c in range(0, n, w):
        xc = x[:, c:c + w]
        outs.append(_dot(xc.astype(BF16), tri) + carry)
        carry = carry + xc.sum(axis=-1, keepdims=True)
    return outs[0] if len(outs) == 1 else jnp.concatenate(outs, axis=1)


def _select_kernel(a_ref, pos_ref, *, cap):
    rb, ne, n = a_ref.shape
    rows = rb * ne
    a = a_ref[...].reshape(rows, n)
    capf = float(cap)

    def body(_, c):
        lo, hi = c
        mid = lo + ((hi - lo + 1) >> 1)
        cnt = jnp.where(a >= pltpu.bitcast(mid, F32), 1.0, 0.0).sum(axis=-1, keepdims=True)
        ok = cnt >= capf
        return jnp.where(ok, mid, lo), jnp.where(ok, hi, mid - 1)

    one_bits = 0x3F800000
    lo, _ = lax.fori_loop(0, 30, body, (jnp.zeros((rows, 1), I32), jnp.full((rows, 1), one_bits, I32)))
    thr = pltpu.bitcast(lo, F32)
    gt = jnp.where(a > thr, 1.0, 0.0)
    eq = jnp.where(a == thr, 1.0, 0.0)
    need = capf - gt.sum(axis=-1, keepdims=True)
    sel = gt + eq * jnp.where(_excl_cumsum_lanes(eq) < need, 1.0, 0.0)
    slot = _excl_cumsum_lanes(sel)
    pos_ref[...] = jnp.where(sel > 0.5, slot, -1.0).astype(I32).reshape(rb, ne, n)


def _select(aff_t, cap):
    R, ne, n = aff_t.shape
    rb = min(R, max(1, 128 * 2048 // (ne * n)))
    return pl.pallas_call(
        functools.partial(_select_kernel, cap=cap),
        out_shape=jax.ShapeDtypeStruct((R, ne, n), I32),
        grid=(R // rb,),
        in_specs=[pl.BlockSpec((rb, ne, n), lambda r: (r, 0, 0))],
        out_specs=pl.BlockSpec((rb, ne, n), lambda r: (r, 0, 0)),
        compiler_params=_cp(1), name="expert_select",
    )(aff_t)


def _gather_kernel(pos_ref, aff_ref, h_ref, x_ref, gate_ref, *, cap):
    eb, _, n = pos_ref.shape
    slot = lax.broadcasted_iota(I32, (cap, n), 0)
    parts = []
    for j in range(eb):
        hit = slot == pos_ref[j]
        parts.append(jnp.where(hit, 1.0, 0.0).astype(BF16))
        gate_ref[j] = jnp.where(hit, aff_ref[j], 0.0).sum(axis=-1, keepdims=True)
    onehot = parts[0] if eb == 1 else jnp.concatenate(parts, axis=0)
    xe = _dot(onehot, h_ref[...]).astype(BF16)
    for j in range(eb):
        x_ref[j] = xe[j * cap:(j + 1) * cap]


def _gather(pos, aff_t, h2, cap):
    R, ne, n = pos.shape
    eb = max(1, min(ne, ROW_TILE // cap))
    row = pl.BlockSpec((None, eb, 1, n), lambda r, e: (r, e, 0, 0))
    return pl.pallas_call(
        functools.partial(_gather_kernel, cap=cap),
        out_shape=[jax.ShapeDtypeStruct((ne, R * cap, D_MODEL), BF16),
                   jax.ShapeDtypeStruct((ne, R * cap, 1), F32)],
        grid=(R, ne // eb),
        in_specs=[row, row, pl.BlockSpec((None, n, D_MODEL), lambda r, e: (r, 0, 0))],
        out_specs=[pl.BlockSpec((eb, cap, D_MODEL), lambda r, e: (e, r, 0)),
                   pl.BlockSpec((eb, cap, 1), lambda r, e: (e, r, 0))],
        compiler_params=_cp(2), name="expert_gather",
    )(pos.reshape(R, ne, 1, n), aff_t.reshape(R, ne, 1, n), h2)


FF_TILE = 512


def _ffn_kernel(x0_ref, x1_ref, wa_ref, wb_ref, wd_ref, g0_ref, g1_ref, y0_ref, y1_ref, acc_ref):
    f = pl.program_id(1)

    @pl.when(f == 0)
    def _():
        acc_ref[...] = jnp.zeros_like(acc_ref)

    x = jnp.concatenate([x0_ref[...], x1_ref[...]], axis=0)
    a = _dot(x, wa_ref[...].astype(BF16))
    b = _dot(x, wb_ref[...].astype(BF16))
    acc_ref[...] += _dot((_silu(a) * b).astype(BF16), wd_ref[...].astype(BF16))

    @pl.when(f == pl.num_programs(1) - 1)
    def _():
        s0 = x0_ref.shape[0]
        y0_ref[...] = (acc_ref[:s0, :] * g0_ref[...]).astype(BF16)
        y1_ref[...] = (acc_ref[s0:, :] * g1_ref[...]).astype(BF16)


def _experts(xe0, gates0, xe1, gates1, w_gate_up, w_down, layer):
    ne, S0, _ = xe0.shape
    S1 = xe1.shape[1]
    nf = D_FF // FF_TILE
    rows = lambda S, w: pl.BlockSpec((None, S, w), lambda e, f: (e, 0, 0))
    return pl.pallas_call(
        _ffn_kernel,
        out_shape=[jax.ShapeDtypeStruct((ne, S0, D_MODEL), BF16), jax.ShapeDtypeStruct((ne, S1, D_MODEL), BF16)],
        grid=(ne, nf),
        in_specs=[rows(S0, D_MODEL), rows(S1, D_MODEL),
                  pl.BlockSpec((None, None, D_MODEL, FF_TILE), lambda e, f: (layer, e, 0, f)),
                  pl.BlockSpec((None, None, D_MODEL, FF_TILE), lambda e, f: (layer, e, 0, nf + f)),
                  pl.BlockSpec((None, None, FF_TILE, D_MODEL), lambda e, f: (layer, e, f, 0)),
                  rows(S0, 1), rows(S1, 1)],
        out_specs=[rows(S0, D_MODEL), rows(S1, D_MODEL)],
        scratch_shapes=[pltpu.VMEM((S0 + S1, D_MODEL), F32)],
        compiler_params=_cp(2), name="expert_ffn",
    )(xe0, xe1, w_gate_up, w_gate_up, w_down, gates0, gates1)


def _scatter_kernel(pos_ref, y_ref, x1_ref, g2_ref, lg_ref, lb_ref, o_ref, *, cap):
    tm, ne = pos_ref.shape
    cw = max(cap, LANES)
    epc = cw // cap
    lane = lax.broadcasted_iota(I32, (tm, cw), 1)
    pos = pos_ref[...]
    def onehot_chunk(c):
        onehot = jnp.zeros((tm, cw), F32)
        for jj in range(epc):
            col = pos[:, c * epc + jj:c * epc + jj + 1]
            tgt = jnp.where(col >= 0, col + jj * cap, -1)
            onehot = onehot + jnp.where(lane == tgt, 1.0, 0.0)
        return onehot.astype(BF16)

    n_chunks = ne // epc
    if ne * cap <= ROW_TILE:
        acc = _dot(jnp.concatenate([onehot_chunk(c) for c in range(n_chunks)], axis=1),
                   y_ref[...].reshape(ne * cap, D_MODEL))
    else:
        acc = jnp.zeros((tm, D_MODEL), F32)
        for c in range(n_chunks):
            yc = y_ref[c] if epc == 1 else y_ref[c * epc:(c + 1) * epc].reshape(cw, D_MODEL)
            acc = acc + _dot(onehot_chunk(c), yc)
    o_ref[...] = _layernorm(ALPHA * x1_ref[...] + g2_ref[...] * acc, lg_ref[...], lb_ref[...])


def _scatter(pos_tok, y, x1, mods, ln_g, ln_b, cap, tm):
    R, n, ne = pos_tok.shape
    vec = pl.BlockSpec((1, D_MODEL), lambda r, i: (0, 0))
    return pl.pallas_call(
        functools.partial(_scatter_kernel, cap=cap),
        out_shape=jax.ShapeDtypeStruct((R, n, D_MODEL), F32),
        grid=(R, n // tm),
        in_specs=[pl.BlockSpec((None, tm, ne), lambda r, i: (r, i, 0)),
                  pl.BlockSpec((ne, cap, D_MODEL), lambda r, i: (0, r, 0)),
                  pl.BlockSpec((None, tm, D_MODEL), lambda r, i: (r, i, 0)),
                  _mod_spec(5, 2), vec, vec],
        out_specs=pl.BlockSpec((None, tm, D_MODEL), lambda r, i: (r, i, 0)),
        compiler_params=_cp(2), name="expert_scatter_ln",
    )(pos_tok, y, x1, mods, ln_g.reshape(1, -1), ln_b.reshape(1, -1))


def _capacity(n):
    return CAPACITY_FACTOR * n // N_EXPERTS


def _route(h2, aff):
    cap = _capacity(h2.shape[1])
    aff_t = jnp.swapaxes(aff, 1, 2)
    pos = _select(aff_t, cap)
    xe, gates = _gather(pos, aff_t, h2, cap)
    return pos, xe, gates


def _channel_mixers(groups, l, w_gate_up, w_down, ln2_g, ln2_b):
    routed = [_route(g["h2"], g["aff"]) for g in groups]
    ys = _experts(routed[0][1], routed[0][2], routed[1][1], routed[1][2], w_gate_up, w_down, l)
    return [_scatter(jnp.swapaxes(r[0], 1, 2), y, g["x1"], g["mods"], ln2_g[l], ln2_b[l],
                     _capacity(g["h2"].shape[1]), g["tm"]) for g, r, y in zip(groups, routed, ys)]


def kernel(x_prompt, x_sample, cache_attn_a_k, cache_attn_a_v, cache_attn_b_k, cache_attn_b_v, state_ret, c, c_ctx,
           w_ada, b_ada, w_in, w_out, attn_sink, na_rpb, ret_decay, ret_gn, ln1_g, ln1_b, ln2_g, ln2_b,
           w_router, w_gate_up, w_down):
    n_ctx, L_ctx, _ = x_prompt.shape
    n_lat, L_lat, _ = x_sample.shape
    P = cache_attn_a_k.shape[2]

    cond = jnp.concatenate([c_ctx[None, :], c, jnp.zeros((8 - 1 - n_lat, D_MODEL), F32)], axis=0)
    mod = _modulation(cond, w_ada, b_ada)
    w_in16, w_out16, w_r16 = w_in.astype(BF16), w_out.astype(BF16), w_router.astype(BF16)
    rope_tabs = _rope_tables(L_lat)
    feature_major = lambda t: jnp.transpose(t, (0, 1, 3, 4, 2)).reshape(n_lat, DEPTH, -1, P)
    cak, cav = feature_major(cache_attn_a_k), feature_major(cache_attn_a_v)
    cbk, cbv = feature_major(cache_attn_b_k), feature_major(cache_attn_b_v)
    s0_pairs = _heads_to_pair_states(state_ret)

    xc, xs = x_prompt, x_sample
    tm = ROW_TILE
    a_k, a_v, b_k, b_v, st = [], [], [], [], []
    flat = lambda a: a.reshape(1, n_ctx * L_ctx, a.shape[-1])
    unflat = lambda a: a.reshape(n_ctx, L_ctx, a.shape[-1])
    for l in range(DEPTH):
        mod_c = mod[l, 0:1][:, None, :]
        mods_c = jnp.broadcast_to(mod_c, (n_ctx, 1, 6 * D_MODEL))
        qkv, ka, va, kb, vb = _project(flat(xc), mod_c, w_in16, l, tm, kv_seq=L_ctx)
        qkv = unflat(qkv)
        oa = _attn_a_ctx(qkv, attn_sink[l])
        ob = _attn_b_ctx(qkv)
        oc, s_l = _retention(qkv, ret_decay[l], ret_gn[l], out_state=True)
        x1, h2, aff = [unflat(t) for t in _outproj(flat(oa), flat(ob), flat(oc), flat(xc), mod_c, w_out16,
                                                   ln1_g[l], ln1_b[l], w_r16, l, tm)]
        ctx = dict(x1=x1, h2=h2, aff=aff, mods=mods_c, tm=L_ctx)
        heads_last = lambda t, h: jnp.transpose(t.reshape(n_ctx, h, HEAD_DIM, L_ctx), (0, 3, 1, 2))
        a_k.append(heads_last(ka, KVH_A))
        a_v.append(heads_last(va, KVH_A))
        b_k.append(heads_last(kb, H_B))
        b_v.append(heads_last(vb, H_B))
        st.append(_pair_states_to_heads(s_l))

        mods_s = mod[l, 1:1 + n_lat][:, None, :]
        qkv = _project(xs, mods_s, w_in16, l, tm, rope_tabs)
        oa = _attn_a_lat(qkv, attn_sink[l], cak, cav, l)
        ob = _attn_b_lat(qkv, cbk, cbv, _bias_table(na_rpb[l]), l)
        oc = _retention(qkv, ret_decay[l], ret_gn[l], s0=s0_pairs, layer=l)
        x1, h2, aff = _outproj(oa, ob, oc, xs, mods_s, w_out16, ln1_g[l], ln1_b[l], w_r16, l, tm)
        lat = dict(x1=x1, h2=h2, aff=aff, mods=mods_s, tm=tm)

        xc, xs = _channel_mixers([ctx, lat], l, w_gate_up, w_down, ln2_g, ln2_b)

    return (xc, xs, jnp.stack(a_k, 1), jnp.stack(a_v, 1), jnp.stack(b_k, 1), jnp.stack(b_v, 1),
            jnp.stack(st, 1))
```

```python
import functools

import numpy as np
import jax
import jax.numpy as jnp
from jax import lax
from jax.experimental import pallas as pl
from jax.experimental.pallas import tpu as pltpu

F32 = jnp.float32
BF16 = jnp.bfloat16
I32 = jnp.int32

D_MODEL = 1024
DEPTH = 2
GRID_W = 64
HEAD_DIM = 64
H_A, KVH_A, H_B, H_C = 8, 2, 4, 4
D_IN = 2560
BLOCK = 128
WINDOW = 128
NA_ROWS, NA_COLS = 8, 16
NA_WIN_BLOCKS = 5
NA_QBLOCKS = 4
WIN_QBLOCKS = 2
CTX_GROUP_A = 4
CTX_GROUP_B = 2
ROPE_BASE = 10000.0
N_EXPERTS = 16
CAPACITY_FACTOR = 2
D_FF = 1024
LN_EPS = 1e-6
ALPHA = (2 * DEPTH) ** 0.25
NEG = -1e30
LANES = 128
QK_SCALE = HEAD_DIM ** -0.5
LOG2E = float(np.log2(np.e))
ATTN_SCALE = QK_SCALE * LOG2E
VMEM_LIMIT = 56 * 1024 * 1024
ROW_TILE = 1024
OUTPROJ_PARTS = 2

COL_QA, COL_KA, COL_VA = 0, 4, 5
COL_QB, COL_KB, COL_VB = 3, 4, 5
COL_QC, COL_KC, COL_VC, COL_GC = 6, 7, 8, 9


def _cp(n_axes):
    return pltpu.CompilerParams(dimension_semantics=("arbitrary",) * n_axes, vmem_limit_bytes=VMEM_LIMIT)


def _dot(a, b):
    return jnp.dot(a, b, preferred_element_type=F32)


def _dot_t(a, b):
    return lax.dot_general(a, b, (((1,), (1,)), ((), ())), preferred_element_type=F32)


def _silu(x):
    return x / (1.0 + jnp.exp(-x))


def _lane_low(shape):
    return lax.broadcasted_iota(I32, shape, len(shape) - 1) < HEAD_DIM


def _layernorm(y, g, b):
    mu = jnp.mean(y, axis=-1, keepdims=True)
    d = y - mu
    var = jnp.mean(d * d, axis=-1, keepdims=True)
    return d * lax.rsqrt(var + LN_EPS) * g + b


def _mod_kernel(c_ref, w_ref, b_ref, o_ref):
    s = _silu(c_ref[...]).astype(BF16)
    o_ref[...] = _dot(s, w_ref[...].astype(BF16)) + b_ref[...]


def _modulation(cond, w_ada, b_ada):
    tn = 1536
    return pl.pallas_call(
        _mod_kernel,
        out_shape=jax.ShapeDtypeStruct((DEPTH, 8, 6 * D_MODEL), F32),
        grid=(DEPTH, 6 * D_MODEL // tn),
        in_specs=[pl.BlockSpec((8, D_MODEL), lambda l, j: (0, 0)),
                  pl.BlockSpec((None, D_MODEL, tn), lambda l, j: (l, 0, j)),
                  pl.BlockSpec((None, 1, tn), lambda l, j: (l, 0, j))],
        out_specs=pl.BlockSpec((None, 8, tn), lambda l, j: (l, 0, j)),
        compiler_params=_cp(2), name="modulation",
    )(cond, w_ada, b_ada.reshape(DEPTH, 1, 6 * D_MODEL))


def _mod_spec(k, n_grid):
    if n_grid == 1:
        return pl.BlockSpec((None, 1, D_MODEL), lambda r: (r, 0, k))
    return pl.BlockSpec((None, 1, D_MODEL), lambda r, i: (r, 0, k))


def _rope_tables(L):
    t = np.arange(L)
    pos = (t // GRID_W, t % GRID_W)
    half = HEAD_DIM // 2
    inv = 1.0 / (ROPE_BASE ** (np.arange(0, half, 2) / half))
    cos = np.zeros((L, HEAD_DIM)); s_up = np.zeros((L, HEAD_DIM)); s_dn = np.zeros((L, HEAD_DIM))
    for part in range(2):
        ang = (pos[part][:, None] * inv[None, :]).astype(np.float32).astype(np.float64)
        q = half // 2
        base = part * half
        cos[:, base:base + q] = np.cos(ang); cos[:, base + q:base + half] = np.cos(ang)
        s_up[:, base:base + q] = -np.sin(ang)
        s_dn[:, base + q:base + half] = np.sin(ang)
    tile = lambda a: jnp.asarray(np.tile(a, (1, LANES // HEAD_DIM)), F32)
    return tile(cos), tile(s_up), tile(s_dn)


KV_COLUMNS = ((512, 128), (640, 128), (1024, 256), (1280, 256))


def _proj_kernel(*refs, rope, emit_kv):
    kv_refs = ()
    if emit_kv:
        refs, kv_refs = refs[:-len(KV_COLUMNS)], refs[-len(KV_COLUMNS):]
    if rope:
        x_ref, sc_ref, sh_ref, w_ref, cos_ref, sup_ref, sdn_ref, o_ref = refs
    else:
        x_ref, sc_ref, sh_ref, w_ref, o_ref = refs
    h = (x_ref[...] * (1.0 + sc_ref[...]) + sh_ref[...]).astype(BF16)
    n_rope = (H_A + KVH_A) * HEAD_DIM
    step = 512
    for j in range(0, D_IN, step):
        acc = _dot(h, w_ref[:, j:j + step])
        if rope and j < n_rope:
            q = HEAD_DIM // 4
            for c in range(0, step, LANES):
                a = acc[:, c:c + LANES]
                if j + c < n_rope:
                    a = (a * cos_ref[...] + pltpu.roll(a, LANES - q, 1) * sup_ref[...]
                         + pltpu.roll(a, q, 1) * sdn_ref[...])
                o_ref[:, j + c:j + c + LANES] = a
        else:
            o_ref[:, j:j + step] = acc
        for ref, (c0, w) in zip(kv_refs, KV_COLUMNS):
            if j <= c0 < j + step:
                n_req, _, seq = ref.shape
                for g in range(n_req):
                    ref[g] = acc[g * seq:(g + 1) * seq, c0 - j:c0 - j + w].T


def _project(x, mods, w_bf16, layer, tm, rope_tabs=None, kv_seq=None):
    R, L, _ = x.shape
    rope = rope_tabs is not None
    emit_kv = kv_seq is not None
    assert not (rope and emit_kv)
    out_shape = [jax.ShapeDtypeStruct((R, L, D_IN), F32)]
    out_specs = [pl.BlockSpec((None, tm, D_IN), lambda r, i: (r, i, 0))]
    if emit_kv:
        assert R == 1 and tm % kv_seq == 0
        out_shape += [jax.ShapeDtypeStruct((L // kv_seq, w, kv_seq), F32) for _, w in KV_COLUMNS]
        out_specs += [pl.BlockSpec((tm // kv_seq, w, kv_seq), lambda r, i: (i, 0, 0)) for _, w in KV_COLUMNS]
    in_specs = [pl.BlockSpec((None, tm, D_MODEL), lambda r, i: (r, i, 0)),
                _mod_spec(1, 2), _mod_spec(0, 2),
                pl.BlockSpec((None, D_MODEL, D_IN), lambda r, i: (layer, 0, 0))]
    args = [x, mods, mods, w_bf16]
    if rope:
        in_specs += [pl.BlockSpec((tm, LANES), lambda r, i: (i, 0))] * 3
        args += list(rope_tabs)
    res = pl.pallas_call(
        functools.partial(_proj_kernel, rope=rope, emit_kv=emit_kv),
        out_shape=out_shape, grid=(R, L // tm), in_specs=in_specs, out_specs=out_specs,
        compiler_params=_cp(2), name="in_proj",
    )(*args)
    return res if emit_kv else res[0]


def _mask_head(x, sub):
    low = _lane_low(x.shape)
    return jnp.where(low, x, 0.0) if sub == 0 else jnp.where(low, 0.0, x)


def _key_variant(k, sub):
    return _mask_head(k, sub).astype(BF16)


def _value_variant(v, sub):
    low = _lane_low(v.shape)
    first = jnp.where(low, v, 1.0) if sub == 0 else jnp.where(low, 1.0, v)
    return jnp.concatenate([first, jnp.ones_like(v)], axis=1).astype(BF16)


def _attend(lhs, keys, values, biases, sink=None):
    scores = []
    for k, b in zip(keys, biases):
        s = _dot_t(lhs, k)
        scores.append(s if b is None else s + b)
    tiles = [s[:, c:c + LANES] for s in scores for c in range(0, s.shape[1], LANES)]
    while len(tiles) > 1:
        tiles = [jnp.maximum(a, b) for a, b in zip(tiles[0::2], tiles[1::2])] + ([tiles[-1]] if len(tiles) % 2 else [])
    m = jnp.broadcast_to(tiles[0].max(axis=-1, keepdims=True), tiles[0].shape)
    if sink is not None:
        m = jnp.maximum(m, sink)
    o = None
    for s, v in zip(scores, values):
        p = [jnp.exp2(s[:, c:c + LANES] - m).astype(BF16) for c in range(0, s.shape[1], LANES)]
        pv = _dot(p[0] if len(p) == 1 else jnp.concatenate(p, axis=1), v)
        o = pv if o is None else o + pv
    return o, (None if sink is None else jnp.exp2(sink - m))


def _normalised(o):
    return o[:, :LANES] / o[:, LANES:]


def _gqa_variants(k, v, hk, wide):
    low = _lane_low(k.shape)
    k_sw, v_sw = pltpu.roll(k, HEAD_DIM, 1), pltpu.roll(v, HEAD_DIM, 1)
    k_dup = jnp.where(low, k, k_sw) if hk == 0 else jnp.where(low, k_sw, k)
    v_low, v_high = (v, v_sw) if hk == 0 else (v_sw, v)
    va = jnp.where(low, v_low, 1.0)
    if wide:
        va = jnp.concatenate([va, jnp.where(low, 1.0, v_high)], axis=1)
    return k_dup.astype(BF16), va.astype(BF16)


def _fill_gqa_variants(k_src, v_src, kv_ref, vv_ref, off, n, feature_major=False):
    step = 256
    for c in range(0, n, step):
        k = k_src[:, c:c + step].T if feature_major else k_src[c:c + step, :]
        v = v_src[:, c:c + step].T if feature_major else v_src[c:c + step, :]
        for hk in range(KVH_A):
            kd, va = _gqa_variants(k, v, hk, wide=False)
            kv_ref[hk, off + c:off + c + step, :] = kd
            vv_ref[hk, off + c:off + c + step, :] = va


def _fill_mha_variants(k_src, v_src, kv_ref, vv_ref, off, n, feature_major=False):
    step = 256
    for c in range(0, n, step):
        rows = slice(off + c, off + c + step)
        for p in range(H_B // 2):
            cs = slice(LANES * p, LANES * p + LANES)
            k = k_src[cs, c:c + step].T if feature_major else k_src[c:c + step, cs]
            v = v_src[cs, c:c + step].T if feature_major else v_src[c:c + step, cs]
            for sub in range(2):
                kv_ref[p, sub, rows, :] = _key_variant(k, sub)
                vv_ref[p, sub, rows, :] = _value_variant(v, sub)


def _sink_col(sink_ref, heads, rows):
    return jnp.concatenate([jnp.full((rows, LANES), sink_ref[h] * LOG2E, F32) for h in heads], axis=0)


def _gqa_attend(sink_ref, q_ref, o_ref, hk, keys, values, biases):
    rows = q_ref.shape[0]
    g = H_A // KVH_A
    heads = list(range(g * hk, g * hk + g))
    lhs = jnp.concatenate([_mask_head(q_ref[:, LANES * (h // 2):LANES * (h // 2) + LANES], h % 2) for h in heads],
                          axis=0)
    lhs = (lhs * ATTN_SCALE).astype(BF16)
    stacked = [None if b is None else jnp.concatenate([b] * g, axis=0) for b in biases]
    o, extra = _attend(lhs, keys, values, stacked, _sink_col(sink_ref, heads, rows))
    low_all = _lane_low((g * rows, LANES))
    low = _lane_low((rows, LANES))
    first = o[:, :LANES] + jnp.where(low_all, 0.0, extra)
    if o.shape[1] == LANES:
        out = first / pltpu.roll(first, HEAD_DIM, 1)
    else:
        second = o[:, LANES:] + jnp.where(low_all, extra, 0.0)
    for j in range(0, g, 2):
        h = heads[j]
        even, odd = slice(j * rows, (j + 1) * rows), slice((j + 1) * rows, (j + 2) * rows)
        if o.shape[1] == LANES:
            pair = jnp.where(low, out[even], pltpu.roll(out[odd], HEAD_DIM, 1))
        else:
            pair = jnp.where(low, first[even] / second[even], second[odd] / first[odd])
        o_ref[:, LANES * (h // 2):LANES * (h // 2) + LANES] = pair.astype(o_ref.dtype)


def _attn_a_ctx_kernel(sink_ref, q_ref, k_ref, v_ref, o_ref):
    for g in range(q_ref.shape[0]):
        for hk in range(KVH_A):
            kd, va = _gqa_variants(k_ref[g], v_ref[g], hk, wide=True)
            _gqa_attend(sink_ref, q_ref.at[g], o_ref.at[g], hk, [kd], [va], [None])


def _window_mask_table():
    nq = WIN_QBLOCKS
    row = np.arange(nq * BLOCK)[:, None]
    col = np.arange((nq + 2) * BLOCK)[None, :]
    tabs = [np.where(np.abs(col - off * BLOCK - row) <= WINDOW, 0.0, NEG) for off in range(3)]
    return jnp.asarray(np.stack(tabs), F32)


def _attn_a_lat_kernel(sink_ref, q_ref, k_ref, v_ref, kc_ref, vc_ref, mask_ref, o_ref, kv_ref, vv_ref, *, nb):
    i = pl.program_id(1)
    L, P = k_ref.shape[0], kc_ref.shape[1]

    @pl.when(i == 0)
    def _():
        _fill_gqa_variants(k_ref, v_ref, kv_ref, vv_ref, 0, L)
        _fill_gqa_variants(kc_ref, vc_ref, kv_ref, vv_ref, L, P, feature_major=True)

    nq = WIN_QBLOCKS
    wk = (nq + 2) * BLOCK
    sblk = jnp.clip(nq * i - 1, 0, nb - (nq + 2))
    start = pl.multiple_of(sblk * BLOCK, BLOCK)
    bias = mask_ref[nq * i - sblk]
    for hk in range(KVH_A):
        ks = [kv_ref[hk, pl.ds(start, wk), :], kv_ref[hk, L:L + P, :]]
        vs = [vv_ref[hk, pl.ds(start, wk), :], vv_ref[hk, L:L + P, :]]
        _gqa_attend(sink_ref, q_ref, o_ref, hk, ks, vs, [bias, None])


def _smem_spec():
    return pl.BlockSpec(memory_space=pltpu.SMEM)


def _attn_a_ctx(qkv, sink):
    R, L, _ = qkv.shape
    wq, wkv = H_A * HEAD_DIM, KVH_A * HEAD_DIM
    G = CTX_GROUP_A
    assert R % G == 0
    return pl.pallas_call(
        _attn_a_ctx_kernel,
        out_shape=jax.ShapeDtypeStruct((R, L, wq), F32),
        grid=(R // G,),
        in_specs=[_smem_spec(),
                  pl.BlockSpec((G, L, wq), lambda r: (r, 0, COL_QA)),
                  pl.BlockSpec((G, L, wkv), lambda r: (r, 0, COL_KA)),
                  pl.BlockSpec((G, L, wkv), lambda r: (r, 0, COL_VA))],
        out_specs=pl.BlockSpec((G, L, wq), lambda r: (r, 0, 0)),
        compiler_params=_cp(1), name="attn_a_ctx",
    )(sink, qkv, qkv, qkv)


def _attn_a_lat(qkv, sink, cache_k, cache_v, layer):
    R, L, _ = qkv.shape
    nb = L // BLOCK
    nq = WIN_QBLOCKS
    assert nb % nq == 0 and nb >= nq + 2
    P = cache_k.shape[3]
    wq, wkv = H_A * HEAD_DIM, KVH_A * HEAD_DIM
    cache_spec = pl.BlockSpec((None, None, wkv, P), lambda r, i: (r, layer, 0, 0))
    masks = _window_mask_table()
    return pl.pallas_call(
        functools.partial(_attn_a_lat_kernel, nb=nb),
        out_shape=jax.ShapeDtypeStruct((R, L, wq), F32),
        grid=(R, nb // nq),
        in_specs=[_smem_spec(),
                  pl.BlockSpec((None, nq * BLOCK, wq), lambda r, i: (r, i, COL_QA)),
                  pl.BlockSpec((None, L, wkv), lambda r, i: (r, 0, COL_KA)),
                  pl.BlockSpec((None, L, wkv), lambda r, i: (r, 0, COL_VA)),
                  cache_spec, cache_spec,
                  pl.BlockSpec(masks.shape, lambda r, i: (0, 0, 0))],
        out_specs=pl.BlockSpec((None, nq * BLOCK, wq), lambda r, i: (r, i, 0)),
        scratch_shapes=[pltpu.VMEM((KVH_A, L + P, LANES), BF16),
                        pltpu.VMEM((KVH_A, L + P, LANES), BF16)],
        compiler_params=_cp(2), name="attn_a_lat",
    )(sink, qkv, qkv, qkv, cache_k, cache_v, masks)


def _attn_b_ctx_kernel(q_ref, k_ref, v_ref, o_ref):
    G, L = q_ref.shape[:2]
    low = _lane_low((L, LANES))
    for g in range(G):
        for p in range(H_B // 2):
            cs = slice(LANES * p, LANES * p + LANES)
            lhs = (q_ref[g, :, cs] * ATTN_SCALE).astype(BF16)
            k, v = k_ref[g, :, cs], v_ref[g, :, cs]
            outs = [_normalised(_attend(lhs, [_key_variant(k, sub)], [_value_variant(v, sub)], [None])[0])
                    for sub in range(2)]
            o_ref[g, :, cs] = jnp.where(low, outs[0], outs[1]).astype(o_ref.dtype)


def _bias_table_kernel(rpb_ref, o_ref, t_ref):
    n_dr, n_dc = 2 * NA_ROWS - 1, 2 * NA_COLS - 1
    n_t = o_ref.shape[1]
    shape = (GRID_W, LANES)
    cq = lax.broadcasted_iota(I32, shape, 0)
    lane = lax.broadcasted_iota(I32, shape, 1)
    right = lane >= GRID_W
    ck = lane & (GRID_W - 1)
    cs = jnp.clip(cq - NA_COLS // 2, 0, GRID_W - NA_COLS)
    col_ok = (ck >= cs) & (ck < cs + NA_COLS)
    dc = ck - cq + (NA_COLS - 1)
    for j in (0, 1, n_t - 1, n_t):
        t_ref[j] = jnp.zeros(shape, F32)

    def build(r, carry, h):
        acc = jnp.zeros(shape, F32)
        for d in range(n_dc):
            acc = jnp.where(dc == d, rpb_ref[(h * n_dr + r) * n_dc + d] * LOG2E, acc)
        t_ref[r + 2] = jnp.where(col_ok, acc, NEG)
        return carry

    def compose(t, carry, h):
        o_ref[h, t] = jnp.where(right, t_ref[t + 1], t_ref[t])
        return carry

    for h in range(H_B):
        lax.fori_loop(0, n_dr, functools.partial(build, h=h), 0)
        lax.fori_loop(0, n_t, functools.partial(compose, h=h), 0)


def _bias_table(rpb_l):
    n_t = 2 * (NA_ROWS + 1)
    return pl.pallas_call(
        _bias_table_kernel,
        out_shape=jax.ShapeDtypeStruct((H_B, n_t, GRID_W, LANES), F32),
        in_specs=[_smem_spec()],
        out_specs=pl.BlockSpec(memory_space=pltpu.VMEM),
        scratch_shapes=[pltpu.VMEM((n_t + 1, GRID_W, LANES), F32)],
        name="na_bias_table",
    )(rpb_l.reshape(-1))


def _attn_b_lat_kernel(q_ref, k_ref, v_ref, kc_ref, vc_ref, tp_ref, o_ref, kv_ref, vv_ref, *, nb, rows_total):
    i = pl.program_id(1)
    L, P = k_ref.shape[0], kc_ref.shape[1]

    @pl.when(i == 0)
    def _():
        _fill_mha_variants(k_ref, v_ref, kv_ref, vv_ref, 0, L)
        _fill_mha_variants(kc_ref, vc_ref, kv_ref, vv_ref, L, P, feature_major=True)

    nq = NA_QBLOCKS
    nwin = NA_WIN_BLOCKS + nq - 1
    wk = nwin * BLOCK
    rows = nq * BLOCK
    sblk = jnp.clip(nq * i - 2, 0, nb - nwin)
    start = pl.multiple_of(sblk * BLOCK, BLOCK)
    shape = (rows, wk)
    rq = 2 * nq * i + (lax.broadcasted_iota(I32, shape, 0) >> 6)
    rk = 2 * sblk + (lax.broadcasted_iota(I32, shape, 1) >> 6)
    rs = jnp.clip(rq - NA_ROWS // 2, 0, rows_total - NA_ROWS)
    row_bias = jnp.where((rk >= rs) & (rk < rs + NA_ROWS), 0.0, NEG)
    n_t = tp_ref.shape[1]
    low = _lane_low((rows, LANES))
    for p in range(H_B // 2):
        cs = slice(LANES * p, LANES * p + LANES)
        lhs = (q_ref[:, cs] * ATTN_SCALE).astype(BF16)
        outs = []
        for sub in range(2):
            h = 2 * p + sub
            strips = []
            for qq in range(nq):
                tiles = []
                for kb in range(nwin):
                    t = jnp.clip(2 * (sblk + kb - (nq * i + qq)) + NA_ROWS + 1, 1, n_t - 1)
                    tiles.append(jnp.concatenate([tp_ref[h, t], tp_ref[h, t - 1]], axis=0))
                strips.append(jnp.concatenate(tiles, axis=1))
            bias = jnp.concatenate(strips, axis=0) + row_bias
            ks = [kv_ref[p, sub, pl.ds(start, wk), :], kv_ref[p, sub, L:L + P, :]]
            vs = [vv_ref[p, sub, pl.ds(start, wk), :], vv_ref[p, sub, L:L + P, :]]
            outs.append(_normalised(_attend(lhs, ks, vs, [bias, None])[0]))
        o_ref[:, cs] = jnp.where(low, outs[0], outs[1]).astype(o_ref.dtype)


def _attn_b_ctx(qkv):
    R, L, _ = qkv.shape
    w = H_B * HEAD_DIM
    G = CTX_GROUP_B
    assert R % G == 0
    return pl.pallas_call(
        _attn_b_ctx_kernel,
        out_shape=jax.ShapeDtypeStruct((R, L, w), F32),
        grid=(R // G,),
        in_specs=[pl.BlockSpec((G, L, w), lambda r: (r, 0, COL_QB)),
                  pl.BlockSpec((G, L, w), lambda r: (r, 0, COL_KB)),
                  pl.BlockSpec((G, L, w), lambda r: (r, 0, COL_VB))],
        out_specs=pl.BlockSpec((G, L, w), lambda r: (r, 0, 0)),
        compiler_params=_cp(1), name="attn_b_ctx",
    )(qkv, qkv, qkv)


def _attn_b_lat(qkv, cache_k, cache_v, bias_tab, layer):
    R, L, _ = qkv.shape
    nb = L // BLOCK
    P = cache_k.shape[3]
    w = H_B * HEAD_DIM
    cache_spec = pl.BlockSpec((None, None, w, P), lambda r, i: (r, layer, 0, 0))
    rows = NA_QBLOCKS * BLOCK
    assert nb % NA_QBLOCKS == 0 and nb >= NA_WIN_BLOCKS + NA_QBLOCKS - 1
    return pl.pallas_call(
        functools.partial(_attn_b_lat_kernel, nb=nb, rows_total=L // GRID_W),
        out_shape=jax.ShapeDtypeStruct((R, L, w), F32),
        grid=(R, nb // NA_QBLOCKS),
        in_specs=[pl.BlockSpec((None, rows, w), lambda r, i: (r, i, COL_QB)),
                  pl.BlockSpec((None, L, w), lambda r, i: (r, 0, COL_KB)),
                  pl.BlockSpec((None, L, w), lambda r, i: (r, 0, COL_VB)),
                  cache_spec, cache_spec,
                  pl.BlockSpec(bias_tab.shape, lambda r, i: (0, 0, 0, 0))],
        out_specs=pl.BlockSpec((None, rows, w), lambda r, i: (r, i, 0)),
        scratch_shapes=[pltpu.VMEM((H_B // 2, 2, L + P, LANES), BF16),
                        pltpu.VMEM((H_B // 2, 2, L + P, 2 * LANES), BF16)],
        compiler_params=_cp(2), name="attn_b_lat",
    )(qkv, qkv, qkv, cache_k, cache_v, bias_tab)


T_ZF, T_ZB, T_XF, T_XB, T_GF, T_GB, T_MA, T_MB = range(8)


def _ret_kernel(*refs, nc, has_s0, out_state):
    refs = list(refs)
    dec_ref, q_ref, k_ref, v_ref, g_ref, gn_ref = refs[:6]
    pos = 6
    s0_ref = None
    if has_s0:
        s0_ref = refs[pos]; pos += 1
    o_ref = refs[pos]; pos += 1
    st_ref = None
    if out_state:
        st_ref = refs[pos]; pos += 1
    sf_ref, sb_ref, tab_ref = refs[pos], refs[pos + 1], refs[pos + 2]

    sq = (BLOCK, LANES)
    low = _lane_low(sq)
    row_low = lax.broadcasted_iota(I32, sq, 0) < HEAD_DIM
    same_head = low == row_low
    n_pairs = H_C // 2

    @pl.when(pl.program_id(0) == 0)
    def _():
        ri = lax.broadcasted_iota(I32, sq, 0).astype(F32)
        ci = lax.broadcasted_iota(I32, sq, 1).astype(F32)

        def log_gamma(d, h):
            return -jnp.exp(jnp.full(sq, dec_ref[d, h], F32))

        for p in range(n_pairs):
            lf = [log_gamma(0, 2 * p), log_gamma(0, 2 * p + 1)]
            lb = [log_gamma(1, 2 * p), log_gamma(1, 2 * p + 1)]
            lf_lane, lb_lane = jnp.where(low, lf[0], lf[1]), jnp.where(low, lb[0], lb[1])
            lf_row, lb_row = jnp.where(row_low, lf[0], lf[1]), jnp.where(row_low, lb[0], lb[1])
            tab_ref[p, T_ZF] = jnp.exp(lf_lane * (BLOCK - 1.0 - ri))
            tab_ref[p, T_ZB] = jnp.exp(lb_lane * ri)
            tab_ref[p, T_XF] = jnp.exp(lf_lane * (ri + 1.0))
            tab_ref[p, T_XB] = jnp.exp(lb_lane * (BLOCK - ri))
            tab_ref[p, T_GF] = jnp.exp(lf_row * float(BLOCK))
            tab_ref[p, T_GB] = jnp.exp(lb_row * float(BLOCK))
            rel = ri - ci
            for sub in range(2):
                fwd = jnp.where(rel >= 0, jnp.exp(lf[sub] * jnp.maximum(rel, 0.0)), 0.0)
                bwd = jnp.where(rel <= 0, jnp.exp(lb[sub] * jnp.maximum(-rel, 0.0)), 0.0)
                tab_ref[p, T_MA + sub] = fwd + bwd

    def chunk(ref, c, p):
        r0 = pl.multiple_of(c * BLOCK, BLOCK)
        return ref[pl.ds(r0, BLOCK), LANES * p:LANES * p + LANES]

    unroll = min(nc, 8)

    def inc_body(c, carry):
        for p in range(n_pairs):
            kt = (chunk(k_ref, c, p) * QK_SCALE).T.astype(BF16)
            v = chunk(v_ref, c, p)
            zv = jnp.concatenate([v * tab_ref[p, T_ZF], v * tab_ref[p, T_ZB]], axis=1).astype(BF16)
            inc = _dot(kt, zv)
            sf_ref[c, p] = jnp.where(same_head, inc[:, :LANES], 0.0)
            sb_ref[c, p] = jnp.where(same_head, inc[:, LANES:], 0.0)
        return carry

    lax.fori_loop(0, nc, inc_body, 0, unroll=unroll)

    def scan_body(t, S, ref, tg, reverse):
        c = nc - 1 - t if reverse else t
        new = []
        for p in range(n_pairs):
            inc = ref[c, p]
            ref[c, p] = S[p]
            new.append(tab_ref[p, tg] * S[p] + inc)
        return tuple(new)

    zero = jnp.zeros(sq, F32)
    s_f0 = tuple(s0_ref[0, p] if has_s0 else zero for p in range(n_pairs))
    s_b0 = tuple(s0_ref[1, p] if has_s0 else zero for p in range(n_pairs))
    s_f = lax.fori_loop(0, nc, functools.partial(scan_body, ref=sf_ref, tg=T_GF, reverse=False), s_f0)
    s_b = lax.fori_loop(0, nc, functools.partial(scan_body, ref=sb_ref, tg=T_GB, reverse=True), s_b0)
    if out_state:
        for p in range(n_pairs):
            st_ref[0, p] = s_f[p]
            st_ref[1, p] = s_b[p]

    def out_body(c, carry):
        for p in range(n_pairs):
            q = chunk(q_ref, c, p)
            k16 = (chunk(k_ref, c, p) * QK_SCALE).astype(BF16)
            v = chunk(v_ref, c, p)
            a = _dot_t(jnp.concatenate([_mask_head(q, 0), _mask_head(q, 1)], axis=0).astype(BF16), k16)
            lhs = jnp.concatenate([a[:BLOCK] * tab_ref[p, T_MA], a[BLOCK:] * tab_ref[p, T_MA + 1],
                                   q * tab_ref[p, T_XF], q * tab_ref[p, T_XB]], axis=1).astype(BF16)
            rhs = jnp.concatenate([_mask_head(v, 0), _mask_head(v, 1), sf_ref[c, p], sb_ref[c, p]],
                                  axis=0).astype(BF16)
            o = _dot(lhs, rhs)
            mu = jnp.where(low, _mask_head(o, 0).sum(-1, keepdims=True),
                           _mask_head(o, 1).sum(-1, keepdims=True)) * (1.0 / HEAD_DIM)
            d = o - mu
            d2 = d * d
            var = jnp.where(low, _mask_head(d2, 0).sum(-1, keepdims=True),
                            _mask_head(d2, 1).sum(-1, keepdims=True)) * (1.0 / HEAD_DIM)
            cs = slice(LANES * p, LANES * p + LANES)
            on = d * lax.rsqrt(var + LN_EPS) * gn_ref[:, cs]
            r0 = pl.multiple_of(c * BLOCK, BLOCK)
            o_ref[pl.ds(r0, BLOCK), cs] = (_silu(chunk(g_ref, c, p)) * on).astype(o_ref.dtype)
        return carry

    lax.fori_loop(0, nc, out_body, 0, unroll=unroll)


def _retention(qkv, decay, gn_w, s0=None, layer=0, out_state=False):
    R, L, _ = qkv.shape
    nc = L // BLOCK
    w = H_C * HEAD_DIM
    n_pairs = H_C // 2
    col = lambda cb: pl.BlockSpec((None, L, w), lambda r: (r, 0, cb))
    in_specs = [_smem_spec(), col(COL_QC), col(COL_KC), col(COL_VC), col(COL_GC),
                pl.BlockSpec((1, w), lambda r: (0, 0))]
    args = [decay, qkv, qkv, qkv, qkv, gn_w.reshape(1, w)]
    if s0 is not None:
        in_specs.append(pl.BlockSpec((None, None, 2, n_pairs, BLOCK, LANES), lambda r: (r, layer, 0, 0, 0, 0)))
        args.append(s0)
    out_shape = [jax.ShapeDtypeStruct((R, L, w), F32)]
    out_specs = [pl.BlockSpec((None, L, w), lambda r: (r, 0, 0))]
    if out_state:
        out_shape.append(jax.ShapeDtypeStruct((R, 2, n_pairs, BLOCK, LANES), F32))
        out_specs.append(pl.BlockSpec((None, 2, n_pairs, BLOCK, LANES), lambda r: (r, 0, 0, 0, 0)))
    res = pl.pallas_call(
        functools.partial(_ret_kernel, nc=nc, has_s0=s0 is not None, out_state=out_state),
        out_shape=out_shape, grid=(R,), in_specs=in_specs, out_specs=out_specs,
        scratch_shapes=[pltpu.VMEM((nc, n_pairs, BLOCK, LANES), F32),
                        pltpu.VMEM((nc, n_pairs, BLOCK, LANES), F32),
                        pltpu.VMEM((n_pairs, 8, BLOCK, LANES), F32)],
        compiler_params=_cp(1), name="retention",
    )(*args)
    return res if out_state else res[0]


def _pair_states_to_heads(st):
    a = st[..., :HEAD_DIM, :HEAD_DIM]
    b = st[..., HEAD_DIM:, HEAD_DIM:]
    return jnp.stack([a, b], axis=3).reshape(st.shape[0], 2, H_C, HEAD_DIM, HEAD_DIM)


def _heads_to_pair_states(s):
    s = s.reshape(*s.shape[:3], H_C // 2, 2, HEAD_DIM, HEAD_DIM)
    z = jnp.zeros_like(s[..., 0, :, :])
    top = jnp.concatenate([s[..., 0, :, :], z], axis=-1)
    bot = jnp.concatenate([z, s[..., 1, :, :]], axis=-1)
    return jnp.concatenate([top, bot], axis=-2)


def _outproj_kernel(oa_ref, ob_ref, oc_ref, x_ref, g1_ref, sc2_ref, sh2_ref, w_ref, lg_ref, lb_ref, wr_ref,
                    x1_ref, h2_ref, aff_ref):
    wa, wb = H_A * HEAD_DIM, H_B * HEAD_DIM
    tm = x_ref.shape[0]
    part = tm // OUTPROJ_PARTS
    for r0 in range(0, tm, part):
        rs = slice(r0, r0 + part)
        mix = (_dot(oa_ref[rs, :].astype(BF16), w_ref[0:wa, :])
               + _dot(ob_ref[rs, :].astype(BF16), w_ref[wa:wa + wb, :])
               + _dot(oc_ref[rs, :].astype(BF16), w_ref[wa + wb:, :]))
        x1 = _layernorm(ALPHA * x_ref[rs, :] + g1_ref[...] * mix, lg_ref[...], lb_ref[...])
        x1_ref[rs, :] = x1
        h2 = (x1 * (1.0 + sc2_ref[...]) + sh2_ref[...]).astype(BF16)
        h2_ref[rs, :] = h2
        logits = _dot(h2, wr_ref[...])
        e = jnp.exp(logits - logits.max(axis=-1, keepdims=True))
        aff_ref[rs, :] = e / e.sum(axis=-1, keepdims=True)


def _outproj(oa, ob, oc, x, mods, w_out_bf16, ln_g, ln_b, w_router_bf16, layer, tm):
    R, L, _ = x.shape
    row = lambda w: pl.BlockSpec((None, tm, w), lambda r, i: (r, i, 0))
    vec = pl.BlockSpec((1, D_MODEL), lambda r, i: (0, 0))
    return pl.pallas_call(
        _outproj_kernel,
        out_shape=[jax.ShapeDtypeStruct((R, L, D_MODEL), F32),
                   jax.ShapeDtypeStruct((R, L, D_MODEL), BF16),
                   jax.ShapeDtypeStruct((R, L, N_EXPERTS), F32)],
        grid=(R, L // tm),
        in_specs=[row(H_A * HEAD_DIM), row(H_B * HEAD_DIM), row(H_C * HEAD_DIM), row(D_MODEL),
                  _mod_spec(2, 2), _mod_spec(4, 2), _mod_spec(3, 2),
                  pl.BlockSpec((None, D_MODEL, D_MODEL), lambda r, i: (layer, 0, 0)), vec, vec,
                  pl.BlockSpec((None, D_MODEL, N_EXPERTS), lambda r, i: (layer, 0, 0))],
        out_specs=[row(D_MODEL), row(D_MODEL), row(N_EXPERTS)],
        compiler_params=_cp(2), name="out_proj_ln_router",
    )(oa, ob, oc, x, mods, mods, mods, w_out_bf16, ln_g.reshape(1, -1), ln_b.reshape(1, -1), w_router_bf16)


CUMSUM_CHUNK = 256


def _excl_cumsum_lanes(x):
    rows, n = x.shape
    w = min(CUMSUM_CHUNK, n)
    tri = (lax.broadcasted_iota(I32, (w, w), 0) < lax.broadcasted_iota(I32, (w, w), 1))
    tri = jnp.where(tri, 1.0, 0.0).astype(BF16)
    carry = jnp.zeros((rows, 1), F32)
    outs = []
    for c in range(0, n, w):
        xc = x[:, c:c + w]
        outs.append(_dot(xc.astype(BF16), tri) + carry)
        carry = carry + xc.sum(axis=-1, keepdims=True)
    return outs[0] if len(outs) == 1 else jnp.concatenate(outs, axis=1)


def _select_kernel(a_ref, pos_ref, *, cap):
    rb, ne, n = a_ref.shape
    rows = rb * ne
    a = a_ref[...].reshape(rows, n)
    capf = float(cap)

    def body(_, c):
        lo, hi = c
        mid = lo + ((hi - lo + 1) >> 1)
        cnt = jnp.where(a >= pltpu.bitcast(mid, F32), 1.0, 0.0).sum(axis=-1, keepdims=True)
        ok = cnt >= capf
        return jnp.where(ok, mid, lo), jnp.where(ok, hi, mid - 1)

    one_bits = 0x3F800000
    lo, _ = lax.fori_loop(0, 30, body, (jnp.zeros((rows, 1), I32), jnp.full((rows, 1), one_bits, I32)))
    thr = pltpu.bitcast(lo, F32)
    gt = jnp.where(a > thr, 1.0, 0.0)
    eq = jnp.where(a == thr, 1.0, 0.0)
    need = capf - gt.sum(axis=-1, keepdims=True)
    sel = gt + eq * jnp.where(_excl_cumsum_lanes(eq) < need, 1.0, 0.0)
    slot = _excl_cumsum_lanes(sel)
    pos_ref[...] = jnp.where(sel > 0.5, slot, -1.0).astype(I32).reshape(rb, ne, n)


def _select(aff_t, cap):
    R, ne, n = aff_t.shape
    rb = min(R, max(1, 128 * 2048 // (ne * n)))
    return pl.pallas_call(
        functools.partial(_select_kernel, cap=cap),
        out_shape=jax.ShapeDtypeStruct((R, ne, n), I32),
        grid=(R // rb,),
        in_specs=[pl.BlockSpec((rb, ne, n), lambda r: (r, 0, 0))],
        out_specs=pl.BlockSpec((rb, ne, n), lambda r: (r, 0, 0)),
        compiler_params=_cp(1), name="expert_select",
    )(aff_t)


def _gather_kernel(pos_ref, aff_ref, h_ref, x_ref, gate_ref, *, cap):
    eb, _, n = pos_ref.shape
    slot = lax.broadcasted_iota(I32, (cap, n), 0)
    parts = []
    for j in range(eb):
        hit = slot == pos_ref[j]
        parts.append(jnp.where(hit, 1.0, 0.0).astype(BF16))
        gate_ref[j] = jnp.where(hit, aff_ref[j], 0.0).sum(axis=-1, keepdims=True)
    onehot = parts[0] if eb == 1 else jnp.concatenate(parts, axis=0)
    xe = _dot(onehot, h_ref[...]).astype(BF16)
    for j in range(eb):
        x_ref[j] = xe[j * cap:(j + 1) * cap]


def _gather(pos, aff_t, h2, cap):
    R, ne, n = pos.shape
    eb = max(1, min(ne, ROW_TILE // cap))
    row = pl.BlockSpec((None, eb, 1, n), lambda r, e: (r, e, 0, 0))
    return pl.pallas_call(
        functools.partial(_gather_kernel, cap=cap),
        out_shape=[jax.ShapeDtypeStruct((ne, R * cap, D_MODEL), BF16),
                   jax.ShapeDtypeStruct((ne, R * cap, 1), F32)],
        grid=(R, ne // eb),
        in_specs=[row, row, pl.BlockSpec((None, n, D_MODEL), lambda r, e: (r, 0, 0))],
        out_specs=[pl.BlockSpec((eb, cap, D_MODEL), lambda r, e: (e, r, 0)),
                   pl.BlockSpec((eb, cap, 1), lambda r, e: (e, r, 0))],
        compiler_params=_cp(2), name="expert_gather",
    )(pos.reshape(R, ne, 1, n), aff_t.reshape(R, ne, 1, n), h2)


FF_TILE = 512


def _ffn_kernel(x0_ref, x1_ref, wa_ref, wb_ref, wd_ref, g0_ref, g1_ref, y0_ref, y1_ref, acc_ref):
    f = pl.program_id(1)

    @pl.when(f == 0)
    def _():
        acc_ref[...] = jnp.zeros_like(acc_ref)

    x = jnp.concatenate([x0_ref[...], x1_ref[...]], axis=0)
    a = _dot(x, wa_ref[...].astype(BF16))
    b = _dot(x, wb_ref[...].astype(BF16))
    acc_ref[...] += _dot((_silu(a) * b).astype(BF16), wd_ref[...].astype(BF16))

    @pl.when(f == pl.num_programs(1) - 1)
    def _():
        s0 = x0_ref.shape[0]
        y0_ref[...] = (acc_ref[:s0, :] * g0_ref[...]).astype(BF16)
        y1_ref[...] = (acc_ref[s0:, :] * g1_ref[...]).astype(BF16)


def _experts(xe0, gates0, xe1, gates1, w_gate_up, w_down, layer):
    ne, S0, _ = xe0.shape
    S1 = xe1.shape[1]
    nf = D_FF // FF_TILE
    rows = lambda S, w: pl.BlockSpec((None, S, w), lambda e, f: (e, 0, 0))
    return pl.pallas_call(
        _ffn_kernel,
        out_shape=[jax.ShapeDtypeStruct((ne, S0, D_MODEL), BF16), jax.ShapeDtypeStruct((ne, S1, D_MODEL), BF16)],
        grid=(ne, nf),
        in_specs=[rows(S0, D_MODEL), rows(S1, D_MODEL),
                  pl.BlockSpec((None, None, D_MODEL, FF_TILE), lambda e, f: (layer, e, 0, f)),
                  pl.BlockSpec((None, None, D_MODEL, FF_TILE), lambda e, f: (layer, e, 0, nf + f)),
                  pl.BlockSpec((None, None, FF_TILE, D_MODEL), lambda e, f: (layer, e, f, 0)),
                  rows(S0, 1), rows(S1, 1)],
        out_specs=[rows(S0, D_MODEL), rows(S1, D_MODEL)],
        scratch_shapes=[pltpu.VMEM((S0 + S1, D_MODEL), F32)],
        compiler_params=_cp(2), name="expert_ffn",
    )(xe0, xe1, w_gate_up, w_gate_up, w_down, gates0, gates1)


def _scatter_kernel(pos_ref, y_ref, x1_ref, g2_ref, lg_ref, lb_ref, o_ref, *, cap):
    tm, ne = pos_ref.shape
    cw = max(cap, LANES)
    epc = cw // cap
    lane = lax.broadcasted_iota(I32, (tm, cw), 1)
    pos = pos_ref[...]
    def onehot_chunk(c):
        onehot = jnp.zeros((tm, cw), F32)
        for jj in range(epc):
            col = pos[:, c * epc + jj:c * epc + jj + 1]
            tgt = jnp.where(col >= 0, col + jj * cap, -1)
            onehot = onehot + jnp.where(lane == tgt, 1.0, 0.0)
        return onehot.astype(BF16)

    n_chunks = ne // epc
    if ne * cap <= ROW_TILE:
        acc = _dot(jnp.concatenate([onehot_chunk(c) for c in range(n_chunks)], axis=1),
                   y_ref[...].reshape(ne * cap, D_MODEL))
    else:
        acc = jnp.zeros((tm, D_MODEL), F32)
        for c in range(n_chunks):
            yc = y_ref[c] if epc == 1 else y_ref[c * epc:(c + 1) * epc].reshape(cw, D_MODEL)
            acc = acc + _dot(onehot_chunk(c), yc)
    o_ref[...] = _layernorm(ALPHA * x1_ref[...] + g2_ref[...] * acc, lg_ref[...], lb_ref[...])


def _scatter(pos_tok, y, x1, mods, ln_g, ln_b, cap, tm):
    R, n, ne = pos_tok.shape
    vec = pl.BlockSpec((1, D_MODEL), lambda r, i: (0, 0))
    return pl.pallas_call(
        functools.partial(_scatter_kernel, cap=cap),
        out_shape=jax.ShapeDtypeStruct((R, n, D_MODEL), F32),
        grid=(R, n // tm),
        in_specs=[pl.BlockSpec((None, tm, ne), lambda r, i: (r, i, 0)),
                  pl.BlockSpec((ne, cap, D_MODEL), lambda r, i: (0, r, 0)),
                  pl.BlockSpec((None, tm, D_MODEL), lambda r, i: (r, i, 0)),
                  _mod_spec(5, 2), vec, vec],
        out_specs=pl.BlockSpec((None, tm, D_MODEL), lambda r, i: (r, i, 0)),
        compiler_params=_cp(2), name="expert_scatter_ln",
    )(pos_tok, y, x1, mods, ln_g.reshape(1, -1), ln_b.reshape(1, -1))


def _capacity(n):
    return CAPACITY_FACTOR * n // N_EXPERTS


def _route(h2, aff):
    cap = _capacity(h2.shape[1])
    aff_t = jnp.swapaxes(aff, 1, 2)
    pos = _select(aff_t, cap)
    xe, gates = _gather(pos, aff_t, h2, cap)
    return pos, xe, gates


def _channel_mixers(groups, l, w_gate_up, w_down, ln2_g, ln2_b):
    routed = [_route(g["h2"], g["aff"]) for g in groups]
    ys = _experts(routed[0][1], routed[0][2], routed[1][1], routed[1][2], w_gate_up, w_down, l)
    return [_scatter(jnp.swapaxes(r[0], 1, 2), y, g["x1"], g["mods"], ln2_g[l], ln2_b[l],
                     _capacity(g["h2"].shape[1]), g["tm"]) for g, r, y in zip(groups, routed, ys)]


def kernel(x_prompt, x_sample, cache_attn_a_k, cache_attn_a_v, cache_attn_b_k, cache_attn_b_v, state_ret, c, c_ctx,
           w_ada, b_ada, w_in, w_out, attn_sink, na_rpb, ret_decay, ret_gn, ln1_g, ln1_b, ln2_g, ln2_b,
           w_router, w_gate_up, w_down):
    n_ctx, L_ctx, _ = x_prompt.shape
    n_lat, L_lat, _ = x_sample.shape
    P = cache_attn_a_k.shape[2]

    cond = jnp.concatenate([c_ctx[None, :], c, jnp.zeros((8 - 1 - n_lat, D_MODEL), F32)], axis=0)
    mod = _modulation(cond, w_ada, b_ada)
    w_in16, w_out16, w_r16 = w_in.astype(BF16), w_out.astype(BF16), w_router.astype(BF16)
    rope_tabs = _rope_tables(L_lat)
    feature_major = lambda t: jnp.transpose(t, (0, 1, 3, 4, 2)).reshape(n_lat, DEPTH, -1, P)
    cak, cav = feature_major(cache_attn_a_k), feature_major(cache_attn_a_v)
    cbk, cbv = feature_major(cache_attn_b_k), feature_major(cache_attn_b_v)
    s0_pairs = _heads_to_pair_states(state_ret)

    xc, xs = x_prompt, x_sample
    tm = ROW_TILE
    a_k, a_v, b_k, b_v, st = [], [], [], [], []
    flat = lambda a: a.reshape(1, n_ctx * L_ctx, a.shape[-1])
    unflat = lambda a: a.reshape(n_ctx, L_ctx, a.shape[-1])
    for l in range(DEPTH):
        mod_c = mod[l, 0:1][:, None, :]
        mods_c = jnp.broadcast_to(mod_c, (n_ctx, 1, 6 * D_MODEL))
        qkv, ka, va, kb, vb = _project(flat(xc), mod_c, w_in16, l, tm, kv_seq=L_ctx)
        qkv = unflat(qkv)
        oa = _attn_a_ctx(qkv, attn_sink[l])
        ob = _attn_b_ctx(qkv)
        oc, s_l = _retention(qkv, ret_decay[l], ret_gn[l], out_state=True)
        x1, h2, aff = [unflat(t) for t in _outproj(flat(oa), flat(ob), flat(oc), flat(xc), mod_c, w_out16,
                                                   ln1_g[l], ln1_b[l], w_r16, l, tm)]
        ctx = dict(x1=x1, h2=h2, aff=aff, mods=mods_c, tm=L_ctx)
        heads_last = lambda t, h: jnp.transpose(t.reshape(n_ctx, h, HEAD_DIM, L_ctx), (0, 3, 1, 2))
        a_k.append(heads_last(ka, KVH_A))
        a_v.append(heads_last(va, KVH_A))
        b_k.append(heads_last(kb, H_B))
        b_v.append(heads_last(vb, H_B))
        st.append(_pair_states_to_heads(s_l))

        mods_s = mod[l, 1:1 + n_lat][:, None, :]
        qkv = _project(xs, mods_s, w_in16, l, tm, rope_tabs)
        oa = _attn_a_lat(qkv, attn_sink[l], cak, cav, l)
        ob = _attn_b_lat(qkv, cbk, cbv, _bias_table(na_rpb[l]), l)
        oc = _retention(qkv, ret_decay[l], ret_gn[l], s0=s0_pairs, layer=l)
        x1, h2, aff = _outproj(oa, ob, oc, xs, mods_s, w_out16, ln1_g[l], ln1_b[l], w_r16, l, tm)
        lat = dict(x1=x1, h2=h2, aff=aff, mods=mods_s, tm=tm)

        xc, xs = _channel_mixers([ctx, lat], l, w_gate_up, w_down, ln2_g, ln2_b)

    return (xc, xs, jnp.stack(a_k, 1), jnp.stack(a_v, 1), jnp.stack(b_k, 1), jnp.stack(b_v, 1),
            jnp.stack(st, 1))
```

```python
import functools

import numpy as np
import jax
import jax.numpy as jnp
from jax import lax
from jax.experimental import pallas as pl
from jax.experimental.pallas import tpu as pltpu

F32 = jnp.float32
BF16 = jnp.bfloat16
I32 = jnp.int32

D_MODEL = 1024
DEPTH = 2
GRID_W = 64
HEAD_DIM = 64
H_A, KVH_A, H_B, H_C = 8, 2, 4, 4
D_IN = 2560
BLOCK = 128
WINDOW = 128
NA_ROWS, NA_COLS = 8, 16
NA_WIN_BLOCKS = 5
NA_QBLOCKS = 4
WIN_QBLOCKS = 2
CTX_GROUP_A = 4
CTX_GROUP_B = 8
ROPE_BASE = 10000.0
N_EXPERTS = 16
CAPACITY_FACTOR = 2
D_FF = 1024
LN_EPS = 1e-6
ALPHA = (2 * DEPTH) ** 0.25
NEG = -1e30
LANES = 128
QK_SCALE = HEAD_DIM ** -0.5
LOG2E = float(np.log2(np.e))
ATTN_SCALE = QK_SCALE * LOG2E
VMEM_LIMIT = 56 * 1024 * 1024
ROW_TILE = 1024
OUTPROJ_PARTS = 2
RET_GROUP_CHUNKS = 8

COL_QA, COL_KA, COL_VA = 0, 4, 5
COL_QB, COL_KB, COL_VB = 3, 4, 5
COL_QC, COL_KC, COL_VC, COL_GC = 6, 7, 8, 9


def _cp(n_axes):
    return pltpu.CompilerParams(dimension_semantics=("arbitrary",) * n_axes, vmem_limit_bytes=VMEM_LIMIT)


def _dot(a, b):
    return jnp.dot(a, b, preferred_element_type=F32)


def _dot_t(a, b):
    return lax.dot_general(a, b, (((1,), (1,)), ((), ())), preferred_element_type=F32)


def _silu(x):
    return x / (1.0 + jnp.exp(-x))


def _lane_low(shape):
    return lax.broadcasted_iota(I32, shape, len(shape) - 1) < HEAD_DIM


def _layernorm(y, g, b):
    mu = jnp.mean(y, axis=-1, keepdims=True)
    d = y - mu
    var = jnp.mean(d * d, axis=-1, keepdims=True)
    return d * lax.rsqrt(var + LN_EPS) * g + b


def _mod_kernel(c_ref, w_ref, b_ref, o_ref):
    s = _silu(c_ref[...]).astype(BF16)
    o_ref[...] = _dot(s, w_ref[...].astype(BF16)) + b_ref[...]


def _modulation(cond, w_ada, b_ada):
    tn = 1536
    return pl.pallas_call(
        _mod_kernel,
        out_shape=jax.ShapeDtypeStruct((DEPTH, 8, 6 * D_MODEL), F32),
        grid=(DEPTH, 6 * D_MODEL // tn),
        in_specs=[pl.BlockSpec((8, D_MODEL), lambda l, j: (0, 0)),
                  pl.BlockSpec((None, D_MODEL, tn), lambda l, j: (l, 0, j)),
                  pl.BlockSpec((None, 1, tn), lambda l, j: (l, 0, j))],
        out_specs=pl.BlockSpec((None, 8, tn), lambda l, j: (l, 0, j)),
        compiler_params=_cp(2), name="modulation",
    )(cond, w_ada, b_ada.reshape(DEPTH, 1, 6 * D_MODEL))


def _mod_spec(k, n_grid):
    if n_grid == 1:
        return pl.BlockSpec((None, 1, D_MODEL), lambda r: (r, 0, k))
    return pl.BlockSpec((None, 1, D_MODEL), lambda r, i: (r, 0, k))


def _rope_tables(L):
    t = np.arange(L)
    pos = (t // GRID_W, t % GRID_W)
    half = HEAD_DIM // 2
    inv = 1.0 / (ROPE_BASE ** (np.arange(0, half, 2) / half))
    cos = np.zeros((L, HEAD_DIM)); s_up = np.zeros((L, HEAD_DIM)); s_dn = np.zeros((L, HEAD_DIM))
    for part in range(2):
        ang = (pos[part][:, None] * inv[None, :]).astype(np.float32).astype(np.float64)
        q = half // 2
        base = part * half
        cos[:, base:base + q] = np.cos(ang); cos[:, base + q:base + half] = np.cos(ang)
        s_up[:, base:base + q] = -np.sin(ang)
        s_dn[:, base + q:base + half] = np.sin(ang)
    tile = lambda a: jnp.asarray(np.tile(a, (1, LANES // HEAD_DIM)), F32)
    return tile(cos), tile(s_up), tile(s_dn)


KV_COLUMNS = ((512, 128), (640, 128), (1024, 256), (1280, 256))


def _proj_kernel(*refs, rope, emit_kv):
    kv_refs = ()
    if emit_kv:
        refs, kv_refs = refs[:-len(KV_COLUMNS)], refs[-len(KV_COLUMNS):]
    if rope:
        x_ref, sc_ref, sh_ref, w_ref, cos_ref, sup_ref, sdn_ref, o_ref = refs
    else:
        x_ref, sc_ref, sh_ref, w_ref, o_ref = refs
    h = (x_ref[...] * (1.0 + sc_ref[...]) + sh_ref[...]).astype(BF16)
    n_rope = (H_A + KVH_A) * HEAD_DIM
    step = 512
    for j in range(0, D_IN, step):
        acc = _dot(h, w_ref[:, j:j + step])
        if rope and j < n_rope:
            q = HEAD_DIM // 4
            for c in range(0, step, LANES):
                a = acc[:, c:c + LANES]
                if j + c < n_rope:
                    a = (a * cos_ref[...] + pltpu.roll(a, LANES - q, 1) * sup_ref[...]
                         + pltpu.roll(a, q, 1) * sdn_ref[...])
                o_ref[:, j + c:j + c + LANES] = a
        else:
            o_ref[:, j:j + step] = acc
        for ref, (c0, w) in zip(kv_refs, KV_COLUMNS):
            if j <= c0 < j + step:
                n_req, _, seq = ref.shape
                for g in range(n_req):
                    ref[g] = acc[g * seq:(g + 1) * seq, c0 - j:c0 - j + w].T


def _project(x, mods, w_bf16, layer, tm, rope_tabs=None, kv_seq=None):
    R, L, _ = x.shape
    rope = rope_tabs is not None
    emit_kv = kv_seq is not None
    assert not (rope and emit_kv)
    out_shape = [jax.ShapeDtypeStruct((R, L, D_IN), F32)]
    out_specs = [pl.BlockSpec((None, tm, D_IN), lambda r, i: (r, i, 0))]
    if emit_kv:
        assert R == 1 and tm % kv_seq == 0
        out_shape += [jax.ShapeDtypeStruct((L // kv_seq, w, kv_seq), F32) for _, w in KV_COLUMNS]
        out_specs += [pl.BlockSpec((tm // kv_seq, w, kv_seq), lambda r, i: (i, 0, 0)) for _, w in KV_COLUMNS]
    in_specs = [pl.BlockSpec((None, tm, D_MODEL), lambda r, i: (r, i, 0)),
                _mod_spec(1, 2), _mod_spec(0, 2),
                pl.BlockSpec((None, D_MODEL, D_IN), lambda r, i: (layer, 0, 0))]
    args = [x, mods, mods, w_bf16]
    if rope:
        in_specs += [pl.BlockSpec((tm, LANES), lambda r, i: (i, 0))] * 3
        args += list(rope_tabs)
    res = pl.pallas_call(
        functools.partial(_proj_kernel, rope=rope, emit_kv=emit_kv),
        out_shape=out_shape, grid=(R, L // tm), in_specs=in_specs, out_specs=out_specs,
        compiler_params=_cp(2), name="in_proj",
    )(*args)
    return res if emit_kv else res[0]


def _mask_head(x, sub):
    low = _lane_low(x.shape)
    return jnp.where(low, x, 0.0) if sub == 0 else jnp.where(low, 0.0, x)


def _key_variant(k, sub):
    return _mask_head(k, sub).astype(BF16)


def _value_variant(v, sub):
    low = _lane_low(v.shape)
    first = jnp.where(low, v, 1.0) if sub == 0 else jnp.where(low, 1.0, v)
    return jnp.concatenate([first, jnp.ones_like(v)], axis=1).astype(BF16)


def _attend(lhs, keys, values, biases, sink=None):
    scores = []
    for k, b in zip(keys, biases):
        s = _dot_t(lhs, k)
        scores.append(s if b is None else s + b)
    tiles = [s[:, c:c + LANES] for s in scores for c in range(0, s.shape[1], LANES)]
    while len(tiles) > 1:
        tiles = [jnp.maximum(a, b) for a, b in zip(tiles[0::2], tiles[1::2])] + ([tiles[-1]] if len(tiles) % 2 else [])
    m = jnp.broadcast_to(tiles[0].max(axis=-1, keepdims=True), tiles[0].shape)
    if sink is not None:
        m = jnp.maximum(m, sink)
    o = None
    for s, v in zip(scores, values):
        p = [jnp.exp2(s[:, c:c + LANES] - m).astype(BF16) for c in range(0, s.shape[1], LANES)]
        pv = _dot(p[0] if len(p) == 1 else jnp.concatenate(p, axis=1), v)
        o = pv if o is None else o + pv
    return o, (None if sink is None else jnp.exp2(sink - m))


def _normalised(o):
    return o[:, :LANES] / o[:, LANES:]


def _gqa_variants(k, v, hk, wide):
    low = _lane_low(k.shape)
    k_sw, v_sw = pltpu.roll(k, HEAD_DIM, 1), pltpu.roll(v, HEAD_DIM, 1)
    k_dup = jnp.where(low, k, k_sw) if hk == 0 else jnp.where(low, k_sw, k)
    v_low, v_high = (v, v_sw) if hk == 0 else (v_sw, v)
    va = jnp.where(low, v_low, 1.0)
    if wide:
        va = jnp.concatenate([va, jnp.where(low, 1.0, v_high)], axis=1)
    return k_dup.astype(BF16), va.astype(BF16)


def _fill_gqa_variants(k_src, v_src, kv_ref, vv_ref, off, n, feature_major=False):
    step = 256
    for c in range(0, n, step):
        k = k_src[:, c:c + step].T if feature_major else k_src[c:c + step, :]
        v = v_src[:, c:c + step].T if feature_major else v_src[c:c + step, :]
        for hk in range(KVH_A):
            kd, va = _gqa_variants(k, v, hk, wide=False)
            kv_ref[hk, off + c:off + c + step, :] = kd
            vv_ref[hk, off + c:off + c + step, :] = va


def _fill_mha_variants(k_src, v_src, kv_ref, vv_ref, off, n, feature_major=False):
    step = 256
    for c in range(0, n, step):
        rows = slice(off + c, off + c + step)
        for p in range(H_B // 2):
            cs = slice(LANES * p, LANES * p + LANES)
            k = k_src[cs, c:c + step].T if feature_major else k_src[c:c + step, cs]
            v = v_src[cs, c:c + step].T if feature_major else v_src[c:c + step, cs]
            for sub in range(2):
                kv_ref[p, sub, rows, :] = _key_variant(k, sub)
                vv_ref[p, sub, rows, :] = _value_variant(v, sub)


def _sink_col(sink_ref, heads, rows):
    return jnp.concatenate([jnp.full((rows, LANES), sink_ref[h] * LOG2E, F32) for h in heads], axis=0)


def _gqa_attend(sink_ref, q_ref, o_ref, hk, keys, values, biases):
    rows = q_ref.shape[0]
    g = H_A // KVH_A
    heads = list(range(g * hk, g * hk + g))
    lhs = jnp.concatenate([_mask_head(q_ref[:, LANES * (h // 2):LANES * (h // 2) + LANES], h % 2) for h in heads],
                          axis=0)
    lhs = (lhs * ATTN_SCALE).astype(BF16)
    stacked = [None if b is None else jnp.concatenate([b] * g, axis=0) for b in biases]
    o, extra = _attend(lhs, keys, values, stacked, _sink_col(sink_ref, heads, rows))
    low_all = _lane_low((g * rows, LANES))
    low = _lane_low((rows, LANES))
    first = o[:, :LANES] + jnp.where(low_all, 0.0, extra)
    if o.shape[1] == LANES:
        out = first / pltpu.roll(first, HEAD_DIM, 1)
    else:
        second = o[:, LANES:] + jnp.where(low_all, extra, 0.0)
    for j in range(0, g, 2):
        h = heads[j]
        even, odd = slice(j * rows, (j + 1) * rows), slice((j + 1) * rows, (j + 2) * rows)
        if o.shape[1] == LANES:
            pair = jnp.where(low, out[even], pltpu.roll(out[odd], HEAD_DIM, 1))
        else:
            pair = jnp.where(low, first[even] / second[even], second[odd] / first[odd])
        o_ref[:, LANES * (h // 2):LANES * (h // 2) + LANES] = pair.astype(o_ref.dtype)


def _attn_a_ctx_kernel(sink_ref, q_ref, k_ref, v_ref, o_ref):
    for g in range(q_ref.shape[0]):
        for hk in range(KVH_A):
            kd, va = _gqa_variants(k_ref[g], v_ref[g], hk, wide=True)
            _gqa_attend(sink_ref, q_ref.at[g], o_ref.at[g], hk, [kd], [va], [None])


def _window_mask_table():
    nq = WIN_QBLOCKS
    row = np.arange(nq * BLOCK)[:, None]
    col = np.arange((nq + 2) * BLOCK)[None, :]
    tabs = [np.where(np.abs(col - off * BLOCK - row) <= WINDOW, 0.0, NEG) for off in range(3)]
    return jnp.asarray(np.stack(tabs), F32)


def _attn_a_lat_kernel(sink_ref, q_ref, k_ref, v_ref, kc_ref, vc_ref, mask_ref, o_ref, kv_ref, vv_ref, *, nb):
    i = pl.program_id(1)
    L, P = k_ref.shape[0], kc_ref.shape[1]

    @pl.when(i == 0)
    def _():
        _fill_gqa_variants(k_ref, v_ref, kv_ref, vv_ref, 0, L)
        _fill_gqa_variants(kc_ref, vc_ref, kv_ref, vv_ref, L, P, feature_major=True)

    nq = WIN_QBLOCKS
    wk = (nq + 2) * BLOCK
    sblk = jnp.clip(nq * i - 1, 0, nb - (nq + 2))
    start = pl.multiple_of(sblk * BLOCK, BLOCK)
    bias = mask_ref[nq * i - sblk]
    for hk in range(KVH_A):
        ks = [kv_ref[hk, pl.ds(start, wk), :], kv_ref[hk, L:L + P, :]]
        vs = [vv_ref[hk, pl.ds(start, wk), :], vv_ref[hk, L:L + P, :]]
        _gqa_attend(sink_ref, q_ref, o_ref, hk, ks, vs, [bias, None])


def _smem_spec():
    return pl.BlockSpec(memory_space=pltpu.SMEM)


def _attn_a_ctx(qkv, sink):
    R, L, _ = qkv.shape
    wq, wkv = H_A * HEAD_DIM, KVH_A * HEAD_DIM
    G = CTX_GROUP_A
    assert R % G == 0
    return pl.pallas_call(
        _attn_a_ctx_kernel,
        out_shape=jax.ShapeDtypeStruct((R, L, wq), F32),
        grid=(R // G,),
        in_specs=[_smem_spec(),
                  pl.BlockSpec((G, L, wq), lambda r: (r, 0, COL_QA)),
                  pl.BlockSpec((G, L, wkv), lambda r: (r, 0, COL_KA)),
                  pl.BlockSpec((G, L, wkv), lambda r: (r, 0, COL_VA))],
        out_specs=pl.BlockSpec((G, L, wq), lambda r: (r, 0, 0)),
        compiler_params=_cp(1), name="attn_a_ctx",
    )(sink, qkv, qkv, qkv)


def _attn_a_lat(qkv, sink, cache_k, cache_v, layer):
    R, L, _ = qkv.shape
    nb = L // BLOCK
    nq = WIN_QBLOCKS
    assert nb % nq == 0 and nb >= nq + 2
    P = cache_k.shape[3]
    wq, wkv = H_A * HEAD_DIM, KVH_A * HEAD_DIM
    cache_spec = pl.BlockSpec((None, None, wkv, P), lambda r, i: (r, layer, 0, 0))
    masks = _window_mask_table()
    return pl.pallas_call(
        functools.partial(_attn_a_lat_kernel, nb=nb),
        out_shape=jax.ShapeDtypeStruct((R, L, wq), F32),
        grid=(R, nb // nq),
        in_specs=[_smem_spec(),
                  pl.BlockSpec((None, nq * BLOCK, wq), lambda r, i: (r, i, COL_QA)),
                  pl.BlockSpec((None, L, wkv), lambda r, i: (r, 0, COL_KA)),
                  pl.BlockSpec((None, L, wkv), lambda r, i: (r, 0, COL_VA)),
                  cache_spec, cache_spec,
                  pl.BlockSpec(masks.shape, lambda r, i: (0, 0, 0))],
        out_specs=pl.BlockSpec((None, nq * BLOCK, wq), lambda r, i: (r, i, 0)),
        scratch_shapes=[pltpu.VMEM((KVH_A, L + P, LANES), BF16),
                        pltpu.VMEM((KVH_A, L + P, LANES), BF16)],
        compiler_params=_cp(2), name="attn_a_lat",
    )(sink, qkv, qkv, qkv, cache_k, cache_v, masks)


def _attn_b_ctx_kernel(q_ref, k_ref, v_ref, o_ref):
    G, L = q_ref.shape[:2]
    low = _lane_low((L, LANES))
    for g in range(G):
        for p in range(H_B // 2):
            cs = slice(LANES * p, LANES * p + LANES)
            lhs = (q_ref[g, :, cs] * ATTN_SCALE).astype(BF16)
            k, v = k_ref[g, :, cs], v_ref[g, :, cs]
            outs = [_normalised(_attend(lhs, [_key_variant(k, sub)], [_value_variant(v, sub)], [None])[0])
                    for sub in range(2)]
            o_ref[g, :, cs] = jnp.where(low, outs[0], outs[1]).astype(o_ref.dtype)


def _bias_table_kernel(rpb_ref, o_ref, t_ref):
    n_dr, n_dc = 2 * NA_ROWS - 1, 2 * NA_COLS - 1
    n_t = o_ref.shape[1]
    shape = (GRID_W, LANES)
    cq = lax.broadcasted_iota(I32, shape, 0)
    lane = lax.broadcasted_iota(I32, shape, 1)
    right = lane >= GRID_W
    ck = lane & (GRID_W - 1)
    cs = jnp.clip(cq - NA_COLS // 2, 0, GRID_W - NA_COLS)
    col_ok = (ck >= cs) & (ck < cs + NA_COLS)
    dc = ck - cq + (NA_COLS - 1)
    for j in (0, 1, n_t - 1, n_t):
        t_ref[j] = jnp.zeros(shape, F32)

    def build(r, carry, h):
        acc = jnp.zeros(shape, F32)
        for d in range(n_dc):
            acc = jnp.where(dc == d, rpb_ref[(h * n_dr + r) * n_dc + d] * LOG2E, acc)
        t_ref[r + 2] = jnp.where(col_ok, acc, NEG)
        return carry

    def compose(t, carry, h):
        o_ref[h, t] = jnp.where(right, t_ref[t + 1], t_ref[t])
        return carry

    for h in range(H_B):
        lax.fori_loop(0, n_dr, functools.partial(build, h=h), 0)
        lax.fori_loop(0, n_t, functools.partial(compose, h=h), 0)


def _bias_table(rpb_l):
    n_t = 2 * (NA_ROWS + 1)
    return pl.pallas_call(
        _bias_table_kernel,
        out_shape=jax.ShapeDtypeStruct((H_B, n_t, GRID_W, LANES), F32),
        in_specs=[_smem_spec()],
        out_specs=pl.BlockSpec(memory_space=pltpu.VMEM),
        scratch_shapes=[pltpu.VMEM((n_t + 1, GRID_W, LANES), F32)],
        name="na_bias_table",
    )(rpb_l.reshape(-1))


def _attn_b_lat_kernel(q_ref, k_ref, v_ref, kc_ref, vc_ref, tp_ref, o_ref, kv_ref, vv_ref, *, nb, rows_total):
    i = pl.program_id(1)
    L, P = k_ref.shape[0], kc_ref.shape[1]

    @pl.when(i == 0)
    def _():
        _fill_mha_variants(k_ref, v_ref, kv_ref, vv_ref, 0, L)
        _fill_mha_variants(kc_ref, vc_ref, kv_ref, vv_ref, L, P, feature_major=True)

    nq = NA_QBLOCKS
    nwin = NA_WIN_BLOCKS + nq - 1
    wk = nwin * BLOCK
    rows = nq * BLOCK
    sblk = jnp.clip(nq * i - 2, 0, nb - nwin)
    start = pl.multiple_of(sblk * BLOCK, BLOCK)
    shape = (rows, wk)
    rq = 2 * nq * i + (lax.broadcasted_iota(I32, shape, 0) >> 6)
    rk = 2 * sblk + (lax.broadcasted_iota(I32, shape, 1) >> 6)
    rs = jnp.clip(rq - NA_ROWS // 2, 0, rows_total - NA_ROWS)
    row_bias = jnp.where((rk >= rs) & (rk < rs + NA_ROWS), 0.0, NEG)
    n_t = tp_ref.shape[1]
    low = _lane_low((rows, LANES))
    for p in range(H_B // 2):
        cs = slice(LANES * p, LANES * p + LANES)
        lhs = (q_ref[:, cs] * ATTN_SCALE).astype(BF16)
        outs = []
        for sub in range(2):
            h = 2 * p + sub
            strips = []
            for qq in range(nq):
                tiles = []
                for kb in range(nwin):
                    t = jnp.clip(2 * (sblk + kb - (nq * i + qq)) + NA_ROWS + 1, 1, n_t - 1)
                    tiles.append(jnp.concatenate([tp_ref[h, t], tp_ref[h, t - 1]], axis=0))
                strips.append(jnp.concatenate(tiles, axis=1))
            bias = jnp.concatenate(strips, axis=0) + row_bias
            ks = [kv_ref[p, sub, pl.ds(start, wk), :], kv_ref[p, sub, L:L + P, :]]
            vs = [vv_ref[p, sub, pl.ds(start, wk), :], vv_ref[p, sub, L:L + P, :]]
            outs.append(_normalised(_attend(lhs, ks, vs, [bias, None])[0]))
        o_ref[:, cs] = jnp.where(low, outs[0], outs[1]).astype(o_ref.dtype)


def _attn_b_ctx(qkv):
    R, L, _ = qkv.shape
    w = H_B * HEAD_DIM
    G = CTX_GROUP_B
    assert R % G == 0
    return pl.pallas_call(
        _attn_b_ctx_kernel,
        out_shape=jax.ShapeDtypeStruct((R, L, w), F32),
        grid=(R // G,),
        in_specs=[pl.BlockSpec((G, L, w), lambda r: (r, 0, COL_QB)),
                  pl.BlockSpec((G, L, w), lambda r: (r, 0, COL_KB)),
                  pl.BlockSpec((G, L, w), lambda r: (r, 0, COL_VB))],
        out_specs=pl.BlockSpec((G, L, w), lambda r: (r, 0, 0)),
        compiler_params=_cp(1), name="attn_b_ctx",
    )(qkv, qkv, qkv)


def _attn_b_lat(qkv, cache_k, cache_v, bias_tab, layer):
    R, L, _ = qkv.shape
    nb = L // BLOCK
    P = cache_k.shape[3]
    w = H_B * HEAD_DIM
    cache_spec = pl.BlockSpec((None, None, w, P), lambda r, i: (r, layer, 0, 0))
    rows = NA_QBLOCKS * BLOCK
    assert nb % NA_QBLOCKS == 0 and nb >= NA_WIN_BLOCKS + NA_QBLOCKS - 1
    return pl.pallas_call(
        functools.partial(_attn_b_lat_kernel, nb=nb, rows_total=L // GRID_W),
        out_shape=jax.ShapeDtypeStruct((R, L, w), F32),
        grid=(R, nb // NA_QBLOCKS),
        in_specs=[pl.BlockSpec((None, rows, w), lambda r, i: (r, i, COL_QB)),
                  pl.BlockSpec((None, L, w), lambda r, i: (r, 0, COL_KB)),
                  pl.BlockSpec((None, L, w), lambda r, i: (r, 0, COL_VB)),
                  cache_spec, cache_spec,
                  pl.BlockSpec(bias_tab.shape, lambda r, i: (0, 0, 0, 0))],
        out_specs=pl.BlockSpec((None, rows, w), lambda r, i: (r, i, 0)),
        scratch_shapes=[pltpu.VMEM((H_B // 2, 2, L + P, LANES), BF16),
                        pltpu.VMEM((H_B // 2, 2, L + P, 2 * LANES), BF16)],
        compiler_params=_cp(2), name="attn_b_lat",
    )(qkv, qkv, qkv, cache_k, cache_v, bias_tab)


T_ZF, T_ZB, T_XF, T_XB, T_GF, T_GB, T_MA, T_MB = range(8)


def _ret_kernel(*refs, nc, has_s0, out_state):
    refs = list(refs)
    dec_ref, q_ref, k_ref, v_ref, g_ref, gn_ref = refs[:6]
    pos = 6
    s0_ref = None
    if has_s0:
        s0_ref = refs[pos]; pos += 1
    o_ref = refs[pos]; pos += 1
    st_ref = None
    if out_state:
        st_ref = refs[pos]; pos += 1
    sf_ref, sb_ref, tab_ref = refs[pos], refs[pos + 1], refs[pos + 2]

    sq = (BLOCK, LANES)
    low = _lane_low(sq)
    row_low = lax.broadcasted_iota(I32, sq, 0) < HEAD_DIM
    same_head = low == row_low
    n_pairs = H_C // 2

    @pl.when(pl.program_id(0) == 0)
    def _():
        ri = lax.broadcasted_iota(I32, sq, 0).astype(F32)
        ci = lax.broadcasted_iota(I32, sq, 1).astype(F32)

        def log_gamma(d, h):
            return -jnp.exp(jnp.full(sq, dec_ref[d, h], F32))

        for p in range(n_pairs):
            lf = [log_gamma(0, 2 * p), log_gamma(0, 2 * p + 1)]
            lb = [log_gamma(1, 2 * p), log_gamma(1, 2 * p + 1)]
            lf_lane, lb_lane = jnp.where(low, lf[0], lf[1]), jnp.where(low, lb[0], lb[1])
            lf_row, lb_row = jnp.where(row_low, lf[0], lf[1]), jnp.where(row_low, lb[0], lb[1])
            tab_ref[p, T_ZF] = jnp.exp(lf_lane * (BLOCK - 1.0 - ri))
            tab_ref[p, T_ZB] = jnp.exp(lb_lane * ri)
            tab_ref[p, T_XF] = jnp.exp(lf_lane * (ri + 1.0))
            tab_ref[p, T_XB] = jnp.exp(lb_lane * (BLOCK - ri))
            tab_ref[p, T_GF] = jnp.exp(lf_row * float(BLOCK))
            tab_ref[p, T_GB] = jnp.exp(lb_row * float(BLOCK))
            rel = ri - ci
            for sub in range(2):
                fwd = jnp.where(rel >= 0, jnp.exp(lf[sub] * jnp.maximum(rel, 0.0)), 0.0)
                bwd = jnp.where(rel <= 0, jnp.exp(lb[sub] * jnp.maximum(-rel, 0.0)), 0.0)
                tab_ref[p, T_MA + sub] = fwd + bwd

    n_req = q_ref.shape[0]
    units = [(r, p) for r in range(n_req) for p in range(n_pairs)]

    def chunk(ref, r, c, p):
        r0 = pl.multiple_of(c * BLOCK, BLOCK)
        return ref[r, pl.ds(r0, BLOCK), LANES * p:LANES * p + LANES]

    unroll = min(nc, 8)

    def inc_body(c, carry):
        for r, p in units:
            kt = (chunk(k_ref, r, c, p) * QK_SCALE).T.astype(BF16)
            v = chunk(v_ref, r, c, p)
            zv = jnp.concatenate([v * tab_ref[p, T_ZF], v * tab_ref[p, T_ZB]], axis=1).astype(BF16)
            inc = _dot(kt, zv)
            sf_ref[r, c, p] = jnp.where(same_head, inc[:, :LANES], 0.0)
            sb_ref[r, c, p] = jnp.where(same_head, inc[:, LANES:], 0.0)
        return carry

    lax.fori_loop(0, nc, inc_body, 0, unroll=unroll)

    def scan_body(t, S, ref, tg, reverse):
        c = nc - 1 - t if reverse else t
        new = []
        for (r, p), state in zip(units, S):
            inc = ref[r, c, p]
            ref[r, c, p] = state
            new.append(tab_ref[p, tg] * state + inc)
        return tuple(new)

    zero = jnp.zeros(sq, F32)
    s_f0 = tuple(s0_ref[r, 0, p] if has_s0 else zero for r, p in units)
    s_b0 = tuple(s0_ref[r, 1, p] if has_s0 else zero for r, p in units)
    scan_unroll = nc if nc <= 4 else 1
    s_f = lax.fori_loop(0, nc, functools.partial(scan_body, ref=sf_ref, tg=T_GF, reverse=False), s_f0,
                        unroll=scan_unroll)
    s_b = lax.fori_loop(0, nc, functools.partial(scan_body, ref=sb_ref, tg=T_GB, reverse=True), s_b0,
                        unroll=scan_unroll)
    if out_state:
        for (r, p), f_state, b_state in zip(units, s_f, s_b):
            st_ref[r, 0, p] = f_state
            st_ref[r, 1, p] = b_state

    def out_body(c, carry):
        for r, p in units:
            q = chunk(q_ref, r, c, p)
            k16 = (chunk(k_ref, r, c, p) * QK_SCALE).astype(BF16)
            v = chunk(v_ref, r, c, p)
            a = _dot_t(jnp.concatenate([_mask_head(q, 0), _mask_head(q, 1)], axis=0).astype(BF16), k16)
            lhs = jnp.concatenate([a[:BLOCK] * tab_ref[p, T_MA], a[BLOCK:] * tab_ref[p, T_MA + 1],
                                   q * tab_ref[p, T_XF], q * tab_ref[p, T_XB]], axis=1).astype(BF16)
            rhs = jnp.concatenate([_mask_head(v, 0), _mask_head(v, 1), sf_ref[r, c, p], sb_ref[r, c, p]],
                                  axis=0).astype(BF16)
            o = _dot(lhs, rhs)
            mu = jnp.where(low, _mask_head(o, 0).sum(-1, keepdims=True),
                           _mask_head(o, 1).sum(-1, keepdims=True)) * (1.0 / HEAD_DIM)
            d = o - mu
            d2 = d * d
            var = jnp.where(low, _mask_head(d2, 0).sum(-1, keepdims=True),
                            _mask_head(d2, 1).sum(-1, keepdims=True)) * (1.0 / HEAD_DIM)
            cs = slice(LANES * p, LANES * p + LANES)
            on = d * lax.rsqrt(var + LN_EPS) * gn_ref[:, cs]
            r0 = pl.multiple_of(c * BLOCK, BLOCK)
            o_ref[r, pl.ds(r0, BLOCK), cs] = (_silu(chunk(g_ref, r, c, p)) * on).astype(o_ref.dtype)
        return carry

    lax.fori_loop(0, nc, out_body, 0, unroll=unroll)


def _retention(qkv, decay, gn_w, s0=None, layer=0, out_state=False):
    R, L, _ = qkv.shape
    nc = L // BLOCK
    w = H_C * HEAD_DIM
    n_pairs = H_C // 2
    G = max(1, min(R, RET_GROUP_CHUNKS // nc))
    assert R % G == 0
    col = lambda cb: pl.BlockSpec((G, L, w), lambda r: (r, 0, cb))
    in_specs = [_smem_spec(), col(COL_QC), col(COL_KC), col(COL_VC), col(COL_GC),
                pl.BlockSpec((1, w), lambda r: (0, 0))]
    args = [decay, qkv, qkv, qkv, qkv, gn_w.reshape(1, w)]
    if s0 is not None:
        in_specs.append(pl.BlockSpec((G, None, 2, n_pairs, BLOCK, LANES), lambda r: (r, layer, 0, 0, 0, 0)))
        args.append(s0)
    out_shape = [jax.ShapeDtypeStruct((R, L, w), F32)]
    out_specs = [pl.BlockSpec((G, L, w), lambda r: (r, 0, 0))]
    if out_state:
        out_shape.append(jax.ShapeDtypeStruct((R, 2, n_pairs, BLOCK, LANES), F32))
        out_specs.append(pl.BlockSpec((G, 2, n_pairs, BLOCK, LANES), lambda r: (r, 0, 0, 0, 0)))
    res = pl.pallas_call(
        functools.partial(_ret_kernel, nc=nc, has_s0=s0 is not None, out_state=out_state),
        out_shape=out_shape, grid=(R // G,), in_specs=in_specs, out_specs=out_specs,
        scratch_shapes=[pltpu.VMEM((G, nc, n_pairs, BLOCK, LANES), F32),
                        pltpu.VMEM((G, nc, n_pairs, BLOCK, LANES), F32),
                        pltpu.VMEM((n_pairs, 8, BLOCK, LANES), F32)],
        compiler_params=_cp(1), name="retention",
    )(*args)
    return res if out_state else res[0]


def _pair_states_to_heads(st):
    a = st[..., :HEAD_DIM, :HEAD_DIM]
    b = st[..., HEAD_DIM:, HEAD_DIM:]
    return jnp.stack([a, b], axis=3).reshape(st.shape[0], 2, H_C, HEAD_DIM, HEAD_DIM)


def _heads_to_pair_states(s):
    s = s.reshape(*s.shape[:3], H_C // 2, 2, HEAD_DIM, HEAD_DIM)
    z = jnp.zeros_like(s[..., 0, :, :])
    top = jnp.concatenate([s[..., 0, :, :], z], axis=-1)
    bot = jnp.concatenate([z, s[..., 1, :, :]], axis=-1)
    return jnp.concatenate([top, bot], axis=-2)


def _outproj_kernel(oa_ref, ob_ref, oc_ref, x_ref, g1_ref, sc2_ref, sh2_ref, w_ref, lg_ref, lb_ref, wr_ref,
                    x1_ref, h2_ref, aff_ref):
    wa, wb = H_A * HEAD_DIM, H_B * HEAD_DIM
    tm = x_ref.shape[0]
    part = tm // OUTPROJ_PARTS
    for r0 in range(0, tm, part):
        rs = slice(r0, r0 + part)
        mix = (_dot(oa_ref[rs, :].astype(BF16), w_ref[0:wa, :])
               + _dot(ob_ref[rs, :].astype(BF16), w_ref[wa:wa + wb, :])
               + _dot(oc_ref[rs, :].astype(BF16), w_ref[wa + wb:, :]))
        x1 = _layernorm(ALPHA * x_ref[rs, :] + g1_ref[...] * mix, lg_ref[...], lb_ref[...])
        x1_ref[rs, :] = x1
        h2 = (x1 * (1.0 + sc2_ref[...]) + sh2_ref[...]).astype(BF16)
        h2_ref[rs, :] = h2
        logits = _dot(h2, wr_ref[...])
        e = jnp.exp(logits - logits.max(axis=-1, keepdims=True))
        aff_ref[rs, :] = e / e.sum(axis=-1, keepdims=True)


def _outproj(oa, ob, oc, x, mods, w_out_bf16, ln_g, ln_b, w_router_bf16, layer, tm):
    R, L, _ = x.shape
    row = lambda w: pl.BlockSpec((None, tm, w), lambda r, i: (r, i, 0))
    vec = pl.BlockSpec((1, D_MODEL), lambda r, i: (0, 0))
    return pl.pallas_call(
        _outproj_kernel,
        out_shape=[jax.ShapeDtypeStruct((R, L, D_MODEL), F32),
                   jax.ShapeDtypeStruct((R, L, D_MODEL), BF16),
                   jax.ShapeDtypeStruct((R, L, N_EXPERTS), F32)],
        grid=(R, L // tm),
        in_specs=[row(H_A * HEAD_DIM), row(H_B * HEAD_DIM), row(H_C * HEAD_DIM), row(D_MODEL),
                  _mod_spec(2, 2), _mod_spec(4, 2), _mod_spec(3, 2),
                  pl.BlockSpec((None, D_MODEL, D_MODEL), lambda r, i: (layer, 0, 0)), vec, vec,
                  pl.BlockSpec((None, D_MODEL, N_EXPERTS), lambda r, i: (layer, 0, 0))],
        out_specs=[row(D_MODEL), row(D_MODEL), row(N_EXPERTS)],
        compiler_params=_cp(2), name="out_proj_ln_router",
    )(oa, ob, oc, x, mods, mods, mods, w_out_bf16, ln_g.reshape(1, -1), ln_b.reshape(1, -1), w_router_bf16)


CUMSUM_CHUNK = 256


def _excl_cumsum_lanes(x):
    rows, n = x.shape
    w = min(CUMSUM_CHUNK, n)
    tri = (lax.broadcasted_iota(I32, (w, w), 0) < lax.broadcasted_iota(I32, (w, w), 1))
    tri = jnp.where(tri, 1.0, 0.0).astype(BF16)
    carry = jnp.zeros((rows, 1), F32)
    outs = []
    for c in range(0, n, w):
        xc = x[:, c:c + w]
        outs.append(_dot(xc.astype(BF16), tri) + carry)
        carry = carry + xc.sum(axis=-1, keepdims=True)
    return outs[0] if len(outs) == 1 else jnp.concatenate(outs, axis=1)


def _select_kernel(a_ref, pos_ref, *, cap):
    rb, ne, n = a_ref.shape
    rows = rb * ne
    a = a_ref[...].reshape(rows, n)
    capf = float(cap)

    def body(_, c):
        lo, hi = c
        mid = lo + ((hi - lo + 1) >> 1)
        cnt = jnp.where(a >= pltpu.bitcast(mid, F32), 1.0, 0.0).sum(axis=-1, keepdims=True)
        ok = cnt >= capf
        return jnp.where(ok, mid, lo), jnp.where(ok, hi, mid - 1)

    one_bits = 0x3F800000
    lo, _ = lax.fori_loop(0, 30, body, (jnp.zeros((rows, 1), I32), jnp.full((rows, 1), one_bits, I32)))
    thr = pltpu.bitcast(lo, F32)
    gt = jnp.where(a > thr, 1.0, 0.0)
    eq = jnp.where(a == thr, 1.0, 0.0)
    need = capf - gt.sum(axis=-1, keepdims=True)
    sel = gt + eq * jnp.where(_excl_cumsum_lanes(eq) < need, 1.0, 0.0)
    slot = _excl_cumsum_lanes(sel)
    pos_ref[...] = jnp.where(sel > 0.5, slot, -1.0).astype(I32).reshape(rb, ne, n)


def _select(aff_t, cap):
    R, ne, n = aff_t.shape
    rb = min(R, max(1, 128 * 2048 // (ne * n)))
    return pl.pallas_call(
        functools.partial(_select_kernel, cap=cap),
        out_shape=jax.ShapeDtypeStruct((R, ne, n), I32),
        grid=(R // rb,),
        in_specs=[pl.BlockSpec((rb, ne, n), lambda r: (r, 0, 0))],
        out_specs=pl.BlockSpec((rb, ne, n), lambda r: (r, 0, 0)),
        compiler_params=_cp(1), name="expert_select",
    )(aff_t)


def _gather_kernel(pos_ref, aff_ref, h_ref, x_ref, gate_ref, *, cap):
    eb, _, n = pos_ref.shape
    slot = lax.broadcasted_iota(I32, (cap, n), 0)
    parts = []
    for j in range(eb):
        hit = slot == pos_ref[j]
        parts.append(jnp.where(hit, 1.0, 0.0).astype(BF16))
        gate_ref[j] = jnp.where(hit, aff_ref[j], 0.0).sum(axis=-1, keepdims=True)
    onehot = parts[0] if eb == 1 else jnp.concatenate(parts, axis=0)
    xe = _dot(onehot, h_ref[...]).astype(BF16)
    for j in range(eb):
        x_ref[j] = xe[j * cap:(j + 1) * cap]


def _gather(pos, aff_t, h2, cap):
    R, ne, n = pos.shape
    eb = max(1, min(ne, 2 * ROW_TILE // cap))
    row = pl.BlockSpec((None, eb, 1, n), lambda r, e: (r, e, 0, 0))
    return pl.pallas_call(
        functools.partial(_gather_kernel, cap=cap),
        out_shape=[jax.ShapeDtypeStruct((ne, R * cap, D_MODEL), BF16),
                   jax.ShapeDtypeStruct((ne, R * cap, 1), F32)],
        grid=(R, ne // eb),
        in_specs=[row, row, pl.BlockSpec((None, n, D_MODEL), lambda r, e: (r, 0, 0))],
        out_specs=[pl.BlockSpec((eb, cap, D_MODEL), lambda r, e: (e, r, 0)),
                   pl.BlockSpec((eb, cap, 1), lambda r, e: (e, r, 0))],
        compiler_params=_cp(2), name="expert_gather",
    )(pos.reshape(R, ne, 1, n), aff_t.reshape(R, ne, 1, n), h2)


FF_TILE = 512


def _ffn_kernel(x0_ref, x1_ref, wa_ref, wb_ref, wd_ref, g0_ref, g1_ref, y0_ref, y1_ref, acc_ref):
    f = pl.program_id(1)

    @pl.when(f == 0)
    def _():
        acc_ref[...] = jnp.zeros_like(acc_ref)

    x = jnp.concatenate([x0_ref[...], x1_ref[...]], axis=0)
    a = _dot(x, wa_ref[...].astype(BF16))
    b = _dot(x, wb_ref[...].astype(BF16))
    acc_ref[...] += _dot((_silu(a) * b).astype(BF16), wd_ref[...].astype(BF16))

    @pl.when(f == pl.num_programs(1) - 1)
    def _():
        s0 = x0_ref.shape[0]
        y0_ref[...] = (acc_ref[:s0, :] * g0_ref[...]).astype(BF16)
        y1_ref[...] = (acc_ref[s0:, :] * g1_ref[...]).astype(BF16)


def _experts(xe0, gates0, xe1, gates1, w_gate_up, w_down, layer):
    ne, S0, _ = xe0.shape
    S1 = xe1.shape[1]
    nf = D_FF // FF_TILE
    rows = lambda S, w: pl.BlockSpec((None, S, w), lambda e, f: (e, 0, 0))
    return pl.pallas_call(
        _ffn_kernel,
        out_shape=[jax.ShapeDtypeStruct((ne, S0, D_MODEL), BF16), jax.ShapeDtypeStruct((ne, S1, D_MODEL), BF16)],
        grid=(ne, nf),
        in_specs=[rows(S0, D_MODEL), rows(S1, D_MODEL),
                  pl.BlockSpec((None, None, D_MODEL, FF_TILE), lambda e, f: (layer, e, 0, f)),
                  pl.BlockSpec((None, None, D_MODEL, FF_TILE), lambda e, f: (layer, e, 0, nf + f)),
                  pl.BlockSpec((None, None, FF_TILE, D_MODEL), lambda e, f: (layer, e, f, 0)),
                  rows(S0, 1), rows(S1, 1)],
        out_specs=[rows(S0, D_MODEL), rows(S1, D_MODEL)],
        scratch_shapes=[pltpu.VMEM((S0 + S1, D_MODEL), F32)],
        compiler_params=_cp(2), name="expert_ffn",
    )(xe0, xe1, w_gate_up, w_gate_up, w_down, gates0, gates1)


def _scatter_kernel(pos_ref, y_ref, x1_ref, g2_ref, lg_ref, lb_ref, o_ref, *, cap):
    tm, ne = pos_ref.shape
    cw = max(cap, LANES)
    epc = cw // cap
    lane = lax.broadcasted_iota(I32, (tm, cw), 1)
    pos = pos_ref[...]
    def onehot_chunk(c):
        onehot = jnp.zeros((tm, cw), F32)
        for jj in range(epc):
            col = pos[:, c * epc + jj:c * epc + jj + 1]
            tgt = jnp.where(col >= 0, col + jj * cap, -1)
            onehot = onehot + jnp.where(lane == tgt, 1.0, 0.0)
        return onehot.astype(BF16)

    n_chunks = ne // epc
    if ne * cap <= ROW_TILE:
        acc = _dot(jnp.concatenate([onehot_chunk(c) for c in range(n_chunks)], axis=1),
                   y_ref[...].reshape(ne * cap, D_MODEL))
    else:
        acc = jnp.zeros((tm, D_MODEL), F32)
        for c in range(n_chunks):
            yc = y_ref[c] if epc == 1 else y_ref[c * epc:(c + 1) * epc].reshape(cw, D_MODEL)
            acc = acc + _dot(onehot_chunk(c), yc)
    o_ref[...] = _layernorm(ALPHA * x1_ref[...] + g2_ref[...] * acc, lg_ref[...], lb_ref[...])


def _scatter(pos_tok, y, x1, mods, ln_g, ln_b, cap, tm):
    R, n, ne = pos_tok.shape
    vec = pl.BlockSpec((1, D_MODEL), lambda r, i: (0, 0))
    return pl.pallas_call(
        functools.partial(_scatter_kernel, cap=cap),
        out_shape=jax.ShapeDtypeStruct((R, n, D_MODEL), F32),
        grid=(R, n // tm),
        in_specs=[pl.BlockSpec((None, tm, ne), lambda r, i: (r, i, 0)),
                  pl.BlockSpec((ne, cap, D_MODEL), lambda r, i: (0, r, 0)),
                  pl.BlockSpec((None, tm, D_MODEL), lambda r, i: (r, i, 0)),
                  _mod_spec(5, 2), vec, vec],
        out_specs=pl.BlockSpec((None, tm, D_MODEL), lambda r, i: (r, i, 0)),
        compiler_params=_cp(2), name="expert_scatter_ln",
    )(pos_tok, y, x1, mods, ln_g.reshape(1, -1), ln_b.reshape(1, -1))


def _capacity(n):
    return CAPACITY_FACTOR * n // N_EXPERTS


def _route(h2, aff):
    cap = _capacity(h2.shape[1])
    aff_t = jnp.swapaxes(aff, 1, 2)
    pos = _select(aff_t, cap)
    xe, gates = _gather(pos, aff_t, h2, cap)
    return pos, xe, gates


def _channel_mixers(groups, l, w_gate_up, w_down, ln2_g, ln2_b):
    routed = [_route(g["h2"], g["aff"]) for g in groups]
    ys = _experts(routed[0][1], routed[0][2], routed[1][1], routed[1][2], w_gate_up, w_down, l)
    return [_scatter(jnp.swapaxes(r[0], 1, 2), y, g["x1"], g["mods"], ln2_g[l], ln2_b[l],
                     _capacity(g["h2"].shape[1]), g["tm"]) for g, r, y in zip(groups, routed, ys)]


def kernel(x_prompt, x_sample, cache_attn_a_k, cache_attn_a_v, cache_attn_b_k, cache_attn_b_v, state_ret, c, c_ctx,
           w_ada, b_ada, w_in, w_out, attn_sink, na_rpb, ret_decay, ret_gn, ln1_g, ln1_b, ln2_g, ln2_b,
           w_router, w_gate_up, w_down):
    n_ctx, L_ctx, _ = x_prompt.shape
    n_lat, L_lat, _ = x_sample.shape
    P = cache_attn_a_k.shape[2]

    cond = jnp.concatenate([c_ctx[None, :], c, jnp.zeros((8 - 1 - n_lat, D_MODEL), F32)], axis=0)
    mod = _modulation(cond, w_ada, b_ada)
    w_in16, w_out16, w_r16 = w_in.astype(BF16), w_out.astype(BF16), w_router.astype(BF16)
    rope_tabs = _rope_tables(L_lat)
    feature_major = lambda t: jnp.transpose(t, (0, 1, 3, 4, 2)).reshape(n_lat, DEPTH, -1, P)
    cak, cav = feature_major(cache_attn_a_k), feature_major(cache_attn_a_v)
    cbk, cbv = feature_major(cache_attn_b_k), feature_major(cache_attn_b_v)
    s0_pairs = _heads_to_pair_states(state_ret)

    xc, xs = x_prompt, x_sample
    tm = ROW_TILE
    a_k, a_v, b_k, b_v, st = [], [], [], [], []
    flat = lambda a: a.reshape(1, n_ctx * L_ctx, a.shape[-1])
    unflat = lambda a: a.reshape(n_ctx, L_ctx, a.shape[-1])
    for l in range(DEPTH):
        mod_c = mod[l, 0:1][:, None, :]
        mods_c = jnp.broadcast_to(mod_c, (n_ctx, 1, 6 * D_MODEL))
        qkv, ka, va, kb, vb = _project(flat(xc), mod_c, w_in16, l, tm, kv_seq=L_ctx)
        qkv = unflat(qkv)
        oa = _attn_a_ctx(qkv, attn_sink[l])
        ob = _attn_b_ctx(qkv)
        oc, s_l = _retention(qkv, ret_decay[l], ret_gn[l], out_state=True)
        x1, h2, aff = [unflat(t) for t in _outproj(flat(oa), flat(ob), flat(oc), flat(xc), mod_c, w_out16,
                                                   ln1_g[l], ln1_b[l], w_r16, l, tm)]
        ctx = dict(x1=x1, h2=h2, aff=aff, mods=mods_c, tm=L_ctx)
        heads_last = lambda t, h: jnp.transpose(t.reshape(n_ctx, h, HEAD_DIM, L_ctx), (0, 3, 1, 2))
        a_k.append(heads_last(ka, KVH_A))
        a_v.append(heads_last(va, KVH_A))
        b_k.append(heads_last(kb, H_B))
        b_v.append(heads_last(vb, H_B))
        st.append(_pair_states_to_heads(s_l))

        mods_s = mod[l, 1:1 + n_lat][:, None, :]
        qkv = _project(xs, mods_s, w_in16, l, tm, rope_tabs)
        oa = _attn_a_lat(qkv, attn_sink[l], cak, cav, l)
        ob = _attn_b_lat(qkv, cbk, cbv, _bias_table(na_rpb[l]), l)
        oc = _retention(qkv, ret_decay[l], ret_gn[l], s0=s0_pairs, layer=l)
        x1, h2, aff = _outproj(oa, ob, oc, xs, mods_s, w_out16, ln1_g[l], ln1_b[l], w_r16, l, tm)
        lat = dict(x1=x1, h2=h2, aff=aff, mods=mods_s, tm=tm)

        xc, xs = _channel_mixers([ctx, lat], l, w_gate_up, w_down, ln2_g, ln2_b)

    return (xc, xs, jnp.stack(a_k, 1), jnp.stack(a_v, 1), jnp.stack(b_k, 1), jnp.stack(b_v, 1),
            jnp.stack(st, 1))
```

```python
import functools

import numpy as np
import jax
import jax.numpy as jnp
from jax import lax
from jax.experimental import pallas as pl
from jax.experimental.pallas import tpu as pltpu

F32 = jnp.float32
BF16 = jnp.bfloat16
I32 = jnp.int32

D_MODEL = 1024
DEPTH = 2
GRID_W = 64
HEAD_DIM = 64
H_A, KVH_A, H_B, H_C = 8, 2, 4, 4
D_IN = 2560
BLOCK = 128
WINDOW = 128
NA_ROWS, NA_COLS = 8, 16
NA_WIN_BLOCKS = 5
NA_QBLOCKS = 4
WIN_QBLOCKS = 2
CTX_GROUP_A = 4
CTX_GROUP_B = 8
ROPE_BASE = 10000.0
N_EXPERTS = 16
CAPACITY_FACTOR = 2
D_FF = 1024
LN_EPS = 1e-6
ALPHA = (2 * DEPTH) ** 0.25
NEG = -1e30
LANES = 128
QK_SCALE = HEAD_DIM ** -0.5
LOG2E = float(np.log2(np.e))
ATTN_SCALE = QK_SCALE * LOG2E
VMEM_LIMIT = 56 * 1024 * 1024
ROW_TILE = 1024
OUTPROJ_PARTS = 2
RET_GROUP_CHUNKS = 8

COL_QA, COL_KA, COL_VA = 0, 4, 5
COL_QB, COL_KB, COL_VB = 3, 4, 5
COL_QC, COL_KC, COL_VC, COL_GC = 6, 7, 8, 9


def _cp(n_axes):
    return pltpu.CompilerParams(dimension_semantics=("arbitrary",) * n_axes, vmem_limit_bytes=VMEM_LIMIT)


def _dot(a, b):
    return jnp.dot(a, b, preferred_element_type=F32)


def _dot_t(a, b):
    return lax.dot_general(a, b, (((1,), (1,)), ((), ())), preferred_element_type=F32)


def _silu(x):
    return x / (1.0 + jnp.exp(-x))


def _lane_low(shape):
    return lax.broadcasted_iota(I32, shape, len(shape) - 1) < HEAD_DIM


def _layernorm(y, g, b):
    mu = jnp.mean(y, axis=-1, keepdims=True)
    d = y - mu
    var = jnp.mean(d * d, axis=-1, keepdims=True)
    return d * lax.rsqrt(var + LN_EPS) * g + b


def _mod_kernel(c_ref, w_ref, b_ref, o_ref):
    s = _silu(c_ref[...]).astype(BF16)
    o_ref[...] = _dot(s, w_ref[...].astype(BF16)) + b_ref[...]


def _modulation(cond, w_ada, b_ada):
    tn = 1536
    return pl.pallas_call(
        _mod_kernel,
        out_shape=jax.ShapeDtypeStruct((DEPTH, 8, 6 * D_MODEL), F32),
        grid=(DEPTH, 6 * D_MODEL // tn),
        in_specs=[pl.BlockSpec((8, D_MODEL), lambda l, j: (0, 0)),
                  pl.BlockSpec((None, D_MODEL, tn), lambda l, j: (l, 0, j)),
                  pl.BlockSpec((None, 1, tn), lambda l, j: (l, 0, j))],
        out_specs=pl.BlockSpec((None, 8, tn), lambda l, j: (l, 0, j)),
        compiler_params=_cp(2), name="modulation",
    )(cond, w_ada, b_ada.reshape(DEPTH, 1, 6 * D_MODEL))


def _mod_spec(k, n_grid):
    if n_grid == 1:
        return pl.BlockSpec((None, 1, D_MODEL), lambda r: (r, 0, k))
    return pl.BlockSpec((None, 1, D_MODEL), lambda r, i: (r, 0, k))


def _rope_tables(L):
    t = np.arange(L)
    pos = (t // GRID_W, t % GRID_W)
    half = HEAD_DIM // 2
    inv = 1.0 / (ROPE_BASE ** (np.arange(0, half, 2) / half))
    cos = np.zeros((L, HEAD_DIM)); s_up = np.zeros((L, HEAD_DIM)); s_dn = np.zeros((L, HEAD_DIM))
    for part in range(2):
        ang = (pos[part][:, None] * inv[None, :]).astype(np.float32).astype(np.float64)
        q = half // 2
        base = part * half
        cos[:, base:base + q] = np.cos(ang); cos[:, base + q:base + half] = np.cos(ang)
        s_up[:, base:base + q] = -np.sin(ang)
        s_dn[:, base + q:base + half] = np.sin(ang)
    tile = lambda a: jnp.asarray(np.tile(a, (1, LANES // HEAD_DIM)), F32)
    return tile(cos), tile(s_up), tile(s_dn)


KV_COLUMNS = ((512, 128), (640, 128), (1024, 256), (1280, 256))


def _proj_kernel(*refs, rope, emit_kv):
    kv_refs = ()
    if emit_kv:
        refs, kv_refs = refs[:-len(KV_COLUMNS)], refs[-len(KV_COLUMNS):]
    if rope:
        x_ref, sc_ref, sh_ref, w_ref, cos_ref, sup_ref, sdn_ref, o_ref = refs
    else:
        x_ref, sc_ref, sh_ref, w_ref, o_ref = refs
    h = (x_ref[...] * (1.0 + sc_ref[...]) + sh_ref[...]).astype(BF16)
    n_rope = (H_A + KVH_A) * HEAD_DIM
    step = 512
    for j in range(0, D_IN, step):
        acc = _dot(h, w_ref[:, j:j + step].astype(BF16))
        if rope and j < n_rope:
            q = HEAD_DIM // 4
            for c in range(0, step, LANES):
                a = acc[:, c:c + LANES]
                if j + c < n_rope:
                    a = (a * cos_ref[...] + pltpu.roll(a, LANES - q, 1) * sup_ref[...]
                         + pltpu.roll(a, q, 1) * sdn_ref[...])
                o_ref[:, j + c:j + c + LANES] = a
        else:
            o_ref[:, j:j + step] = acc
        for ref, (c0, w) in zip(kv_refs, KV_COLUMNS):
            if j <= c0 < j + step:
                n_req, _, seq = ref.shape
                for g in range(n_req):
                    ref[g] = acc[g * seq:(g + 1) * seq, c0 - j:c0 - j + w].T


def _project(x, mods, w_bf16, layer, tm, rope_tabs=None, kv_seq=None):
    R, L, _ = x.shape
    rope = rope_tabs is not None
    emit_kv = kv_seq is not None
    assert not (rope and emit_kv)
    out_shape = [jax.ShapeDtypeStruct((R, L, D_IN), F32)]
    out_specs = [pl.BlockSpec((None, tm, D_IN), lambda r, i: (r, i, 0))]
    if emit_kv:
        assert R == 1 and tm % kv_seq == 0
        out_shape += [jax.ShapeDtypeStruct((L // kv_seq, w, kv_seq), F32) for _, w in KV_COLUMNS]
        out_specs += [pl.BlockSpec((tm // kv_seq, w, kv_seq), lambda r, i: (i, 0, 0)) for _, w in KV_COLUMNS]
    in_specs = [pl.BlockSpec((None, tm, D_MODEL), lambda r, i: (r, i, 0)),
                _mod_spec(1, 2), _mod_spec(0, 2),
                pl.BlockSpec((None, D_MODEL, D_IN), lambda r, i: (layer, 0, 0), pipeline_mode=pl.Buffered(1))]
    args = [x, mods, mods, w_bf16]
    if rope:
        in_specs += [pl.BlockSpec((tm, LANES), lambda r, i: (i, 0))] * 3
        args += list(rope_tabs)
    res = pl.pallas_call(
        functools.partial(_proj_kernel, rope=rope, emit_kv=emit_kv),
        out_shape=out_shape, grid=(R, L // tm), in_specs=in_specs, out_specs=out_specs,
        compiler_params=_cp(2), name="in_proj",
    )(*args)
    return res if emit_kv else res[0]


def _mask_head(x, sub):
    low = _lane_low(x.shape)
    return jnp.where(low, x, 0.0) if sub == 0 else jnp.where(low, 0.0, x)


def _key_variant(k, sub):
    return _mask_head(k, sub).astype(BF16)


def _value_variant(v, sub):
    low = _lane_low(v.shape)
    first = jnp.where(low, v, 1.0) if sub == 0 else jnp.where(low, 1.0, v)
    return jnp.concatenate([first, jnp.ones_like(v)], axis=1).astype(BF16)


def _attend(lhs, keys, values, biases, sink=None):
    scores = []
    for k, b in zip(keys, biases):
        s = _dot_t(lhs, k)
        scores.append(s if b is None else s + b)
    tiles = [s[:, c:c + LANES] for s in scores for c in range(0, s.shape[1], LANES)]
    while len(tiles) > 1:
        tiles = [jnp.maximum(a, b) for a, b in zip(tiles[0::2], tiles[1::2])] + ([tiles[-1]] if len(tiles) % 2 else [])
    m = jnp.broadcast_to(tiles[0].max(axis=-1, keepdims=True), tiles[0].shape)
    if sink is not None:
        m = jnp.maximum(m, sink)
    o = None
    for s, v in zip(scores, values):
        p = [jnp.exp2(s[:, c:c + LANES] - m).astype(BF16) for c in range(0, s.shape[1], LANES)]
        pv = _dot(p[0] if len(p) == 1 else jnp.concatenate(p, axis=1), v)
        o = pv if o is None else o + pv
    return o, (None if sink is None else jnp.exp2(sink - m))


def _normalised(o):
    return o[:, :LANES] / o[:, LANES:]


def _gqa_variants(k, v, hk, wide):
    low = _lane_low(k.shape)
    k_sw, v_sw = pltpu.roll(k, HEAD_DIM, 1), pltpu.roll(v, HEAD_DIM, 1)
    k_dup = jnp.where(low, k, k_sw) if hk == 0 else jnp.where(low, k_sw, k)
    v_low, v_high = (v, v_sw) if hk == 0 else (v_sw, v)
    va = jnp.where(low, v_low, 1.0)
    if wide:
        va = jnp.concatenate([va, jnp.where(low, 1.0, v_high)], axis=1)
    return k_dup.astype(BF16), va.astype(BF16)


def _fill_gqa_variants(k_src, v_src, kv_ref, vv_ref, off, n, feature_major=False):
    step = 256
    for c in range(0, n, step):
        k = k_src[:, c:c + step].T if feature_major else k_src[c:c + step, :]
        v = v_src[:, c:c + step].T if feature_major else v_src[c:c + step, :]
        for hk in range(KVH_A):
            kd, va = _gqa_variants(k, v, hk, wide=False)
            kv_ref[hk, off + c:off + c + step, :] = kd
            vv_ref[hk, off + c:off + c + step, :] = va


def _fill_mha_variants(k_src, v_src, kv_ref, vv_ref, off, n, feature_major=False):
    step = 256
    for c in range(0, n, step):
        rows = slice(off + c, off + c + step)
        for p in range(H_B // 2):
            cs = slice(LANES * p, LANES * p + LANES)
            k = k_src[cs, c:c + step].T if feature_major else k_src[c:c + step, cs]
            v = v_src[cs, c:c + step].T if feature_major else v_src[c:c + step, cs]
            for sub in range(2):
                kv_ref[p, sub, rows, :] = _key_variant(k, sub)
                vv_ref[p, sub, rows, :] = _value_variant(v, sub)


def _sink_col(sink_ref, heads, rows):
    return jnp.concatenate([jnp.full((rows, LANES), sink_ref[h] * LOG2E, F32) for h in heads], axis=0)


def _gqa_attend(sink_ref, q_ref, o_ref, hk, keys, values, biases):
    rows = q_ref.shape[0]
    g = H_A // KVH_A
    heads = list(range(g * hk, g * hk + g))
    lhs = jnp.concatenate([_mask_head(q_ref[:, LANES * (h // 2):LANES * (h // 2) + LANES], h % 2) for h in heads],
                          axis=0)
    lhs = (lhs * ATTN_SCALE).astype(BF16)
    stacked = [None if b is None else jnp.concatenate([b] * g, axis=0) for b in biases]
    o, extra = _attend(lhs, keys, values, stacked, _sink_col(sink_ref, heads, rows))
    low_all = _lane_low((g * rows, LANES))
    low = _lane_low((rows, LANES))
    first = o[:, :LANES] + jnp.where(low_all, 0.0, extra)
    if o.shape[1] == LANES:
        out = first / pltpu.roll(first, HEAD_DIM, 1)
    else:
        second = o[:, LANES:] + jnp.where(low_all, extra, 0.0)
    for j in range(0, g, 2):
        h = heads[j]
        even, odd = slice(j * rows, (j + 1) * rows), slice((j + 1) * rows, (j + 2) * rows)
        if o.shape[1] == LANES:
            pair = jnp.where(low, out[even], pltpu.roll(out[odd], HEAD_DIM, 1))
        else:
            pair = jnp.where(low, first[even] / second[even], second[odd] / first[odd])
        o_ref[:, LANES * (h // 2):LANES * (h // 2) + LANES] = pair.astype(o_ref.dtype)


def _attn_a_ctx_kernel(sink_ref, q_ref, k_ref, v_ref, o_ref):
    for g in range(q_ref.shape[0]):
        for hk in range(KVH_A):
            kd, va = _gqa_variants(k_ref[g], v_ref[g], hk, wide=True)
            _gqa_attend(sink_ref, q_ref.at[g], o_ref.at[g], hk, [kd], [va], [None])


def _window_mask_table():
    nq = WIN_QBLOCKS
    row = np.arange(nq * BLOCK)[:, None]
    col = np.arange((nq + 2) * BLOCK)[None, :]
    tabs = [np.where(np.abs(col - off * BLOCK - row) <= WINDOW, 0.0, NEG) for off in range(3)]
    return jnp.asarray(np.stack(tabs), F32)


def _attn_a_lat_kernel(sink_ref, q_ref, k_ref, v_ref, kc_ref, vc_ref, mask_ref, o_ref, kv_ref, vv_ref, *, nb):
    i = pl.program_id(1)
    L, P = k_ref.shape[0], kc_ref.shape[1]

    @pl.when(i == 0)
    def _():
        _fill_gqa_variants(k_ref, v_ref, kv_ref, vv_ref, 0, L)
        _fill_gqa_variants(kc_ref, vc_ref, kv_ref, vv_ref, L, P, feature_major=True)

    nq = WIN_QBLOCKS
    wk = (nq + 2) * BLOCK
    sblk = jnp.clip(nq * i - 1, 0, nb - (nq + 2))
    start = pl.multiple_of(sblk * BLOCK, BLOCK)
    bias = mask_ref[nq * i - sblk]
    for hk in range(KVH_A):
        ks = [kv_ref[hk, pl.ds(start, wk), :], kv_ref[hk, L:L + P, :]]
        vs = [vv_ref[hk, pl.ds(start, wk), :], vv_ref[hk, L:L + P, :]]
        _gqa_attend(sink_ref, q_ref, o_ref, hk, ks, vs, [bias, None])


def _smem_spec():
    return pl.BlockSpec(memory_space=pltpu.SMEM)


def _attn_a_ctx(qkv, sink):
    R, L, _ = qkv.shape
    wq, wkv = H_A * HEAD_DIM, KVH_A * HEAD_DIM
    G = CTX_GROUP_A
    assert R % G == 0
    return pl.pallas_call(
        _attn_a_ctx_kernel,
        out_shape=jax.ShapeDtypeStruct((R, L, wq), F32),
        grid=(R // G,),
        in_specs=[_smem_spec(),
                  pl.BlockSpec((G, L, wq), lambda r: (r, 0, COL_QA)),
                  pl.BlockSpec((G, L, wkv), lambda r: (r, 0, COL_KA)),
                  pl.BlockSpec((G, L, wkv), lambda r: (r, 0, COL_VA))],
        out_specs=pl.BlockSpec((G, L, wq), lambda r: (r, 0, 0)),
        compiler_params=_cp(1), name="attn_a_ctx",
    )(sink, qkv, qkv, qkv)


def _attn_a_lat(qkv, sink, cache_k, cache_v, layer):
    R, L, _ = qkv.shape
    nb = L // BLOCK
    nq = WIN_QBLOCKS
    assert nb % nq == 0 and nb >= nq + 2
    P = cache_k.shape[3]
    wq, wkv = H_A * HEAD_DIM, KVH_A * HEAD_DIM
    cache_spec = pl.BlockSpec((None, None, wkv, P), lambda r, i: (r, layer, 0, 0))
    masks = _window_mask_table()
    return pl.pallas_call(
        functools.partial(_attn_a_lat_kernel, nb=nb),
        out_shape=jax.ShapeDtypeStruct((R, L, wq), F32),
        grid=(R, nb // nq),
        in_specs=[_smem_spec(),
                  pl.BlockSpec((None, nq * BLOCK, wq), lambda r, i: (r, i, COL_QA)),
                  pl.BlockSpec((None, L, wkv), lambda r, i: (r, 0, COL_KA)),
                  pl.BlockSpec((None, L, wkv), lambda r, i: (r, 0, COL_VA)),
                  cache_spec, cache_spec,
                  pl.BlockSpec(masks.shape, lambda r, i: (0, 0, 0))],
        out_specs=pl.BlockSpec((None, nq * BLOCK, wq), lambda r, i: (r, i, 0)),
        scratch_shapes=[pltpu.VMEM((KVH_A, L + P, LANES), BF16),
                        pltpu.VMEM((KVH_A, L + P, LANES), BF16)],
        compiler_params=_cp(2), name="attn_a_lat",
    )(sink, qkv, qkv, qkv, cache_k, cache_v, masks)


def _attn_b_ctx_kernel(q_ref, k_ref, v_ref, o_ref):
    G, L = q_ref.shape[:2]
    low = _lane_low((L, LANES))
    for g in range(G):
        for p in range(H_B // 2):
            cs = slice(LANES * p, LANES * p + LANES)
            lhs = (q_ref[g, :, cs] * ATTN_SCALE).astype(BF16)
            k, v = k_ref[g, :, cs], v_ref[g, :, cs]
            outs = [_normalised(_attend(lhs, [_key_variant(k, sub)], [_value_variant(v, sub)], [None])[0])
                    for sub in range(2)]
            o_ref[g, :, cs] = jnp.where(low, outs[0], outs[1]).astype(o_ref.dtype)


def _bias_table_kernel(rpb_ref, o_ref, t_ref):
    n_dr, n_dc = 2 * NA_ROWS - 1, 2 * NA_COLS - 1
    n_t = o_ref.shape[1]
    shape = (GRID_W, LANES)
    cq = lax.broadcasted_iota(I32, shape, 0)
    lane = lax.broadcasted_iota(I32, shape, 1)
    right = lane >= GRID_W
    ck = lane & (GRID_W - 1)
    cs = jnp.clip(cq - NA_COLS // 2, 0, GRID_W - NA_COLS)
    col_ok = (ck >= cs) & (ck < cs + NA_COLS)
    dc = ck - cq + (NA_COLS - 1)
    for j in (0, 1, n_t - 1, n_t):
        t_ref[j] = jnp.zeros(shape, F32)

    def build(r, carry, h):
        acc = jnp.zeros(shape, F32)
        for d in range(n_dc):
            acc = jnp.where(dc == d, rpb_ref[(h * n_dr + r) * n_dc + d] * LOG2E, acc)
        t_ref[r + 2] = jnp.where(col_ok, acc, NEG)
        return carry

    def compose(t, carry, h):
        o_ref[h, t] = jnp.where(right, t_ref[t + 1], t_ref[t])
        return carry

    for h in range(H_B):
        lax.fori_loop(0, n_dr, functools.partial(build, h=h), 0)
        lax.fori_loop(0, n_t, functools.partial(compose, h=h), 0)


def _bias_table(rpb_l):
    n_t = 2 * (NA_ROWS + 1)
    return pl.pallas_call(
        _bias_table_kernel,
        out_shape=jax.ShapeDtypeStruct((H_B, n_t, GRID_W, LANES), F32),
        in_specs=[_smem_spec()],
        out_specs=pl.BlockSpec(memory_space=pltpu.VMEM),
        scratch_shapes=[pltpu.VMEM((n_t + 1, GRID_W, LANES), F32)],
        name="na_bias_table",
    )(rpb_l.reshape(-1))


def _attn_b_lat_kernel(q_ref, k_ref, v_ref, kc_ref, vc_ref, tp_ref, o_ref, kv_ref, vv_ref, *, nb, rows_total):
    i = pl.program_id(1)
    L, P = k_ref.shape[0], kc_ref.shape[1]

    @pl.when(i == 0)
    def _():
        _fill_mha_variants(k_ref, v_ref, kv_ref, vv_ref, 0, L)
        _fill_mha_variants(kc_ref, vc_ref, kv_ref, vv_ref, L, P, feature_major=True)

    nq = NA_QBLOCKS
    nwin = NA_WIN_BLOCKS + nq - 1
    wk = nwin * BLOCK
    rows = nq * BLOCK
    sblk = jnp.clip(nq * i - 2, 0, nb - nwin)
    start = pl.multiple_of(sblk * BLOCK, BLOCK)
    shape = (rows, wk)
    rq = 2 * nq * i + (lax.broadcasted_iota(I32, shape, 0) >> 6)
    rk = 2 * sblk + (lax.broadcasted_iota(I32, shape, 1) >> 6)
    rs = jnp.clip(rq - NA_ROWS // 2, 0, rows_total - NA_ROWS)
    row_bias = jnp.where((rk >= rs) & (rk < rs + NA_ROWS), 0.0, NEG)
    n_t = tp_ref.shape[1]
    low = _lane_low((rows, LANES))
    for p in range(H_B // 2):
        cs = slice(LANES * p, LANES * p + LANES)
        lhs = (q_ref[:, cs] * ATTN_SCALE).astype(BF16)
        outs = []
        for sub in range(2):
            h = 2 * p + sub
            strips = []
            for qq in range(nq):
                tiles = []
                for kb in range(nwin):
                    t = jnp.clip(2 * (sblk + kb - (nq * i + qq)) + NA_ROWS + 1, 1, n_t - 1)
                    tiles.append(jnp.concatenate([tp_ref[h, t], tp_ref[h, t - 1]], axis=0))
                strips.append(jnp.concatenate(tiles, axis=1))
            bias = jnp.concatenate(strips, axis=0) + row_bias
            ks = [kv_ref[p, sub, pl.ds(start, wk), :], kv_ref[p, sub, L:L + P, :]]
            vs = [vv_ref[p, sub, pl.ds(start, wk), :], vv_ref[p, sub, L:L + P, :]]
            outs.append(_normalised(_attend(lhs, ks, vs, [bias, None])[0]))
        o_ref[:, cs] = jnp.where(low, outs[0], outs[1]).astype(o_ref.dtype)


def _attn_b_ctx(qkv):
    R, L, _ = qkv.shape
    w = H_B * HEAD_DIM
    G = CTX_GROUP_B
    assert R % G == 0
    return pl.pallas_call(
        _attn_b_ctx_kernel,
        out_shape=jax.ShapeDtypeStruct((R, L, w), F32),
        grid=(R // G,),
        in_specs=[pl.BlockSpec((G, L, w), lambda r: (r, 0, COL_QB)),
                  pl.BlockSpec((G, L, w), lambda r: (r, 0, COL_KB)),
                  pl.BlockSpec((G, L, w), lambda r: (r, 0, COL_VB))],
        out_specs=pl.BlockSpec((G, L, w), lambda r: (r, 0, 0)),
        compiler_params=_cp(1), name="attn_b_ctx",
    )(qkv, qkv, qkv)


def _attn_b_lat(qkv, cache_k, cache_v, bias_tab, layer):
    R, L, _ = qkv.shape
    nb = L // BLOCK
    P = cache_k.shape[3]
    w = H_B * HEAD_DIM
    cache_spec = pl.BlockSpec((None, None, w, P), lambda r, i: (r, layer, 0, 0))
    rows = NA_QBLOCKS * BLOCK
    assert nb % NA_QBLOCKS == 0 and nb >= NA_WIN_BLOCKS + NA_QBLOCKS - 1
    return pl.pallas_call(
        functools.partial(_attn_b_lat_kernel, nb=nb, rows_total=L // GRID_W),
        out_shape=jax.ShapeDtypeStruct((R, L, w), F32),
        grid=(R, nb // NA_QBLOCKS),
        in_specs=[pl.BlockSpec((None, rows, w), lambda r, i: (r, i, COL_QB)),
                  pl.BlockSpec((None, L, w), lambda r, i: (r, 0, COL_KB)),
                  pl.BlockSpec((None, L, w), lambda r, i: (r, 0, COL_VB)),
                  cache_spec, cache_spec,
                  pl.BlockSpec(bias_tab.shape, lambda r, i: (0, 0, 0, 0))],
        out_specs=pl.BlockSpec((None, rows, w), lambda r, i: (r, i, 0)),
        scratch_shapes=[pltpu.VMEM((H_B // 2, 2, L + P, LANES), BF16),
                        pltpu.VMEM((H_B // 2, 2, L + P, 2 * LANES), BF16)],
        compiler_params=_cp(2), name="attn_b_lat",
    )(qkv, qkv, qkv, cache_k, cache_v, bias_tab)


T_ZF, T_ZB, T_XF, T_XB, T_GF, T_GB, T_MA, T_MB = range(8)


def _ret_kernel(*refs, nc, has_s0, out_state):
    refs = list(refs)
    dec_ref, q_ref, k_ref, v_ref, g_ref, gn_ref = refs[:6]
    pos = 6
    s0_ref = None
    if has_s0:
        s0_ref = refs[pos]; pos += 1
    o_ref = refs[pos]; pos += 1
    st_ref = None
    if out_state:
        st_ref = refs[pos]; pos += 1
    sf_ref, sb_ref, tab_ref = refs[pos], refs[pos + 1], refs[pos + 2]

    sq = (BLOCK, LANES)
    low = _lane_low(sq)
    row_low = lax.broadcasted_iota(I32, sq, 0) < HEAD_DIM
    same_head = low == row_low
    n_pairs = H_C // 2

    @pl.when(pl.program_id(0) == 0)
    def _():
        ri = lax.broadcasted_iota(I32, sq, 0).astype(F32)
        ci = lax.broadcasted_iota(I32, sq, 1).astype(F32)

        def log_gamma(d, h):
            return -jnp.exp(jnp.full(sq, dec_ref[d, h], F32))

        for p in range(n_pairs):
            lf = [log_gamma(0, 2 * p), log_gamma(0, 2 * p + 1)]
            lb = [log_gamma(1, 2 * p), log_gamma(1, 2 * p + 1)]
            lf_lane, lb_lane = jnp.where(low, lf[0], lf[1]), jnp.where(low, lb[0], lb[1])
            lf_row, lb_row = jnp.where(row_low, lf[0], lf[1]), jnp.where(row_low, lb[0], lb[1])
            tab_ref[p, T_ZF] = jnp.exp(lf_lane * (BLOCK - 1.0 - ri))
            tab_ref[p, T_ZB] = jnp.exp(lb_lane * ri)
            tab_ref[p, T_XF] = jnp.exp(lf_lane * (ri + 1.0))
            tab_ref[p, T_XB] = jnp.exp(lb_lane * (BLOCK - ri))
            tab_ref[p, T_GF] = jnp.exp(lf_row * float(BLOCK))
            tab_ref[p, T_GB] = jnp.exp(lb_row * float(BLOCK))
            rel = ri - ci
            for sub in range(2):
                fwd = jnp.where(rel >= 0, jnp.exp(lf[sub] * jnp.maximum(rel, 0.0)), 0.0)
                bwd = jnp.where(rel <= 0, jnp.exp(lb[sub] * jnp.maximum(-rel, 0.0)), 0.0)
                tab_ref[p, T_MA + sub] = fwd + bwd

    n_req = q_ref.shape[0]
    units = [(r, p) for r in range(n_req) for p in range(n_pairs)]

    def chunk(ref, r, c, p):
        r0 = pl.multiple_of(c * BLOCK, BLOCK)
        return ref[r, pl.ds(r0, BLOCK), LANES * p:LANES * p + LANES]

    unroll = min(nc, 8)

    def inc_body(c, carry):
        for r, p in units:
            kt = (chunk(k_ref, r, c, p) * QK_SCALE).T.astype(BF16)
            v = chunk(v_ref, r, c, p)
            zv = jnp.concatenate([v * tab_ref[p, T_ZF], v * tab_ref[p, T_ZB]], axis=1).astype(BF16)
            inc = _dot(kt, zv)
            sf_ref[r, c, p] = jnp.where(same_head, inc[:, :LANES], 0.0)
            sb_ref[r, c, p] = jnp.where(same_head, inc[:, LANES:], 0.0)
        return carry

    lax.fori_loop(0, nc, inc_body, 0, unroll=unroll)

    def scan_body(t, S, ref, tg, reverse):
        c = nc - 1 - t if reverse else t
        new = []
        for (r, p), state in zip(units, S):
            inc = ref[r, c, p]
            ref[r, c, p] = state
            new.append(tab_ref[p, tg] * state + inc)
        return tuple(new)

    zero = jnp.zeros(sq, F32)
    s_f0 = tuple(s0_ref[r, 0, p] if has_s0 else zero for r, p in units)
    s_b0 = tuple(s0_ref[r, 1, p] if has_s0 else zero for r, p in units)
    scan_unroll = nc if nc <= 4 else 1
    s_f = lax.fori_loop(0, nc, functools.partial(scan_body, ref=sf_ref, tg=T_GF, reverse=False), s_f0,
                        unroll=scan_unroll)
    s_b = lax.fori_loop(0, nc, functools.partial(scan_body, ref=sb_ref, tg=T_GB, reverse=True), s_b0,
                        unroll=scan_unroll)
    if out_state:
        for (r, p), f_state, b_state in zip(units, s_f, s_b):
            st_ref[r, 0, p] = f_state
            st_ref[r, 1, p] = b_state

    def out_body(c, carry):
        for r, p in units:
            q = chunk(q_ref, r, c, p)
            k16 = (chunk(k_ref, r, c, p) * QK_SCALE).astype(BF16)
            v = chunk(v_ref, r, c, p)
            a = _dot_t(jnp.concatenate([_mask_head(q, 0), _mask_head(q, 1)], axis=0).astype(BF16), k16)
            lhs = jnp.concatenate([a[:BLOCK] * tab_ref[p, T_MA], a[BLOCK:] * tab_ref[p, T_MA + 1],
                                   q * tab_ref[p, T_XF], q * tab_ref[p, T_XB]], axis=1).astype(BF16)
            rhs = jnp.concatenate([_mask_head(v, 0), _mask_head(v, 1), sf_ref[r, c, p], sb_ref[r, c, p]],
                                  axis=0).astype(BF16)
            o = _dot(lhs, rhs)
            mu = jnp.where(low, _mask_head(o, 0).sum(-1, keepdims=True),
                           _mask_head(o, 1).sum(-1, keepdims=True)) * (1.0 / HEAD_DIM)
            d = o - mu
            d2 = d * d
            var = jnp.where(low, _mask_head(d2, 0).sum(-1, keepdims=True),
                            _mask_head(d2, 1).sum(-1, keepdims=True)) * (1.0 / HEAD_DIM)
            cs = slice(LANES * p, LANES * p + LANES)
            on = d * lax.rsqrt(var + LN_EPS) * gn_ref[:, cs]
            r0 = pl.multiple_of(c * BLOCK, BLOCK)
            o_ref[r, pl.ds(r0, BLOCK), cs] = (_silu(chunk(g_ref, r, c, p)) * on).astype(o_ref.dtype)
        return carry

    lax.fori_loop(0, nc, out_body, 0, unroll=unroll)


def _retention(qkv, decay, gn_w, s0=None, layer=0, out_state=False):
    R, L, _ = qkv.shape
    nc = L // BLOCK
    w = H_C * HEAD_DIM
    n_pairs = H_C // 2
    G = max(1, min(R, RET_GROUP_CHUNKS // nc))
    assert R % G == 0
    col = lambda cb: pl.BlockSpec((G, L, w), lambda r: (r, 0, cb))
    in_specs = [_smem_spec(), col(COL_QC), col(COL_KC), col(COL_VC), col(COL_GC),
                pl.BlockSpec((1, w), lambda r: (0, 0))]
    args = [decay, qkv, qkv, qkv, qkv, gn_w.reshape(1, w)]
    if s0 is not None:
        in_specs.append(pl.BlockSpec((G, None, 2, n_pairs, BLOCK, LANES), lambda r: (r, layer, 0, 0, 0, 0)))
        args.append(s0)
    out_shape = [jax.ShapeDtypeStruct((R, L, w), F32)]
    out_specs = [pl.BlockSpec((G, L, w), lambda r: (r, 0, 0))]
    if out_state:
        out_shape.append(jax.ShapeDtypeStruct((R, 2, n_pairs, BLOCK, LANES), F32))
        out_specs.append(pl.BlockSpec((G, 2, n_pairs, BLOCK, LANES), lambda r: (r, 0, 0, 0, 0)))
    res = pl.pallas_call(
        functools.partial(_ret_kernel, nc=nc, has_s0=s0 is not None, out_state=out_state),
        out_shape=out_shape, grid=(R // G,), in_specs=in_specs, out_specs=out_specs,
        scratch_shapes=[pltpu.VMEM((G, nc, n_pairs, BLOCK, LANES), F32),
                        pltpu.VMEM((G, nc, n_pairs, BLOCK, LANES), F32),
                        pltpu.VMEM((n_pairs, 8, BLOCK, LANES), F32)],
        compiler_params=_cp(1), name="retention",
    )(*args)
    return res if out_state else res[0]


def _pair_states_to_heads(st):
    a = st[..., :HEAD_DIM, :HEAD_DIM]
    b = st[..., HEAD_DIM:, HEAD_DIM:]
    return jnp.stack([a, b], axis=3).reshape(st.shape[0], 2, H_C, HEAD_DIM, HEAD_DIM)


def _heads_to_pair_states(s):
    s = s.reshape(*s.shape[:3], H_C // 2, 2, HEAD_DIM, HEAD_DIM)
    z = jnp.zeros_like(s[..., 0, :, :])
    top = jnp.concatenate([s[..., 0, :, :], z], axis=-1)
    bot = jnp.concatenate([z, s[..., 1, :, :]], axis=-1)
    return jnp.concatenate([top, bot], axis=-2)


def _outproj_kernel(oa_ref, ob_ref, oc_ref, x_ref, g1_ref, sc2_ref, sh2_ref, w_ref, lg_ref, lb_ref, wr_ref,
                    x1_ref, h2_ref, aff_ref):
    wa, wb = H_A * HEAD_DIM, H_B * HEAD_DIM
    tm = x_ref.shape[0]
    part = tm // OUTPROJ_PARTS
    for r0 in range(0, tm, part):
        rs = slice(r0, r0 + part)
        mix = (_dot(oa_ref[rs, :].astype(BF16), w_ref[0:wa, :])
               + _dot(ob_ref[rs, :].astype(BF16), w_ref[wa:wa + wb, :])
               + _dot(oc_ref[rs, :].astype(BF16), w_ref[wa + wb:, :]))
        x1 = _layernorm(ALPHA * x_ref[rs, :] + g1_ref[...] * mix, lg_ref[...], lb_ref[...])
        x1_ref[rs, :] = x1
        h2 = (x1 * (1.0 + sc2_ref[...]) + sh2_ref[...]).astype(BF16)
        h2_ref[rs, :] = h2
        logits = _dot(h2, wr_ref[...])
        e = jnp.exp(logits - logits.max(axis=-1, keepdims=True))
        aff_ref[rs, :] = e / e.sum(axis=-1, keepdims=True)


def _outproj(oa, ob, oc, x, mods, w_out_bf16, ln_g, ln_b, w_router_bf16, layer, tm):
    R, L, _ = x.shape
    row = lambda w: pl.BlockSpec((None, tm, w), lambda r, i: (r, i, 0))
    vec = pl.BlockSpec((1, D_MODEL), lambda r, i: (0, 0))
    return pl.pallas_call(
        _outproj_kernel,
        out_shape=[jax.ShapeDtypeStruct((R, L, D_MODEL), F32),
                   jax.ShapeDtypeStruct((R, L, D_MODEL), BF16),
                   jax.ShapeDtypeStruct((R, L, N_EXPERTS), F32)],
        grid=(R, L // tm),
        in_specs=[row(H_A * HEAD_DIM), row(H_B * HEAD_DIM), row(H_C * HEAD_DIM), row(D_MODEL),
                  _mod_spec(2, 2), _mod_spec(4, 2), _mod_spec(3, 2),
                  pl.BlockSpec((None, D_MODEL, D_MODEL), lambda r, i: (layer, 0, 0)), vec, vec,
                  pl.BlockSpec((None, D_MODEL, N_EXPERTS), lambda r, i: (layer, 0, 0))],
        out_specs=[row(D_MODEL), row(D_MODEL), row(N_EXPERTS)],
        compiler_params=_cp(2), name="out_proj_ln_router",
    )(oa, ob, oc, x, mods, mods, mods, w_out_bf16, ln_g.reshape(1, -1), ln_b.reshape(1, -1), w_router_bf16)


CUMSUM_CHUNK = 256


def _excl_cumsum_lanes(x):
    rows, n = x.shape
    w = min(CUMSUM_CHUNK, n)
    tri = (lax.broadcasted_iota(I32, (w, w), 0) < lax.broadcasted_iota(I32, (w, w), 1))
    tri = jnp.where(tri, 1.0, 0.0).astype(BF16)
    carry = jnp.zeros((rows, 1), F32)
    outs = []
    for c in range(0, n, w):
        xc = x[:, c:c + w]
        outs.append(_dot(xc.astype(BF16), tri) + carry)
        carry = carry + xc.sum(axis=-1, keepdims=True)
    return outs[0] if len(outs) == 1 else jnp.concatenate(outs, axis=1)


def _select_kernel(a_ref, pos_ref, *, cap):
    rb, ne, n = a_ref.shape
    rows = rb * ne
    a = a_ref[...].reshape(rows, n)
    capf = float(cap)

    def body(_, c):
        lo, hi = c
        mid = lo + ((hi - lo + 1) >> 1)
        cnt = jnp.where(a >= pltpu.bitcast(mid, F32), 1.0, 0.0).sum(axis=-1, keepdims=True)
        ok = cnt >= capf
        return jnp.where(ok, mid, lo), jnp.where(ok, hi, mid - 1)

    one_bits = 0x3F800000
    lo, _ = lax.fori_loop(0, 30, body, (jnp.zeros((rows, 1), I32), jnp.full((rows, 1), one_bits, I32)))
    thr = pltpu.bitcast(lo, F32)
    gt = jnp.where(a > thr, 1.0, 0.0)
    eq = jnp.where(a == thr, 1.0, 0.0)
    need = capf - gt.sum(axis=-1, keepdims=True)
    sel = gt + eq * jnp.where(_excl_cumsum_lanes(eq) < need, 1.0, 0.0)
    slot = _excl_cumsum_lanes(sel)
    pos_ref[...] = jnp.where(sel > 0.5, slot, -1.0).astype(I32).reshape(rb, ne, n)


def _select(aff_t, cap):
    R, ne, n = aff_t.shape
    rb = min(R, max(1, 128 * 2048 // (ne * n)))
    return pl.pallas_call(
        functools.partial(_select_kernel, cap=cap),
        out_shape=jax.ShapeDtypeStruct((R, ne, n), I32),
        grid=(R // rb,),
        in_specs=[pl.BlockSpec((rb, ne, n), lambda r: (r, 0, 0))],
        out_specs=pl.BlockSpec((rb, ne, n), lambda r: (r, 0, 0)),
        compiler_params=_cp(1), name="expert_select",
    )(aff_t)


def _gather_kernel(pos_ref, aff_ref, h_ref, x_ref, gate_ref, *, cap):
    eb, _, n = pos_ref.shape
    slot = lax.broadcasted_iota(I32, (cap, n), 0)
    parts = []
    for j in range(eb):
        hit = slot == pos_ref[j]
        parts.append(jnp.where(hit, 1.0, 0.0).astype(BF16))
        gate_ref[j] = jnp.where(hit, aff_ref[j], 0.0).sum(axis=-1, keepdims=True)
    onehot = parts[0] if eb == 1 else jnp.concatenate(parts, axis=0)
    xe = _dot(onehot, h_ref[...]).astype(BF16)
    for j in range(eb):
        x_ref[j] = xe[j * cap:(j + 1) * cap]


def _gather(pos, aff_t, h2, cap):
    R, ne, n = pos.shape
    eb = max(1, min(ne, 2 * ROW_TILE // cap))
    row = pl.BlockSpec((None, eb, 1, n), lambda r, e: (r, e, 0, 0))
    return pl.pallas_call(
        functools.partial(_gather_kernel, cap=cap),
        out_shape=[jax.ShapeDtypeStruct((ne, R * cap, D_MODEL), BF16),
                   jax.ShapeDtypeStruct((ne, R * cap, 1), F32)],
        grid=(R, ne // eb),
        in_specs=[row, row, pl.BlockSpec((None, n, D_MODEL), lambda r, e: (r, 0, 0))],
        out_specs=[pl.BlockSpec((eb, cap, D_MODEL), lambda r, e: (e, r, 0)),
                   pl.BlockSpec((eb, cap, 1), lambda r, e: (e, r, 0))],
        compiler_params=_cp(2), name="expert_gather",
    )(pos.reshape(R, ne, 1, n), aff_t.reshape(R, ne, 1, n), h2)


FF_TILE = 512


def _ffn_kernel(x0_ref, x1_ref, wa_ref, wb_ref, wd_ref, g0_ref, g1_ref, y0_ref, y1_ref, acc_ref):
    f = pl.program_id(1)

    @pl.when(f == 0)
    def _():
        acc_ref[...] = jnp.zeros_like(acc_ref)

    x = jnp.concatenate([x0_ref[...], x1_ref[...]], axis=0)
    a = _dot(x, wa_ref[...].astype(BF16))
    b = _dot(x, wb_ref[...].astype(BF16))
    acc_ref[...] += _dot((_silu(a) * b).astype(BF16), wd_ref[...].astype(BF16))

    @pl.when(f == pl.num_programs(1) - 1)
    def _():
        s0 = x0_ref.shape[0]
        y0_ref[...] = (acc_ref[:s0, :] * g0_ref[...]).astype(BF16)
        y1_ref[...] = (acc_ref[s0:, :] * g1_ref[...]).astype(BF16)


def _experts(xe0, gates0, xe1, gates1, w_gate_up, w_down, layer):
    ne, S0, _ = xe0.shape
    S1 = xe1.shape[1]
    nf = D_FF // FF_TILE
    rows = lambda S, w: pl.BlockSpec((None, S, w), lambda e, f: (e, 0, 0))
    return pl.pallas_call(
        _ffn_kernel,
        out_shape=[jax.ShapeDtypeStruct((ne, S0, D_MODEL), BF16), jax.ShapeDtypeStruct((ne, S1, D_MODEL), BF16)],
        grid=(ne, nf),
        in_specs=[rows(S0, D_MODEL), rows(S1, D_MODEL),
                  pl.BlockSpec((None, None, D_MODEL, FF_TILE), lambda e, f: (layer, e, 0, f)),
                  pl.BlockSpec((None, None, D_MODEL, FF_TILE), lambda e, f: (layer, e, 0, nf + f)),
                  pl.BlockSpec((None, None, FF_TILE, D_MODEL), lambda e, f: (layer, e, f, 0)),
                  rows(S0, 1), rows(S1, 1)],
        out_specs=[rows(S0, D_MODEL), rows(S1, D_MODEL)],
        scratch_shapes=[pltpu.VMEM((S0 + S1, D_MODEL), F32)],
        compiler_params=_cp(2), name="expert_ffn",
    )(xe0, xe1, w_gate_up, w_gate_up, w_down, gates0, gates1)


def _scatter_kernel(pos_ref, y_ref, x1_ref, g2_ref, lg_ref, lb_ref, o_ref, *, cap):
    tm, ne = pos_ref.shape
    cw = max(cap, LANES)
    epc = cw // cap
    lane = lax.broadcasted_iota(I32, (tm, cw), 1)
    pos = pos_ref[...]
    def onehot_chunk(c):
        onehot = jnp.zeros((tm, cw), F32)
        for jj in range(epc):
            col = pos[:, c * epc + jj:c * epc + jj + 1]
            tgt = jnp.where(col >= 0, col + jj * cap, -1)
            onehot = onehot + jnp.where(lane == tgt, 1.0, 0.0)
        return onehot.astype(BF16)

    n_chunks = ne // epc
    if ne * cap <= ROW_TILE:
        acc = _dot(jnp.concatenate([onehot_chunk(c) for c in range(n_chunks)], axis=1),
                   y_ref[...].reshape(ne * cap, D_MODEL))
    else:
        acc = jnp.zeros((tm, D_MODEL), F32)
        for c in range(n_chunks):
            yc = y_ref[c] if epc == 1 else y_ref[c * epc:(c + 1) * epc].reshape(cw, D_MODEL)
            acc = acc + _dot(onehot_chunk(c), yc)
    o_ref[...] = _layernorm(ALPHA * x1_ref[...] + g2_ref[...] * acc, lg_ref[...], lb_ref[...])


def _scatter(pos_tok, y, x1, mods, ln_g, ln_b, cap, tm):
    R, n, ne = pos_tok.shape
    vec = pl.BlockSpec((1, D_MODEL), lambda r, i: (0, 0))
    return pl.pallas_call(
        functools.partial(_scatter_kernel, cap=cap),
        out_shape=jax.ShapeDtypeStruct((R, n, D_MODEL), F32),
        grid=(R, n // tm),
        in_specs=[pl.BlockSpec((None, tm, ne), lambda r, i: (r, i, 0)),
                  pl.BlockSpec((ne, cap, D_MODEL), lambda r, i: (0, r, 0)),
                  pl.BlockSpec((None, tm, D_MODEL), lambda r, i: (r, i, 0)),
                  _mod_spec(5, 2), vec, vec],
        out_specs=pl.BlockSpec((None, tm, D_MODEL), lambda r, i: (r, i, 0)),
        compiler_params=_cp(2), name="expert_scatter_ln",
    )(pos_tok, y, x1, mods, ln_g.reshape(1, -1), ln_b.reshape(1, -1))


def _capacity(n):
    return CAPACITY_FACTOR * n // N_EXPERTS


def _route(h2, aff):
    cap = _capacity(h2.shape[1])
    aff_t = jnp.swapaxes(aff, 1, 2)
    pos = _select(aff_t, cap)
    xe, gates = _gather(pos, aff_t, h2, cap)
    return pos, xe, gates


def _channel_mixers(groups, l, w_gate_up, w_down, ln2_g, ln2_b):
    routed = [_route(g["h2"], g["aff"]) for g in groups]
    ys = _experts(routed[0][1], routed[0][2], routed[1][1], routed[1][2], w_gate_up, w_down, l)
    return [_scatter(jnp.swapaxes(r[0], 1, 2), y, g["x1"], g["mods"], ln2_g[l], ln2_b[l],
                     _capacity(g["h2"].shape[1]), g["tm"]) for g, r, y in zip(groups, routed, ys)]


def kernel(x_prompt, x_sample, cache_attn_a_k, cache_attn_a_v, cache_attn_b_k, cache_attn_b_v, state_ret, c, c_ctx,
           w_ada, b_ada, w_in, w_out, attn_sink, na_rpb, ret_decay, ret_gn, ln1_g, ln1_b, ln2_g, ln2_b,
           w_router, w_gate_up, w_down):
    n_ctx, L_ctx, _ = x_prompt.shape
    n_lat, L_lat, _ = x_sample.shape
    P = cache_attn_a_k.shape[2]

    cond = jnp.concatenate([c_ctx[None, :], c, jnp.zeros((8 - 1 - n_lat, D_MODEL), F32)], axis=0)
    mod = _modulation(cond, w_ada, b_ada)
    w_out16, w_r16 = w_out.astype(BF16), w_router.astype(BF16)
    rope_tabs = _rope_tables(L_lat)
    feature_major = lambda t: jnp.transpose(t, (0, 1, 3, 4, 2)).reshape(n_lat, DEPTH, -1, P)
    cak, cav = feature_major(cache_attn_a_k), feature_major(cache_attn_a_v)
    cbk, cbv = feature_major(cache_attn_b_k), feature_major(cache_attn_b_v)
    s0_pairs = _heads_to_pair_states(state_ret)

    xc, xs = x_prompt, x_sample
    tm = ROW_TILE
    a_k, a_v, b_k, b_v, st = [], [], [], [], []
    flat = lambda a: a.reshape(1, n_ctx * L_ctx, a.shape[-1])
    unflat = lambda a: a.reshape(n_ctx, L_ctx, a.shape[-1])
    for l in range(DEPTH):
        mod_c = mod[l, 0:1][:, None, :]
        mods_c = jnp.broadcast_to(mod_c, (n_ctx, 1, 6 * D_MODEL))
        qkv, ka, va, kb, vb = _project(flat(xc), mod_c, w_in, l, tm, kv_seq=L_ctx)
        qkv = unflat(qkv)
        oa = _attn_a_ctx(qkv, attn_sink[l])
        ob = _attn_b_ctx(qkv)
        oc, s_l = _retention(qkv, ret_decay[l], ret_gn[l], out_state=True)
        x1, h2, aff = [unflat(t) for t in _outproj(flat(oa), flat(ob), flat(oc), flat(xc), mod_c, w_out16,
                                                   ln1_g[l], ln1_b[l], w_r16, l, tm)]
        ctx = dict(x1=x1, h2=h2, aff=aff, mods=mods_c, tm=L_ctx)
        heads_last = lambda t, h: jnp.transpose(t.reshape(n_ctx, h, HEAD_DIM, L_ctx), (0, 3, 1, 2))
        a_k.append(heads_last(ka, KVH_A))
        a_v.append(heads_last(va, KVH_A))
        b_k.append(heads_last(kb, H_B))
        b_v.append(heads_last(vb, H_B))
        st.append(_pair_states_to_heads(s_l))

        mods_s = mod[l, 1:1 + n_lat][:, None, :]
        qkv = _project(xs, mods_s, w_in, l, tm, rope_tabs)
        oa = _attn_a_lat(qkv, attn_sink[l], cak, cav, l)
        ob = _attn_b_lat(qkv, cbk, cbv, _bias_table(na_rpb[l]), l)
        oc = _retention(qkv, ret_decay[l], ret_gn[l], s0=s0_pairs, layer=l)
        x1, h2, aff = _outproj(oa, ob, oc, xs, mods_s, w_out16, ln1_g[l], ln1_b[l], w_r16, l, tm)
        lat = dict(x1=x1, h2=h2, aff=aff, mods=mods_s, tm=tm)

        xc, xs = _channel_mixers([ctx, lat], l, w_gate_up, w_down, ln2_g, ln2_b)

    return (xc, xs, jnp.stack(a_k, 1), jnp.stack(a_v, 1), jnp.stack(b_k, 1), jnp.stack(b_v, 1),
            jnp.stack(st, 1))
```

```python
import functools

import numpy as np
import jax
import jax.numpy as jnp
from jax import lax
from jax.experimental import pallas as pl
from jax.experimental.pallas import tpu as pltpu

F32 = jnp.float32
BF16 = jnp.bfloat16
I32 = jnp.int32

D_MODEL = 1024
DEPTH = 2
GRID_W = 64
HEAD_DIM = 64
H_A, KVH_A, H_B, H_C = 8, 2, 4, 4
D_IN = 2560
BLOCK = 128
WINDOW = 128
NA_ROWS, NA_COLS = 8, 16
NA_WIN_BLOCKS = 5
NA_QBLOCKS = 4
WIN_QBLOCKS = 2
CTX_GROUP_A = 4
CTX_GROUP_B = 8
ROPE_BASE = 10000.0
N_EXPERTS = 16
CAPACITY_FACTOR = 2
D_FF = 1024
LN_EPS = 1e-6
ALPHA = (2 * DEPTH) ** 0.25
NEG = -1e30
LANES = 128
QK_SCALE = HEAD_DIM ** -0.5
LOG2E = float(np.log2(np.e))
ATTN_SCALE = QK_SCALE * LOG2E
VMEM_LIMIT = 56 * 1024 * 1024
ROW_TILE = 1024
OUTPROJ_PARTS = 2
RET_GROUP_CHUNKS = 8
SCATTER_CONTRACTION = 4096

COL_QA, COL_KA, COL_VA = 0, 4, 5
COL_QB, COL_KB, COL_VB = 3, 4, 5
COL_QC, COL_KC, COL_VC, COL_GC = 6, 7, 8, 9


def _cp(n_axes):
    return pltpu.CompilerParams(dimension_semantics=("arbitrary",) * n_axes, vmem_limit_bytes=VMEM_LIMIT)


def _dot(a, b):
    return jnp.dot(a, b, preferred_element_type=F32)


def _dot_t(a, b):
    return lax.dot_general(a, b, (((1,), (1,)), ((), ())), preferred_element_type=F32)


def _silu(x):
    return x / (1.0 + jnp.exp(-x))


def _lane_low(shape):
    return lax.broadcasted_iota(I32, shape, len(shape) - 1) < HEAD_DIM


def _layernorm(y, g, b):
    mu = jnp.mean(y, axis=-1, keepdims=True)
    d = y - mu
    var = jnp.mean(d * d, axis=-1, keepdims=True)
    return d * lax.rsqrt(var + LN_EPS) * g + b


def _mod_kernel(c_ref, w_ref, b_ref, o_ref):
    s = _silu(c_ref[...]).astype(BF16)
    o_ref[...] = _dot(s, w_ref[...].astype(BF16)) + b_ref[...]


def _modulation(cond, w_ada, b_ada):
    tn = 1536
    return pl.pallas_call(
        _mod_kernel,
        out_shape=jax.ShapeDtypeStruct((DEPTH, 8, 6 * D_MODEL), F32),
        grid=(DEPTH, 6 * D_MODEL // tn),
        in_specs=[pl.BlockSpec((8, D_MODEL), lambda l, j: (0, 0)),
                  pl.BlockSpec((None, D_MODEL, tn), lambda l, j: (l, 0, j)),
                  pl.BlockSpec((None, 1, tn), lambda l, j: (l, 0, j))],
        out_specs=pl.BlockSpec((None, 8, tn), lambda l, j: (l, 0, j)),
        compiler_params=_cp(2), name="modulation",
    )(cond, w_ada, b_ada.reshape(DEPTH, 1, 6 * D_MODEL))


def _mod_spec(k, n_grid):
    if n_grid == 1:
        return pl.BlockSpec((None, 1, D_MODEL), lambda r: (r, 0, k))
    return pl.BlockSpec((None, 1, D_MODEL), lambda r, i: (r, 0, k))


def _rope_tables(L):
    t = np.arange(L)
    pos = (t // GRID_W, t % GRID_W)
    half = HEAD_DIM // 2
    inv = 1.0 / (ROPE_BASE ** (np.arange(0, half, 2) / half))
    cos = np.zeros((L, HEAD_DIM)); s_up = np.zeros((L, HEAD_DIM)); s_dn = np.zeros((L, HEAD_DIM))
    for part in range(2):
        ang = (pos[part][:, None] * inv[None, :]).astype(np.float32).astype(np.float64)
        q = half // 2
        base = part * half
        cos[:, base:base + q] = np.cos(ang); cos[:, base + q:base + half] = np.cos(ang)
        s_up[:, base:base + q] = -np.sin(ang)
        s_dn[:, base + q:base + half] = np.sin(ang)
    tile = lambda a: jnp.asarray(np.tile(a, (1, LANES // HEAD_DIM)), F32)
    return tile(cos), tile(s_up), tile(s_dn)


KV_COLUMNS = ((512, 128), (640, 128), (1024, 256), (1280, 256))


def _proj_kernel(*refs, rope, emit_kv):
    kv_refs = ()
    if emit_kv:
        refs, kv_refs = refs[:-len(KV_COLUMNS)], refs[-len(KV_COLUMNS):]
    if rope:
        x_ref, sc_ref, sh_ref, w_ref, cos_ref, sup_ref, sdn_ref, o_ref = refs
    else:
        x_ref, sc_ref, sh_ref, w_ref, o_ref = refs
    h = (x_ref[...] * (1.0 + sc_ref[...]) + sh_ref[...]).astype(BF16)
    n_rope = (H_A + KVH_A) * HEAD_DIM
    step = 512
    for j in range(0, D_IN, step):
        acc = _dot(h, w_ref[:, j:j + step].astype(BF16))
        if rope and j < n_rope:
            q = HEAD_DIM // 4
            for c in range(0, step, LANES):
                a = acc[:, c:c + LANES]
                if j + c < n_rope:
                    a = (a * cos_ref[...] + pltpu.roll(a, LANES - q, 1) * sup_ref[...]
                         + pltpu.roll(a, q, 1) * sdn_ref[...])
                o_ref[:, j + c:j + c + LANES] = a
        else:
            o_ref[:, j:j + step] = acc
        for ref, (c0, w) in zip(kv_refs, KV_COLUMNS):
            if j <= c0 < j + step:
                n_req, _, seq = ref.shape
                for g in range(n_req):
                    ref[g] = acc[g * seq:(g + 1) * seq, c0 - j:c0 - j + w].T


def _project(x, mods, w_bf16, layer, tm, rope_tabs=None, kv_seq=None):
    R, L, _ = x.shape
    rope = rope_tabs is not None
    emit_kv = kv_seq is not None
    assert not (rope and emit_kv)
    out_shape = [jax.ShapeDtypeStruct((R, L, D_IN), F32)]
    out_specs = [pl.BlockSpec((None, tm, D_IN), lambda r, i: (r, i, 0))]
    if emit_kv:
        assert R == 1 and tm % kv_seq == 0
        out_shape += [jax.ShapeDtypeStruct((L // kv_seq, w, kv_seq), F32) for _, w in KV_COLUMNS]
        out_specs += [pl.BlockSpec((tm // kv_seq, w, kv_seq), lambda r, i: (i, 0, 0)) for _, w in KV_COLUMNS]
    in_specs = [pl.BlockSpec((None, tm, D_MODEL), lambda r, i: (r, i, 0)),
                _mod_spec(1, 2), _mod_spec(0, 2),
                pl.BlockSpec((None, D_MODEL, D_IN), lambda r, i: (layer, 0, 0), pipeline_mode=pl.Buffered(1))]
    args = [x, mods, mods, w_bf16]
    if rope:
        in_specs += [pl.BlockSpec((tm, LANES), lambda r, i: (i, 0))] * 3
        args += list(rope_tabs)
    res = pl.pallas_call(
        functools.partial(_proj_kernel, rope=rope, emit_kv=emit_kv),
        out_shape=out_shape, grid=(R, L // tm), in_specs=in_specs, out_specs=out_specs,
        compiler_params=_cp(2), name="in_proj",
    )(*args)
    return res if emit_kv else res[0]


def _mask_head(x, sub):
    low = _lane_low(x.shape)
    return jnp.where(low, x, 0.0) if sub == 0 else jnp.where(low, 0.0, x)


def _key_variant(k, sub):
    return _mask_head(k, sub).astype(BF16)


def _value_variant(v, sub):
    low = _lane_low(v.shape)
    first = jnp.where(low, v, 1.0) if sub == 0 else jnp.where(low, 1.0, v)
    return jnp.concatenate([first, jnp.ones_like(v)], axis=1).astype(BF16)


def _attend(lhs, keys, values, biases, sink=None):
    scores = []
    for k, b in zip(keys, biases):
        s = _dot_t(lhs, k)
        scores.append(s if b is None else s + b)
    tiles = [s[:, c:c + LANES] for s in scores for c in range(0, s.shape[1], LANES)]
    while len(tiles) > 1:
        tiles = [jnp.maximum(a, b) for a, b in zip(tiles[0::2], tiles[1::2])] + ([tiles[-1]] if len(tiles) % 2 else [])
    m = jnp.broadcast_to(tiles[0].max(axis=-1, keepdims=True), tiles[0].shape)
    if sink is not None:
        m = jnp.maximum(m, sink)
    o = None
    for s, v in zip(scores, values):
        p = [jnp.exp2(s[:, c:c + LANES] - m).astype(BF16) for c in range(0, s.shape[1], LANES)]
        pv = _dot(p[0] if len(p) == 1 else jnp.concatenate(p, axis=1), v)
        o = pv if o is None else o + pv
    return o, (None if sink is None else jnp.exp2(sink - m))


def _normalised(o):
    return o[:, :LANES] / o[:, LANES:]


def _gqa_variants(k, v, hk, wide):
    low = _lane_low(k.shape)
    k_sw, v_sw = pltpu.roll(k, HEAD_DIM, 1), pltpu.roll(v, HEAD_DIM, 1)
    k_dup = jnp.where(low, k, k_sw) if hk == 0 else jnp.where(low, k_sw, k)
    v_low, v_high = (v, v_sw) if hk == 0 else (v_sw, v)
    va = jnp.where(low, v_low, 1.0)
    if wide:
        va = jnp.concatenate([va, jnp.where(low, 1.0, v_high)], axis=1)
    return k_dup.astype(BF16), va.astype(BF16)


def _fill_gqa_variants(k_src, v_src, kv_ref, vv_ref, off, n, feature_major=False):
    step = 256
    for c in range(0, n, step):
        k = k_src[:, c:c + step].T if feature_major else k_src[c:c + step, :]
        v = v_src[:, c:c + step].T if feature_major else v_src[c:c + step, :]
        for hk in range(KVH_A):
            kd, va = _gqa_variants(k, v, hk, wide=False)
            kv_ref[hk, off + c:off + c + step, :] = kd
            vv_ref[hk, off + c:off + c + step, :] = va


def _fill_mha_variants(k_src, v_src, kv_ref, vv_ref, off, n, feature_major=False):
    step = 256
    for c in range(0, n, step):
        rows = slice(off + c, off + c + step)
        for p in range(H_B // 2):
            cs = slice(LANES * p, LANES * p + LANES)
            k = k_src[cs, c:c + step].T if feature_major else k_src[c:c + step, cs]
            v = v_src[cs, c:c + step].T if feature_major else v_src[c:c + step, cs]
            for sub in range(2):
                kv_ref[p, sub, rows, :] = _key_variant(k, sub)
                vv_ref[p, sub, rows, :] = _value_variant(v, sub)


def _sink_col(sink_ref, heads, rows):
    return jnp.concatenate([jnp.full((rows, LANES), sink_ref[h] * LOG2E, F32) for h in heads], axis=0)


def _gqa_attend(sink_ref, q_ref, o_ref, hk, keys, values, biases):
    rows = q_ref.shape[0]
    g = H_A // KVH_A
    heads = list(range(g * hk, g * hk + g))
    lhs = jnp.concatenate([_mask_head(q_ref[:, LANES * (h // 2):LANES * (h // 2) + LANES], h % 2) for h in heads],
                          axis=0)
    lhs = (lhs * ATTN_SCALE).astype(BF16)
    stacked = [None if b is None else jnp.concatenate([b] * g, axis=0) for b in biases]
    o, extra = _attend(lhs, keys, values, stacked, _sink_col(sink_ref, heads, rows))
    low_all = _lane_low((g * rows, LANES))
    low = _lane_low((rows, LANES))
    first = o[:, :LANES] + jnp.where(low_all, 0.0, extra)
    if o.shape[1] == LANES:
        out = first / pltpu.roll(first, HEAD_DIM, 1)
    else:
        second = o[:, LANES:] + jnp.where(low_all, extra, 0.0)
    for j in range(0, g, 2):
        h = heads[j]
        even, odd = slice(j * rows, (j + 1) * rows), slice((j + 1) * rows, (j + 2) * rows)
        if o.shape[1] == LANES:
            pair = jnp.where(low, out[even], pltpu.roll(out[odd], HEAD_DIM, 1))
        else:
            pair = jnp.where(low, first[even] / second[even], second[odd] / first[odd])
        o_ref[:, LANES * (h // 2):LANES * (h // 2) + LANES] = pair.astype(o_ref.dtype)


def _attn_a_ctx_kernel(sink_ref, q_ref, k_ref, v_ref, o_ref):
    for g in range(q_ref.shape[0]):
        for hk in range(KVH_A):
            kd, va = _gqa_variants(k_ref[g], v_ref[g], hk, wide=True)
            _gqa_attend(sink_ref, q_ref.at[g], o_ref.at[g], hk, [kd], [va], [None])


def _window_mask_table():
    nq = WIN_QBLOCKS
    row = np.arange(nq * BLOCK)[:, None]
    col = np.arange((nq + 2) * BLOCK)[None, :]
    tabs = [np.where(np.abs(col - off * BLOCK - row) <= WINDOW, 0.0, NEG) for off in range(3)]
    return jnp.asarray(np.stack(tabs), F32)


def _attn_a_lat_kernel(sink_ref, q_ref, k_ref, v_ref, kc_ref, vc_ref, mask_ref, o_ref, kv_ref, vv_ref, *, nb):
    i = pl.program_id(1)
    L, P = k_ref.shape[0], kc_ref.shape[1]

    @pl.when(i == 0)
    def _():
        _fill_gqa_variants(k_ref, v_ref, kv_ref, vv_ref, 0, L)
        _fill_gqa_variants(kc_ref, vc_ref, kv_ref, vv_ref, L, P, feature_major=True)

    nq = WIN_QBLOCKS
    wk = (nq + 2) * BLOCK
    sblk = jnp.clip(nq * i - 1, 0, nb - (nq + 2))
    start = pl.multiple_of(sblk * BLOCK, BLOCK)
    bias = mask_ref[nq * i - sblk]
    for hk in range(KVH_A):
        ks = [kv_ref[hk, pl.ds(start, wk), :], kv_ref[hk, L:L + P, :]]
        vs = [vv_ref[hk, pl.ds(start, wk), :], vv_ref[hk, L:L + P, :]]
        _gqa_attend(sink_ref, q_ref, o_ref, hk, ks, vs, [bias, None])


def _smem_spec():
    return pl.BlockSpec(memory_space=pltpu.SMEM)


def _attn_a_ctx(qkv, sink):
    R, L, _ = qkv.shape
    wq, wkv = H_A * HEAD_DIM, KVH_A * HEAD_DIM
    G = CTX_GROUP_A
    assert R % G == 0
    return pl.pallas_call(
        _attn_a_ctx_kernel,
        out_shape=jax.ShapeDtypeStruct((R, L, wq), F32),
        grid=(R // G,),
        in_specs=[_smem_spec(),
                  pl.BlockSpec((G, L, wq), lambda r: (r, 0, COL_QA)),
                  pl.BlockSpec((G, L, wkv), lambda r: (r, 0, COL_KA)),
                  pl.BlockSpec((G, L, wkv), lambda r: (r, 0, COL_VA))],
        out_specs=pl.BlockSpec((G, L, wq), lambda r: (r, 0, 0)),
        compiler_params=_cp(1), name="attn_a_ctx",
    )(sink, qkv, qkv, qkv)


def _attn_a_lat(qkv, sink, cache_k, cache_v, layer):
    R, L, _ = qkv.shape
    nb = L // BLOCK
    nq = WIN_QBLOCKS
    assert nb % nq == 0 and nb >= nq + 2
    P = cache_k.shape[3]
    wq, wkv = H_A * HEAD_DIM, KVH_A * HEAD_DIM
    cache_spec = pl.BlockSpec((None, None, wkv, P), lambda r, i: (r, layer, 0, 0))
    masks = _window_mask_table()
    return pl.pallas_call(
        functools.partial(_attn_a_lat_kernel, nb=nb),
        out_shape=jax.ShapeDtypeStruct((R, L, wq), F32),
        grid=(R, nb // nq),
        in_specs=[_smem_spec(),
                  pl.BlockSpec((None, nq * BLOCK, wq), lambda r, i: (r, i, COL_QA)),
                  pl.BlockSpec((None, L, wkv), lambda r, i: (r, 0, COL_KA)),
                  pl.BlockSpec((None, L, wkv), lambda r, i: (r, 0, COL_VA)),
                  cache_spec, cache_spec,
                  pl.BlockSpec(masks.shape, lambda r, i: (0, 0, 0))],
        out_specs=pl.BlockSpec((None, nq * BLOCK, wq), lambda r, i: (r, i, 0)),
        scratch_shapes=[pltpu.VMEM((KVH_A, L + P, LANES), BF16),
                        pltpu.VMEM((KVH_A, L + P, LANES), BF16)],
        compiler_params=_cp(2), name="attn_a_lat",
    )(sink, qkv, qkv, qkv, cache_k, cache_v, masks)


def _attn_b_ctx_kernel(q_ref, k_ref, v_ref, o_ref):
    G, L = q_ref.shape[:2]
    low = _lane_low((L, LANES))
    for g in range(G):
        for p in range(H_B // 2):
            cs = slice(LANES * p, LANES * p + LANES)
            lhs = (q_ref[g, :, cs] * ATTN_SCALE).astype(BF16)
            k, v = k_ref[g, :, cs], v_ref[g, :, cs]
            outs = [_normalised(_attend(lhs, [_key_variant(k, sub)], [_value_variant(v, sub)], [None])[0])
                    for sub in range(2)]
            o_ref[g, :, cs] = jnp.where(low, outs[0], outs[1]).astype(o_ref.dtype)


def _bias_table_kernel(rpb_ref, o_ref, t_ref):
    n_dr, n_dc = 2 * NA_ROWS - 1, 2 * NA_COLS - 1
    n_t = o_ref.shape[1]
    shape = (GRID_W, LANES)
    cq = lax.broadcasted_iota(I32, shape, 0)
    lane = lax.broadcasted_iota(I32, shape, 1)
    right = lane >= GRID_W
    ck = lane & (GRID_W - 1)
    cs = jnp.clip(cq - NA_COLS // 2, 0, GRID_W - NA_COLS)
    col_ok = (ck >= cs) & (ck < cs + NA_COLS)
    dc = ck - cq + (NA_COLS - 1)
    for j in (0, 1, n_t - 1, n_t):
        t_ref[j] = jnp.zeros(shape, F32)

    def build(r, carry, h):
        acc = jnp.zeros(shape, F32)
        for d in range(n_dc):
            acc = jnp.where(dc == d, rpb_ref[(h * n_dr + r) * n_dc + d] * LOG2E, acc)
        t_ref[r + 2] = jnp.where(col_ok, acc, NEG)
        return carry

    def compose(t, carry, h):
        o_ref[h, t] = jnp.where(right, t_ref[t + 1], t_ref[t])
        return carry

    for h in range(H_B):
        lax.fori_loop(0, n_dr, functools.partial(build, h=h), 0)
        lax.fori_loop(0, n_t, functools.partial(compose, h=h), 0)


def _bias_table(rpb_l):
    n_t = 2 * (NA_ROWS + 1)
    return pl.pallas_call(
        _bias_table_kernel,
        out_shape=jax.ShapeDtypeStruct((H_B, n_t, GRID_W, LANES), F32),
        in_specs=[_smem_spec()],
        out_specs=pl.BlockSpec(memory_space=pltpu.VMEM),
        scratch_shapes=[pltpu.VMEM((n_t + 1, GRID_W, LANES), F32)],
        name="na_bias_table",
    )(rpb_l.reshape(-1))


def _attn_b_lat_kernel(q_ref, k_ref, v_ref, kc_ref, vc_ref, tp_ref, o_ref, kv_ref, vv_ref, *, nb, rows_total):
    i = pl.program_id(1)
    L, P = k_ref.shape[0], kc_ref.shape[1]

    @pl.when(i == 0)
    def _():
        _fill_mha_variants(k_ref, v_ref, kv_ref, vv_ref, 0, L)
        _fill_mha_variants(kc_ref, vc_ref, kv_ref, vv_ref, L, P, feature_major=True)

    nq = NA_QBLOCKS
    nwin = NA_WIN_BLOCKS + nq - 1
    wk = nwin * BLOCK
    rows = nq * BLOCK
    sblk = jnp.clip(nq * i - 2, 0, nb - nwin)
    start = pl.multiple_of(sblk * BLOCK, BLOCK)
    shape = (rows, wk)
    rq = 2 * nq * i + (lax.broadcasted_iota(I32, shape, 0) >> 6)
    rk = 2 * sblk + (lax.broadcasted_iota(I32, shape, 1) >> 6)
    rs = jnp.clip(rq - NA_ROWS // 2, 0, rows_total - NA_ROWS)
    row_bias = jnp.where((rk >= rs) & (rk < rs + NA_ROWS), 0.0, NEG)
    n_t = tp_ref.shape[1]
    low = _lane_low((rows, LANES))
    for p in range(H_B // 2):
        cs = slice(LANES * p, LANES * p + LANES)
        lhs = (q_ref[:, cs] * ATTN_SCALE).astype(BF16)
        outs = []
        for sub in range(2):
            h = 2 * p + sub
            strips = []
            for qq in range(nq):
                tiles = []
                for kb in range(nwin):
                    t = jnp.clip(2 * (sblk + kb - (nq * i + qq)) + NA_ROWS + 1, 1, n_t - 1)
                    tiles.append(jnp.concatenate([tp_ref[h, t], tp_ref[h, t - 1]], axis=0))
                strips.append(jnp.concatenate(tiles, axis=1))
            bias = jnp.concatenate(strips, axis=0) + row_bias
            ks = [kv_ref[p, sub, pl.ds(start, wk), :], kv_ref[p, sub, L:L + P, :]]
            vs = [vv_ref[p, sub, pl.ds(start, wk), :], vv_ref[p, sub, L:L + P, :]]
            outs.append(_normalised(_attend(lhs, ks, vs, [bias, None])[0]))
        o_ref[:, cs] = jnp.where(low, outs[0], outs[1]).astype(o_ref.dtype)


def _attn_b_ctx(qkv):
    R, L, _ = qkv.shape
    w = H_B * HEAD_DIM
    G = CTX_GROUP_B
    assert R % G == 0
    return pl.pallas_call(
        _attn_b_ctx_kernel,
        out_shape=jax.ShapeDtypeStruct((R, L, w), F32),
        grid=(R // G,),
        in_specs=[pl.BlockSpec((G, L, w), lambda r: (r, 0, COL_QB)),
                  pl.BlockSpec((G, L, w), lambda r: (r, 0, COL_KB)),
                  pl.BlockSpec((G, L, w), lambda r: (r, 0, COL_VB))],
        out_specs=pl.BlockSpec((G, L, w), lambda r: (r, 0, 0)),
        compiler_params=_cp(1), name="attn_b_ctx",
    )(qkv, qkv, qkv)


def _attn_b_lat(qkv, cache_k, cache_v, bias_tab, layer):
    R, L, _ = qkv.shape
    nb = L // BLOCK
    P = cache_k.shape[3]
    w = H_B * HEAD_DIM
    cache_spec = pl.BlockSpec((None, None, w, P), lambda r, i: (r, layer, 0, 0))
    rows = NA_QBLOCKS * BLOCK
    assert nb % NA_QBLOCKS == 0 and nb >= NA_WIN_BLOCKS + NA_QBLOCKS - 1
    return pl.pallas_call(
        functools.partial(_attn_b_lat_kernel, nb=nb, rows_total=L // GRID_W),
        out_shape=jax.ShapeDtypeStruct((R, L, w), F32),
        grid=(R, nb // NA_QBLOCKS),
        in_specs=[pl.BlockSpec((None, rows, w), lambda r, i: (r, i, COL_QB)),
                  pl.BlockSpec((None, L, w), lambda r, i: (r, 0, COL_KB)),
                  pl.BlockSpec((None, L, w), lambda r, i: (r, 0, COL_VB)),
                  cache_spec, cache_spec,
                  pl.BlockSpec(bias_tab.shape, lambda r, i: (0, 0, 0, 0))],
        out_specs=pl.BlockSpec((None, rows, w), lambda r, i: (r, i, 0)),
        scratch_shapes=[pltpu.VMEM((H_B // 2, 2, L + P, LANES), BF16),
                        pltpu.VMEM((H_B // 2, 2, L + P, 2 * LANES), BF16)],
        compiler_params=_cp(2), name="attn_b_lat",
    )(qkv, qkv, qkv, cache_k, cache_v, bias_tab)


T_ZF, T_ZB, T_XF, T_XB, T_GF, T_GB, T_MA, T_MB = range(8)


def _ret_kernel(*refs, nc, has_s0, out_state):
    refs = list(refs)
    dec_ref, q_ref, k_ref, v_ref, g_ref, gn_ref = refs[:6]
    pos = 6
    s0_ref = None
    if has_s0:
        s0_ref = refs[pos]; pos += 1
    o_ref = refs[pos]; pos += 1
    st_ref = None
    if out_state:
        st_ref = refs[pos]; pos += 1
    sf_ref, sb_ref, tab_ref = refs[pos], refs[pos + 1], refs[pos + 2]

    sq = (BLOCK, LANES)
    low = _lane_low(sq)
    row_low = lax.broadcasted_iota(I32, sq, 0) < HEAD_DIM
    same_head = low == row_low
    n_pairs = H_C // 2

    @pl.when(pl.program_id(0) == 0)
    def _():
        ri = lax.broadcasted_iota(I32, sq, 0).astype(F32)
        ci = lax.broadcasted_iota(I32, sq, 1).astype(F32)

        def log_gamma(d, h):
            return -jnp.exp(jnp.full(sq, dec_ref[d, h], F32))

        for p in range(n_pairs):
            lf = [log_gamma(0, 2 * p), log_gamma(0, 2 * p + 1)]
            lb = [log_gamma(1, 2 * p), log_gamma(1, 2 * p + 1)]
            lf_lane, lb_lane = jnp.where(low, lf[0], lf[1]), jnp.where(low, lb[0], lb[1])
            lf_row, lb_row = jnp.where(row_low, lf[0], lf[1]), jnp.where(row_low, lb[0], lb[1])
            tab_ref[p, T_ZF] = jnp.exp(lf_lane * (BLOCK - 1.0 - ri))
            tab_ref[p, T_ZB] = jnp.exp(lb_lane * ri)
            tab_ref[p, T_XF] = jnp.exp(lf_lane * (ri + 1.0))
            tab_ref[p, T_XB] = jnp.exp(lb_lane * (BLOCK - ri))
            tab_ref[p, T_GF] = jnp.exp(lf_row * float(BLOCK))
            tab_ref[p, T_GB] = jnp.exp(lb_row * float(BLOCK))
            rel = ri - ci
            for sub in range(2):
                fwd = jnp.where(rel >= 0, jnp.exp(lf[sub] * jnp.maximum(rel, 0.0)), 0.0)
                bwd = jnp.where(rel <= 0, jnp.exp(lb[sub] * jnp.maximum(-rel, 0.0)), 0.0)
                tab_ref[p, T_MA + sub] = fwd + bwd

    n_req = q_ref.shape[0]
    units = [(r, p) for r in range(n_req) for p in range(n_pairs)]

    def chunk(ref, r, c, p):
        r0 = pl.multiple_of(c * BLOCK, BLOCK)
        return ref[r, pl.ds(r0, BLOCK), LANES * p:LANES * p + LANES]

    unroll = min(nc, 8)

    def inc_body(c, carry):
        for r, p in units:
            kt = (chunk(k_ref, r, c, p) * QK_SCALE).T.astype(BF16)
            v = chunk(v_ref, r, c, p)
            zv = jnp.concatenate([v * tab_ref[p, T_ZF], v * tab_ref[p, T_ZB]], axis=1).astype(BF16)
            inc = _dot(kt, zv)
            sf_ref[r, c, p] = jnp.where(same_head, inc[:, :LANES], 0.0)
            sb_ref[r, c, p] = jnp.where(same_head, inc[:, LANES:], 0.0)
        return carry

    lax.fori_loop(0, nc, inc_body, 0, unroll=unroll)

    def scan_body(t, S, ref, tg, reverse):
        c = nc - 1 - t if reverse else t
        new = []
        for (r, p), state in zip(units, S):
            inc = ref[r, c, p]
            ref[r, c, p] = state
            new.append(tab_ref[p, tg] * state + inc)
        return tuple(new)

    zero = jnp.zeros(sq, F32)
    s_f0 = tuple(s0_ref[r, 0, p] if has_s0 else zero for r, p in units)
    s_b0 = tuple(s0_ref[r, 1, p] if has_s0 else zero for r, p in units)
    scan_unroll = nc if nc <= 4 else 1
    s_f = lax.fori_loop(0, nc, functools.partial(scan_body, ref=sf_ref, tg=T_GF, reverse=False), s_f0,
                        unroll=scan_unroll)
    s_b = lax.fori_loop(0, nc, functools.partial(scan_body, ref=sb_ref, tg=T_GB, reverse=True), s_b0,
                        unroll=scan_unroll)
    if out_state:
        for (r, p), f_state, b_state in zip(units, s_f, s_b):
            st_ref[r, 0, p] = f_state
            st_ref[r, 1, p] = b_state

    def out_body(c, carry):
        for r, p in units:
            q = chunk(q_ref, r, c, p)
            k16 = (chunk(k_ref, r, c, p) * QK_SCALE).astype(BF16)
            v = chunk(v_ref, r, c, p)
            a = _dot_t(jnp.concatenate([_mask_head(q, 0), _mask_head(q, 1)], axis=0).astype(BF16), k16)
            lhs = jnp.concatenate([a[:BLOCK] * tab_ref[p, T_MA], a[BLOCK:] * tab_ref[p, T_MA + 1],
                                   q * tab_ref[p, T_XF], q * tab_ref[p, T_XB]], axis=1).astype(BF16)
            rhs = jnp.concatenate([_mask_head(v, 0), _mask_head(v, 1), sf_ref[r, c, p], sb_ref[r, c, p]],
                                  axis=0).astype(BF16)
            o = _dot(lhs, rhs)
            mu = jnp.where(low, _mask_head(o, 0).sum(-1, keepdims=True),
                           _mask_head(o, 1).sum(-1, keepdims=True)) * (1.0 / HEAD_DIM)
            d = o - mu
            d2 = d * d
            var = jnp.where(low, _mask_head(d2, 0).sum(-1, keepdims=True),
                            _mask_head(d2, 1).sum(-1, keepdims=True)) * (1.0 / HEAD_DIM)
            cs = slice(LANES * p, LANES * p + LANES)
            on = d * lax.rsqrt(var + LN_EPS) * gn_ref[:, cs]
            r0 = pl.multiple_of(c * BLOCK, BLOCK)
            o_ref[r, pl.ds(r0, BLOCK), cs] = (_silu(chunk(g_ref, r, c, p)) * on).astype(o_ref.dtype)
        return carry

    lax.fori_loop(0, nc, out_body, 0, unroll=unroll)


def _retention(qkv, decay, gn_w, s0=None, layer=0, out_state=False):
    R, L, _ = qkv.shape
    nc = L // BLOCK
    w = H_C * HEAD_DIM
    n_pairs = H_C // 2
    G = max(1, min(R, RET_GROUP_CHUNKS // nc))
    assert R % G == 0
    col = lambda cb: pl.BlockSpec((G, L, w), lambda r: (r, 0, cb))
    in_specs = [_smem_spec(), col(COL_QC), col(COL_KC), col(COL_VC), col(COL_GC),
                pl.BlockSpec((1, w), lambda r: (0, 0))]
    args = [decay, qkv, qkv, qkv, qkv, gn_w.reshape(1, w)]
    if s0 is not None:
        in_specs.append(pl.BlockSpec((G, None, 2, n_pairs, BLOCK, LANES), lambda r: (r, layer, 0, 0, 0, 0)))
        args.append(s0)
    out_shape = [jax.ShapeDtypeStruct((R, L, w), F32)]
    out_specs = [pl.BlockSpec((G, L, w), lambda r: (r, 0, 0))]
    if out_state:
        out_shape.append(jax.ShapeDtypeStruct((R, 2, n_pairs, BLOCK, LANES), F32))
        out_specs.append(pl.BlockSpec((G, 2, n_pairs, BLOCK, LANES), lambda r: (r, 0, 0, 0, 0)))
    res = pl.pallas_call(
        functools.partial(_ret_kernel, nc=nc, has_s0=s0 is not None, out_state=out_state),
        out_shape=out_shape, grid=(R // G,), in_specs=in_specs, out_specs=out_specs,
        scratch_shapes=[pltpu.VMEM((G, nc, n_pairs, BLOCK, LANES), F32),
                        pltpu.VMEM((G, nc, n_pairs, BLOCK, LANES), F32),
                        pltpu.VMEM((n_pairs, 8, BLOCK, LANES), F32)],
        compiler_params=_cp(1), name="retention",
    )(*args)
    return res if out_state else res[0]


def _pair_states_to_heads(st):
    a = st[..., :HEAD_DIM, :HEAD_DIM]
    b = st[..., HEAD_DIM:, HEAD_DIM:]
    return jnp.stack([a, b], axis=3).reshape(st.shape[0], 2, H_C, HEAD_DIM, HEAD_DIM)


def _heads_to_pair_states(s):
    s = s.reshape(*s.shape[:3], H_C // 2, 2, HEAD_DIM, HEAD_DIM)
    z = jnp.zeros_like(s[..., 0, :, :])
    top = jnp.concatenate([s[..., 0, :, :], z], axis=-1)
    bot = jnp.concatenate([z, s[..., 1, :, :]], axis=-1)
    return jnp.concatenate([top, bot], axis=-2)


def _outproj_kernel(oa_ref, ob_ref, oc_ref, x_ref, g1_ref, sc2_ref, sh2_ref, w_ref, lg_ref, lb_ref, wr_ref,
                    x1_ref, h2_ref, aff_ref):
    wa, wb = H_A * HEAD_DIM, H_B * HEAD_DIM
    tm = x_ref.shape[0]
    part = tm // OUTPROJ_PARTS
    for r0 in range(0, tm, part):
        rs = slice(r0, r0 + part)
        mix = (_dot(oa_ref[rs, :].astype(BF16), w_ref[0:wa, :])
               + _dot(ob_ref[rs, :].astype(BF16), w_ref[wa:wa + wb, :])
               + _dot(oc_ref[rs, :].astype(BF16), w_ref[wa + wb:, :]))
        x1 = _layernorm(ALPHA * x_ref[rs, :] + g1_ref[...] * mix, lg_ref[...], lb_ref[...])
        x1_ref[rs, :] = x1
        h2 = (x1 * (1.0 + sc2_ref[...]) + sh2_ref[...]).astype(BF16)
        h2_ref[rs, :] = h2
        logits = _dot(h2, wr_ref[...])
        e = jnp.exp(logits - logits.max(axis=-1, keepdims=True))
        aff_ref[rs, :] = e / e.sum(axis=-1, keepdims=True)


def _outproj(oa, ob, oc, x, mods, w_out_bf16, ln_g, ln_b, w_router_bf16, layer, tm):
    R, L, _ = x.shape
    row = lambda w: pl.BlockSpec((None, tm, w), lambda r, i: (r, i, 0))
    vec = pl.BlockSpec((1, D_MODEL), lambda r, i: (0, 0))
    return pl.pallas_call(
        _outproj_kernel,
        out_shape=[jax.ShapeDtypeStruct((R, L, D_MODEL), F32),
                   jax.ShapeDtypeStruct((R, L, D_MODEL), BF16),
                   jax.ShapeDtypeStruct((R, L, N_EXPERTS), F32)],
        grid=(R, L // tm),
        in_specs=[row(H_A * HEAD_DIM), row(H_B * HEAD_DIM), row(H_C * HEAD_DIM), row(D_MODEL),
                  _mod_spec(2, 2), _mod_spec(4, 2), _mod_spec(3, 2),
                  pl.BlockSpec((None, D_MODEL, D_MODEL), lambda r, i: (layer, 0, 0)), vec, vec,
                  pl.BlockSpec((None, D_MODEL, N_EXPERTS), lambda r, i: (layer, 0, 0))],
        out_specs=[row(D_MODEL), row(D_MODEL), row(N_EXPERTS)],
        compiler_params=_cp(2), name="out_proj_ln_router",
    )(oa, ob, oc, x, mods, mods, mods, w_out_bf16, ln_g.reshape(1, -1), ln_b.reshape(1, -1), w_router_bf16)


CUMSUM_CHUNK = 256


def _excl_cumsum_lanes(x):
    rows, n = x.shape
    w = min(CUMSUM_CHUNK, n)
    tri = (lax.broadcasted_iota(I32, (w, w), 0) < lax.broadcasted_iota(I32, (w, w), 1))
    tri = jnp.where(tri, 1.0, 0.0).astype(BF16)
    carry = jnp.zeros((rows, 1), F32)
    outs = []
    for c in range(0, n, w):
        xc = x[:, c:c + w]
        outs.append(_dot(xc.astype(BF16), tri) + carry)
        carry = carry + xc.sum(axis=-1, keepdims=True)
    return outs[0] if len(outs) == 1 else jnp.concatenate(outs, axis=1)


def _select_kernel(a_ref, pos_ref, *, cap):
    rb, ne, n = a_ref.shape
    rows = rb * ne
    a = a_ref[...].reshape(rows, n)
    capf = float(cap)

    def body(_, c):
        lo, hi = c
        mid = lo + ((hi - lo + 1) >> 1)
        cnt = jnp.where(a >= pltpu.bitcast(mid, F32), 1.0, 0.0).sum(axis=-1, keepdims=True)
        ok = cnt >= capf
        return jnp.where(ok, mid, lo), jnp.where(ok, hi, mid - 1)

    one_bits = 0x3F800000
    lo, _ = lax.fori_loop(0, 30, body, (jnp.zeros((rows, 1), I32), jnp.full((rows, 1), one_bits, I32)))
    thr = pltpu.bitcast(lo, F32)
    gt = jnp.where(a > thr, 1.0, 0.0)
    eq = jnp.where(a == thr, 1.0, 0.0)
    need = capf - gt.sum(axis=-1, keepdims=True)
    sel = gt + eq * jnp.where(_excl_cumsum_lanes(eq) < need, 1.0, 0.0)
    slot = _excl_cumsum_lanes(sel)
    pos_ref[...] = jnp.where(sel > 0.5, slot, -1.0).astype(I32).reshape(rb, ne, n)


def _select(aff_t, cap):
    R, ne, n = aff_t.shape
    rb = min(R, max(1, 128 * 2048 // (ne * n)))
    return pl.pallas_call(
        functools.partial(_select_kernel, cap=cap),
        out_shape=jax.ShapeDtypeStruct((R, ne, n), I32),
        grid=(R // rb,),
        in_specs=[pl.BlockSpec((rb, ne, n), lambda r: (r, 0, 0))],
        out_specs=pl.BlockSpec((rb, ne, n), lambda r: (r, 0, 0)),
        compiler_params=_cp(1), name="expert_select",
    )(aff_t)


def _gather_kernel(pos_ref, aff_ref, h_ref, x_ref, gate_ref, *, cap):
    eb, _, n = pos_ref.shape
    slot = lax.broadcasted_iota(I32, (cap, n), 0)
    parts = []
    for j in range(eb):
        hit = slot == pos_ref[j]
        parts.append(jnp.where(hit, 1.0, 0.0).astype(BF16))
        gate_ref[j] = jnp.where(hit, aff_ref[j], 0.0).sum(axis=-1, keepdims=True)
    onehot = parts[0] if eb == 1 else jnp.concatenate(parts, axis=0)
    xe = _dot(onehot, h_ref[...]).astype(BF16)
    for j in range(eb):
        x_ref[j] = xe[j * cap:(j + 1) * cap]


def _gather(pos, aff_t, h2, cap):
    R, ne, n = pos.shape
    eb = max(1, min(ne, 2 * ROW_TILE // cap))
    row = pl.BlockSpec((None, eb, 1, n), lambda r, e: (r, e, 0, 0))
    return pl.pallas_call(
        functools.partial(_gather_kernel, cap=cap),
        out_shape=[jax.ShapeDtypeStruct((ne, R * cap, D_MODEL), BF16),
                   jax.ShapeDtypeStruct((ne, R * cap, 1), F32)],
        grid=(R, ne // eb),
        in_specs=[row, row, pl.BlockSpec((None, n, D_MODEL), lambda r, e: (r, 0, 0))],
        out_specs=[pl.BlockSpec((eb, cap, D_MODEL), lambda r, e: (e, r, 0)),
                   pl.BlockSpec((eb, cap, 1), lambda r, e: (e, r, 0))],
        compiler_params=_cp(2), name="expert_gather",
    )(pos.reshape(R, ne, 1, n), aff_t.reshape(R, ne, 1, n), h2)


FF_TILE = 512


def _ffn_kernel(x0_ref, x1_ref, wa_ref, wb_ref, wd_ref, g0_ref, g1_ref, y0_ref, y1_ref, acc_ref):
    f = pl.program_id(1)

    @pl.when(f == 0)
    def _():
        acc_ref[...] = jnp.zeros_like(acc_ref)

    x = jnp.concatenate([x0_ref[...], x1_ref[...]], axis=0)
    a = _dot(x, wa_ref[...].astype(BF16))
    b = _dot(x, wb_ref[...].astype(BF16))
    acc_ref[...] += _dot((_silu(a) * b).astype(BF16), wd_ref[...].astype(BF16))

    @pl.when(f == pl.num_programs(1) - 1)
    def _():
        s0 = x0_ref.shape[0]
        y0_ref[...] = (acc_ref[:s0, :] * g0_ref[...]).astype(BF16)
        y1_ref[...] = (acc_ref[s0:, :] * g1_ref[...]).astype(BF16)


def _experts(xe0, gates0, xe1, gates1, w_gate_up, w_down, layer):
    ne, S0, _ = xe0.shape
    S1 = xe1.shape[1]
    nf = D_FF // FF_TILE
    rows = lambda S, w: pl.BlockSpec((None, S, w), lambda e, f: (e, 0, 0))
    return pl.pallas_call(
        _ffn_kernel,
        out_shape=[jax.ShapeDtypeStruct((ne, S0, D_MODEL), BF16), jax.ShapeDtypeStruct((ne, S1, D_MODEL), BF16)],
        grid=(ne, nf),
        in_specs=[rows(S0, D_MODEL), rows(S1, D_MODEL),
                  pl.BlockSpec((None, None, D_MODEL, FF_TILE), lambda e, f: (layer, e, 0, f)),
                  pl.BlockSpec((None, None, D_MODEL, FF_TILE), lambda e, f: (layer, e, 0, nf + f)),
                  pl.BlockSpec((None, None, FF_TILE, D_MODEL), lambda e, f: (layer, e, f, 0)),
                  rows(S0, 1), rows(S1, 1)],
        out_specs=[rows(S0, D_MODEL), rows(S1, D_MODEL)],
        scratch_shapes=[pltpu.VMEM((S0 + S1, D_MODEL), F32)],
        compiler_params=_cp(2), name="expert_ffn",
    )(xe0, xe1, w_gate_up, w_gate_up, w_down, gates0, gates1)


def _scatter_kernel(pos_ref, y_ref, x1_ref, g2_ref, lg_ref, lb_ref, o_ref, *, cap):
    tm, ne = pos_ref.shape
    cw = max(cap, LANES)
    epc = cw // cap
    lane = lax.broadcasted_iota(I32, (tm, cw), 1)
    pos = pos_ref[...]
    def onehot_chunk(c):
        onehot = jnp.zeros((tm, cw), F32)
        for jj in range(epc):
            col = pos[:, c * epc + jj:c * epc + jj + 1]
            tgt = jnp.where(col >= 0, col + jj * cap, -1)
            onehot = onehot + jnp.where(lane == tgt, 1.0, 0.0)
        return onehot.astype(BF16)

    n_chunks = ne // epc
    per_dot = max(1, min(n_chunks, SCATTER_CONTRACTION // cw))
    acc = None
    for c0 in range(0, n_chunks, per_dot):
        onehot = [onehot_chunk(c) for c in range(c0, c0 + per_dot)]
        onehot = onehot[0] if per_dot == 1 else jnp.concatenate(onehot, axis=1)
        yc = y_ref[c0 * epc:(c0 + per_dot) * epc].reshape(per_dot * cw, D_MODEL)
        part = _dot(onehot, yc)
        acc = part if acc is None else acc + part
    o_ref[...] = _layernorm(ALPHA * x1_ref[...] + g2_ref[...] * acc, lg_ref[...], lb_ref[...])


def _scatter(pos_tok, y, x1, mods, ln_g, ln_b, cap, tm):
    R, n, ne = pos_tok.shape
    vec = pl.BlockSpec((1, D_MODEL), lambda r, i: (0, 0))
    return pl.pallas_call(
        functools.partial(_scatter_kernel, cap=cap),
        out_shape=jax.ShapeDtypeStruct((R, n, D_MODEL), F32),
        grid=(R, n // tm),
        in_specs=[pl.BlockSpec((None, tm, ne), lambda r, i: (r, i, 0)),
                  pl.BlockSpec((ne, cap, D_MODEL), lambda r, i: (0, r, 0)),
                  pl.BlockSpec((None, tm, D_MODEL), lambda r, i: (r, i, 0)),
                  _mod_spec(5, 2), vec, vec],
        out_specs=pl.BlockSpec((None, tm, D_MODEL), lambda r, i: (r, i, 0)),
        compiler_params=_cp(2), name="expert_scatter_ln",
    )(pos_tok, y, x1, mods, ln_g.reshape(1, -1), ln_b.reshape(1, -1))


def _capacity(n):
    return CAPACITY_FACTOR * n // N_EXPERTS


def _route(h2, aff):
    cap = _capacity(h2.shape[1])
    aff_t = jnp.swapaxes(aff, 1, 2)
    pos = _select(aff_t, cap)
    xe, gates = _gather(pos, aff_t, h2, cap)
    return pos, xe, gates


def _channel_mixers(groups, l, w_gate_up, w_down, ln2_g, ln2_b):
    routed = [_route(g["h2"], g["aff"]) for g in groups]
    ys = _experts(routed[0][1], routed[0][2], routed[1][1], routed[1][2], w_gate_up, w_down, l)
    return [_scatter(jnp.swapaxes(r[0], 1, 2), y, g["x1"], g["mods"], ln2_g[l], ln2_b[l],
                     _capacity(g["h2"].shape[1]), g["tm"]) for g, r, y in zip(groups, routed, ys)]


def kernel(x_prompt, x_sample, cache_attn_a_k, cache_attn_a_v, cache_attn_b_k, cache_attn_b_v, state_ret, c, c_ctx,
           w_ada, b_ada, w_in, w_out, attn_sink, na_rpb, ret_decay, ret_gn, ln1_g, ln1_b, ln2_g, ln2_b,
           w_router, w_gate_up, w_down):
    n_ctx, L_ctx, _ = x_prompt.shape
    n_lat, L_lat, _ = x_sample.shape
    P = cache_attn_a_k.shape[2]

    cond = jnp.concatenate([c_ctx[None, :], c, jnp.zeros((8 - 1 - n_lat, D_MODEL), F32)], axis=0)
    mod = _modulation(cond, w_ada, b_ada)
    w_out16, w_r16 = w_out.astype(BF16), w_router.astype(BF16)
    rope_tabs = _rope_tables(L_lat)
    feature_major = lambda t: jnp.transpose(t, (0, 1, 3, 4, 2)).reshape(n_lat, DEPTH, -1, P)
    cak, cav = feature_major(cache_attn_a_k), feature_major(cache_attn_a_v)
    cbk, cbv = feature_major(cache_attn_b_k), feature_major(cache_attn_b_v)
    s0_pairs = _heads_to_pair_states(state_ret)

    xc, xs = x_prompt, x_sample
    tm = ROW_TILE
    a_k, a_v, b_k, b_v, st = [], [], [], [], []
    flat = lambda a: a.reshape(1, n_ctx * L_ctx, a.shape[-1])
    unflat = lambda a: a.reshape(n_ctx, L_ctx, a.shape[-1])
    for l in range(DEPTH):
        mod_c = mod[l, 0:1][:, None, :]
        mods_c = jnp.broadcast_to(mod_c, (n_ctx, 1, 6 * D_MODEL))
        qkv, ka, va, kb, vb = _project(flat(xc), mod_c, w_in, l, tm, kv_seq=L_ctx)
        qkv = unflat(qkv)
        oa = _attn_a_ctx(qkv, attn_sink[l])
        ob = _attn_b_ctx(qkv)
        oc, s_l = _retention(qkv, ret_decay[l], ret_gn[l], out_state=True)
        x1, h2, aff = [unflat(t) for t in _outproj(flat(oa), flat(ob), flat(oc), flat(xc), mod_c, w_out16,
                                                   ln1_g[l], ln1_b[l], w_r16, l, tm)]
        ctx = dict(x1=x1, h2=h2, aff=aff, mods=mods_c, tm=L_ctx)
        heads_last = lambda t, h: jnp.transpose(t.reshape(n_ctx, h, HEAD_DIM, L_ctx), (0, 3, 1, 2))
        a_k.append(heads_last(ka, KVH_A))
        a_v.append(heads_last(va, KVH_A))
        b_k.append(heads_last(kb, H_B))
        b_v.append(heads_last(vb, H_B))
        st.append(_pair_states_to_heads(s_l))

        mods_s = mod[l, 1:1 + n_lat][:, None, :]
        qkv = _project(xs, mods_s, w_in, l, tm, rope_tabs)
        oa = _attn_a_lat(qkv, attn_sink[l], cak, cav, l)
        ob = _attn_b_lat(qkv, cbk, cbv, _bias_table(na_rpb[l]), l)
        oc = _retention(qkv, ret_decay[l], ret_gn[l], s0=s0_pairs, layer=l)
        x1, h2, aff = _outproj(oa, ob, oc, xs, mods_s, w_out16, ln1_g[l], ln1_b[l], w_r16, l, tm)
        lat = dict(x1=x1, h2=h2, aff=aff, mods=mods_s, tm=tm)

        xc, xs = _channel_mixers([ctx, lat], l, w_gate_up, w_down, ln2_g, ln2_b)

    return (xc, xs, jnp.stack(a_k, 1), jnp.stack(a_v, 1), jnp.stack(b_k, 1), jnp.stack(b_v, 1),
            jnp.stack(st, 1))
```

```python
import functools

import numpy as np
import jax
import jax.numpy as jnp
from jax import lax
from jax.experimental import pallas as pl
from jax.experimental.pallas import tpu as pltpu

F32 = jnp.float32
BF16 = jnp.bfloat16
I32 = jnp.int32

D_MODEL = 1024
DEPTH = 2
GRID_W = 64
HEAD_DIM = 64
H_A, KVH_A, H_B, H_C = 8, 2, 4, 4
D_IN = 2560
BLOCK = 128
WINDOW = 128
NA_ROWS, NA_COLS = 8, 16
NA_WIN_BLOCKS = 5
NA_QBLOCKS = 4
WIN_QBLOCKS = 2
CTX_GROUP_A = 4
CTX_GROUP_B = 8
ROPE_BASE = 10000.0
N_EXPERTS = 16
CAPACITY_FACTOR = 2
D_FF = 1024
LN_EPS = 1e-6
ALPHA = (2 * DEPTH) ** 0.25
NEG = -1e30
LANES = 128
QK_SCALE = HEAD_DIM ** -0.5
LOG2E = float(np.log2(np.e))
ATTN_SCALE = QK_SCALE * LOG2E
VMEM_LIMIT = 56 * 1024 * 1024
ROW_TILE = 1024
OUTPROJ_PARTS = 2
RET_GROUP_CHUNKS = 8
SCATTER_CONTRACTION = 4096
PROJ_COL_CHUNK = 512
VARIANT_ROWS = 256
SELECT_ELEMS = 128 * 2048

COL_QA, COL_KA, COL_VA = 0, 4, 5
COL_QB, COL_KB, COL_VB = 3, 4, 5
COL_QC, COL_KC, COL_VC, COL_GC = 6, 7, 8, 9


def _cp(n_axes):
    return pltpu.CompilerParams(dimension_semantics=("arbitrary",) * n_axes, vmem_limit_bytes=VMEM_LIMIT)


def _dot(a, b):
    return jnp.dot(a, b, preferred_element_type=F32)


def _dot_t(a, b):
    return lax.dot_general(a, b, (((1,), (1,)), ((), ())), preferred_element_type=F32)


def _silu(x):
    return x / (1.0 + jnp.exp(-x))


def _lane_low(shape):
    return lax.broadcasted_iota(I32, shape, len(shape) - 1) < HEAD_DIM


def _layernorm(y, g, b):
    mu = jnp.mean(y, axis=-1, keepdims=True)
    d = y - mu
    var = jnp.mean(d * d, axis=-1, keepdims=True)
    return d * lax.rsqrt(var + LN_EPS) * g + b


def _mod_kernel(c_ref, w_ref, b_ref, o_ref):
    s = _silu(c_ref[...]).astype(BF16)
    o_ref[...] = _dot(s, w_ref[...].astype(BF16)) + b_ref[...]


def _modulation(cond, w_ada, b_ada):
    tn = 1536
    return pl.pallas_call(
        _mod_kernel,
        out_shape=jax.ShapeDtypeStruct((DEPTH, 8, 6 * D_MODEL), F32),
        grid=(DEPTH, 6 * D_MODEL // tn),
        in_specs=[pl.BlockSpec((8, D_MODEL), lambda l, j: (0, 0)),
                  pl.BlockSpec((None, D_MODEL, tn), lambda l, j: (l, 0, j)),
                  pl.BlockSpec((None, 1, tn), lambda l, j: (l, 0, j))],
        out_specs=pl.BlockSpec((None, 8, tn), lambda l, j: (l, 0, j)),
        compiler_params=_cp(2), name="modulation",
    )(cond, w_ada, b_ada.reshape(DEPTH, 1, 6 * D_MODEL))


def _mod_spec(k, n_grid):
    if n_grid == 1:
        return pl.BlockSpec((None, 1, D_MODEL), lambda r: (r, 0, k))
    return pl.BlockSpec((None, 1, D_MODEL), lambda r, i: (r, 0, k))


def _rope_tables(L):
    t = np.arange(L)
    pos = (t // GRID_W, t % GRID_W)
    half = HEAD_DIM // 2
    inv = 1.0 / (ROPE_BASE ** (np.arange(0, half, 2) / half))
    cos = np.zeros((L, HEAD_DIM)); s_up = np.zeros((L, HEAD_DIM)); s_dn = np.zeros((L, HEAD_DIM))
    for part in range(2):
        ang = (pos[part][:, None] * inv[None, :]).astype(np.float32).astype(np.float64)
        q = half // 2
        base = part * half
        cos[:, base:base + q] = np.cos(ang); cos[:, base + q:base + half] = np.cos(ang)
        s_up[:, base:base + q] = -np.sin(ang)
        s_dn[:, base + q:base + half] = np.sin(ang)
    tile = lambda a: jnp.asarray(np.tile(a, (1, LANES // HEAD_DIM)), F32)
    return tile(cos), tile(s_up), tile(s_dn)


KV_COLUMNS = ((512, 128), (640, 128), (1024, 256), (1280, 256))


def _proj_kernel(*refs, rope, emit_kv):
    kv_refs = ()
    if emit_kv:
        refs, kv_refs = refs[:-len(KV_COLUMNS)], refs[-len(KV_COLUMNS):]
    if rope:
        x_ref, sc_ref, sh_ref, w_ref, cos_ref, sup_ref, sdn_ref, o_ref = refs
    else:
        x_ref, sc_ref, sh_ref, w_ref, o_ref = refs
    h = (x_ref[...] * (1.0 + sc_ref[...]) + sh_ref[...]).astype(BF16)
    n_rope = (H_A + KVH_A) * HEAD_DIM
    step = PROJ_COL_CHUNK
    for j in range(0, D_IN, step):
        acc = _dot(h, w_ref[:, j:j + step].astype(BF16))
        if rope and j < n_rope:
            q = HEAD_DIM // 4
            for c in range(0, step, LANES):
                a = acc[:, c:c + LANES]
                if j + c < n_rope:
                    a = (a * cos_ref[...] + pltpu.roll(a, LANES - q, 1) * sup_ref[...]
                         + pltpu.roll(a, q, 1) * sdn_ref[...])
                o_ref[:, j + c:j + c + LANES] = a
        else:
            o_ref[:, j:j + step] = acc
        for ref, (c0, w) in zip(kv_refs, KV_COLUMNS):
            if j <= c0 < j + step:
                n_req, _, seq = ref.shape
                for g in range(n_req):
                    ref[g] = acc[g * seq:(g + 1) * seq, c0 - j:c0 - j + w].T


def _project(x, mods, w_in, layer, tm, rope_tabs=None, kv_seq=None):
    R, L, _ = x.shape
    rope = rope_tabs is not None
    emit_kv = kv_seq is not None
    assert not (rope and emit_kv)
    out_shape = [jax.ShapeDtypeStruct((R, L, D_IN), F32)]
    out_specs = [pl.BlockSpec((None, tm, D_IN), lambda r, i: (r, i, 0))]
    if emit_kv:
        assert R == 1 and tm % kv_seq == 0
        out_shape += [jax.ShapeDtypeStruct((L // kv_seq, w, kv_seq), F32) for _, w in KV_COLUMNS]
        out_specs += [pl.BlockSpec((tm // kv_seq, w, kv_seq), lambda r, i: (i, 0, 0)) for _, w in KV_COLUMNS]
    in_specs = [pl.BlockSpec((None, tm, D_MODEL), lambda r, i: (r, i, 0)),
                _mod_spec(1, 2), _mod_spec(0, 2),
                pl.BlockSpec((None, D_MODEL, D_IN), lambda r, i: (layer, 0, 0), pipeline_mode=pl.Buffered(1))]
    args = [x, mods, mods, w_in]
    if rope:
        in_specs += [pl.BlockSpec((tm, LANES), lambda r, i: (i, 0))] * 3
        args += list(rope_tabs)
    res = pl.pallas_call(
        functools.partial(_proj_kernel, rope=rope, emit_kv=emit_kv),
        out_shape=out_shape, grid=(R, L // tm), in_specs=in_specs, out_specs=out_specs,
        compiler_params=_cp(2), name="in_proj",
    )(*args)
    return res if emit_kv else res[0]


def _mask_head(x, sub):
    low = _lane_low(x.shape)
    return jnp.where(low, x, 0.0) if sub == 0 else jnp.where(low, 0.0, x)


def _key_variant(k, sub):
    return _mask_head(k, sub).astype(BF16)


def _value_variant(v, sub):
    low = _lane_low(v.shape)
    first = jnp.where(low, v, 1.0) if sub == 0 else jnp.where(low, 1.0, v)
    return jnp.concatenate([first, jnp.ones_like(v)], axis=1).astype(BF16)


def _attend(lhs, keys, values, biases, sink=None):
    scores = []
    for k, b in zip(keys, biases):
        s = _dot_t(lhs, k)
        scores.append(s if b is None else s + b)
    tiles = [s[:, c:c + LANES] for s in scores for c in range(0, s.shape[1], LANES)]
    while len(tiles) > 1:
        tiles = [jnp.maximum(a, b) for a, b in zip(tiles[0::2], tiles[1::2])] + ([tiles[-1]] if len(tiles) % 2 else [])
    m = jnp.broadcast_to(tiles[0].max(axis=-1, keepdims=True), tiles[0].shape)
    if sink is not None:
        m = jnp.maximum(m, sink)
    o = None
    for s, v in zip(scores, values):
        p = [jnp.exp2(s[:, c:c + LANES] - m).astype(BF16) for c in range(0, s.shape[1], LANES)]
        pv = _dot(p[0] if len(p) == 1 else jnp.concatenate(p, axis=1), v)
        o = pv if o is None else o + pv
    return o, (None if sink is None else jnp.exp2(sink - m))


def _normalised(o):
    return o[:, :LANES] / o[:, LANES:]


def _gqa_variants(k, v, hk, wide):
    low = _lane_low(k.shape)
    k_sw, v_sw = pltpu.roll(k, HEAD_DIM, 1), pltpu.roll(v, HEAD_DIM, 1)
    k_dup = jnp.where(low, k, k_sw) if hk == 0 else jnp.where(low, k_sw, k)
    v_low, v_high = (v, v_sw) if hk == 0 else (v_sw, v)
    va = jnp.where(low, v_low, 1.0)
    if wide:
        va = jnp.concatenate([va, jnp.where(low, 1.0, v_high)], axis=1)
    return k_dup.astype(BF16), va.astype(BF16)


def _fill_gqa_variants(k_src, v_src, kv_ref, vv_ref, off, n, feature_major=False):
    step = VARIANT_ROWS
    for c in range(0, n, step):
        k = k_src[:, c:c + step].T if feature_major else k_src[c:c + step, :]
        v = v_src[:, c:c + step].T if feature_major else v_src[c:c + step, :]
        for hk in range(KVH_A):
            kd, va = _gqa_variants(k, v, hk, wide=False)
            kv_ref[hk, off + c:off + c + step, :] = kd
            vv_ref[hk, off + c:off + c + step, :] = va


def _fill_mha_variants(k_src, v_src, kv_ref, vv_ref, off, n, feature_major=False):
    step = VARIANT_ROWS
    for c in range(0, n, step):
        rows = slice(off + c, off + c + step)
        for p in range(H_B // 2):
            cs = slice(LANES * p, LANES * p + LANES)
            k = k_src[cs, c:c + step].T if feature_major else k_src[c:c + step, cs]
            v = v_src[cs, c:c + step].T if feature_major else v_src[c:c + step, cs]
            for sub in range(2):
                kv_ref[p, sub, rows, :] = _key_variant(k, sub)
                vv_ref[p, sub, rows, :] = _value_variant(v, sub)


def _sink_col(sink_ref, heads, rows):
    return jnp.concatenate([jnp.full((rows, LANES), sink_ref[h] * LOG2E, F32) for h in heads], axis=0)


def _gqa_attend(sink_ref, q_ref, o_ref, hk, keys, values, biases):
    rows = q_ref.shape[0]
    g = H_A // KVH_A
    heads = list(range(g * hk, g * hk + g))
    lhs = jnp.concatenate([_mask_head(q_ref[:, LANES * (h // 2):LANES * (h // 2) + LANES], h % 2) for h in heads],
                          axis=0)
    lhs = (lhs * ATTN_SCALE).astype(BF16)
    stacked = [None if b is None else jnp.concatenate([b] * g, axis=0) for b in biases]
    o, extra = _attend(lhs, keys, values, stacked, _sink_col(sink_ref, heads, rows))
    low_all = _lane_low((g * rows, LANES))
    low = _lane_low((rows, LANES))
    first = o[:, :LANES] + jnp.where(low_all, 0.0, extra)
    if o.shape[1] == LANES:
        out = first / pltpu.roll(first, HEAD_DIM, 1)
    else:
        second = o[:, LANES:] + jnp.where(low_all, extra, 0.0)
    for j in range(0, g, 2):
        h = heads[j]
        even, odd = slice(j * rows, (j + 1) * rows), slice((j + 1) * rows, (j + 2) * rows)
        if o.shape[1] == LANES:
            pair = jnp.where(low, out[even], pltpu.roll(out[odd], HEAD_DIM, 1))
        else:
            pair = jnp.where(low, first[even] / second[even], second[odd] / first[odd])
        o_ref[:, LANES * (h // 2):LANES * (h // 2) + LANES] = pair.astype(o_ref.dtype)


def _attn_a_ctx_kernel(sink_ref, q_ref, k_ref, v_ref, o_ref):
    for g in range(q_ref.shape[0]):
        for hk in range(KVH_A):
            kd, va = _gqa_variants(k_ref[g], v_ref[g], hk, wide=True)
            _gqa_attend(sink_ref, q_ref.at[g], o_ref.at[g], hk, [kd], [va], [None])


def _window_mask_table():
    nq = WIN_QBLOCKS
    row = np.arange(nq * BLOCK)[:, None]
    col = np.arange((nq + 2) * BLOCK)[None, :]
    tabs = [np.where(np.abs(col - off * BLOCK - row) <= WINDOW, 0.0, NEG) for off in range(3)]
    return jnp.asarray(np.stack(tabs), F32)


def _attn_a_lat_kernel(sink_ref, q_ref, k_ref, v_ref, kc_ref, vc_ref, mask_ref, o_ref, kv_ref, vv_ref, *, nb):
    i = pl.program_id(1)
    L, P = k_ref.shape[0], kc_ref.shape[1]

    @pl.when(i == 0)
    def _():
        _fill_gqa_variants(k_ref, v_ref, kv_ref, vv_ref, 0, L)
        _fill_gqa_variants(kc_ref, vc_ref, kv_ref, vv_ref, L, P, feature_major=True)

    nq = WIN_QBLOCKS
    wk = (nq + 2) * BLOCK
    sblk = jnp.clip(nq * i - 1, 0, nb - (nq + 2))
    start = pl.multiple_of(sblk * BLOCK, BLOCK)
    bias = mask_ref[nq * i - sblk]
    for hk in range(KVH_A):
        ks = [kv_ref[hk, pl.ds(start, wk), :], kv_ref[hk, L:L + P, :]]
        vs = [vv_ref[hk, pl.ds(start, wk), :], vv_ref[hk, L:L + P, :]]
        _gqa_attend(sink_ref, q_ref, o_ref, hk, ks, vs, [bias, None])


def _smem_spec():
    return pl.BlockSpec(memory_space=pltpu.SMEM)


def _attn_a_ctx(qkv, sink):
    R, L, _ = qkv.shape
    wq, wkv = H_A * HEAD_DIM, KVH_A * HEAD_DIM
    G = CTX_GROUP_A
    assert R % G == 0
    return pl.pallas_call(
        _attn_a_ctx_kernel,
        out_shape=jax.ShapeDtypeStruct((R, L, wq), F32),
        grid=(R // G,),
        in_specs=[_smem_spec(),
                  pl.BlockSpec((G, L, wq), lambda r: (r, 0, COL_QA)),
                  pl.BlockSpec((G, L, wkv), lambda r: (r, 0, COL_KA)),
                  pl.BlockSpec((G, L, wkv), lambda r: (r, 0, COL_VA))],
        out_specs=pl.BlockSpec((G, L, wq), lambda r: (r, 0, 0)),
        compiler_params=_cp(1), name="attn_a_ctx",
    )(sink, qkv, qkv, qkv)


def _attn_a_lat(qkv, sink, cache_k, cache_v, layer):
    R, L, _ = qkv.shape
    nb = L // BLOCK
    nq = WIN_QBLOCKS
    assert nb % nq == 0 and nb >= nq + 2
    P = cache_k.shape[3]
    wq, wkv = H_A * HEAD_DIM, KVH_A * HEAD_DIM
    cache_spec = pl.BlockSpec((None, None, wkv, P), lambda r, i: (r, layer, 0, 0))
    masks = _window_mask_table()
    return pl.pallas_call(
        functools.partial(_attn_a_lat_kernel, nb=nb),
        out_shape=jax.ShapeDtypeStruct((R, L, wq), F32),
        grid=(R, nb // nq),
        in_specs=[_smem_spec(),
                  pl.BlockSpec((None, nq * BLOCK, wq), lambda r, i: (r, i, COL_QA)),
                  pl.BlockSpec((None, L, wkv), lambda r, i: (r, 0, COL_KA)),
                  pl.BlockSpec((None, L, wkv), lambda r, i: (r, 0, COL_VA)),
                  cache_spec, cache_spec,
                  pl.BlockSpec(masks.shape, lambda r, i: (0, 0, 0))],
        out_specs=pl.BlockSpec((None, nq * BLOCK, wq), lambda r, i: (r, i, 0)),
        scratch_shapes=[pltpu.VMEM((KVH_A, L + P, LANES), BF16),
                        pltpu.VMEM((KVH_A, L + P, LANES), BF16)],
        compiler_params=_cp(2), name="attn_a_lat",
    )(sink, qkv, qkv, qkv, cache_k, cache_v, masks)


def _attn_b_ctx_kernel(q_ref, k_ref, v_ref, o_ref):
    G, L = q_ref.shape[:2]
    low = _lane_low((L, LANES))
    for g in range(G):
        for p in range(H_B // 2):
            cs = slice(LANES * p, LANES * p + LANES)
            lhs = (q_ref[g, :, cs] * ATTN_SCALE).astype(BF16)
            k, v = k_ref[g, :, cs], v_ref[g, :, cs]
            outs = [_normalised(_attend(lhs, [_key_variant(k, sub)], [_value_variant(v, sub)], [None])[0])
                    for sub in range(2)]
            o_ref[g, :, cs] = jnp.where(low, outs[0], outs[1]).astype(o_ref.dtype)


def _bias_table_kernel(rpb_ref, o_ref, t_ref):
    n_dr, n_dc = 2 * NA_ROWS - 1, 2 * NA_COLS - 1
    n_t = o_ref.shape[1]
    shape = (GRID_W, LANES)
    cq = lax.broadcasted_iota(I32, shape, 0)
    lane = lax.broadcasted_iota(I32, shape, 1)
    right = lane >= GRID_W
    ck = lane & (GRID_W - 1)
    cs = jnp.clip(cq - NA_COLS // 2, 0, GRID_W - NA_COLS)
    col_ok = (ck >= cs) & (ck < cs + NA_COLS)
    dc = ck - cq + (NA_COLS - 1)
    for j in (0, 1, n_t - 1, n_t):
        t_ref[j] = jnp.zeros(shape, F32)

    def build(r, carry, h):
        acc = jnp.zeros(shape, F32)
        for d in range(n_dc):
            acc = jnp.where(dc == d, rpb_ref[(h * n_dr + r) * n_dc + d] * LOG2E, acc)
        t_ref[r + 2] = jnp.where(col_ok, acc, NEG)
        return carry

    def compose(t, carry, h):
        o_ref[h, t] = jnp.where(right, t_ref[t + 1], t_ref[t])
        return carry

    for h in range(H_B):
        lax.fori_loop(0, n_dr, functools.partial(build, h=h), 0)
        lax.fori_loop(0, n_t, functools.partial(compose, h=h), 0)


def _bias_table(rpb_l):
    n_t = 2 * (NA_ROWS + 1)
    return pl.pallas_call(
        _bias_table_kernel,
        out_shape=jax.ShapeDtypeStruct((H_B, n_t, GRID_W, LANES), F32),
        in_specs=[_smem_spec()],
        out_specs=pl.BlockSpec(memory_space=pltpu.VMEM),
        scratch_shapes=[pltpu.VMEM((n_t + 1, GRID_W, LANES), F32)],
        name="na_bias_table",
    )(rpb_l.reshape(-1))


def _attn_b_lat_kernel(q_ref, k_ref, v_ref, kc_ref, vc_ref, tp_ref, o_ref, kv_ref, vv_ref, *, nb, rows_total):
    i = pl.program_id(1)
    L, P = k_ref.shape[0], kc_ref.shape[1]

    @pl.when(i == 0)
    def _():
        _fill_mha_variants(k_ref, v_ref, kv_ref, vv_ref, 0, L)
        _fill_mha_variants(kc_ref, vc_ref, kv_ref, vv_ref, L, P, feature_major=True)

    nq = NA_QBLOCKS
    nwin = NA_WIN_BLOCKS + nq - 1
    wk = nwin * BLOCK
    rows = nq * BLOCK
    sblk = jnp.clip(nq * i - 2, 0, nb - nwin)
    start = pl.multiple_of(sblk * BLOCK, BLOCK)
    shape = (rows, wk)
    rq = 2 * nq * i + (lax.broadcasted_iota(I32, shape, 0) >> 6)
    rk = 2 * sblk + (lax.broadcasted_iota(I32, shape, 1) >> 6)
    rs = jnp.clip(rq - NA_ROWS // 2, 0, rows_total - NA_ROWS)
    row_bias = jnp.where((rk >= rs) & (rk < rs + NA_ROWS), 0.0, NEG)
    n_t = tp_ref.shape[1]
    low = _lane_low((rows, LANES))
    for p in range(H_B // 2):
        cs = slice(LANES * p, LANES * p + LANES)
        lhs = (q_ref[:, cs] * ATTN_SCALE).astype(BF16)
        outs = []
        for sub in range(2):
            h = 2 * p + sub
            strips = []
            for qq in range(nq):
                tiles = []
                for kb in range(nwin):
                    t = jnp.clip(2 * (sblk + kb - (nq * i + qq)) + NA_ROWS + 1, 1, n_t - 1)
                    tiles.append(jnp.concatenate([tp_ref[h, t], tp_ref[h, t - 1]], axis=0))
                strips.append(jnp.concatenate(tiles, axis=1))
            bias = jnp.concatenate(strips, axis=0) + row_bias
            ks = [kv_ref[p, sub, pl.ds(start, wk), :], kv_ref[p, sub, L:L + P, :]]
            vs = [vv_ref[p, sub, pl.ds(start, wk), :], vv_ref[p, sub, L:L + P, :]]
            outs.append(_normalised(_attend(lhs, ks, vs, [bias, None])[0]))
        o_ref[:, cs] = jnp.where(low, outs[0], outs[1]).astype(o_ref.dtype)


def _attn_b_ctx(qkv):
    R, L, _ = qkv.shape
    w = H_B * HEAD_DIM
    G = CTX_GROUP_B
    assert R % G == 0
    return pl.pallas_call(
        _attn_b_ctx_kernel,
        out_shape=jax.ShapeDtypeStruct((R, L, w), F32),
        grid=(R // G,),
        in_specs=[pl.BlockSpec((G, L, w), lambda r: (r, 0, COL_QB)),
                  pl.BlockSpec((G, L, w), lambda r: (r, 0, COL_KB)),
                  pl.BlockSpec((G, L, w), lambda r: (r, 0, COL_VB))],
        out_specs=pl.BlockSpec((G, L, w), lambda r: (r, 0, 0)),
        compiler_params=_cp(1), name="attn_b_ctx",
    )(qkv, qkv, qkv)


def _attn_b_lat(qkv, cache_k, cache_v, bias_tab, layer):
    R, L, _ = qkv.shape
    nb = L // BLOCK
    P = cache_k.shape[3]
    w = H_B * HEAD_DIM
    cache_spec = pl.BlockSpec((None, None, w, P), lambda r, i: (r, layer, 0, 0))
    rows = NA_QBLOCKS * BLOCK
    assert nb % NA_QBLOCKS == 0 and nb >= NA_WIN_BLOCKS + NA_QBLOCKS - 1
    return pl.pallas_call(
        functools.partial(_attn_b_lat_kernel, nb=nb, rows_total=L // GRID_W),
        out_shape=jax.ShapeDtypeStruct((R, L, w), F32),
        grid=(R, nb // NA_QBLOCKS),
        in_specs=[pl.BlockSpec((None, rows, w), lambda r, i: (r, i, COL_QB)),
                  pl.BlockSpec((None, L, w), lambda r, i: (r, 0, COL_KB)),
                  pl.BlockSpec((None, L, w), lambda r, i: (r, 0, COL_VB)),
                  cache_spec, cache_spec,
                  pl.BlockSpec(bias_tab.shape, lambda r, i: (0, 0, 0, 0))],
        out_specs=pl.BlockSpec((None, rows, w), lambda r, i: (r, i, 0)),
        scratch_shapes=[pltpu.VMEM((H_B // 2, 2, L + P, LANES), BF16),
                        pltpu.VMEM((H_B // 2, 2, L + P, 2 * LANES), BF16)],
        compiler_params=_cp(2), name="attn_b_lat",
    )(qkv, qkv, qkv, cache_k, cache_v, bias_tab)


T_ZF, T_ZB, T_XF, T_XB, T_GF, T_GB, T_MA, T_MB = range(8)


def _ret_kernel(*refs, nc, has_s0, out_state):
    refs = list(refs)
    dec_ref, q_ref, k_ref, v_ref, g_ref, gn_ref = refs[:6]
    pos = 6
    s0_ref = None
    if has_s0:
        s0_ref = refs[pos]; pos += 1
    o_ref = refs[pos]; pos += 1
    st_ref = None
    if out_state:
        st_ref = refs[pos]; pos += 1
    sf_ref, sb_ref, tab_ref = refs[pos], refs[pos + 1], refs[pos + 2]

    sq = (BLOCK, LANES)
    low = _lane_low(sq)
    row_low = lax.broadcasted_iota(I32, sq, 0) < HEAD_DIM
    same_head = low == row_low
    n_pairs = H_C // 2

    @pl.when(pl.program_id(0) == 0)
    def _():
        ri = lax.broadcasted_iota(I32, sq, 0).astype(F32)
        ci = lax.broadcasted_iota(I32, sq, 1).astype(F32)

        def log_gamma(d, h):
            return -jnp.exp(jnp.full(sq, dec_ref[d, h], F32))

        for p in range(n_pairs):
            lf = [log_gamma(0, 2 * p), log_gamma(0, 2 * p + 1)]
            lb = [log_gamma(1, 2 * p), log_gamma(1, 2 * p + 1)]
            lf_lane, lb_lane = jnp.where(low, lf[0], lf[1]), jnp.where(low, lb[0], lb[1])
            lf_row, lb_row = jnp.where(row_low, lf[0], lf[1]), jnp.where(row_low, lb[0], lb[1])
            tab_ref[p, T_ZF] = jnp.exp(lf_lane * (BLOCK - 1.0 - ri))
            tab_ref[p, T_ZB] = jnp.exp(lb_lane * ri)
            tab_ref[p, T_XF] = jnp.exp(lf_lane * (ri + 1.0))
            tab_ref[p, T_XB] = jnp.exp(lb_lane * (BLOCK - ri))
            tab_ref[p, T_GF] = jnp.exp(lf_row * float(BLOCK))
            tab_ref[p, T_GB] = jnp.exp(lb_row * float(BLOCK))
            rel = ri - ci
            for sub in range(2):
                fwd = jnp.where(rel >= 0, jnp.exp(lf[sub] * jnp.maximum(rel, 0.0)), 0.0)
                bwd = jnp.where(rel <= 0, jnp.exp(lb[sub] * jnp.maximum(-rel, 0.0)), 0.0)
                tab_ref[p, T_MA + sub] = fwd + bwd

    n_req = q_ref.shape[0]
    units = [(r, p) for r in range(n_req) for p in range(n_pairs)]

    def chunk(ref, r, c, p):
        r0 = pl.multiple_of(c * BLOCK, BLOCK)
        return ref[r, pl.ds(r0, BLOCK), LANES * p:LANES * p + LANES]

    unroll = min(nc, 8)

    def inc_body(c, carry):
        for r, p in units:
            kt = (chunk(k_ref, r, c, p) * QK_SCALE).T.astype(BF16)
            v = chunk(v_ref, r, c, p)
            zv = jnp.concatenate([v * tab_ref[p, T_ZF], v * tab_ref[p, T_ZB]], axis=1).astype(BF16)
            inc = _dot(kt, zv)
            sf_ref[r, c, p] = jnp.where(same_head, inc[:, :LANES], 0.0)
            sb_ref[r, c, p] = jnp.where(same_head, inc[:, LANES:], 0.0)
        return carry

    lax.fori_loop(0, nc, inc_body, 0, unroll=unroll)

    def scan_body(t, S, ref, tg, reverse):
        c = nc - 1 - t if reverse else t
        new = []
        for (r, p), state in zip(units, S):
            inc = ref[r, c, p]
            ref[r, c, p] = state
            new.append(tab_ref[p, tg] * state + inc)
        return tuple(new)

    zero = jnp.zeros(sq, F32)
    s_f0 = tuple(s0_ref[r, 0, p] if has_s0 else zero for r, p in units)
    s_b0 = tuple(s0_ref[r, 1, p] if has_s0 else zero for r, p in units)
    scan_unroll = nc if nc <= 4 else 1
    s_f = lax.fori_loop(0, nc, functools.partial(scan_body, ref=sf_ref, tg=T_GF, reverse=False), s_f0,
                        unroll=scan_unroll)
    s_b = lax.fori_loop(0, nc, functools.partial(scan_body, ref=sb_ref, tg=T_GB, reverse=True), s_b0,
                        unroll=scan_unroll)
    if out_state:
        for (r, p), f_state, b_state in zip(units, s_f, s_b):
            st_ref[r, 0, p] = f_state
            st_ref[r, 1, p] = b_state

    def out_body(c, carry):
        for r, p in units:
            q = chunk(q_ref, r, c, p)
            k16 = (chunk(k_ref, r, c, p) * QK_SCALE).astype(BF16)
            v = chunk(v_ref, r, c, p)
            a = _dot_t(jnp.concatenate([_mask_head(q, 0), _mask_head(q, 1)], axis=0).astype(BF16), k16)
            lhs = jnp.concatenate([a[:BLOCK] * tab_ref[p, T_MA], a[BLOCK:] * tab_ref[p, T_MA + 1],
                                   q * tab_ref[p, T_XF], q * tab_ref[p, T_XB]], axis=1).astype(BF16)
            rhs = jnp.concatenate([_mask_head(v, 0), _mask_head(v, 1), sf_ref[r, c, p], sb_ref[r, c, p]],
                                  axis=0).astype(BF16)
            o = _dot(lhs, rhs)
            mu = jnp.where(low, _mask_head(o, 0).sum(-1, keepdims=True),
                           _mask_head(o, 1).sum(-1, keepdims=True)) * (1.0 / HEAD_DIM)
            d = o - mu
            d2 = d * d
            var = jnp.where(low, _mask_head(d2, 0).sum(-1, keepdims=True),
                            _mask_head(d2, 1).sum(-1, keepdims=True)) * (1.0 / HEAD_DIM)
            cs = slice(LANES * p, LANES * p + LANES)
            on = d * lax.rsqrt(var + LN_EPS) * gn_ref[:, cs]
            r0 = pl.multiple_of(c * BLOCK, BLOCK)
            o_ref[r, pl.ds(r0, BLOCK), cs] = (_silu(chunk(g_ref, r, c, p)) * on).astype(o_ref.dtype)
        return carry

    lax.fori_loop(0, nc, out_body, 0, unroll=unroll)


def _retention(qkv, decay, gn_w, s0=None, layer=0, out_state=False):
    R, L, _ = qkv.shape
    nc = L // BLOCK
    w = H_C * HEAD_DIM
    n_pairs = H_C // 2
    G = max(1, min(R, RET_GROUP_CHUNKS // nc))
    assert R % G == 0
    col = lambda cb: pl.BlockSpec((G, L, w), lambda r: (r, 0, cb))
    in_specs = [_smem_spec(), col(COL_QC), col(COL_KC), col(COL_VC), col(COL_GC),
                pl.BlockSpec((1, w), lambda r: (0, 0))]
    args = [decay, qkv, qkv, qkv, qkv, gn_w.reshape(1, w)]
    if s0 is not None:
        in_specs.append(pl.BlockSpec((G, None, 2, n_pairs, BLOCK, LANES), lambda r: (r, layer, 0, 0, 0, 0)))
        args.append(s0)
    out_shape = [jax.ShapeDtypeStruct((R, L, w), F32)]
    out_specs = [pl.BlockSpec((G, L, w), lambda r: (r, 0, 0))]
    if out_state:
        out_shape.append(jax.ShapeDtypeStruct((R, 2, n_pairs, BLOCK, LANES), F32))
        out_specs.append(pl.BlockSpec((G, 2, n_pairs, BLOCK, LANES), lambda r: (r, 0, 0, 0, 0)))
    res = pl.pallas_call(
        functools.partial(_ret_kernel, nc=nc, has_s0=s0 is not None, out_state=out_state),
        out_shape=out_shape, grid=(R // G,), in_specs=in_specs, out_specs=out_specs,
        scratch_shapes=[pltpu.VMEM((G, nc, n_pairs, BLOCK, LANES), F32),
                        pltpu.VMEM((G, nc, n_pairs, BLOCK, LANES), F32),
                        pltpu.VMEM((n_pairs, 8, BLOCK, LANES), F32)],
        compiler_params=_cp(1), name="retention",
    )(*args)
    return res if out_state else res[0]


def _pair_states_to_heads(st):
    a = st[..., :HEAD_DIM, :HEAD_DIM]
    b = st[..., HEAD_DIM:, HEAD_DIM:]
    return jnp.stack([a, b], axis=3).reshape(st.shape[0], 2, H_C, HEAD_DIM, HEAD_DIM)


def _heads_to_pair_states(s):
    s = s.reshape(*s.shape[:3], H_C // 2, 2, HEAD_DIM, HEAD_DIM)
    z = jnp.zeros_like(s[..., 0, :, :])
    top = jnp.concatenate([s[..., 0, :, :], z], axis=-1)
    bot = jnp.concatenate([z, s[..., 1, :, :]], axis=-1)
    return jnp.concatenate([top, bot], axis=-2)


def _outproj_kernel(oa_ref, ob_ref, oc_ref, x_ref, g1_ref, sc2_ref, sh2_ref, w_ref, lg_ref, lb_ref, wr_ref,
                    x1_ref, h2_ref, aff_ref):
    wa, wb = H_A * HEAD_DIM, H_B * HEAD_DIM
    tm = x_ref.shape[0]
    part = tm // OUTPROJ_PARTS
    for r0 in range(0, tm, part):
        rs = slice(r0, r0 + part)
        mix = (_dot(oa_ref[rs, :].astype(BF16), w_ref[0:wa, :])
               + _dot(ob_ref[rs, :].astype(BF16), w_ref[wa:wa + wb, :])
               + _dot(oc_ref[rs, :].astype(BF16), w_ref[wa + wb:, :]))
        x1 = _layernorm(ALPHA * x_ref[rs, :] + g1_ref[...] * mix, lg_ref[...], lb_ref[...])
        x1_ref[rs, :] = x1
        h2 = (x1 * (1.0 + sc2_ref[...]) + sh2_ref[...]).astype(BF16)
        h2_ref[rs, :] = h2
        logits = _dot(h2, wr_ref[...])
        e = jnp.exp(logits - logits.max(axis=-1, keepdims=True))
        aff_ref[rs, :] = e / e.sum(axis=-1, keepdims=True)


def _outproj(oa, ob, oc, x, mods, w_out_bf16, ln_g, ln_b, w_router_bf16, layer, tm):
    R, L, _ = x.shape
    row = lambda w: pl.BlockSpec((None, tm, w), lambda r, i: (r, i, 0))
    vec = pl.BlockSpec((1, D_MODEL), lambda r, i: (0, 0))
    return pl.pallas_call(
        _outproj_kernel,
        out_shape=[jax.ShapeDtypeStruct((R, L, D_MODEL), F32),
                   jax.ShapeDtypeStruct((R, L, D_MODEL), BF16),
                   jax.ShapeDtypeStruct((R, L, N_EXPERTS), F32)],
        grid=(R, L // tm),
        in_specs=[row(H_A * HEAD_DIM), row(H_B * HEAD_DIM), row(H_C * HEAD_DIM), row(D_MODEL),
                  _mod_spec(2, 2), _mod_spec(4, 2), _mod_spec(3, 2),
                  pl.BlockSpec((None, D_MODEL, D_MODEL), lambda r, i: (layer, 0, 0)), vec, vec,
                  pl.BlockSpec((None, D_MODEL, N_EXPERTS), lambda r, i: (layer, 0, 0))],
        out_specs=[row(D_MODEL), row(D_MODEL), row(N_EXPERTS)],
        compiler_params=_cp(2), name="out_proj_ln_router",
    )(oa, ob, oc, x, mods, mods, mods, w_out_bf16, ln_g.reshape(1, -1), ln_b.reshape(1, -1), w_router_bf16)


CUMSUM_CHUNK = 256


def _excl_cumsum_lanes(x):
    rows, n = x.shape
    w = min(CUMSUM_CHUNK, n)
    tri = (lax.broadcasted_iota(I32, (w, w), 0) < lax.broadcasted_iota(I32, (w, w), 1))
    tri = jnp.where(tri, 1.0, 0.0).astype(BF16)
    carry = jnp.zeros((rows, 1), F32)
    outs = []
    for c in range(0, n, w):
        xc = x[:, c:c + w]
        outs.append(_dot(xc.astype(BF16), tri) + carry)
        carry = carry + xc.sum(axis=-1, keepdims=True)
    return outs[0] if len(outs) == 1 else jnp.concatenate(outs, axis=1)


def _select_kernel(a_ref, pos_ref, *, cap):
    rb, ne, n = a_ref.shape
    rows = rb * ne
    a = a_ref[...].reshape(rows, n)
    capf = float(cap)

    def body(_, c):
        lo, hi = c
        mid = lo + ((hi - lo + 1) >> 1)
        cnt = jnp.where(a >= pltpu.bitcast(mid, F32), 1.0, 0.0).sum(axis=-1, keepdims=True)
        ok = cnt >= capf
        return jnp.where(ok, mid, lo), jnp.where(ok, hi, mid - 1)

    one_bits = 0x3F800000
    lo, _ = lax.fori_loop(0, 30, body, (jnp.zeros((rows, 1), I32), jnp.full((rows, 1), one_bits, I32)))
    thr = pltpu.bitcast(lo, F32)
    gt = jnp.where(a > thr, 1.0, 0.0)
    eq = jnp.where(a == thr, 1.0, 0.0)
    need = capf - gt.sum(axis=-1, keepdims=True)
    sel = gt + eq * jnp.where(_excl_cumsum_lanes(eq) < need, 1.0, 0.0)
    slot = _excl_cumsum_lanes(sel)
    pos_ref[...] = jnp.where(sel > 0.5, slot, -1.0).astype(I32).reshape(rb, ne, n)


def _select(aff_t, cap):
    R, ne, n = aff_t.shape
    rb = min(R, max(1, SELECT_ELEMS // (ne * n)))
    return pl.pallas_call(
        functools.partial(_select_kernel, cap=cap),
        out_shape=jax.ShapeDtypeStruct((R, ne, n), I32),
        grid=(R // rb,),
        in_specs=[pl.BlockSpec((rb, ne, n), lambda r: (r, 0, 0))],
        out_specs=pl.BlockSpec((rb, ne, n), lambda r: (r, 0, 0)),
        compiler_params=_cp(1), name="expert_select",
    )(aff_t)


def _gather_kernel(pos_ref, aff_ref, h_ref, x_ref, gate_ref, *, cap):
    eb, _, n = pos_ref.shape
    slot = lax.broadcasted_iota(I32, (cap, n), 0)
    parts = []
    for j in range(eb):
        hit = slot == pos_ref[j]
        parts.append(jnp.where(hit, 1.0, 0.0).astype(BF16))
        gate_ref[j] = jnp.where(hit, aff_ref[j], 0.0).sum(axis=-1, keepdims=True)
    onehot = parts[0] if eb == 1 else jnp.concatenate(parts, axis=0)
    xe = _dot(onehot, h_ref[...]).astype(BF16)
    for j in range(eb):
        x_ref[j] = xe[j * cap:(j + 1) * cap]


def _gather(pos, aff_t, h2, cap):
    R, ne, n = pos.shape
    eb = max(1, min(ne, 2 * ROW_TILE // cap))
    row = pl.BlockSpec((None, eb, 1, n), lambda r, e: (r, e, 0, 0))
    return pl.pallas_call(
        functools.partial(_gather_kernel, cap=cap),
        out_shape=[jax.ShapeDtypeStruct((ne, R * cap, D_MODEL), BF16),
                   jax.ShapeDtypeStruct((ne, R * cap, 1), F32)],
        grid=(R, ne // eb),
        in_specs=[row, row, pl.BlockSpec((None, n, D_MODEL), lambda r, e: (r, 0, 0))],
        out_specs=[pl.BlockSpec((eb, cap, D_MODEL), lambda r, e: (e, r, 0)),
                   pl.BlockSpec((eb, cap, 1), lambda r, e: (e, r, 0))],
        compiler_params=_cp(2), name="expert_gather",
    )(pos.reshape(R, ne, 1, n), aff_t.reshape(R, ne, 1, n), h2)


FF_TILE = 512


def _ffn_kernel(x0_ref, x1_ref, wa_ref, wb_ref, wd_ref, g0_ref, g1_ref, y0_ref, y1_ref, acc_ref):
    f = pl.program_id(1)

    @pl.when(f == 0)
    def _():
        acc_ref[...] = jnp.zeros_like(acc_ref)

    x = jnp.concatenate([x0_ref[...], x1_ref[...]], axis=0)
    a = _dot(x, wa_ref[...].astype(BF16))
    b = _dot(x, wb_ref[...].astype(BF16))
    acc_ref[...] += _dot((_silu(a) * b).astype(BF16), wd_ref[...].astype(BF16))

    @pl.when(f == pl.num_programs(1) - 1)
    def _():
        s0 = x0_ref.shape[0]
        y0_ref[...] = (acc_ref[:s0, :] * g0_ref[...]).astype(BF16)
        y1_ref[...] = (acc_ref[s0:, :] * g1_ref[...]).astype(BF16)


def _experts(xe0, gates0, xe1, gates1, w_gate_up, w_down, layer):
    ne, S0, _ = xe0.shape
    S1 = xe1.shape[1]
    nf = D_FF // FF_TILE
    rows = lambda S, w: pl.BlockSpec((None, S, w), lambda e, f: (e, 0, 0))
    return pl.pallas_call(
        _ffn_kernel,
        out_shape=[jax.ShapeDtypeStruct((ne, S0, D_MODEL), BF16), jax.ShapeDtypeStruct((ne, S1, D_MODEL), BF16)],
        grid=(ne, nf),
        in_specs=[rows(S0, D_MODEL), rows(S1, D_MODEL),
                  pl.BlockSpec((None, None, D_MODEL, FF_TILE), lambda e, f: (layer, e, 0, f)),
                  pl.BlockSpec((None, None, D_MODEL, FF_TILE), lambda e, f: (layer, e, 0, nf + f)),
                  pl.BlockSpec((None, None, FF_TILE, D_MODEL), lambda e, f: (layer, e, f, 0)),
                  rows(S0, 1), rows(S1, 1)],
        out_specs=[rows(S0, D_MODEL), rows(S1, D_MODEL)],
        scratch_shapes=[pltpu.VMEM((S0 + S1, D_MODEL), F32)],
        compiler_params=_cp(2), name="expert_ffn",
    )(xe0, xe1, w_gate_up, w_gate_up, w_down, gates0, gates1)


def _scatter_kernel(pos_ref, y_ref, x1_ref, g2_ref, lg_ref, lb_ref, o_ref, *, cap):
    tm, ne = pos_ref.shape
    cw = max(cap, LANES)
    epc = cw // cap
    lane = lax.broadcasted_iota(I32, (tm, cw), 1)
    pos = pos_ref[...]
    def onehot_chunk(c):
        onehot = jnp.zeros((tm, cw), F32)
        for jj in range(epc):
            col = pos[:, c * epc + jj:c * epc + jj + 1]
            tgt = jnp.where(col >= 0, col + jj * cap, -1)
            onehot = onehot + jnp.where(lane == tgt, 1.0, 0.0)
        return onehot.astype(BF16)

    n_chunks = ne // epc
    per_dot = max(1, min(n_chunks, SCATTER_CONTRACTION // cw))
    acc = None
    for c0 in range(0, n_chunks, per_dot):
        onehot = [onehot_chunk(c) for c in range(c0, c0 + per_dot)]
        onehot = onehot[0] if per_dot == 1 else jnp.concatenate(onehot, axis=1)
        yc = y_ref[c0 * epc:(c0 + per_dot) * epc].reshape(per_dot * cw, D_MODEL)
        part = _dot(onehot, yc)
        acc = part if acc is None else acc + part
    o_ref[...] = _layernorm(ALPHA * x1_ref[...] + g2_ref[...] * acc, lg_ref[...], lb_ref[...])


def _scatter(pos_tok, y, x1, mods, ln_g, ln_b, cap, tm):
    R, n, ne = pos_tok.shape
    vec = pl.BlockSpec((1, D_MODEL), lambda r, i: (0, 0))
    return pl.pallas_call(
        functools.partial(_scatter_kernel, cap=cap),
        out_shape=jax.ShapeDtypeStruct((R, n, D_MODEL), F32),
        grid=(R, n // tm),
        in_specs=[pl.BlockSpec((None, tm, ne), lambda r, i: (r, i, 0)),
                  pl.BlockSpec((ne, cap, D_MODEL), lambda r, i: (0, r, 0)),
                  pl.BlockSpec((None, tm, D_MODEL), lambda r, i: (r, i, 0)),
                  _mod_spec(5, 2), vec, vec],
        out_specs=pl.BlockSpec((None, tm, D_MODEL), lambda r, i: (r, i, 0)),
        compiler_params=_cp(2), name="expert_scatter_ln",
    )(pos_tok, y, x1, mods, ln_g.reshape(1, -1), ln_b.reshape(1, -1))


def _capacity(n):
    return CAPACITY_FACTOR * n // N_EXPERTS


def _route(h2, aff):
    cap = _capacity(h2.shape[1])
    aff_t = jnp.swapaxes(aff, 1, 2)
    pos = _select(aff_t, cap)
    xe, gates = _gather(pos, aff_t, h2, cap)
    return pos, xe, gates


def _channel_mixers(groups, l, w_gate_up, w_down, ln2_g, ln2_b):
    routed = [_route(g["h2"], g["aff"]) for g in groups]
    ys = _experts(routed[0][1], routed[0][2], routed[1][1], routed[1][2], w_gate_up, w_down, l)
    return [_scatter(jnp.swapaxes(r[0], 1, 2), y, g["x1"], g["mods"], ln2_g[l], ln2_b[l],
                     _capacity(g["h2"].shape[1]), g["tm"]) for g, r, y in zip(groups, routed, ys)]


def kernel(x_prompt, x_sample, cache_attn_a_k, cache_attn_a_v, cache_attn_b_k, cache_attn_b_v, state_ret, c, c_ctx,
           w_ada, b_ada, w_in, w_out, attn_sink, na_rpb, ret_decay, ret_gn, ln1_g, ln1_b, ln2_g, ln2_b,
           w_router, w_gate_up, w_down):
    n_ctx, L_ctx, _ = x_prompt.shape
    n_lat, L_lat, _ = x_sample.shape
    P = cache_attn_a_k.shape[2]

    cond = jnp.concatenate([c_ctx[None, :], c, jnp.zeros((8 - 1 - n_lat, D_MODEL), F32)], axis=0)
    mod = _modulation(cond, w_ada, b_ada)
    w_out16, w_r16 = w_out.astype(BF16), w_router.astype(BF16)
    rope_tabs = _rope_tables(L_lat)
    feature_major = lambda t: jnp.transpose(t, (0, 1, 3, 4, 2)).reshape(n_lat, DEPTH, -1, P)
    cak, cav = feature_major(cache_attn_a_k), feature_major(cache_attn_a_v)
    cbk, cbv = feature_major(cache_attn_b_k), feature_major(cache_attn_b_v)
    s0_pairs = _heads_to_pair_states(state_ret)

    xc, xs = x_prompt, x_sample
    tm = ROW_TILE
    a_k, a_v, b_k, b_v, st = [], [], [], [], []
    flat = lambda a: a.reshape(1, n_ctx * L_ctx, a.shape[-1])
    unflat = lambda a: a.reshape(n_ctx, L_ctx, a.shape[-1])
    for l in range(DEPTH):
        mod_c = mod[l, 0:1][:, None, :]
        mods_c = jnp.broadcast_to(mod_c, (n_ctx, 1, 6 * D_MODEL))
        qkv, ka, va, kb, vb = _project(flat(xc), mod_c, w_in, l, tm, kv_seq=L_ctx)
        qkv = unflat(qkv)
        oa = _attn_a_ctx(qkv, attn_sink[l])
        ob = _attn_b_ctx(qkv)
        oc, s_l = _retention(qkv, ret_decay[l], ret_gn[l], out_state=True)
        x1, h2, aff = [unflat(t) for t in _outproj(flat(oa), flat(ob), flat(oc), flat(xc), mod_c, w_out16,
                                                   ln1_g[l], ln1_b[l], w_r16, l, tm)]
        ctx = dict(x1=x1, h2=h2, aff=aff, mods=mods_c, tm=L_ctx)
        heads_last = lambda t, h: jnp.transpose(t.reshape(n_ctx, h, HEAD_DIM, L_ctx), (0, 3, 1, 2))
        a_k.append(heads_last(ka, KVH_A))
        a_v.append(heads_last(va, KVH_A))
        b_k.append(heads_last(kb, H_B))
        b_v.append(heads_last(vb, H_B))
        st.append(_pair_states_to_heads(s_l))

        mods_s = mod[l, 1:1 + n_lat][:, None, :]
        qkv = _project(xs, mods_s, w_in, l, tm, rope_tabs)
        oa = _attn_a_lat(qkv, attn_sink[l], cak, cav, l)
        ob = _attn_b_lat(qkv, cbk, cbv, _bias_table(na_rpb[l]), l)
        oc = _retention(qkv, ret_decay[l], ret_gn[l], s0=s0_pairs, layer=l)
        x1, h2, aff = _outproj(oa, ob, oc, xs, mods_s, w_out16, ln1_g[l], ln1_b[l], w_r16, l, tm)
        lat = dict(x1=x1, h2=h2, aff=aff, mods=mods_s, tm=tm)

        xc, xs = _channel_mixers([ctx, lat], l, w_gate_up, w_down, ln2_g, ln2_b)

    return (xc, xs, jnp.stack(a_k, 1), jnp.stack(a_v, 1), jnp.stack(b_k, 1), jnp.stack(b_v, 1),
            jnp.stack(st, 1))
```

```python
import functools

import numpy as np
import jax
import jax.numpy as jnp
from jax import lax
from jax.experimental import pallas as pl
from jax.experimental.pallas import tpu as pltpu

F32 = jnp.float32
BF16 = jnp.bfloat16
I32 = jnp.int32

D_MODEL = 1024
DEPTH = 2
GRID_W = 64
HEAD_DIM = 64
H_A, KVH_A, H_B, H_C = 8, 2, 4, 4
D_IN = 2560
BLOCK = 128
WINDOW = 128
NA_ROWS, NA_COLS = 8, 16
NA_WIN_BLOCKS = 5
NA_QBLOCKS = 4
WIN_QBLOCKS = 2
CTX_GROUP_A = 4
CTX_GROUP_B = 8
ROPE_BASE = 10000.0
N_EXPERTS = 16
CAPACITY_FACTOR = 2
D_FF = 1024
LN_EPS = 1e-6
ALPHA = (2 * DEPTH) ** 0.25
NEG = -1e30
LANES = 128
QK_SCALE = HEAD_DIM ** -0.5
LOG2E = float(np.log2(np.e))
ATTN_SCALE = QK_SCALE * LOG2E
VMEM_LIMIT = 56 * 1024 * 1024
ROW_TILE = 1024
OUTPROJ_PARTS = 2
RET_GROUP_CHUNKS = 8
SCATTER_CONTRACTION = 4096
PROJ_COL_CHUNK = 512
VARIANT_ROWS = 256
SELECT_ELEMS = 128 * 2048
ONEHOT_GROUP_TOKENS = 512

COL_QA, COL_KA, COL_VA = 0, 4, 5
COL_QB, COL_KB, COL_VB = 3, 4, 5
COL_QC, COL_KC, COL_VC, COL_GC = 6, 7, 8, 9


def _cp(n_axes):
    return pltpu.CompilerParams(dimension_semantics=("arbitrary",) * n_axes, vmem_limit_bytes=VMEM_LIMIT)


def _dot(a, b):
    return jnp.dot(a, b, preferred_element_type=F32)


def _dot_t(a, b):
    return lax.dot_general(a, b, (((1,), (1,)), ((), ())), preferred_element_type=F32)


def _silu(x):
    return x / (1.0 + jnp.exp(-x))


def _lane_low(shape):
    return lax.broadcasted_iota(I32, shape, len(shape) - 1) < HEAD_DIM


def _layernorm(y, g, b):
    mu = jnp.mean(y, axis=-1, keepdims=True)
    d = y - mu
    var = jnp.mean(d * d, axis=-1, keepdims=True)
    return d * lax.rsqrt(var + LN_EPS) * g + b


def _mod_kernel(c_ref, w_ref, b_ref, o_ref):
    s = _silu(c_ref[...]).astype(BF16)
    o_ref[...] = _dot(s, w_ref[...].astype(BF16)) + b_ref[...]


def _modulation(cond, w_ada, b_ada):
    tn = 1536
    return pl.pallas_call(
        _mod_kernel,
        out_shape=jax.ShapeDtypeStruct((DEPTH, 8, 6 * D_MODEL), F32),
        grid=(DEPTH, 6 * D_MODEL // tn),
        in_specs=[pl.BlockSpec((8, D_MODEL), lambda l, j: (0, 0)),
                  pl.BlockSpec((None, D_MODEL, tn), lambda l, j: (l, 0, j)),
                  pl.BlockSpec((None, 1, tn), lambda l, j: (l, 0, j))],
        out_specs=pl.BlockSpec((None, 8, tn), lambda l, j: (l, 0, j)),
        compiler_params=_cp(2), name="modulation",
    )(cond, w_ada, b_ada.reshape(DEPTH, 1, 6 * D_MODEL))


def _mod_spec(k, n_grid):
    if n_grid == 1:
        return pl.BlockSpec((None, 1, D_MODEL), lambda r: (r, 0, k))
    return pl.BlockSpec((None, 1, D_MODEL), lambda r, i: (r, 0, k))


def _rope_tables(L):
    t = np.arange(L)
    pos = (t // GRID_W, t % GRID_W)
    half = HEAD_DIM // 2
    inv = 1.0 / (ROPE_BASE ** (np.arange(0, half, 2) / half))
    cos = np.zeros((L, HEAD_DIM)); s_up = np.zeros((L, HEAD_DIM)); s_dn = np.zeros((L, HEAD_DIM))
    for part in range(2):
        ang = (pos[part][:, None] * inv[None, :]).astype(np.float32).astype(np.float64)
        q = half // 2
        base = part * half
        cos[:, base:base + q] = np.cos(ang); cos[:, base + q:base + half] = np.cos(ang)
        s_up[:, base:base + q] = -np.sin(ang)
        s_dn[:, base + q:base + half] = np.sin(ang)
    tile = lambda a: jnp.asarray(np.tile(a, (1, LANES // HEAD_DIM)), F32)
    return tile(cos), tile(s_up), tile(s_dn)


KV_COLUMNS = ((512, 128), (640, 128), (1024, 256), (1280, 256))


def _proj_kernel(*refs, rope, emit_kv):
    kv_refs = ()
    if emit_kv:
        refs, kv_refs = refs[:-len(KV_COLUMNS)], refs[-len(KV_COLUMNS):]
    if rope:
        x_ref, sc_ref, sh_ref, w_ref, cos_ref, sup_ref, sdn_ref, o_ref = refs
    else:
        x_ref, sc_ref, sh_ref, w_ref, o_ref = refs
    h = (x_ref[...] * (1.0 + sc_ref[...]) + sh_ref[...]).astype(BF16)
    n_rope = (H_A + KVH_A) * HEAD_DIM
    step = PROJ_COL_CHUNK
    for j in range(0, D_IN, step):
        acc = _dot(h, w_ref[:, j:j + step].astype(BF16))
        if rope and j < n_rope:
            q = HEAD_DIM // 4
            for c in range(0, step, LANES):
                a = acc[:, c:c + LANES]
                if j + c < n_rope:
                    a = (a * cos_ref[...] + pltpu.roll(a, LANES - q, 1) * sup_ref[...]
                         + pltpu.roll(a, q, 1) * sdn_ref[...])
                o_ref[:, j + c:j + c + LANES] = a
        else:
            o_ref[:, j:j + step] = acc
        for ref, (c0, w) in zip(kv_refs, KV_COLUMNS):
            if j <= c0 < j + step:
                n_req, _, seq = ref.shape
                for g in range(n_req):
                    ref[g] = acc[g * seq:(g + 1) * seq, c0 - j:c0 - j + w].T


def _project(x, mods, w_in, layer, tm, rope_tabs=None, kv_seq=None):
    R, L, _ = x.shape
    rope = rope_tabs is not None
    emit_kv = kv_seq is not None
    assert not (rope and emit_kv)
    out_shape = [jax.ShapeDtypeStruct((R, L, D_IN), F32)]
    out_specs = [pl.BlockSpec((None, tm, D_IN), lambda r, i: (r, i, 0))]
    if emit_kv:
        assert R == 1 and tm % kv_seq == 0
        out_shape += [jax.ShapeDtypeStruct((L // kv_seq, w, kv_seq), F32) for _, w in KV_COLUMNS]
        out_specs += [pl.BlockSpec((tm // kv_seq, w, kv_seq), lambda r, i: (i, 0, 0)) for _, w in KV_COLUMNS]
    in_specs = [pl.BlockSpec((None, tm, D_MODEL), lambda r, i: (r, i, 0)),
                _mod_spec(1, 2), _mod_spec(0, 2),
                pl.BlockSpec((None, D_MODEL, D_IN), lambda r, i: (layer, 0, 0), pipeline_mode=pl.Buffered(1))]
    args = [x, mods, mods, w_in]
    if rope:
        in_specs += [pl.BlockSpec((tm, LANES), lambda r, i: (i, 0))] * 3
        args += list(rope_tabs)
    res = pl.pallas_call(
        functools.partial(_proj_kernel, rope=rope, emit_kv=emit_kv),
        out_shape=out_shape, grid=(R, L // tm), in_specs=in_specs, out_specs=out_specs,
        compiler_params=_cp(2), name="in_proj",
    )(*args)
    return res if emit_kv else res[0]


def _mask_head(x, sub):
    low = _lane_low(x.shape)
    return jnp.where(low, x, 0.0) if sub == 0 else jnp.where(low, 0.0, x)


def _key_variant(k, sub):
    return _mask_head(k, sub).astype(BF16)


def _value_variant(v, sub):
    low = _lane_low(v.shape)
    first = jnp.where(low, v, 1.0) if sub == 0 else jnp.where(low, 1.0, v)
    return jnp.concatenate([first, jnp.ones_like(v)], axis=1).astype(BF16)


def _attend(lhs, keys, values, biases, sink=None):
    scores = []
    for k, b in zip(keys, biases):
        s = _dot_t(lhs, k)
        scores.append(s if b is None else s + b)
    tiles = [s[:, c:c + LANES] for s in scores for c in range(0, s.shape[1], LANES)]
    while len(tiles) > 1:
        tiles = [jnp.maximum(a, b) for a, b in zip(tiles[0::2], tiles[1::2])] + ([tiles[-1]] if len(tiles) % 2 else [])
    m = jnp.broadcast_to(tiles[0].max(axis=-1, keepdims=True), tiles[0].shape)
    if sink is not None:
        m = jnp.maximum(m, sink)
    o = None
    for s, v in zip(scores, values):
        p = [jnp.exp2(s[:, c:c + LANES] - m).astype(BF16) for c in range(0, s.shape[1], LANES)]
        pv = _dot(p[0] if len(p) == 1 else jnp.concatenate(p, axis=1), v)
        o = pv if o is None else o + pv
    return o, (None if sink is None else jnp.exp2(sink - m))


def _normalised(o):
    return o[:, :LANES] / o[:, LANES:]


def _gqa_variants(k, v, hk, wide):
    low = _lane_low(k.shape)
    k_sw, v_sw = pltpu.roll(k, HEAD_DIM, 1), pltpu.roll(v, HEAD_DIM, 1)
    k_dup = jnp.where(low, k, k_sw) if hk == 0 else jnp.where(low, k_sw, k)
    v_low, v_high = (v, v_sw) if hk == 0 else (v_sw, v)
    va = jnp.where(low, v_low, 1.0)
    if wide:
        va = jnp.concatenate([va, jnp.where(low, 1.0, v_high)], axis=1)
    return k_dup.astype(BF16), va.astype(BF16)


def _fill_gqa_variants(k_src, v_src, kv_ref, vv_ref, off, n, feature_major=False):
    step = VARIANT_ROWS
    for c in range(0, n, step):
        k = k_src[:, c:c + step].T if feature_major else k_src[c:c + step, :]
        v = v_src[:, c:c + step].T if feature_major else v_src[c:c + step, :]
        for hk in range(KVH_A):
            kd, va = _gqa_variants(k, v, hk, wide=False)
            kv_ref[hk, off + c:off + c + step, :] = kd
            vv_ref[hk, off + c:off + c + step, :] = va


def _fill_mha_variants(k_src, v_src, kv_ref, vv_ref, off, n, feature_major=False):
    step = VARIANT_ROWS
    for c in range(0, n, step):
        rows = slice(off + c, off + c + step)
        for p in range(H_B // 2):
            cs = slice(LANES * p, LANES * p + LANES)
            k = k_src[cs, c:c + step].T if feature_major else k_src[c:c + step, cs]
            v = v_src[cs, c:c + step].T if feature_major else v_src[c:c + step, cs]
            for sub in range(2):
                kv_ref[p, sub, rows, :] = _key_variant(k, sub)
                vv_ref[p, sub, rows, :] = _value_variant(v, sub)


def _sink_col(sink_ref, heads, rows):
    return jnp.concatenate([jnp.full((rows, LANES), sink_ref[h] * LOG2E, F32) for h in heads], axis=0)


def _gqa_attend(sink_ref, q_ref, o_ref, hk, keys, values, biases):
    rows = q_ref.shape[0]
    g = H_A // KVH_A
    heads = list(range(g * hk, g * hk + g))
    lhs = jnp.concatenate([_mask_head(q_ref[:, LANES * (h // 2):LANES * (h // 2) + LANES], h % 2) for h in heads],
                          axis=0)
    lhs = (lhs * ATTN_SCALE).astype(BF16)
    stacked = [None if b is None else jnp.concatenate([b] * g, axis=0) for b in biases]
    o, extra = _attend(lhs, keys, values, stacked, _sink_col(sink_ref, heads, rows))
    low_all = _lane_low((g * rows, LANES))
    low = _lane_low((rows, LANES))
    first = o[:, :LANES] + jnp.where(low_all, 0.0, extra)
    if o.shape[1] == LANES:
        out = first / pltpu.roll(first, HEAD_DIM, 1)
    else:
        second = o[:, LANES:] + jnp.where(low_all, extra, 0.0)
    for j in range(0, g, 2):
        h = heads[j]
        even, odd = slice(j * rows, (j + 1) * rows), slice((j + 1) * rows, (j + 2) * rows)
        if o.shape[1] == LANES:
            pair = jnp.where(low, out[even], pltpu.roll(out[odd], HEAD_DIM, 1))
        else:
            pair = jnp.where(low, first[even] / second[even], second[odd] / first[odd])
        o_ref[:, LANES * (h // 2):LANES * (h // 2) + LANES] = pair.astype(o_ref.dtype)


def _attn_a_ctx_kernel(sink_ref, q_ref, k_ref, v_ref, o_ref):
    for g in range(q_ref.shape[0]):
        for hk in range(KVH_A):
            kd, va = _gqa_variants(k_ref[g], v_ref[g], hk, wide=True)
            _gqa_attend(sink_ref, q_ref.at[g], o_ref.at[g], hk, [kd], [va], [None])


def _window_mask_table():
    nq = WIN_QBLOCKS
    row = np.arange(nq * BLOCK)[:, None]
    col = np.arange((nq + 2) * BLOCK)[None, :]
    tabs = [np.where(np.abs(col - off * BLOCK - row) <= WINDOW, 0.0, NEG) for off in range(3)]
    return jnp.asarray(np.stack(tabs), F32)


def _attn_a_lat_kernel(sink_ref, q_ref, k_ref, v_ref, kc_ref, vc_ref, mask_ref, o_ref, kv_ref, vv_ref, *, nb):
    i = pl.program_id(1)
    L, P = k_ref.shape[0], kc_ref.shape[1]

    @pl.when(i == 0)
    def _():
        _fill_gqa_variants(k_ref, v_ref, kv_ref, vv_ref, 0, L)
        _fill_gqa_variants(kc_ref, vc_ref, kv_ref, vv_ref, L, P, feature_major=True)

    nq = WIN_QBLOCKS
    wk = (nq + 2) * BLOCK
    sblk = jnp.clip(nq * i - 1, 0, nb - (nq + 2))
    start = pl.multiple_of(sblk * BLOCK, BLOCK)
    bias = mask_ref[nq * i - sblk]
    for hk in range(KVH_A):
        ks = [kv_ref[hk, pl.ds(start, wk), :], kv_ref[hk, L:L + P, :]]
        vs = [vv_ref[hk, pl.ds(start, wk), :], vv_ref[hk, L:L + P, :]]
        _gqa_attend(sink_ref, q_ref, o_ref, hk, ks, vs, [bias, None])


def _smem_spec():
    return pl.BlockSpec(memory_space=pltpu.SMEM)


def _attn_a_ctx(qkv, sink):
    R, L, _ = qkv.shape
    wq, wkv = H_A * HEAD_DIM, KVH_A * HEAD_DIM
    G = CTX_GROUP_A
    assert R % G == 0
    return pl.pallas_call(
        _attn_a_ctx_kernel,
        out_shape=jax.ShapeDtypeStruct((R, L, wq), F32),
        grid=(R // G,),
        in_specs=[_smem_spec(),
                  pl.BlockSpec((G, L, wq), lambda r: (r, 0, COL_QA)),
                  pl.BlockSpec((G, L, wkv), lambda r: (r, 0, COL_KA)),
                  pl.BlockSpec((G, L, wkv), lambda r: (r, 0, COL_VA))],
        out_specs=pl.BlockSpec((G, L, wq), lambda r: (r, 0, 0)),
        compiler_params=_cp(1), name="attn_a_ctx",
    )(sink, qkv, qkv, qkv)


def _attn_a_lat(qkv, sink, cache_k, cache_v, layer):
    R, L, _ = qkv.shape
    nb = L // BLOCK
    nq = WIN_QBLOCKS
    assert nb % nq == 0 and nb >= nq + 2
    P = cache_k.shape[3]
    wq, wkv = H_A * HEAD_DIM, KVH_A * HEAD_DIM
    cache_spec = pl.BlockSpec((None, None, wkv, P), lambda r, i: (r, layer, 0, 0))
    masks = _window_mask_table()
    return pl.pallas_call(
        functools.partial(_attn_a_lat_kernel, nb=nb),
        out_shape=jax.ShapeDtypeStruct((R, L, wq), F32),
        grid=(R, nb // nq),
        in_specs=[_smem_spec(),
                  pl.BlockSpec((None, nq * BLOCK, wq), lambda r, i: (r, i, COL_QA)),
                  pl.BlockSpec((None, L, wkv), lambda r, i: (r, 0, COL_KA)),
                  pl.BlockSpec((None, L, wkv), lambda r, i: (r, 0, COL_VA)),
                  cache_spec, cache_spec,
                  pl.BlockSpec(masks.shape, lambda r, i: (0, 0, 0))],
        out_specs=pl.BlockSpec((None, nq * BLOCK, wq), lambda r, i: (r, i, 0)),
        scratch_shapes=[pltpu.VMEM((KVH_A, L + P, LANES), BF16),
                        pltpu.VMEM((KVH_A, L + P, LANES), BF16)],
        compiler_params=_cp(2), name="attn_a_lat",
    )(sink, qkv, qkv, qkv, cache_k, cache_v, masks)


def _attn_b_ctx_kernel(q_ref, k_ref, v_ref, o_ref):
    G, L = q_ref.shape[:2]
    low = _lane_low((L, LANES))
    for g in range(G):
        for p in range(H_B // 2):
            cs = slice(LANES * p, LANES * p + LANES)
            lhs = (q_ref[g, :, cs] * ATTN_SCALE).astype(BF16)
            k, v = k_ref[g, :, cs], v_ref[g, :, cs]
            outs = [_normalised(_attend(lhs, [_key_variant(k, sub)], [_value_variant(v, sub)], [None])[0])
                    for sub in range(2)]
            o_ref[g, :, cs] = jnp.where(low, outs[0], outs[1]).astype(o_ref.dtype)


def _bias_table_kernel(rpb_ref, o_ref, t_ref):
    n_dr, n_dc = 2 * NA_ROWS - 1, 2 * NA_COLS - 1
    n_t = o_ref.shape[1]
    shape = (GRID_W, LANES)
    cq = lax.broadcasted_iota(I32, shape, 0)
    lane = lax.broadcasted_iota(I32, shape, 1)
    right = lane >= GRID_W
    ck = lane & (GRID_W - 1)
    cs = jnp.clip(cq - NA_COLS // 2, 0, GRID_W - NA_COLS)
    col_ok = (ck >= cs) & (ck < cs + NA_COLS)
    dc = ck - cq + (NA_COLS - 1)
    for j in (0, 1, n_t - 1, n_t):
        t_ref[j] = jnp.zeros(shape, F32)

    def build(r, carry, h):
        acc = jnp.zeros(shape, F32)
        for d in range(n_dc):
            acc = jnp.where(dc == d, rpb_ref[(h * n_dr + r) * n_dc + d] * LOG2E, acc)
        t_ref[r + 2] = jnp.where(col_ok, acc, NEG)
        return carry

    def compose(t, carry, h):
        o_ref[h, t] = jnp.where(right, t_ref[t + 1], t_ref[t])
        return carry

    for h in range(H_B):
        lax.fori_loop(0, n_dr, functools.partial(build, h=h), 0)
        lax.fori_loop(0, n_t, functools.partial(compose, h=h), 0)


def _bias_table(rpb_l):
    n_t = 2 * (NA_ROWS + 1)
    return pl.pallas_call(
        _bias_table_kernel,
        out_shape=jax.ShapeDtypeStruct((H_B, n_t, GRID_W, LANES), F32),
        in_specs=[_smem_spec()],
        out_specs=pl.BlockSpec(memory_space=pltpu.VMEM),
        scratch_shapes=[pltpu.VMEM((n_t + 1, GRID_W, LANES), F32)],
        name="na_bias_table",
    )(rpb_l.reshape(-1))


def _attn_b_lat_kernel(q_ref, k_ref, v_ref, kc_ref, vc_ref, tp_ref, o_ref, kv_ref, vv_ref, *, nb, rows_total):
    i = pl.program_id(1)
    L, P = k_ref.shape[0], kc_ref.shape[1]

    @pl.when(i == 0)
    def _():
        _fill_mha_variants(k_ref, v_ref, kv_ref, vv_ref, 0, L)
        _fill_mha_variants(kc_ref, vc_ref, kv_ref, vv_ref, L, P, feature_major=True)

    nq = NA_QBLOCKS
    nwin = NA_WIN_BLOCKS + nq - 1
    wk = nwin * BLOCK
    rows = nq * BLOCK
    sblk = jnp.clip(nq * i - 2, 0, nb - nwin)
    start = pl.multiple_of(sblk * BLOCK, BLOCK)
    shape = (rows, wk)
    rq = 2 * nq * i + (lax.broadcasted_iota(I32, shape, 0) >> 6)
    rk = 2 * sblk + (lax.broadcasted_iota(I32, shape, 1) >> 6)
    rs = jnp.clip(rq - NA_ROWS // 2, 0, rows_total - NA_ROWS)
    row_bias = jnp.where((rk >= rs) & (rk < rs + NA_ROWS), 0.0, NEG)
    n_t = tp_ref.shape[1]
    low = _lane_low((rows, LANES))
    for p in range(H_B // 2):
        cs = slice(LANES * p, LANES * p + LANES)
        lhs = (q_ref[:, cs] * ATTN_SCALE).astype(BF16)
        outs = []
        for sub in range(2):
            h = 2 * p + sub
            strips = []
            for qq in range(nq):
                tiles = []
                for kb in range(nwin):
                    t = jnp.clip(2 * (sblk + kb - (nq * i + qq)) + NA_ROWS + 1, 1, n_t - 1)
                    tiles.append(jnp.concatenate([tp_ref[h, t], tp_ref[h, t - 1]], axis=0))
                strips.append(jnp.concatenate(tiles, axis=1))
            bias = jnp.concatenate(strips, axis=0) + row_bias
            ks = [kv_ref[p, sub, pl.ds(start, wk), :], kv_ref[p, sub, L:L + P, :]]
            vs = [vv_ref[p, sub, pl.ds(start, wk), :], vv_ref[p, sub, L:L + P, :]]
            outs.append(_normalised(_attend(lhs, ks, vs, [bias, None])[0]))
        o_ref[:, cs] = jnp.where(low, outs[0], outs[1]).astype(o_ref.dtype)


def _attn_b_ctx(qkv):
    R, L, _ = qkv.shape
    w = H_B * HEAD_DIM
    G = CTX_GROUP_B
    assert R % G == 0
    return pl.pallas_call(
        _attn_b_ctx_kernel,
        out_shape=jax.ShapeDtypeStruct((R, L, w), F32),
        grid=(R // G,),
        in_specs=[pl.BlockSpec((G, L, w), lambda r: (r, 0, COL_QB)),
                  pl.BlockSpec((G, L, w), lambda r: (r, 0, COL_KB)),
                  pl.BlockSpec((G, L, w), lambda r: (r, 0, COL_VB))],
        out_specs=pl.BlockSpec((G, L, w), lambda r: (r, 0, 0)),
        compiler_params=_cp(1), name="attn_b_ctx",
    )(qkv, qkv, qkv)


def _attn_b_lat(qkv, cache_k, cache_v, bias_tab, layer):
    R, L, _ = qkv.shape
    nb = L // BLOCK
    P = cache_k.shape[3]
    w = H_B * HEAD_DIM
    cache_spec = pl.BlockSpec((None, None, w, P), lambda r, i: (r, layer, 0, 0))
    rows = NA_QBLOCKS * BLOCK
    assert nb % NA_QBLOCKS == 0 and nb >= NA_WIN_BLOCKS + NA_QBLOCKS - 1
    return pl.pallas_call(
        functools.partial(_attn_b_lat_kernel, nb=nb, rows_total=L // GRID_W),
        out_shape=jax.ShapeDtypeStruct((R, L, w), F32),
        grid=(R, nb // NA_QBLOCKS),
        in_specs=[pl.BlockSpec((None, rows, w), lambda r, i: (r, i, COL_QB)),
                  pl.BlockSpec((None, L, w), lambda r, i: (r, 0, COL_KB)),
                  pl.BlockSpec((None, L, w), lambda r, i: (r, 0, COL_VB)),
                  cache_spec, cache_spec,
                  pl.BlockSpec(bias_tab.shape, lambda r, i: (0, 0, 0, 0))],
        out_specs=pl.BlockSpec((None, rows, w), lambda r, i: (r, i, 0)),
        scratch_shapes=[pltpu.VMEM((H_B // 2, 2, L + P, LANES), BF16),
                        pltpu.VMEM((H_B // 2, 2, L + P, 2 * LANES), BF16)],
        compiler_params=_cp(2), name="attn_b_lat",
    )(qkv, qkv, qkv, cache_k, cache_v, bias_tab)


T_ZF, T_ZB, T_XF, T_XB, T_GF, T_GB, T_MA, T_MB = range(8)


def _ret_kernel(*refs, nc, has_s0, out_state):
    refs = list(refs)
    dec_ref, q_ref, k_ref, v_ref, g_ref, gn_ref = refs[:6]
    pos = 6
    s0_ref = None
    if has_s0:
        s0_ref = refs[pos]; pos += 1
    o_ref = refs[pos]; pos += 1
    st_ref = None
    if out_state:
        st_ref = refs[pos]; pos += 1
    sf_ref, sb_ref, tab_ref = refs[pos], refs[pos + 1], refs[pos + 2]

    sq = (BLOCK, LANES)
    low = _lane_low(sq)
    row_low = lax.broadcasted_iota(I32, sq, 0) < HEAD_DIM
    same_head = low == row_low
    n_pairs = H_C // 2

    @pl.when(pl.program_id(0) == 0)
    def _():
        ri = lax.broadcasted_iota(I32, sq, 0).astype(F32)
        ci = lax.broadcasted_iota(I32, sq, 1).astype(F32)

        def log_gamma(d, h):
            return -jnp.exp(jnp.full(sq, dec_ref[d, h], F32))

        for p in range(n_pairs):
            lf = [log_gamma(0, 2 * p), log_gamma(0, 2 * p + 1)]
            lb = [log_gamma(1, 2 * p), log_gamma(1, 2 * p + 1)]
            lf_lane, lb_lane = jnp.where(low, lf[0], lf[1]), jnp.where(low, lb[0], lb[1])
            lf_row, lb_row = jnp.where(row_low, lf[0], lf[1]), jnp.where(row_low, lb[0], lb[1])
            tab_ref[p, T_ZF] = jnp.exp(lf_lane * (BLOCK - 1.0 - ri))
            tab_ref[p, T_ZB] = jnp.exp(lb_lane * ri)
            tab_ref[p, T_XF] = jnp.exp(lf_lane * (ri + 1.0))
            tab_ref[p, T_XB] = jnp.exp(lb_lane * (BLOCK - ri))
            tab_ref[p, T_GF] = jnp.exp(lf_row * float(BLOCK))
            tab_ref[p, T_GB] = jnp.exp(lb_row * float(BLOCK))
            rel = ri - ci
            for sub in range(2):
                fwd = jnp.where(rel >= 0, jnp.exp(lf[sub] * jnp.maximum(rel, 0.0)), 0.0)
                bwd = jnp.where(rel <= 0, jnp.exp(lb[sub] * jnp.maximum(-rel, 0.0)), 0.0)
                tab_ref[p, T_MA + sub] = fwd + bwd

    n_req = q_ref.shape[0]
    units = [(r, p) for r in range(n_req) for p in range(n_pairs)]

    def chunk(ref, r, c, p):
        r0 = pl.multiple_of(c * BLOCK, BLOCK)
        return ref[r, pl.ds(r0, BLOCK), LANES * p:LANES * p + LANES]

    unroll = min(nc, 8)

    def inc_body(c, carry):
        for r, p in units:
            kt = (chunk(k_ref, r, c, p) * QK_SCALE).T.astype(BF16)
            v = chunk(v_ref, r, c, p)
            zv = jnp.concatenate([v * tab_ref[p, T_ZF], v * tab_ref[p, T_ZB]], axis=1).astype(BF16)
            inc = _dot(kt, zv)
            sf_ref[r, c, p] = jnp.where(same_head, inc[:, :LANES], 0.0)
            sb_ref[r, c, p] = jnp.where(same_head, inc[:, LANES:], 0.0)
        return carry

    lax.fori_loop(0, nc, inc_body, 0, unroll=unroll)

    def scan_body(t, S, ref, tg, reverse):
        c = nc - 1 - t if reverse else t
        new = []
        for (r, p), state in zip(units, S):
            inc = ref[r, c, p]
            ref[r, c, p] = state
            new.append(tab_ref[p, tg] * state + inc)
        return tuple(new)

    zero = jnp.zeros(sq, F32)
    s_f0 = tuple(s0_ref[r, 0, p] if has_s0 else zero for r, p in units)
    s_b0 = tuple(s0_ref[r, 1, p] if has_s0 else zero for r, p in units)
    scan_unroll = nc if nc <= 4 else 1
    s_f = lax.fori_loop(0, nc, functools.partial(scan_body, ref=sf_ref, tg=T_GF, reverse=False), s_f0,
                        unroll=scan_unroll)
    s_b = lax.fori_loop(0, nc, functools.partial(scan_body, ref=sb_ref, tg=T_GB, reverse=True), s_b0,
                        unroll=scan_unroll)
    if out_state:
        for (r, p), f_state, b_state in zip(units, s_f, s_b):
            st_ref[r, 0, p] = f_state
            st_ref[r, 1, p] = b_state

    def out_body(c, carry):
        for r, p in units:
            q = chunk(q_ref, r, c, p)
            k16 = (chunk(k_ref, r, c, p) * QK_SCALE).astype(BF16)
            v = chunk(v_ref, r, c, p)
            a = _dot_t(jnp.concatenate([_mask_head(q, 0), _mask_head(q, 1)], axis=0).astype(BF16), k16)
            lhs = jnp.concatenate([a[:BLOCK] * tab_ref[p, T_MA], a[BLOCK:] * tab_ref[p, T_MA + 1],
                                   q * tab_ref[p, T_XF], q * tab_ref[p, T_XB]], axis=1).astype(BF16)
            rhs = jnp.concatenate([_mask_head(v, 0), _mask_head(v, 1), sf_ref[r, c, p], sb_ref[r, c, p]],
                                  axis=0).astype(BF16)
            o = _dot(lhs, rhs)
            mu = jnp.where(low, _mask_head(o, 0).sum(-1, keepdims=True),
                           _mask_head(o, 1).sum(-1, keepdims=True)) * (1.0 / HEAD_DIM)
            d = o - mu
            d2 = d * d
            var = jnp.where(low, _mask_head(d2, 0).sum(-1, keepdims=True),
                            _mask_head(d2, 1).sum(-1, keepdims=True)) * (1.0 / HEAD_DIM)
            cs = slice(LANES * p, LANES * p + LANES)
            on = d * lax.rsqrt(var + LN_EPS) * gn_ref[:, cs]
            r0 = pl.multiple_of(c * BLOCK, BLOCK)
            o_ref[r, pl.ds(r0, BLOCK), cs] = (_silu(chunk(g_ref, r, c, p)) * on).astype(o_ref.dtype)
        return carry

    lax.fori_loop(0, nc, out_body, 0, unroll=unroll)


def _retention(qkv, decay, gn_w, s0=None, layer=0, out_state=False):
    R, L, _ = qkv.shape
    nc = L // BLOCK
    w = H_C * HEAD_DIM
    n_pairs = H_C // 2
    G = max(1, min(R, RET_GROUP_CHUNKS // nc))
    assert R % G == 0
    col = lambda cb: pl.BlockSpec((G, L, w), lambda r: (r, 0, cb))
    in_specs = [_smem_spec(), col(COL_QC), col(COL_KC), col(COL_VC), col(COL_GC),
                pl.BlockSpec((1, w), lambda r: (0, 0))]
    args = [decay, qkv, qkv, qkv, qkv, gn_w.reshape(1, w)]
    if s0 is not None:
        in_specs.append(pl.BlockSpec((G, None, 2, n_pairs, BLOCK, LANES), lambda r: (r, layer, 0, 0, 0, 0)))
        args.append(s0)
    out_shape = [jax.ShapeDtypeStruct((R, L, w), F32)]
    out_specs = [pl.BlockSpec((G, L, w), lambda r: (r, 0, 0))]
    if out_state:
        out_shape.append(jax.ShapeDtypeStruct((R, 2, n_pairs, BLOCK, LANES), F32))
        out_specs.append(pl.BlockSpec((G, 2, n_pairs, BLOCK, LANES), lambda r: (r, 0, 0, 0, 0)))
    res = pl.pallas_call(
        functools.partial(_ret_kernel, nc=nc, has_s0=s0 is not None, out_state=out_state),
        out_shape=out_shape, grid=(R // G,), in_specs=in_specs, out_specs=out_specs,
        scratch_shapes=[pltpu.VMEM((G, nc, n_pairs, BLOCK, LANES), F32),
                        pltpu.VMEM((G, nc, n_pairs, BLOCK, LANES), F32),
                        pltpu.VMEM((n_pairs, 8, BLOCK, LANES), F32)],
        compiler_params=_cp(1), name="retention",
    )(*args)
    return res if out_state else res[0]


def _pair_states_to_heads(st):
    a = st[..., :HEAD_DIM, :HEAD_DIM]
    b = st[..., HEAD_DIM:, HEAD_DIM:]
    return jnp.stack([a, b], axis=3).reshape(st.shape[0], 2, H_C, HEAD_DIM, HEAD_DIM)


def _heads_to_pair_states(s):
    s = s.reshape(*s.shape[:3], H_C // 2, 2, HEAD_DIM, HEAD_DIM)
    z = jnp.zeros_like(s[..., 0, :, :])
    top = jnp.concatenate([s[..., 0, :, :], z], axis=-1)
    bot = jnp.concatenate([z, s[..., 1, :, :]], axis=-1)
    return jnp.concatenate([top, bot], axis=-2)


def _outproj_kernel(oa_ref, ob_ref, oc_ref, x_ref, g1_ref, sc2_ref, sh2_ref, w_ref, lg_ref, lb_ref, wr_ref,
                    x1_ref, h2_ref, aff_ref):
    wa, wb = H_A * HEAD_DIM, H_B * HEAD_DIM
    tm = x_ref.shape[0]
    part = tm // OUTPROJ_PARTS
    for r0 in range(0, tm, part):
        rs = slice(r0, r0 + part)
        mix = (_dot(oa_ref[rs, :].astype(BF16), w_ref[0:wa, :])
               + _dot(ob_ref[rs, :].astype(BF16), w_ref[wa:wa + wb, :])
               + _dot(oc_ref[rs, :].astype(BF16), w_ref[wa + wb:, :]))
        x1 = _layernorm(ALPHA * x_ref[rs, :] + g1_ref[...] * mix, lg_ref[...], lb_ref[...])
        x1_ref[rs, :] = x1
        h2 = (x1 * (1.0 + sc2_ref[...]) + sh2_ref[...]).astype(BF16)
        h2_ref[rs, :] = h2
        logits = _dot(h2, wr_ref[...])
        e = jnp.exp(logits - logits.max(axis=-1, keepdims=True))
        aff_ref[rs, :] = e / e.sum(axis=-1, keepdims=True)


def _outproj(oa, ob, oc, x, mods, w_out_bf16, ln_g, ln_b, w_router_bf16, layer, tm):
    R, L, _ = x.shape
    row = lambda w: pl.BlockSpec((None, tm, w), lambda r, i: (r, i, 0))
    vec = pl.BlockSpec((1, D_MODEL), lambda r, i: (0, 0))
    return pl.pallas_call(
        _outproj_kernel,
        out_shape=[jax.ShapeDtypeStruct((R, L, D_MODEL), F32),
                   jax.ShapeDtypeStruct((R, L, D_MODEL), BF16),
                   jax.ShapeDtypeStruct((R, L, N_EXPERTS), F32)],
        grid=(R, L // tm),
        in_specs=[row(H_A * HEAD_DIM), row(H_B * HEAD_DIM), row(H_C * HEAD_DIM), row(D_MODEL),
                  _mod_spec(2, 2), _mod_spec(4, 2), _mod_spec(3, 2),
                  pl.BlockSpec((None, D_MODEL, D_MODEL), lambda r, i: (layer, 0, 0)), vec, vec,
                  pl.BlockSpec((None, D_MODEL, N_EXPERTS), lambda r, i: (layer, 0, 0))],
        out_specs=[row(D_MODEL), row(D_MODEL), row(N_EXPERTS)],
        compiler_params=_cp(2), name="out_proj_ln_router",
    )(oa, ob, oc, x, mods, mods, mods, w_out_bf16, ln_g.reshape(1, -1), ln_b.reshape(1, -1), w_router_bf16)


CUMSUM_CHUNK = 256


def _excl_cumsum_lanes(x):
    rows, n = x.shape
    w = min(CUMSUM_CHUNK, n)
    tri = (lax.broadcasted_iota(I32, (w, w), 0) < lax.broadcasted_iota(I32, (w, w), 1))
    tri = jnp.where(tri, 1.0, 0.0).astype(BF16)
    carry = jnp.zeros((rows, 1), F32)
    outs = []
    for c in range(0, n, w):
        xc = x[:, c:c + w]
        outs.append(_dot(xc.astype(BF16), tri) + carry)
        carry = carry + xc.sum(axis=-1, keepdims=True)
    return outs[0] if len(outs) == 1 else jnp.concatenate(outs, axis=1)


def _select_kernel(a_ref, pos_ref, *, cap):
    rb, ne, n = a_ref.shape
    rows = rb * ne
    a = a_ref[...].reshape(rows, n)
    capf = float(cap)

    def body(_, c):
        lo, hi = c
        mid = lo + ((hi - lo + 1) >> 1)
        cnt = jnp.where(a >= pltpu.bitcast(mid, F32), 1.0, 0.0).sum(axis=-1, keepdims=True)
        ok = cnt >= capf
        return jnp.where(ok, mid, lo), jnp.where(ok, hi, mid - 1)

    one_bits = 0x3F800000
    lo, _ = lax.fori_loop(0, 30, body, (jnp.zeros((rows, 1), I32), jnp.full((rows, 1), one_bits, I32)))
    thr = pltpu.bitcast(lo, F32)
    gt = jnp.where(a > thr, 1.0, 0.0)
    eq = jnp.where(a == thr, 1.0, 0.0)
    need = capf - gt.sum(axis=-1, keepdims=True)
    sel = gt + eq * jnp.where(_excl_cumsum_lanes(eq) < need, 1.0, 0.0)
    slot = _excl_cumsum_lanes(sel)
    pos_ref[...] = jnp.where(sel > 0.5, slot, -1.0).astype(I32).reshape(rb, ne, n)


def _select(aff_t, cap):
    R, ne, n = aff_t.shape
    rb = min(R, max(1, SELECT_ELEMS // (ne * n)))
    return pl.pallas_call(
        functools.partial(_select_kernel, cap=cap),
        out_shape=jax.ShapeDtypeStruct((R, ne, n), I32),
        grid=(R // rb,),
        in_specs=[pl.BlockSpec((rb, ne, n), lambda r: (r, 0, 0))],
        out_specs=pl.BlockSpec((rb, ne, n), lambda r: (r, 0, 0)),
        compiler_params=_cp(1), name="expert_select",
    )(aff_t)


def _gather_kernel(pos_ref, aff_ref, h_ref, x_ref, gate_ref, *, cap):
    n_req, eb, _, n = pos_ref.shape
    slot = lax.broadcasted_iota(I32, (cap, n), 0)
    for g in range(n_req):
        parts = []
        for j in range(eb):
            hit = slot == pos_ref[g, j]
            parts.append(jnp.where(hit, 1.0, 0.0).astype(BF16))
            gate_ref[j, g * cap:(g + 1) * cap, :] = jnp.where(hit, aff_ref[g, j], 0.0).sum(axis=-1, keepdims=True)
        onehot = parts[0] if eb == 1 else jnp.concatenate(parts, axis=0)
        xe = _dot(onehot, h_ref[g]).astype(BF16)
        for j in range(eb):
            x_ref[j, g * cap:(g + 1) * cap, :] = xe[j * cap:(j + 1) * cap]


def _gather(pos, aff_t, h2, cap):
    R, ne, n = pos.shape
    eb = max(1, min(ne, 2 * ROW_TILE // cap))
    G = max(1, min(R, ONEHOT_GROUP_TOKENS // n))
    assert R % G == 0
    row = pl.BlockSpec((G, eb, 1, n), lambda r, e: (r, e, 0, 0))
    return pl.pallas_call(
        functools.partial(_gather_kernel, cap=cap),
        out_shape=[jax.ShapeDtypeStruct((ne, R * cap, D_MODEL), BF16),
                   jax.ShapeDtypeStruct((ne, R * cap, 1), F32)],
        grid=(R // G, ne // eb),
        in_specs=[row, row, pl.BlockSpec((G, n, D_MODEL), lambda r, e: (r, 0, 0))],
        out_specs=[pl.BlockSpec((eb, G * cap, D_MODEL), lambda r, e: (e, r, 0)),
                   pl.BlockSpec((eb, G * cap, 1), lambda r, e: (e, r, 0))],
        compiler_params=_cp(2), name="expert_gather",
    )(pos.reshape(R, ne, 1, n), aff_t.reshape(R, ne, 1, n), h2)


FF_TILE = 512


def _ffn_kernel(x0_ref, x1_ref, wa_ref, wb_ref, wd_ref, g0_ref, g1_ref, y0_ref, y1_ref, acc_ref):
    f = pl.program_id(1)

    @pl.when(f == 0)
    def _():
        acc_ref[...] = jnp.zeros_like(acc_ref)

    x = jnp.concatenate([x0_ref[...], x1_ref[...]], axis=0)
    a = _dot(x, wa_ref[...].astype(BF16))
    b = _dot(x, wb_ref[...].astype(BF16))
    acc_ref[...] += _dot((_silu(a) * b).astype(BF16), wd_ref[...].astype(BF16))

    @pl.when(f == pl.num_programs(1) - 1)
    def _():
        s0 = x0_ref.shape[0]
        y0_ref[...] = (acc_ref[:s0, :] * g0_ref[...]).astype(BF16)
        y1_ref[...] = (acc_ref[s0:, :] * g1_ref[...]).astype(BF16)


def _experts(xe0, gates0, xe1, gates1, w_gate_up, w_down, layer):
    ne, S0, _ = xe0.shape
    S1 = xe1.shape[1]
    nf = D_FF // FF_TILE
    rows = lambda S, w: pl.BlockSpec((None, S, w), lambda e, f: (e, 0, 0))
    return pl.pallas_call(
        _ffn_kernel,
        out_shape=[jax.ShapeDtypeStruct((ne, S0, D_MODEL), BF16), jax.ShapeDtypeStruct((ne, S1, D_MODEL), BF16)],
        grid=(ne, nf),
        in_specs=[rows(S0, D_MODEL), rows(S1, D_MODEL),
                  pl.BlockSpec((None, None, D_MODEL, FF_TILE), lambda e, f: (layer, e, 0, f)),
                  pl.BlockSpec((None, None, D_MODEL, FF_TILE), lambda e, f: (layer, e, 0, nf + f)),
                  pl.BlockSpec((None, None, FF_TILE, D_MODEL), lambda e, f: (layer, e, f, 0)),
                  rows(S0, 1), rows(S1, 1)],
        out_specs=[rows(S0, D_MODEL), rows(S1, D_MODEL)],
        scratch_shapes=[pltpu.VMEM((S0 + S1, D_MODEL), F32)],
        compiler_params=_cp(2), name="expert_ffn",
    )(xe0, xe1, w_gate_up, w_gate_up, w_down, gates0, gates1)


def _scatter_kernel(pos_ref, y_ref, x1_ref, g2_ref, lg_ref, lb_ref, o_ref, *, cap):
    n_req, tm, ne = pos_ref.shape
    cw = max(cap, LANES)
    epc = cw // cap
    lane = lax.broadcasted_iota(I32, (tm, cw), 1)
    n_chunks = ne // epc
    per_dot = max(1, min(n_chunks, SCATTER_CONTRACTION // cw))
    for g in range(n_req):
        pos = pos_ref[g]

        def onehot_chunk(c):
            onehot = jnp.zeros((tm, cw), F32)
            for jj in range(epc):
                col = pos[:, c * epc + jj:c * epc + jj + 1]
                tgt = jnp.where(col >= 0, col + jj * cap, -1)
                onehot = onehot + jnp.where(lane == tgt, 1.0, 0.0)
            return onehot.astype(BF16)

        acc = None
        for c0 in range(0, n_chunks, per_dot):
            onehot = [onehot_chunk(c) for c in range(c0, c0 + per_dot)]
            onehot = onehot[0] if per_dot == 1 else jnp.concatenate(onehot, axis=1)
            yc = y_ref[c0 * epc:(c0 + per_dot) * epc, g * cap:(g + 1) * cap, :].reshape(per_dot * cw, D_MODEL)
            part = _dot(onehot, yc)
            acc = part if acc is None else acc + part
        o_ref[g] = _layernorm(ALPHA * x1_ref[g] + g2_ref[g] * acc, lg_ref[...], lb_ref[...])


def _scatter(pos_tok, y, x1, mods, ln_g, ln_b, cap, tm):
    R, n, ne = pos_tok.shape
    G = max(1, min(R, ONEHOT_GROUP_TOKENS // n))
    assert R % G == 0 and (G == 1 or tm == n)
    vec = pl.BlockSpec((1, D_MODEL), lambda r, i: (0, 0))
    return pl.pallas_call(
        functools.partial(_scatter_kernel, cap=cap),
        out_shape=jax.ShapeDtypeStruct((R, n, D_MODEL), F32),
        grid=(R // G, n // tm),
        in_specs=[pl.BlockSpec((G, tm, ne), lambda r, i: (r, i, 0)),
                  pl.BlockSpec((ne, G * cap, D_MODEL), lambda r, i: (0, r, 0)),
                  pl.BlockSpec((G, tm, D_MODEL), lambda r, i: (r, i, 0)),
                  pl.BlockSpec((G, 1, D_MODEL), lambda r, i: (r, 0, 5)), vec, vec],
        out_specs=pl.BlockSpec((G, tm, D_MODEL), lambda r, i: (r, i, 0)),
        compiler_params=_cp(2), name="expert_scatter_ln",
    )(pos_tok, y, x1, mods, ln_g.reshape(1, -1), ln_b.reshape(1, -1))


def _capacity(n):
    return CAPACITY_FACTOR * n // N_EXPERTS


def _route(h2, aff):
    cap = _capacity(h2.shape[1])
    aff_t = jnp.swapaxes(aff, 1, 2)
    pos = _select(aff_t, cap)
    xe, gates = _gather(pos, aff_t, h2, cap)
    return pos, xe, gates


def _channel_mixers(groups, l, w_gate_up, w_down, ln2_g, ln2_b):
    routed = [_route(g["h2"], g["aff"]) for g in groups]
    ys = _experts(routed[0][1], routed[0][2], routed[1][1], routed[1][2], w_gate_up, w_down, l)
    return [_scatter(jnp.swapaxes(r[0], 1, 2), y, g["x1"], g["mods"], ln2_g[l], ln2_b[l],
                     _capacity(g["h2"].shape[1]), g["tm"]) for g, r, y in zip(groups, routed, ys)]


def kernel(x_prompt, x_sample, cache_attn_a_k, cache_attn_a_v, cache_attn_b_k, cache_attn_b_v, state_ret, c, c_ctx,
           w_ada, b_ada, w_in, w_out, attn_sink, na_rpb, ret_decay, ret_gn, ln1_g, ln1_b, ln2_g, ln2_b,
           w_router, w_gate_up, w_down):
    n_ctx, L_ctx, _ = x_prompt.shape
    n_lat, L_lat, _ = x_sample.shape
    P = cache_attn_a_k.shape[2]

    cond = jnp.concatenate([c_ctx[None, :], c, jnp.zeros((8 - 1 - n_lat, D_MODEL), F32)], axis=0)
    mod = _modulation(cond, w_ada, b_ada)
    w_out16, w_r16 = w_out.astype(BF16), w_router.astype(BF16)
    rope_tabs = _rope_tables(L_lat)
    feature_major = lambda t: jnp.transpose(t, (0, 1, 3, 4, 2)).reshape(n_lat, DEPTH, -1, P)
    cak, cav = feature_major(cache_attn_a_k), feature_major(cache_attn_a_v)
    cbk, cbv = feature_major(cache_attn_b_k), feature_major(cache_attn_b_v)
    s0_pairs = _heads_to_pair_states(state_ret)

    xc, xs = x_prompt, x_sample
    tm = ROW_TILE
    a_k, a_v, b_k, b_v, st = [], [], [], [], []
    flat = lambda a: a.reshape(1, n_ctx * L_ctx, a.shape[-1])
    unflat = lambda a: a.reshape(n_ctx, L_ctx, a.shape[-1])
    for l in range(DEPTH):
        mod_c = mod[l, 0:1][:, None, :]
        mods_c = jnp.broadcast_to(mod_c, (n_ctx, 1, 6 * D_MODEL))
        qkv, ka, va, kb, vb = _project(flat(xc), mod_c, w_in, l, tm, kv_seq=L_ctx)
        qkv = unflat(qkv)
        oa = _attn_a_ctx(qkv, attn_sink[l])
        ob = _attn_b_ctx(qkv)
        oc, s_l = _retention(qkv, ret_decay[l], ret_gn[l], out_state=True)
        x1, h2, aff = [unflat(t) for t in _outproj(flat(oa), flat(ob), flat(oc), flat(xc), mod_c, w_out16,
                                                   ln1_g[l], ln1_b[l], w_r16, l, tm)]
        ctx = dict(x1=x1, h2=h2, aff=aff, mods=mods_c, tm=L_ctx)
        heads_last = lambda t, h: jnp.transpose(t.reshape(n_ctx, h, HEAD_DIM, L_ctx), (0, 3, 1, 2))
        a_k.append(heads_last(ka, KVH_A))
        a_v.append(heads_last(va, KVH_A))
        b_k.append(heads_last(kb, H_B))
        b_v.append(heads_last(vb, H_B))
        st.append(_pair_states_to_heads(s_l))

        mods_s = mod[l, 1:1 + n_lat][:, None, :]
        qkv = _project(xs, mods_s, w_in, l, tm, rope_tabs)
        oa = _attn_a_lat(qkv, attn_sink[l], cak, cav, l)
        ob = _attn_b_lat(qkv, cbk, cbv, _bias_table(na_rpb[l]), l)
        oc = _retention(qkv, ret_decay[l], ret_gn[l], s0=s0_pairs, layer=l)
        x1, h2, aff = _outproj(oa, ob, oc, xs, mods_s, w_out16, ln1_g[l], ln1_b[l], w_r16, l, tm)
        lat = dict(x1=x1, h2=h2, aff=aff, mods=mods_s, tm=tm)

        xc, xs = _channel_mixers([ctx, lat], l, w_gate_up, w_down, ln2_g, ln2_b)

    return (xc, xs, jnp.stack(a_k, 1), jnp.stack(a_v, 1), jnp.stack(b_k, 1), jnp.stack(b_v, 1),
            jnp.stack(st, 1))
```

```python
import functools

import numpy as np
import jax
import jax.numpy as jnp
from jax import lax
from jax.experimental import pallas as pl
from jax.experimental.pallas import tpu as pltpu

F32 = jnp.float32
BF16 = jnp.bfloat16
I32 = jnp.int32

D_MODEL = 1024
DEPTH = 2
GRID_W = 64
HEAD_DIM = 64
H_A, KVH_A, H_B, H_C = 8, 2, 4, 4
D_IN = 2560
BLOCK = 128
WINDOW = 128
NA_ROWS, NA_COLS = 8, 16
NA_WIN_BLOCKS = 5
NA_QBLOCKS = 4
WIN_QBLOCKS = 2
CTX_GROUP_A = 4
ROPE_BASE = 10000.0
N_EXPERTS = 16
CAPACITY_FACTOR = 2
D_FF = 1024
LN_EPS = 1e-6
ALPHA = (2 * DEPTH) ** 0.25
NEG = -1e30
LANES = 128
QK_SCALE = HEAD_DIM ** -0.5
LOG2E = float(np.log2(np.e))
ATTN_SCALE = QK_SCALE * LOG2E
VMEM_LIMIT = 56 * 1024 * 1024
ROW_TILE = 1024
OUTPROJ_PARTS = 2
RET_GROUP_CHUNKS = 8
SCATTER_CONTRACTION = 4096
PROJ_COL_CHUNK = 512
VARIANT_ROWS = 256
SELECT_ELEMS = 128 * 2048
ONEHOT_GROUP_TOKENS = 512

COL_QA, COL_KA, COL_VA = 0, 4, 5
COL_QB, COL_KB, COL_VB = 3, 4, 5
COL_QC, COL_KC, COL_VC, COL_GC = 6, 7, 8, 9


def _cp(n_axes):
    return pltpu.CompilerParams(dimension_semantics=("arbitrary",) * n_axes, vmem_limit_bytes=VMEM_LIMIT)


def _dot(a, b):
    return jnp.dot(a, b, preferred_element_type=F32)


def _dot_t(a, b):
    return lax.dot_general(a, b, (((1,), (1,)), ((), ())), preferred_element_type=F32)


def _silu(x):
    return x / (1.0 + jnp.exp(-x))


def _lane_low(shape):
    return lax.broadcasted_iota(I32, shape, len(shape) - 1) < HEAD_DIM


def _layernorm(y, g, b):
    mu = jnp.mean(y, axis=-1, keepdims=True)
    d = y - mu
    var = jnp.mean(d * d, axis=-1, keepdims=True)
    return d * lax.rsqrt(var + LN_EPS) * g + b


def _mod_kernel(c_ref, w_ref, b_ref, o_ref):
    s = _silu(c_ref[...]).astype(BF16)
    o_ref[...] = _dot(s, w_ref[...].astype(BF16)) + b_ref[...]


def _modulation(cond, w_ada, b_ada):
    tn = 1536
    return pl.pallas_call(
        _mod_kernel,
        out_shape=jax.ShapeDtypeStruct((DEPTH, 8, 6 * D_MODEL), F32),
        grid=(DEPTH, 6 * D_MODEL // tn),
        in_specs=[pl.BlockSpec((8, D_MODEL), lambda l, j: (0, 0)),
                  pl.BlockSpec((None, D_MODEL, tn), lambda l, j: (l, 0, j)),
                  pl.BlockSpec((None, 1, tn), lambda l, j: (l, 0, j))],
        out_specs=pl.BlockSpec((None, 8, tn), lambda l, j: (l, 0, j)),
        compiler_params=_cp(2), name="modulation",
    )(cond, w_ada, b_ada.reshape(DEPTH, 1, 6 * D_MODEL))


def _mod_spec(k, n_grid):
    if n_grid == 1:
        return pl.BlockSpec((None, 1, D_MODEL), lambda r: (r, 0, k))
    return pl.BlockSpec((None, 1, D_MODEL), lambda r, i: (r, 0, k))


def _rope_tables(L):
    t = np.arange(L)
    pos = (t // GRID_W, t % GRID_W)
    half = HEAD_DIM // 2
    inv = 1.0 / (ROPE_BASE ** (np.arange(0, half, 2) / half))
    cos = np.zeros((L, HEAD_DIM)); s_up = np.zeros((L, HEAD_DIM)); s_dn = np.zeros((L, HEAD_DIM))
    for part in range(2):
        ang = (pos[part][:, None] * inv[None, :]).astype(np.float32).astype(np.float64)
        q = half // 2
        base = part * half
        cos[:, base:base + q] = np.cos(ang); cos[:, base + q:base + half] = np.cos(ang)
        s_up[:, base:base + q] = -np.sin(ang)
        s_dn[:, base + q:base + half] = np.sin(ang)
    tile = lambda a: jnp.asarray(np.tile(a, (1, LANES // HEAD_DIM)), F32)
    return tile(cos), tile(s_up), tile(s_dn)


KV_COLUMNS = ((512, 128), (640, 128), (1024, 256), (1280, 256))


def _proj_kernel(*refs, rope, emit_kv):
    kv_refs = ()
    if emit_kv:
        refs, kv_refs = refs[:-len(KV_COLUMNS)], refs[-len(KV_COLUMNS):]
    if rope:
        x_ref, sc_ref, sh_ref, w_ref, cos_ref, sup_ref, sdn_ref, o_ref = refs
    else:
        x_ref, sc_ref, sh_ref, w_ref, o_ref = refs
    h = (x_ref[...] * (1.0 + sc_ref[...]) + sh_ref[...]).astype(BF16)
    n_rope = (H_A + KVH_A) * HEAD_DIM
    step = PROJ_COL_CHUNK
    for j in range(0, D_IN, step):
        acc = _dot(h, w_ref[:, j:j + step].astype(BF16))
        if rope and j < n_rope:
            q = HEAD_DIM // 4
            for c in range(0, step, LANES):
                a = acc[:, c:c + LANES]
                if j + c < n_rope:
                    a = (a * cos_ref[...] + pltpu.roll(a, LANES - q, 1) * sup_ref[...]
                         + pltpu.roll(a, q, 1) * sdn_ref[...])
                o_ref[:, j + c:j + c + LANES] = a
        else:
            o_ref[:, j:j + step] = acc
        for ref, (c0, w) in zip(kv_refs, KV_COLUMNS):
            if j <= c0 < j + step:
                n_req, _, seq = ref.shape
                for g in range(n_req):
                    ref[g] = acc[g * seq:(g + 1) * seq, c0 - j:c0 - j + w].T


def _project(x, mods, w_in, layer, tm, rope_tabs=None, kv_seq=None):
    R, L, _ = x.shape
    rope = rope_tabs is not None
    emit_kv = kv_seq is not None
    assert not (rope and emit_kv)
    out_shape = [jax.ShapeDtypeStruct((R, L, D_IN), F32)]
    out_specs = [pl.BlockSpec((None, tm, D_IN), lambda r, i: (r, i, 0))]
    if emit_kv:
        assert R == 1 and tm % kv_seq == 0
        out_shape += [jax.ShapeDtypeStruct((L // kv_seq, w, kv_seq), F32) for _, w in KV_COLUMNS]
        out_specs += [pl.BlockSpec((tm // kv_seq, w, kv_seq), lambda r, i: (i, 0, 0)) for _, w in KV_COLUMNS]
    in_specs = [pl.BlockSpec((None, tm, D_MODEL), lambda r, i: (r, i, 0)),
                _mod_spec(1, 2), _mod_spec(0, 2),
                pl.BlockSpec((None, D_MODEL, D_IN), lambda r, i: (layer, 0, 0), pipeline_mode=pl.Buffered(1))]
    args = [x, mods, mods, w_in]
    if rope:
        in_specs += [pl.BlockSpec((tm, LANES), lambda r, i: (i, 0))] * 3
        args += list(rope_tabs)
    res = pl.pallas_call(
        functools.partial(_proj_kernel, rope=rope, emit_kv=emit_kv),
        out_shape=out_shape, grid=(R, L // tm), in_specs=in_specs, out_specs=out_specs,
        compiler_params=_cp(2), name="in_proj",
    )(*args)
    return res if emit_kv else res[0]


def _mask_head(x, sub):
    low = _lane_low(x.shape)
    return jnp.where(low, x, 0.0) if sub == 0 else jnp.where(low, 0.0, x)


def _key_variant(k, sub):
    return _mask_head(k, sub).astype(BF16)


def _value_variant(v, sub):
    low = _lane_low(v.shape)
    first = jnp.where(low, v, 1.0) if sub == 0 else jnp.where(low, 1.0, v)
    return jnp.concatenate([first, jnp.ones_like(v)], axis=1).astype(BF16)


def _attend(lhs, keys, values, biases, sink=None):
    scores = []
    for k, b in zip(keys, biases):
        s = _dot_t(lhs, k)
        scores.append(s if b is None else s + b)
    tiles = [s[:, c:c + LANES] for s in scores for c in range(0, s.shape[1], LANES)]
    while len(tiles) > 1:
        tiles = [jnp.maximum(a, b) for a, b in zip(tiles[0::2], tiles[1::2])] + ([tiles[-1]] if len(tiles) % 2 else [])
    m = jnp.broadcast_to(tiles[0].max(axis=-1, keepdims=True), tiles[0].shape)
    if sink is not None:
        m = jnp.maximum(m, sink)
    o = None
    for s, v in zip(scores, values):
        p = [jnp.exp2(s[:, c:c + LANES] - m).astype(BF16) for c in range(0, s.shape[1], LANES)]
        pv = _dot(p[0] if len(p) == 1 else jnp.concatenate(p, axis=1), v)
        o = pv if o is None else o + pv
    return o, (None if sink is None else jnp.exp2(sink - m))


def _normalised(o):
    return o[:, :LANES] / o[:, LANES:]


def _gqa_variants(k, v, hk, wide):
    low = _lane_low(k.shape)
    k_sw, v_sw = pltpu.roll(k, HEAD_DIM, 1), pltpu.roll(v, HEAD_DIM, 1)
    k_dup = jnp.where(low, k, k_sw) if hk == 0 else jnp.where(low, k_sw, k)
    v_low, v_high = (v, v_sw) if hk == 0 else (v_sw, v)
    va = jnp.where(low, v_low, 1.0)
    if wide:
        va = jnp.concatenate([va, jnp.where(low, 1.0, v_high)], axis=1)
    return k_dup.astype(BF16), va.astype(BF16)


def _fill_gqa_variants(k_src, v_src, kv_ref, vv_ref, off, n, feature_major=False):
    step = VARIANT_ROWS
    for c in range(0, n, step):
        k = k_src[:, c:c + step].T if feature_major else k_src[c:c + step, :]
        v = v_src[:, c:c + step].T if feature_major else v_src[c:c + step, :]
        for hk in range(KVH_A):
            kd, va = _gqa_variants(k, v, hk, wide=False)
            kv_ref[hk, off + c:off + c + step, :] = kd
            vv_ref[hk, off + c:off + c + step, :] = va


def _fill_mha_variants(k_src, v_src, kv_ref, vv_ref, off, n, feature_major=False):
    step = VARIANT_ROWS
    for c in range(0, n, step):
        rows = slice(off + c, off + c + step)
        for p in range(H_B // 2):
            cs = slice(LANES * p, LANES * p + LANES)
            k = k_src[cs, c:c + step].T if feature_major else k_src[c:c + step, cs]
            v = v_src[cs, c:c + step].T if feature_major else v_src[c:c + step, cs]
            for sub in range(2):
                kv_ref[p, sub, rows, :] = _key_variant(k, sub)
                vv_ref[p, sub, rows, :] = _value_variant(v, sub)


def _sink_col(sink_ref, heads, rows):
    return jnp.concatenate([jnp.full((rows, LANES), sink_ref[h] * LOG2E, F32) for h in heads], axis=0)


def _gqa_attend(sink_ref, q_ref, o_ref, hk, keys, values, biases):
    rows = q_ref.shape[0]
    g = H_A // KVH_A
    heads = list(range(g * hk, g * hk + g))
    lhs = jnp.concatenate([_mask_head(q_ref[:, LANES * (h // 2):LANES * (h // 2) + LANES], h % 2) for h in heads],
                          axis=0)
    lhs = (lhs * ATTN_SCALE).astype(BF16)
    stacked = [None if b is None else jnp.concatenate([b] * g, axis=0) for b in biases]
    o, extra = _attend(lhs, keys, values, stacked, _sink_col(sink_ref, heads, rows))
    low_all = _lane_low((g * rows, LANES))
    low = _lane_low((rows, LANES))
    first = o[:, :LANES] + jnp.where(low_all, 0.0, extra)
    if o.shape[1] == LANES:
        out = first / pltpu.roll(first, HEAD_DIM, 1)
    else:
        second = o[:, LANES:] + jnp.where(low_all, extra, 0.0)
    for j in range(0, g, 2):
        h = heads[j]
        even, odd = slice(j * rows, (j + 1) * rows), slice((j + 1) * rows, (j + 2) * rows)
        if o.shape[1] == LANES:
            pair = jnp.where(low, out[even], pltpu.roll(out[odd], HEAD_DIM, 1))
        else:
            pair = jnp.where(low, first[even] / second[even], second[odd] / first[odd])
        o_ref[:, LANES * (h // 2):LANES * (h // 2) + LANES] = pair.astype(o_ref.dtype)


def _attn_a_ctx_kernel(sink_ref, q_ref, k_ref, v_ref, o_ref):
    for g in range(q_ref.shape[0]):
        for hk in range(KVH_A):
            kd, va = _gqa_variants(k_ref[g], v_ref[g], hk, wide=True)
            _gqa_attend(sink_ref, q_ref.at[g], o_ref.at[g], hk, [kd], [va], [None])


def _window_mask_table():
    nq = WIN_QBLOCKS
    row = np.arange(nq * BLOCK)[:, None]
    col = np.arange((nq + 2) * BLOCK)[None, :]
    tabs = [np.where(np.abs(col - off * BLOCK - row) <= WINDOW, 0.0, NEG) for off in range(3)]
    return jnp.asarray(np.stack(tabs), F32)


def _attn_a_lat_kernel(sink_ref, q_ref, k_ref, v_ref, kc_ref, vc_ref, mask_ref, o_ref, kv_ref, vv_ref, *, nb):
    i = pl.program_id(1)
    L, P = k_ref.shape[0], kc_ref.shape[1]

    @pl.when(i == 0)
    def _():
        _fill_gqa_variants(k_ref, v_ref, kv_ref, vv_ref, 0, L)
        _fill_gqa_variants(kc_ref, vc_ref, kv_ref, vv_ref, L, P, feature_major=True)

    nq = WIN_QBLOCKS
    wk = (nq + 2) * BLOCK
    sblk = jnp.clip(nq * i - 1, 0, nb - (nq + 2))
    start = pl.multiple_of(sblk * BLOCK, BLOCK)
    bias = mask_ref[nq * i - sblk]
    for hk in range(KVH_A):
        ks = [kv_ref[hk, pl.ds(start, wk), :], kv_ref[hk, L:L + P, :]]
        vs = [vv_ref[hk, pl.ds(start, wk), :], vv_ref[hk, L:L + P, :]]
        _gqa_attend(sink_ref, q_ref, o_ref, hk, ks, vs, [bias, None])


def _smem_spec():
    return pl.BlockSpec(memory_space=pltpu.SMEM)


def _attn_a_lat(qkv, sink, cache_k, cache_v, layer):
    R, L, _ = qkv.shape
    nb = L // BLOCK
    nq = WIN_QBLOCKS
    assert nb % nq == 0 and nb >= nq + 2
    P = cache_k.shape[3]
    wq, wkv = H_A * HEAD_DIM, KVH_A * HEAD_DIM
    cache_spec = pl.BlockSpec((None, None, wkv, P), lambda r, i: (r, layer, 0, 0))
    masks = _window_mask_table()
    return pl.pallas_call(
        functools.partial(_attn_a_lat_kernel, nb=nb),
        out_shape=jax.ShapeDtypeStruct((R, L, wq), F32),
        grid=(R, nb // nq),
        in_specs=[_smem_spec(),
                  pl.BlockSpec((None, nq * BLOCK, wq), lambda r, i: (r, i, COL_QA)),
                  pl.BlockSpec((None, L, wkv), lambda r, i: (r, 0, COL_KA)),
                  pl.BlockSpec((None, L, wkv), lambda r, i: (r, 0, COL_VA)),
                  cache_spec, cache_spec,
                  pl.BlockSpec(masks.shape, lambda r, i: (0, 0, 0))],
        out_specs=pl.BlockSpec((None, nq * BLOCK, wq), lambda r, i: (r, i, 0)),
        scratch_shapes=[pltpu.VMEM((KVH_A, L + P, LANES), BF16),
                        pltpu.VMEM((KVH_A, L + P, LANES), BF16)],
        compiler_params=_cp(2), name="attn_a_lat",
    )(sink, qkv, qkv, qkv, cache_k, cache_v, masks)


def _attn_b_ctx_kernel(q_ref, k_ref, v_ref, o_ref):
    G, L = q_ref.shape[:2]
    low = _lane_low((L, LANES))
    for g in range(G):
        for p in range(H_B // 2):
            cs = slice(LANES * p, LANES * p + LANES)
            lhs = (q_ref[g, :, cs] * ATTN_SCALE).astype(BF16)
            k, v = k_ref[g, :, cs], v_ref[g, :, cs]
            outs = [_normalised(_attend(lhs, [_key_variant(k, sub)], [_value_variant(v, sub)], [None])[0])
                    for sub in range(2)]
            o_ref[g, :, cs] = jnp.where(low, outs[0], outs[1]).astype(o_ref.dtype)


def _bias_table_kernel(rpb_ref, o_ref, t_ref):
    n_dr, n_dc = 2 * NA_ROWS - 1, 2 * NA_COLS - 1
    n_t = o_ref.shape[1]
    shape = (GRID_W, LANES)
    cq = lax.broadcasted_iota(I32, shape, 0)
    lane = lax.broadcasted_iota(I32, shape, 1)
    right = lane >= GRID_W
    ck = lane & (GRID_W - 1)
    cs = jnp.clip(cq - NA_COLS // 2, 0, GRID_W - NA_COLS)
    col_ok = (ck >= cs) & (ck < cs + NA_COLS)
    dc = ck - cq + (NA_COLS - 1)
    for j in (0, 1, n_t - 1, n_t):
        t_ref[j] = jnp.zeros(shape, F32)

    def build(r, carry, h):
        acc = jnp.zeros(shape, F32)
        for d in range(n_dc):
            acc = jnp.where(dc == d, rpb_ref[(h * n_dr + r) * n_dc + d] * LOG2E, acc)
        t_ref[r + 2] = jnp.where(col_ok, acc, NEG)
        return carry

    def compose(t, carry, h):
        o_ref[h, t] = jnp.where(right, t_ref[t + 1], t_ref[t])
        return carry

    for h in range(H_B):
        lax.fori_loop(0, n_dr, functools.partial(build, h=h), 0)
        lax.fori_loop(0, n_t, functools.partial(compose, h=h), 0)


def _bias_table(rpb_l):
    n_t = 2 * (NA_ROWS + 1)
    return pl.pallas_call(
        _bias_table_kernel,
        out_shape=jax.ShapeDtypeStruct((H_B, n_t, GRID_W, LANES), F32),
        in_specs=[_smem_spec()],
        out_specs=pl.BlockSpec(memory_space=pltpu.VMEM),
        scratch_shapes=[pltpu.VMEM((n_t + 1, GRID_W, LANES), F32)],
        name="na_bias_table",
    )(rpb_l.reshape(-1))


def _attn_b_lat_kernel(q_ref, k_ref, v_ref, kc_ref, vc_ref, tp_ref, o_ref, kv_ref, vv_ref, *, nb, rows_total):
    i = pl.program_id(1)
    L, P = k_ref.shape[0], kc_ref.shape[1]

    @pl.when(i == 0)
    def _():
        _fill_mha_variants(k_ref, v_ref, kv_ref, vv_ref, 0, L)
        _fill_mha_variants(kc_ref, vc_ref, kv_ref, vv_ref, L, P, feature_major=True)

    nq = NA_QBLOCKS
    nwin = NA_WIN_BLOCKS + nq - 1
    wk = nwin * BLOCK
    rows = nq * BLOCK
    sblk = jnp.clip(nq * i - 2, 0, nb - nwin)
    start = pl.multiple_of(sblk * BLOCK, BLOCK)
    shape = (rows, wk)
    rq = 2 * nq * i + (lax.broadcasted_iota(I32, shape, 0) >> 6)
    rk = 2 * sblk + (lax.broadcasted_iota(I32, shape, 1) >> 6)
    rs = jnp.clip(rq - NA_ROWS // 2, 0, rows_total - NA_ROWS)
    row_bias = jnp.where((rk >= rs) & (rk < rs + NA_ROWS), 0.0, NEG)
    n_t = tp_ref.shape[1]
    low = _lane_low((rows, LANES))
    for p in range(H_B // 2):
        cs = slice(LANES * p, LANES * p + LANES)
        lhs = (q_ref[:, cs] * ATTN_SCALE).astype(BF16)
        outs = []
        for sub in range(2):
            h = 2 * p + sub
            strips = []
            for qq in range(nq):
                tiles = []
                for kb in range(nwin):
                    t = jnp.clip(2 * (sblk + kb - (nq * i + qq)) + NA_ROWS + 1, 1, n_t - 1)
                    tiles.append(jnp.concatenate([tp_ref[h, t], tp_ref[h, t - 1]], axis=0))
                strips.append(jnp.concatenate(tiles, axis=1))
            bias = jnp.concatenate(strips, axis=0) + row_bias
            ks = [kv_ref[p, sub, pl.ds(start, wk), :], kv_ref[p, sub, L:L + P, :]]
            vs = [vv_ref[p, sub, pl.ds(start, wk), :], vv_ref[p, sub, L:L + P, :]]
            outs.append(_normalised(_attend(lhs, ks, vs, [bias, None])[0]))
        o_ref[:, cs] = jnp.where(low, outs[0], outs[1]).astype(o_ref.dtype)


def _attn_ab_ctx_kernel(sink_ref, qa_ref, ka_ref, va_ref, qb_ref, kb_ref, vb_ref, oa_ref, ob_ref):
    _attn_a_ctx_kernel(sink_ref, qa_ref, ka_ref, va_ref, oa_ref)
    _attn_b_ctx_kernel(qb_ref, kb_ref, vb_ref, ob_ref)


def _attn_ab_ctx(qkv, sink):
    R, L, _ = qkv.shape
    wq, wkv, w = H_A * HEAD_DIM, KVH_A * HEAD_DIM, H_B * HEAD_DIM
    G = CTX_GROUP_A
    assert R % G == 0
    blk = lambda width, cb: pl.BlockSpec((G, L, width), lambda r: (r, 0, cb))
    return pl.pallas_call(
        _attn_ab_ctx_kernel,
        out_shape=[jax.ShapeDtypeStruct((R, L, wq), F32), jax.ShapeDtypeStruct((R, L, w), F32)],
        grid=(R // G,),
        in_specs=[_smem_spec(), blk(wq, COL_QA), blk(wkv, COL_KA), blk(wkv, COL_VA),
                  blk(w, COL_QB), blk(w, COL_KB), blk(w, COL_VB)],
        out_specs=[blk(wq, 0), blk(w, 0)],
        compiler_params=_cp(1), name="attn_ab_ctx",
    )(sink, qkv, qkv, qkv, qkv, qkv, qkv)


def _attn_b_lat(qkv, cache_k, cache_v, bias_tab, layer):
    R, L, _ = qkv.shape
    nb = L // BLOCK
    P = cache_k.shape[3]
    w = H_B * HEAD_DIM
    cache_spec = pl.BlockSpec((None, None, w, P), lambda r, i: (r, layer, 0, 0))
    rows = NA_QBLOCKS * BLOCK
    assert nb % NA_QBLOCKS == 0 and nb >= NA_WIN_BLOCKS + NA_QBLOCKS - 1
    return pl.pallas_call(
        functools.partial(_attn_b_lat_kernel, nb=nb, rows_total=L // GRID_W),
        out_shape=jax.ShapeDtypeStruct((R, L, w), F32),
        grid=(R, nb // NA_QBLOCKS),
        in_specs=[pl.BlockSpec((None, rows, w), lambda r, i: (r, i, COL_QB)),
                  pl.BlockSpec((None, L, w), lambda r, i: (r, 0, COL_KB)),
                  pl.BlockSpec((None, L, w), lambda r, i: (r, 0, COL_VB)),
                  cache_spec, cache_spec,
                  pl.BlockSpec(bias_tab.shape, lambda r, i: (0, 0, 0, 0))],
        out_specs=pl.BlockSpec((None, rows, w), lambda r, i: (r, i, 0)),
        scratch_shapes=[pltpu.VMEM((H_B // 2, 2, L + P, LANES), BF16),
                        pltpu.VMEM((H_B // 2, 2, L + P, 2 * LANES), BF16)],
        compiler_params=_cp(2), name="attn_b_lat",
    )(qkv, qkv, qkv, cache_k, cache_v, bias_tab)


T_ZF, T_ZB, T_XF, T_XB, T_GF, T_GB, T_MA, T_MB = range(8)


def _ret_kernel(*refs, nc, has_s0, out_state):
    refs = list(refs)
    dec_ref, q_ref, k_ref, v_ref, g_ref, gn_ref = refs[:6]
    pos = 6
    s0_ref = None
    if has_s0:
        s0_ref = refs[pos]; pos += 1
    o_ref = refs[pos]; pos += 1
    st_ref = None
    if out_state:
        st_ref = refs[pos]; pos += 1
    sf_ref, sb_ref, tab_ref = refs[pos], refs[pos + 1], refs[pos + 2]

    sq = (BLOCK, LANES)
    low = _lane_low(sq)
    row_low = lax.broadcasted_iota(I32, sq, 0) < HEAD_DIM
    same_head = low == row_low
    n_pairs = H_C // 2

    @pl.when(pl.program_id(0) == 0)
    def _():
        ri = lax.broadcasted_iota(I32, sq, 0).astype(F32)
        ci = lax.broadcasted_iota(I32, sq, 1).astype(F32)

        def log_gamma(d, h):
            return -jnp.exp(jnp.full(sq, dec_ref[d, h], F32))

        for p in range(n_pairs):
            lf = [log_gamma(0, 2 * p), log_gamma(0, 2 * p + 1)]
            lb = [log_gamma(1, 2 * p), log_gamma(1, 2 * p + 1)]
            lf_lane, lb_lane = jnp.where(low, lf[0], lf[1]), jnp.where(low, lb[0], lb[1])
            lf_row, lb_row = jnp.where(row_low, lf[0], lf[1]), jnp.where(row_low, lb[0], lb[1])
            tab_ref[p, T_ZF] = jnp.exp(lf_lane * (BLOCK - 1.0 - ri))
            tab_ref[p, T_ZB] = jnp.exp(lb_lane * ri)
            tab_ref[p, T_XF] = jnp.exp(lf_lane * (ri + 1.0))
            tab_ref[p, T_XB] = jnp.exp(lb_lane * (BLOCK - ri))
            tab_ref[p, T_GF] = jnp.exp(lf_row * float(BLOCK))
            tab_ref[p, T_GB] = jnp.exp(lb_row * float(BLOCK))
            rel = ri - ci
            for sub in range(2):
                fwd = jnp.where(rel >= 0, jnp.exp(lf[sub] * jnp.maximum(rel, 0.0)), 0.0)
                bwd = jnp.where(rel <= 0, jnp.exp(lb[sub] * jnp.maximum(-rel, 0.0)), 0.0)
                tab_ref[p, T_MA + sub] = fwd + bwd

    n_req = q_ref.shape[0]
    units = [(r, p) for r in range(n_req) for p in range(n_pairs)]

    def chunk(ref, r, c, p):
        r0 = pl.multiple_of(c * BLOCK, BLOCK)
        return ref[r, pl.ds(r0, BLOCK), LANES * p:LANES * p + LANES]

    unroll = min(nc, 8)

    def inc_body(c, carry):
        for r, p in units:
            kt = (chunk(k_ref, r, c, p) * QK_SCALE).T.astype(BF16)
            v = chunk(v_ref, r, c, p)
            zv = jnp.concatenate([v * tab_ref[p, T_ZF], v * tab_ref[p, T_ZB]], axis=1).astype(BF16)
            inc = _dot(kt, zv)
            sf_ref[r, c, p] = jnp.where(same_head, inc[:, :LANES], 0.0)
            sb_ref[r, c, p] = jnp.where(same_head, inc[:, LANES:], 0.0)
        return carry

    lax.fori_loop(0, nc, inc_body, 0, unroll=unroll)

    def scan_body(t, S, ref, tg, reverse):
        c = nc - 1 - t if reverse else t
        new = []
        for (r, p), state in zip(units, S):
            inc = ref[r, c, p]
            ref[r, c, p] = state
            new.append(tab_ref[p, tg] * state + inc)
        return tuple(new)

    zero = jnp.zeros(sq, F32)
    s_f0 = tuple(s0_ref[r, 0, p] if has_s0 else zero for r, p in units)
    s_b0 = tuple(s0_ref[r, 1, p] if has_s0 else zero for r, p in units)
    scan_unroll = nc if nc <= 4 else 1
    s_f = lax.fori_loop(0, nc, functools.partial(scan_body, ref=sf_ref, tg=T_GF, reverse=False), s_f0,
                        unroll=scan_unroll)
    s_b = lax.fori_loop(0, nc, functools.partial(scan_body, ref=sb_ref, tg=T_GB, reverse=True), s_b0,
                        unroll=scan_unroll)
    if out_state:
        for (r, p), f_state, b_state in zip(units, s_f, s_b):
            st_ref[r, 0, p] = f_state
            st_ref[r, 1, p] = b_state

    def out_body(c, carry):
        for r, p in units:
            q = chunk(q_ref, r, c, p)
            k16 = (chunk(k_ref, r, c, p) * QK_SCALE).astype(BF16)
            v = chunk(v_ref, r, c, p)
            a = _dot_t(jnp.concatenate([_mask_head(q, 0), _mask_head(q, 1)], axis=0).astype(BF16), k16)
            lhs = jnp.concatenate([a[:BLOCK] * tab_ref[p, T_MA], a[BLOCK:] * tab_ref[p, T_MA + 1],
                                   q * tab_ref[p, T_XF], q * tab_ref[p, T_XB]], axis=1).astype(BF16)
            rhs = jnp.concatenate([_mask_head(v, 0), _mask_head(v, 1), sf_ref[r, c, p], sb_ref[r, c, p]],
                                  axis=0).astype(BF16)
            o = _dot(lhs, rhs)
            mu = jnp.where(low, _mask_head(o, 0).sum(-1, keepdims=True),
                           _mask_head(o, 1).sum(-1, keepdims=True)) * (1.0 / HEAD_DIM)
            d = o - mu
            d2 = d * d
            var = jnp.where(low, _mask_head(d2, 0).sum(-1, keepdims=True),
                            _mask_head(d2, 1).sum(-1, keepdims=True)) * (1.0 / HEAD_DIM)
            cs = slice(LANES * p, LANES * p + LANES)
            on = d * lax.rsqrt(var + LN_EPS) * gn_ref[:, cs]
            r0 = pl.multiple_of(c * BLOCK, BLOCK)
            o_ref[r, pl.ds(r0, BLOCK), cs] = (_silu(chunk(g_ref, r, c, p)) * on).astype(o_ref.dtype)
        return carry

    lax.fori_loop(0, nc, out_body, 0, unroll=unroll)


def _retention(qkv, decay, gn_w, s0=None, layer=0, out_state=False):
    R, L, _ = qkv.shape
    nc = L // BLOCK
    w = H_C * HEAD_DIM
    n_pairs = H_C // 2
    G = max(1, min(R, RET_GROUP_CHUNKS // nc))
    assert R % G == 0
    col = lambda cb: pl.BlockSpec((G, L, w), lambda r: (r, 0, cb))
    in_specs = [_smem_spec(), col(COL_QC), col(COL_KC), col(COL_VC), col(COL_GC),
                pl.BlockSpec((1, w), lambda r: (0, 0))]
    args = [decay, qkv, qkv, qkv, qkv, gn_w.reshape(1, w)]
    if s0 is not None:
        in_specs.append(pl.BlockSpec((G, None, 2, n_pairs, BLOCK, LANES), lambda r: (r, layer, 0, 0, 0, 0)))
        args.append(s0)
    out_shape = [jax.ShapeDtypeStruct((R, L, w), F32)]
    out_specs = [pl.BlockSpec((G, L, w), lambda r: (r, 0, 0))]
    if out_state:
        out_shape.append(jax.ShapeDtypeStruct((R, 2, n_pairs, BLOCK, LANES), F32))
        out_specs.append(pl.BlockSpec((G, 2, n_pairs, BLOCK, LANES), lambda r: (r, 0, 0, 0, 0)))
    res = pl.pallas_call(
        functools.partial(_ret_kernel, nc=nc, has_s0=s0 is not None, out_state=out_state),
        out_shape=out_shape, grid=(R // G,), in_specs=in_specs, out_specs=out_specs,
        scratch_shapes=[pltpu.VMEM((G, nc, n_pairs, BLOCK, LANES), F32),
                        pltpu.VMEM((G, nc, n_pairs, BLOCK, LANES), F32),
                        pltpu.VMEM((n_pairs, 8, BLOCK, LANES), F32)],
        compiler_params=_cp(1), name="retention",
    )(*args)
    return res if out_state else res[0]


def _pair_states_to_heads(st):
    a = st[..., :HEAD_DIM, :HEAD_DIM]
    b = st[..., HEAD_DIM:, HEAD_DIM:]
    return jnp.stack([a, b], axis=3).reshape(st.shape[0], 2, H_C, HEAD_DIM, HEAD_DIM)


def _heads_to_pair_states(s):
    s = s.reshape(*s.shape[:3], H_C // 2, 2, HEAD_DIM, HEAD_DIM)
    z = jnp.zeros_like(s[..., 0, :, :])
    top = jnp.concatenate([s[..., 0, :, :], z], axis=-1)
    bot = jnp.concatenate([z, s[..., 1, :, :]], axis=-1)
    return jnp.concatenate([top, bot], axis=-2)


def _outproj_kernel(oa_ref, ob_ref, oc_ref, x_ref, g1_ref, sc2_ref, sh2_ref, w_ref, lg_ref, lb_ref, wr_ref,
                    x1_ref, h2_ref, aff_ref):
    wa, wb = H_A * HEAD_DIM, H_B * HEAD_DIM
    tm = x_ref.shape[0]
    part = tm // OUTPROJ_PARTS
    for r0 in range(0, tm, part):
        rs = slice(r0, r0 + part)
        mix = (_dot(oa_ref[rs, :].astype(BF16), w_ref[0:wa, :])
               + _dot(ob_ref[rs, :].astype(BF16), w_ref[wa:wa + wb, :])
               + _dot(oc_ref[rs, :].astype(BF16), w_ref[wa + wb:, :]))
        x1 = _layernorm(ALPHA * x_ref[rs, :] + g1_ref[...] * mix, lg_ref[...], lb_ref[...])
        x1_ref[rs, :] = x1
        h2 = (x1 * (1.0 + sc2_ref[...]) + sh2_ref[...]).astype(BF16)
        h2_ref[rs, :] = h2
        logits = _dot(h2, wr_ref[...])
        e = jnp.exp(logits - logits.max(axis=-1, keepdims=True))
        aff_ref[rs, :] = e / e.sum(axis=-1, keepdims=True)


def _outproj(oa, ob, oc, x, mods, w_out_bf16, ln_g, ln_b, w_router_bf16, layer, tm):
    R, L, _ = x.shape
    row = lambda w: pl.BlockSpec((None, tm, w), lambda r, i: (r, i, 0))
    vec = pl.BlockSpec((1, D_MODEL), lambda r, i: (0, 0))
    return pl.pallas_call(
        _outproj_kernel,
        out_shape=[jax.ShapeDtypeStruct((R, L, D_MODEL), F32),
                   jax.ShapeDtypeStruct((R, L, D_MODEL), BF16),
                   jax.ShapeDtypeStruct((R, L, N_EXPERTS), F32)],
        grid=(R, L // tm),
        in_specs=[row(H_A * HEAD_DIM), row(H_B * HEAD_DIM), row(H_C * HEAD_DIM), row(D_MODEL),
                  _mod_spec(2, 2), _mod_spec(4, 2), _mod_spec(3, 2),
                  pl.BlockSpec((None, D_MODEL, D_MODEL), lambda r, i: (layer, 0, 0)), vec, vec,
                  pl.BlockSpec((None, D_MODEL, N_EXPERTS), lambda r, i: (layer, 0, 0))],
        out_specs=[row(D_MODEL), row(D_MODEL), row(N_EXPERTS)],
        compiler_params=_cp(2), name="out_proj_ln_router",
    )(oa, ob, oc, x, mods, mods, mods, w_out_bf16, ln_g.reshape(1, -1), ln_b.reshape(1, -1), w_router_bf16)


CUMSUM_CHUNK = 256


def _excl_cumsum_lanes(x):
    rows, n = x.shape
    w = min(CUMSUM_CHUNK, n)
    tri = (lax.broadcasted_iota(I32, (w, w), 0) < lax.broadcasted_iota(I32, (w, w), 1))
    tri = jnp.where(tri, 1.0, 0.0).astype(BF16)
    carry = jnp.zeros((rows, 1), F32)
    outs = []
    for c in range(0, n, w):
        xc = x[:, c:c + w]
        outs.append(_dot(xc.astype(BF16), tri) + carry)
        carry = carry + xc.sum(axis=-1, keepdims=True)
    return outs[0] if len(outs) == 1 else jnp.concatenate(outs, axis=1)


def _select_kernel(a_ref, pos_ref, *, cap):
    rb, ne, n = a_ref.shape
    rows = rb * ne
    a = a_ref[...].reshape(rows, n)
    capf = float(cap)

    def body(_, c):
        lo, hi = c
        mid = lo + ((hi - lo + 1) >> 1)
        cnt = jnp.where(a >= pltpu.bitcast(mid, F32), 1.0, 0.0).sum(axis=-1, keepdims=True)
        ok = cnt >= capf
        return jnp.where(ok, mid, lo), jnp.where(ok, hi, mid - 1)

    one_bits = 0x3F800000
    lo, _ = lax.fori_loop(0, 30, body, (jnp.zeros((rows, 1), I32), jnp.full((rows, 1), one_bits, I32)))
    thr = pltpu.bitcast(lo, F32)
    gt = jnp.where(a > thr, 1.0, 0.0)
    eq = jnp.where(a == thr, 1.0, 0.0)
    need = capf - gt.sum(axis=-1, keepdims=True)
    sel = gt + eq * jnp.where(_excl_cumsum_lanes(eq) < need, 1.0, 0.0)
    slot = _excl_cumsum_lanes(sel)
    pos_ref[...] = jnp.where(sel > 0.5, slot, -1.0).astype(I32).reshape(rb, ne, n)


def _select(aff_t, cap):
    R, ne, n = aff_t.shape
    rb = min(R, max(1, SELECT_ELEMS // (ne * n)))
    return pl.pallas_call(
        functools.partial(_select_kernel, cap=cap),
        out_shape=jax.ShapeDtypeStruct((R, ne, n), I32),
        grid=(R // rb,),
        in_specs=[pl.BlockSpec((rb, ne, n), lambda r: (r, 0, 0))],
        out_specs=pl.BlockSpec((rb, ne, n), lambda r: (r, 0, 0)),
        compiler_params=_cp(1), name="expert_select",
    )(aff_t)


def _gather_kernel(pos_ref, aff_ref, h_ref, x_ref, gate_ref, *, cap):
    n_req, eb, _, n = pos_ref.shape
    slot = lax.broadcasted_iota(I32, (cap, n), 0)
    for g in range(n_req):
        parts = []
        for j in range(eb):
            hit = slot == pos_ref[g, j]
            parts.append(jnp.where(hit, 1.0, 0.0).astype(BF16))
            gate_ref[j, g * cap:(g + 1) * cap, :] = jnp.where(hit, aff_ref[g, j], 0.0).sum(axis=-1, keepdims=True)
        onehot = parts[0] if eb == 1 else jnp.concatenate(parts, axis=0)
        xe = _dot(onehot, h_ref[g]).astype(BF16)
        for j in range(eb):
            x_ref[j, g * cap:(g + 1) * cap, :] = xe[j * cap:(j + 1) * cap]


def _gather(pos, aff_t, h2, cap):
    R, ne, n = pos.shape
    eb = max(1, min(ne, 2 * ROW_TILE // cap))
    G = max(1, min(R, ONEHOT_GROUP_TOKENS // n))
    assert R % G == 0
    row = pl.BlockSpec((G, eb, 1, n), lambda r, e: (r, e, 0, 0))
    return pl.pallas_call(
        functools.partial(_gather_kernel, cap=cap),
        out_shape=[jax.ShapeDtypeStruct((ne, R * cap, D_MODEL), BF16),
                   jax.ShapeDtypeStruct((ne, R * cap, 1), F32)],
        grid=(R // G, ne // eb),
        in_specs=[row, row, pl.BlockSpec((G, n, D_MODEL), lambda r, e: (r, 0, 0))],
        out_specs=[pl.BlockSpec((eb, G * cap, D_MODEL), lambda r, e: (e, r, 0)),
                   pl.BlockSpec((eb, G * cap, 1), lambda r, e: (e, r, 0))],
        compiler_params=_cp(2), name="expert_gather",
    )(pos.reshape(R, ne, 1, n), aff_t.reshape(R, ne, 1, n), h2)


FF_TILE = 512


def _ffn_kernel(x0_ref, x1_ref, wa_ref, wb_ref, wd_ref, g0_ref, g1_ref, y0_ref, y1_ref, acc_ref):
    f = pl.program_id(1)

    @pl.when(f == 0)
    def _():
        acc_ref[...] = jnp.zeros_like(acc_ref)

    x = jnp.concatenate([x0_ref[...], x1_ref[...]], axis=0)
    a = _dot(x, wa_ref[...].astype(BF16))
    b = _dot(x, wb_ref[...].astype(BF16))
    acc_ref[...] += _dot((_silu(a) * b).astype(BF16), wd_ref[...].astype(BF16))

    @pl.when(f == pl.num_programs(1) - 1)
    def _():
        s0 = x0_ref.shape[0]
        y0_ref[...] = (acc_ref[:s0, :] * g0_ref[...]).astype(BF16)
        y1_ref[...] = (acc_ref[s0:, :] * g1_ref[...]).astype(BF16)


def _experts(xe0, gates0, xe1, gates1, w_gate_up, w_down, layer):
    ne, S0, _ = xe0.shape
    S1 = xe1.shape[1]
    nf = D_FF // FF_TILE
    rows = lambda S, w: pl.BlockSpec((None, S, w), lambda e, f: (e, 0, 0))
    return pl.pallas_call(
        _ffn_kernel,
        out_shape=[jax.ShapeDtypeStruct((ne, S0, D_MODEL), BF16), jax.ShapeDtypeStruct((ne, S1, D_MODEL), BF16)],
        grid=(ne, nf),
        in_specs=[rows(S0, D_MODEL), rows(S1, D_MODEL),
                  pl.BlockSpec((None, None, D_MODEL, FF_TILE), lambda e, f: (layer, e, 0, f)),
                  pl.BlockSpec((None, None, D_MODEL, FF_TILE), lambda e, f: (layer, e, 0, nf + f)),
                  pl.BlockSpec((None, None, FF_TILE, D_MODEL), lambda e, f: (layer, e, f, 0)),
                  rows(S0, 1), rows(S1, 1)],
        out_specs=[rows(S0, D_MODEL), rows(S1, D_MODEL)],
        scratch_shapes=[pltpu.VMEM((S0 + S1, D_MODEL), F32)],
        compiler_params=_cp(2), name="expert_ffn",
    )(xe0, xe1, w_gate_up, w_gate_up, w_down, gates0, gates1)


def _scatter_kernel(pos_ref, y_ref, x1_ref, g2_ref, lg_ref, lb_ref, o_ref, *, cap):
    n_req, tm, ne = pos_ref.shape
    cw = max(cap, LANES)
    epc = cw // cap
    lane = lax.broadcasted_iota(I32, (tm, cw), 1)
    n_chunks = ne // epc
    per_dot = max(1, min(n_chunks, SCATTER_CONTRACTION // cw))
    for g in range(n_req):
        pos = pos_ref[g]

        def onehot_chunk(c):
            onehot = jnp.zeros((tm, cw), F32)
            for jj in range(epc):
                col = pos[:, c * epc + jj:c * epc + jj + 1]
                tgt = jnp.where(col >= 0, col + jj * cap, -1)
                onehot = onehot + jnp.where(lane == tgt, 1.0, 0.0)
            return onehot.astype(BF16)

        acc = None
        for c0 in range(0, n_chunks, per_dot):
            onehot = [onehot_chunk(c) for c in range(c0, c0 + per_dot)]
            onehot = onehot[0] if per_dot == 1 else jnp.concatenate(onehot, axis=1)
            yc = y_ref[c0 * epc:(c0 + per_dot) * epc, g * cap:(g + 1) * cap, :].reshape(per_dot * cw, D_MODEL)
            part = _dot(onehot, yc)
            acc = part if acc is None else acc + part
        o_ref[g] = _layernorm(ALPHA * x1_ref[g] + g2_ref[g] * acc, lg_ref[...], lb_ref[...])


def _scatter(pos_tok, y, x1, mods, ln_g, ln_b, cap, tm):
    R, n, ne = pos_tok.shape
    G = max(1, min(R, ONEHOT_GROUP_TOKENS // n))
    assert R % G == 0 and (G == 1 or tm == n)
    vec = pl.BlockSpec((1, D_MODEL), lambda r, i: (0, 0))
    return pl.pallas_call(
        functools.partial(_scatter_kernel, cap=cap),
        out_shape=jax.ShapeDtypeStruct((R, n, D_MODEL), F32),
        grid=(R // G, n // tm),
        in_specs=[pl.BlockSpec((G, tm, ne), lambda r, i: (r, i, 0)),
                  pl.BlockSpec((ne, G * cap, D_MODEL), lambda r, i: (0, r, 0)),
                  pl.BlockSpec((G, tm, D_MODEL), lambda r, i: (r, i, 0)),
                  pl.BlockSpec((G, 1, D_MODEL), lambda r, i: (r, 0, 5)), vec, vec],
        out_specs=pl.BlockSpec((G, tm, D_MODEL), lambda r, i: (r, i, 0)),
        compiler_params=_cp(2), name="expert_scatter_ln",
    )(pos_tok, y, x1, mods, ln_g.reshape(1, -1), ln_b.reshape(1, -1))


def _capacity(n):
    return CAPACITY_FACTOR * n // N_EXPERTS


def _route(h2, aff):
    cap = _capacity(h2.shape[1])
    aff_t = jnp.swapaxes(aff, 1, 2)
    pos = _select(aff_t, cap)
    xe, gates = _gather(pos, aff_t, h2, cap)
    return pos, xe, gates


def _channel_mixers(groups, l, w_gate_up, w_down, ln2_g, ln2_b):
    routed = [_route(g["h2"], g["aff"]) for g in groups]
    ys = _experts(routed[0][1], routed[0][2], routed[1][1], routed[1][2], w_gate_up, w_down, l)
    return [_scatter(jnp.swapaxes(r[0], 1, 2), y, g["x1"], g["mods"], ln2_g[l], ln2_b[l],
                     _capacity(g["h2"].shape[1]), g["tm"]) for g, r, y in zip(groups, routed, ys)]


def kernel(x_prompt, x_sample, cache_attn_a_k, cache_attn_a_v, cache_attn_b_k, cache_attn_b_v, state_ret, c, c_ctx,
           w_ada, b_ada, w_in, w_out, attn_sink, na_rpb, ret_decay, ret_gn, ln1_g, ln1_b, ln2_g, ln2_b,
           w_router, w_gate_up, w_down):
    n_ctx, L_ctx, _ = x_prompt.shape
    n_lat, L_lat, _ = x_sample.shape
    P = cache_attn_a_k.shape[2]

    cond = jnp.concatenate([c_ctx[None, :], c, jnp.zeros((8 - 1 - n_lat, D_MODEL), F32)], axis=0)
    mod = _modulation(cond, w_ada, b_ada)
    w_out16, w_r16 = w_out.astype(BF16), w_router.astype(BF16)
    rope_tabs = _rope_tables(L_lat)
    feature_major = lambda t: jnp.transpose(t, (0, 1, 3, 4, 2)).reshape(n_lat, DEPTH, -1, P)
    cak, cav = feature_major(cache_attn_a_k), feature_major(cache_attn_a_v)
    cbk, cbv = feature_major(cache_attn_b_k), feature_major(cache_attn_b_v)
    s0_pairs = _heads_to_pair_states(state_ret)

    xc, xs = x_prompt, x_sample
    tm = ROW_TILE
    a_k, a_v, b_k, b_v, st = [], [], [], [], []
    flat = lambda a: a.reshape(1, n_ctx * L_ctx, a.shape[-1])
    unflat = lambda a: a.reshape(n_ctx, L_ctx, a.shape[-1])
    for l in range(DEPTH):
        mod_c = mod[l, 0:1][:, None, :]
        mods_c = jnp.broadcast_to(mod_c, (n_ctx, 1, 6 * D_MODEL))
        qkv, ka, va, kb, vb = _project(flat(xc), mod_c, w_in, l, tm, kv_seq=L_ctx)
        qkv = unflat(qkv)
        oa, ob = _attn_ab_ctx(qkv, attn_sink[l])
        oc, s_l = _retention(qkv, ret_decay[l], ret_gn[l], out_state=True)
        x1, h2, aff = [unflat(t) for t in _outproj(flat(oa), flat(ob), flat(oc), flat(xc), mod_c, w_out16,
                                                   ln1_g[l], ln1_b[l], w_r16, l, tm)]
        ctx = dict(x1=x1, h2=h2, aff=aff, mods=mods_c, tm=L_ctx)
        heads_last = lambda t, h: jnp.transpose(t.reshape(n_ctx, h, HEAD_DIM, L_ctx), (0, 3, 1, 2))
        a_k.append(heads_last(ka, KVH_A))
        a_v.append(heads_last(va, KVH_A))
        b_k.append(heads_last(kb, H_B))
        b_v.append(heads_last(vb, H_B))
        st.append(_pair_states_to_heads(s_l))

        mods_s = mod[l, 1:1 + n_lat][:, None, :]
        qkv = _project(xs, mods_s, w_in, l, tm, rope_tabs)
        oa = _attn_a_lat(qkv, attn_sink[l], cak, cav, l)
        ob = _attn_b_lat(qkv, cbk, cbv, _bias_table(na_rpb[l]), l)
        oc = _retention(qkv, ret_decay[l], ret_gn[l], s0=s0_pairs, layer=l)
        x1, h2, aff = _outproj(oa, ob, oc, xs, mods_s, w_out16, ln1_g[l], ln1_b[l], w_r16, l, tm)
        lat = dict(x1=x1, h2=h2, aff=aff, mods=mods_s, tm=tm)

        xc, xs = _channel_mixers([ctx, lat], l, w_gate_up, w_down, ln2_g, ln2_b)

    return (xc, xs, jnp.stack(a_k, 1), jnp.stack(a_v, 1), jnp.stack(b_k, 1), jnp.stack(b_v, 1),
            jnp.stack(st, 1))
```
